```python
import jax, jax.numpy as jnp
from jax import lax
import numpy as np

D_MODEL = 1024
BATCH = 4
SEQ = 4096
DEPTH = 2

N_MIXERS = 2
N_A = (DEPTH + N_MIXERS - 1) // N_MIXERS
N_B = DEPTH // N_MIXERS

CHUNK = 128
GM_HALF = D_MODEL
GM_GROUPS = 8
GM_GROUP_DIM = GM_HALF // GM_GROUPS

ATT_HEADS = 8
ATT_HEAD_DIM = D_MODEL // ATT_HEADS
KV_LATENT = 256
IDX_HEADS = 8
IDX_DIM = 64
TOPK_MAX = 256
Q_BLOCK = 128
Q_WIDTH = ATT_HEADS * ATT_HEAD_DIM
PROJ_B = Q_WIDTH + KV_LATENT + IDX_HEADS * IDX_DIM + IDX_DIM + IDX_HEADS
SPLITS_B = (Q_WIDTH, Q_WIDTH + KV_LATENT, Q_WIDTH + KV_LATENT + IDX_HEADS * IDX_DIM,
            Q_WIDTH + KV_LATENT + IDX_HEADS * IDX_DIM + IDX_DIM)

N_EGROUPS = 4
EXPERTS_PER_GROUP = 8
N_EXPERTS = N_EGROUPS * EXPERTS_PER_GROUP
TOPK_E = 2
EXPERT_FF = 256

PLE_DIM = 256

ALPHA = (2 * DEPTH) ** 0.25
BETA = (8 * DEPTH) ** -0.25

kernel_name = 'hybrid_gmlp_dsa_hiermoe_deepnorm'


def _layer_norm(x, g, b, eps=1e-5):
    xf = x.astype(jnp.float32)
    mu = jnp.mean(xf, axis=-1, keepdims=True)
    var = jnp.mean(jnp.square(xf - mu), axis=-1, keepdims=True)
    return ((xf - mu) * lax.rsqrt(var + eps)).astype(x.dtype) * g + b


def _rms_norm(x, g, eps=1e-6):
    xf = x.astype(jnp.float32)
    ms = jnp.mean(jnp.square(xf), axis=-1, keepdims=True)
    return (xf * lax.rsqrt(ms + eps)).astype(x.dtype) * g


def chunked_gmlp(x, w_in, b_in, vn_g, vn_b, w_s, b_s, w_out):
    bsz, seq, _ = x.shape
    z = jax.nn.gelu(x @ w_in + b_in, approximate=False)
    u, v = jnp.split(z, 2, axis=-1)
    v = _layer_norm(v, vn_g, vn_b)
    v = v.reshape(bsz, seq // CHUNK, CHUNK, GM_GROUPS, GM_GROUP_DIM)
    causal = jnp.tril(jnp.ones((CHUNK, CHUNK), dtype=bool))
    w_c = jnp.where(causal[None], w_s, jnp.zeros_like(w_s))
    s = jnp.einsum('gts,bcsgd->bctgd', w_c, v) + b_s.T[:, :, None]
    return (u * s.reshape(bsz, seq, GM_HALF)) @ w_out


def dsa_attention(x, w_in, kv_g, w_uk, w_uv, w_out):
    bsz, seq, _ = x.shape
    k_sel = min(TOPK_MAX, seq // 4)
    n_blk = seq // Q_BLOCK
    proj = x @ w_in
    q, c_kv, q_idx, k_idx, w_idx = jnp.split(proj, SPLITS_B, axis=-1)
    q = q.reshape(bsz, seq, ATT_HEADS, ATT_HEAD_DIM)
    c_kv = _rms_norm(c_kv, kv_g)
    q_idx = q_idx.reshape(bsz, seq, IDX_HEADS, IDX_DIM)
    w_idx = w_idx * (IDX_HEADS * IDX_DIM) ** -0.5
    q_lat = jnp.einsum('blhd,hcd->blhc', q, w_uk) * ATT_HEAD_DIM ** -0.5
    key_pos = jnp.arange(seq)

    def to_blocks(a):
        return a.reshape((bsz, n_blk, Q_BLOCK) + a.shape[2:]).swapaxes(0, 1)

    def block(args):
        ql_b, qi_b, wi_b, t0 = args
        tq = t0 + jnp.arange(Q_BLOCK)
        rel = jax.nn.relu(jnp.einsum('bqhd,bsd->bqhs', qi_b, k_idx))
        score = jnp.einsum('bqhs,bqh->bqs', rel, wi_b).astype(jnp.float32)
        causal = key_pos[None, :] <= tq[:, None]
        score = jnp.where(causal[None], score, -jnp.inf)
        _, idx = lax.top_k(score, k_sel)
        c_sel = jax.vmap(lambda c, i: c[i])(c_kv, idx)
        valid = idx <= tq[None, :, None]
        logits = jnp.einsum('bqhc,bqkc->bqhk', ql_b, c_sel).astype(jnp.float32)
        logits = jnp.where(valid[:, :, None, :], logits, -jnp.inf)
        probs = jax.nn.softmax(logits, axis=-1).astype(c_sel.dtype)
        return jnp.einsum('bqhk,bqkc->bqhc', probs, c_sel)

    o = lax.map(block, (to_blocks(q_lat), to_blocks(q_idx), to_blocks(w_idx),
                        jnp.arange(n_blk) * Q_BLOCK))
    o = o.swapaxes(0, 1).reshape(bsz, seq, ATT_HEADS, KV_LATENT)
    o = jnp.einsum('blhc,hcd->blhd', o, w_uv).reshape(bsz, seq, Q_WIDTH)
    return o @ w_out


def hier_moe(x, wg, bg, we, be, w_in, w_out):
    bsz, seq, dm = x.shape
    xt = x.reshape(bsz * seq, dm)
    g_logit = (xt @ wg + bg).astype(jnp.float32)
    g_prob = jax.nn.softmax(g_logit, axis=-1)
    g_sel = jnp.argmax(g_logit, axis=-1)
    p_g = jnp.take_along_axis(g_prob, g_sel[:, None], axis=1)[:, 0]
    e_logit = (xt @ we + be).astype(jnp.float32).reshape(-1, N_EGROUPS, EXPERTS_PER_GROUP)
    e_logit = jnp.take_along_axis(e_logit, g_sel[:, None, None], axis=1)[:, 0]
    top_p, top_i = lax.top_k(jax.nn.softmax(e_logit, axis=-1), TOPK_E)
    top_p = top_p / jnp.sum(top_p, axis=-1, keepdims=True) * p_g[:, None]
    in_group = jnp.sum(jax.nn.one_hot(top_i, EXPERTS_PER_GROUP) * top_p[..., None], axis=1)
    gate = (jax.nn.one_hot(g_sel, N_EGROUPS)[:, :, None] * in_group[:, None, :]).astype(x.dtype)
    w_in_g = w_in.reshape(N_EGROUPS, EXPERTS_PER_GROUP, dm, 2 * EXPERT_FF)
    w_out_g = w_out.reshape(N_EGROUPS, EXPERTS_PER_GROUP, EXPERT_FF, dm)
    out = jnp.zeros_like(xt)
    for g in range(N_EGROUPS):
        h = jnp.einsum('td,edf->tef', xt, w_in_g[g])
        a, b = jnp.split(h, 2, axis=-1)
        h = jax.nn.silu(a) * b * gate[:, g, :, None]
        out = out + jnp.einsum('tef,efd->td', h, w_out_g[g])
    return out.reshape(bsz, seq, dm)


def setup_inputs(seed: int = 0) -> dict:
    key = jax.random.key(seed)
    ks = iter(jax.random.split(key, 40))

    def nrm(shape, scale):
        return jax.random.normal(next(ks), shape, jnp.float32) * scale

    def gain(shape):
        return 1.0 + nrm(shape, 0.05)

    d = D_MODEL
    return {
        'x': nrm((BATCH, SEQ, d), 1.0),
        'p': nrm((DEPTH, BATCH, SEQ, PLE_DIM), 1.0),
        'a_w_in': nrm((N_A, d, 2 * GM_HALF), d ** -0.5),
        'a_b_in': nrm((N_A, 2 * GM_HALF), 0.02),
        'a_vn_g': gain((N_A, GM_HALF)),
        'a_vn_b': nrm((N_A, GM_HALF), 0.02),
        'a_w_s': nrm((N_A, GM_GROUPS, CHUNK, CHUNK), CHUNK ** -0.5),
        'a_b_s': 1.0 + nrm((N_A, GM_GROUPS, CHUNK), 0.1),
        'a_w_out': nrm((N_A, GM_HALF, d), BETA * GM_HALF ** -0.5),
        'b_w_in': nrm((N_B, d, PROJ_B), d ** -0.5),
        'b_kv_g': gain((N_B, KV_LATENT)),
        'b_w_uk': nrm((N_B, ATT_HEADS, KV_LATENT, ATT_HEAD_DIM), KV_LATENT ** -0.5),
        'b_w_uv': nrm((N_B, ATT_HEADS, KV_LATENT, ATT_HEAD_DIM), KV_LATENT ** -0.5),
        'b_w_out': nrm((N_B, Q_WIDTH, d), BETA * Q_WIDTH ** -0.5),
        'ln1_g': gain((DEPTH, d)),
        'ln1_b': nrm((DEPTH, d), 0.02),
        'ln2_g': gain((DEPTH, d)),
        'ln2_b': nrm((DEPTH, d), 0.02),
        'r_wg': nrm((DEPTH, d, N_EGROUPS), d ** -0.5),
        'r_bg': nrm((DEPTH, N_EGROUPS), 0.01),
        'r_we': nrm((DEPTH, d, N_EXPERTS), d ** -0.5),
        'r_be': nrm((DEPTH, N_EXPERTS), 0.01),
        'e_w_in': nrm((DEPTH, N_EXPERTS, d, 2 * EXPERT_FF), d ** -0.5),
        'e_w_out': nrm((DEPTH, N_EXPERTS, EXPERT_FF, d), BETA * EXPERT_FF ** -0.5),
        'ple_w': nrm((DEPTH, PLE_DIM, d), PLE_DIM ** -0.5),
        'ple_gw': nrm((DEPTH, d, d), d ** -0.5),
        'ple_gb': nrm((DEPTH, d), 0.02),
    }


def reference(x, p, a_w_in, a_b_in, a_vn_g, a_vn_b, a_w_s, a_b_s, a_w_out,
              b_w_in, b_kv_g, b_w_uk, b_w_uv, b_w_out,
              ln1_g, ln1_b, ln2_g, ln2_b,
              r_wg, r_bg, r_we, r_be, e_w_in, e_w_out,
              ple_w, ple_gw, ple_gb):
    for i in range(DEPTH):
        j = i // N_MIXERS
        if i % N_MIXERS == 0:
            mix = chunked_gmlp(x, a_w_in[j], a_b_in[j], a_vn_g[j], a_vn_b[j],
                               a_w_s[j], a_b_s[j], a_w_out[j])
        else:
            mix = dsa_attention(x, b_w_in[j], b_kv_g[j], b_w_uk[j], b_w_uv[j], b_w_out[j])
        x = _layer_norm(ALPHA * x + mix, ln1_g[i], ln1_b[i])
        ffn = hier_moe(x, r_wg[i], r_bg[i], r_we[i], r_be[i], e_w_in[i], e_w_out[i])
        x = _layer_norm(ALPHA * x + ffn, ln2_g[i], ln2_b[i])
        x = x + (p[i] @ ple_w[i]) * jax.nn.sigmoid(x @ ple_gw[i] + ple_gb[i])
    return x
```

```python
import functools

import jax
import jax.numpy as jnp
from jax import lax
from jax.experimental import pallas as pl
from jax.experimental.pallas import tpu as pltpu

F32, BF16, I32 = jnp.float32, jnp.bfloat16, jnp.int32

IDX_HEADS = 8
IDX_DIM = 64
TOPK_MAX = 256
N_EGROUPS = 4
LN_EPS = 1e-5
RMS_EPS = 1e-6

LANES = 128
Q_BLOCK = 128
KEY_CHUNK = 512
VMEM_LIMIT = 48 * 1024 * 1024

ROUTE_COLS = LANES
AUX = LANES
INT_MIN = -(2 ** 31)
NEG_BIG = -1e30


def _cparams(sem, vmem=VMEM_LIMIT):
    return pltpu.CompilerParams(dimension_semantics=sem, vmem_limit_bytes=vmem)


def _full(shape):
    n = len(shape)
    return pl.BlockSpec(shape, lambda *_: (0,) * n)


def _dot(a, b):
    return jnp.dot(a, b, preferred_element_type=F32)


def _dot_t(a, b):
    return lax.dot_general(a, b, (((1,), (1,)), ((), ())), preferred_element_type=F32)


def _layer_norm(x, g, b):
    mu = jnp.mean(x, axis=-1, keepdims=True)
    xc = x - mu
    var = jnp.mean(xc * xc, axis=-1, keepdims=True)
    return xc * lax.rsqrt(var + LN_EPS) * g + b


def _route_rows(lg_t):
    g = [lg_t[i:i + 1] for i in range(N_EGROUPS)]
    gmax = functools.reduce(jnp.maximum, g)
    gsel = jnp.where(g[0] >= gmax, 0.0, jnp.where(g[1] >= gmax, 1.0, jnp.where(g[2] >= gmax, 2.0, 3.0)))
    den = functools.reduce(lambda a, b: a + b, [jnp.exp(gi - gmax) for gi in g])
    p_g = 1.0 / den
    el = lg_t[8:16]
    for gi in range(1, N_EGROUPS):
        el = jnp.where(gsel == float(gi), lg_t[8 + 8 * gi:16 + 8 * gi], el)
    eidx = lax.broadcasted_iota(I32, el.shape, 0).astype(F32)
    m1 = jnp.max(el, axis=0, keepdims=True)
    i1 = jnp.min(jnp.where(el == m1, eidx, 8.0), axis=0, keepdims=True)
    el2 = jnp.where(eidx == i1, -jnp.inf, el)
    m2 = jnp.max(el2, axis=0, keepdims=True)
    i2 = jnp.min(jnp.where(el2 == m2, eidx, 8.0), axis=0, keepdims=True)
    r = jnp.exp(m2 - m1)
    inv = 1.0 / (1.0 + r)
    gate1 = p_g * inv
    gate2 = p_g * r * inv
    first_lo = i1 < i2
    lo = jnp.minimum(i1, i2)
    hi = jnp.maximum(i1, i2)
    cls = gsel * 64.0 + lo * 8.0 + hi
    return cls, jnp.where(first_lo, gate1, gate2), jnp.where(first_lo, gate2, gate1)


def _norm_and_route(y, ln_g, ln_b, wr_hi, wr_lo, br, out_ref):
    tm, d = y.shape
    x1 = _layer_norm(y, ln_g, ln_b)
    out_ref[:, :d] = x1
    x_hi = x1.astype(BF16)
    x_lo = (x1 - x_hi.astype(F32)).astype(BF16)
    lg = _dot(x_hi, wr_hi) + _dot(x_lo, wr_hi) + _dot(x_hi, wr_lo) + br
    cls, g_lo, g_hi = _route_rows(lg.T)
    row = lax.broadcasted_iota(I32, (8, tm), 0)
    packed = jnp.where(row == 0, cls, jnp.where(row == 1, g_lo, jnp.where(row == 2, g_hi, 0.0)))
    packed = jnp.concatenate([packed, jnp.zeros((AUX - 8, tm), F32)], axis=0)
    out_ref[:, d:] = packed.T


def _gmlp_kernel(x_ref, w_in_ref, b_in_ref, vn_g_ref, vn_b_ref, w_s_ref, b_st_ref, w_out_ref,
                 ln_g_ref, ln_b_ref, wr_hi_ref, wr_lo_ref, br_ref, out_ref, gated_ref, *, alpha):
    x = x_ref[...]
    tm, _ = x.shape
    groups, chunk, _ = w_s_ref.shape
    z = _dot(x.astype(BF16), w_in_ref[...]) + b_in_ref[...]
    z = 0.5 * z * (1.0 + lax.erf(z * (2.0 ** -0.5)))
    half = z.shape[1] // 2
    gd = half // groups
    u = z[:, :half]
    v = _layer_norm(z[:, half:], vn_g_ref[...], vn_b_ref[...]).astype(BF16)
    r = lax.broadcasted_iota(I32, (chunk, chunk), 0)
    c = lax.broadcasted_iota(I32, (chunk, chunk), 1)
    causal = r >= c
    for g in range(groups):
        w_c = jnp.where(causal, w_s_ref[g], 0.0).astype(BF16)
        bias = b_st_ref[:, g:g + 1]
        for ci in range(tm // chunk):
            rows = slice(ci * chunk, (ci + 1) * chunk)
            cols = slice(g * gd, (g + 1) * gd)
            s = _dot(w_c, v[rows, cols]) + bias
            gated_ref[rows, cols] = (u[rows, cols] * s).astype(BF16)
    mix = _dot(gated_ref[...], w_out_ref[...])
    _norm_and_route(alpha * x + mix, ln_g_ref[...], ln_b_ref[...], wr_hi_ref[...], wr_lo_ref[...],
                    br_ref[...], out_ref)


def _gmlp_layer(xt, w_in, b_in, vn_g, vn_b, w_s, b_s, w_out, ln_g, ln_b, router, alpha, tm):
    t, d = xt.shape
    half = w_out.shape[0]
    wr_hi, wr_lo, br = router
    args = (xt, w_in.astype(BF16), b_in[None], vn_g[None], vn_b[None], w_s, b_s.T, w_out.astype(BF16),
            ln_g[None], ln_b[None], wr_hi, wr_lo, br)
    in_specs = [pl.BlockSpec((tm, d), lambda i: (i, 0))] + [_full(a.shape) for a in args[1:]]
    return pl.pallas_call(
        functools.partial(_gmlp_kernel, alpha=alpha),
        grid=(t // tm,),
        in_specs=in_specs,
        out_specs=pl.BlockSpec((tm, d + AUX), lambda i: (i, 0)),
        out_shape=jax.ShapeDtypeStruct((t, d + AUX), F32),
        scratch_shapes=[pltpu.VMEM((tm, half), BF16)],
        compiler_params=_cparams(("parallel",)),
        name="gmlp",
    )(*args)


def _dsa_proj_kernel(x_ref, wq_ref, wc_ref, wqi_ref, wkw_ref, kvg_ref, wuk_t_ref,
                     qlat_ref, ckv_ref, qidx_ref, kidx_ref, wi_ref, *, scale_q, scale_w):
    xb = x_ref[...].astype(BF16)
    nq = qlat_ref.shape[0]
    heads, hd, _ = wuk_t_ref.shape
    q = _dot(xb, wq_ref[...])
    for h in range(heads):
        ql = (_dot(q[:, h * hd:(h + 1) * hd].astype(BF16), wuk_t_ref[h]) * scale_q).astype(BF16)
        for b in range(nq):
            qlat_ref[b, h * Q_BLOCK:(h + 1) * Q_BLOCK, :] = ql[b * Q_BLOCK:(b + 1) * Q_BLOCK]
    c = _dot(xb, wc_ref[...])
    ms = jnp.mean(c * c, axis=-1, keepdims=True)
    ckv_ref[...] = (c * lax.rsqrt(ms + RMS_EPS) * kvg_ref[...]).astype(BF16)
    qi = _dot(xb, wqi_ref[...]).astype(BF16)
    for h in range(IDX_HEADS):
        for b in range(nq):
            qidx_ref[b, h * Q_BLOCK:(h + 1) * Q_BLOCK, :] = qi[b * Q_BLOCK:(b + 1) * Q_BLOCK,
                                                               h * IDX_DIM:(h + 1) * IDX_DIM]
    kw = _dot(xb, wkw_ref[...])
    kidx_ref[...] = kw[:, :IDX_DIM].astype(BF16)
    wi_ref[...] = kw[:, IDX_DIM:IDX_DIM + IDX_HEADS] * scale_w


def _dsa_attn_kernel(qidx_ref, wi_ref, qlat_ref, kidx_ref, ckv_ref, o_ref,
                     key_ref, acc_ref, m_ref, l_ref, a_ref, p_ref, *, k_sel, heads):
    qb = pl.program_id(1)
    kc = key_ref.shape[2]
    nq = Q_BLOCK
    n_chunks = (qb * nq + nq + kc - 1) // kc
    sub = kc // LANES

    q_pos = qb * nq + lax.broadcasted_iota(I32, (nq, kc), 0)
    k_off = lax.broadcasted_iota(I32, (nq, kc), 1)

    def score_chunk(j, carry):
        kch = kidx_ref[pl.ds(pl.multiple_of(j * kc, kc), kc), :]
        sc = _dot_t(qidx_ref[...], kch)
        tot = None
        for h in range(IDX_HEADS):
            r = jnp.maximum(sc[h * nq:(h + 1) * nq], 0.0) * wi_ref[:, h:h + 1]
            tot = r if tot is None else tot + r
        tot = jnp.where(tot == 0.0, 0.0, tot)
        bits = lax.bitcast_convert_type(tot, I32)
        key = bits ^ ((bits >> 31) & 0x7FFFFFFF)
        key_ref[j] = jnp.where(k_off + j * kc <= q_pos, key, INT_MIN)
        return carry

    lax.fori_loop(0, n_chunks, score_chunk, 0)

    def count_ge(t):
        tb = jnp.broadcast_to(t, (nq, kc))

        def body(j, acc):
            hit = jnp.where(key_ref[j] >= tb, 1.0, 0.0)
            for s in range(sub):
                acc = acc + hit[:, s * LANES:(s + 1) * LANES]
            return acc

        acc = lax.fori_loop(0, n_chunks, body, jnp.zeros((nq, LANES), F32))
        return jnp.sum(acc, axis=1, keepdims=True)

    def bit_step(b, c):
        t = c + jnp.left_shift(jnp.int32(1), 31 - b)
        return jnp.where(count_ge(t) >= float(k_sel), t, c)

    c_sel = lax.fori_loop(0, 32, bit_step, jnp.full((nq, 1), INT_MIN, I32))

    n_ge = count_ge(c_sel)
    has_tie = jnp.max(n_ge) > float(k_sel)

    @pl.when(has_tie)
    def _():
        need = float(k_sel) - count_ge(c_sel + 1)
        tie_row = n_ge > float(k_sel)
        cb = jnp.broadcast_to(c_sel, (nq, kc))
        upper = (lax.broadcasted_iota(I32, (kc, kc), 0) < lax.broadcasted_iota(I32, (kc, kc), 1))
        upper = jnp.where(upper, 1.0, 0.0).astype(BF16)

        def fix(j, seen):
            blk = key_ref[j]
            eq = blk == cb
            eq_f = jnp.where(eq, 1.0, 0.0)
            rank = seen + _dot(eq_f.astype(BF16), upper)
            drop = eq & tie_row & (rank >= need)
            key_ref[j] = jnp.where(drop, INT_MIN, blk)
            return seen + jnp.sum(eq_f, axis=1, keepdims=True)

        lax.fori_loop(0, n_chunks, fix, jnp.zeros((nq, 1), F32))

    thr = jnp.broadcast_to(jnp.maximum(c_sel, INT_MIN + 1), (nq, kc))
    m_ref[...] = jnp.full(m_ref.shape, NEG_BIG, F32)
    l_ref[...] = jnp.zeros(l_ref.shape, F32)
    acc_ref[...] = jnp.zeros(acc_ref.shape, F32)

    def att_chunk(j, carry):
        ck = ckv_ref[pl.ds(pl.multiple_of(j * kc, kc), kc), :]
        lg = _dot_t(qlat_ref[...], ck)
        sel = key_ref[j] >= thr
        for h in range(heads):
            rows = slice(h * nq, (h + 1) * nq)
            lh = jnp.where(sel, lg[rows], NEG_BIG)
            m_old = m_ref[rows]
            m_new = jnp.maximum(m_old, jnp.max(lh, axis=1, keepdims=True))
            p = jnp.exp(lh - m_new)
            alpha = jnp.exp(m_old - m_new)
            l_ref[rows] = alpha * l_ref[rows] + jnp.sum(p, axis=1, keepdims=True)
            m_ref[rows] = m_new
            a_ref[rows] = alpha
            p_ref[rows] = p.astype(BF16)
        acc_ref[...] = a_ref[...] * acc_ref[...] + _dot(p_ref[...], ck)
        return carry

    lax.fori_loop(0, n_chunks, att_chunk, 0)
    o_ref[...] = (acc_ref[...] * (1.0 / l_ref[...])).astype(o_ref.dtype)


def _dsa_out_kernel(o_ref, x_ref, wuv_ref, wout_ref, ln_g_ref, ln_b_ref, wr_hi_ref, wr_lo_ref, br_ref,
                    out_ref, o2_ref, *, alpha):
    nq = o_ref.shape[0]
    heads, _, hd = wuv_ref.shape
    for b in range(nq):
        for h in range(heads):
            oh = o_ref[b, h * Q_BLOCK:(h + 1) * Q_BLOCK, :]
            o2_ref[b * Q_BLOCK:(b + 1) * Q_BLOCK, h * hd:(h + 1) * hd] = _dot(oh, wuv_ref[h]).astype(BF16)
    mix = _dot(o2_ref[...], wout_ref[...])
    _norm_and_route(alpha * x_ref[...] + mix, ln_g_ref[...], ln_b_ref[...], wr_hi_ref[...], wr_lo_ref[...],
                    br_ref[...], out_ref)


def _dsa_layer(xt, batch, w_in, kv_g, w_uk, w_uv, w_out, ln_g, ln_b, router, alpha, tm):
    t, d = xt.shape
    seq = t // batch
    heads, lat, hd = w_uk.shape
    qw = heads * hd
    nqb = t // Q_BLOCK
    nq = tm // Q_BLOCK
    k_sel = min(TOPK_MAX, seq // 4)
    kc = min(KEY_CHUNK, seq)
    w_in = w_in.astype(BF16)
    o1, o2, o3 = qw, qw + lat, qw + lat + IDX_HEADS * IDX_DIM
    wkw = jnp.pad(w_in[:, o3:], ((0, 0), (0, LANES - (w_in.shape[1] - o3))))
    proj_args = (xt, w_in[:, :o1], w_in[:, o1:o2], w_in[:, o2:o3], wkw, kv_g[None],
                 jnp.swapaxes(w_uk, 1, 2).astype(BF16))
    qlat, ckv, qidx, kidx, wi = pl.pallas_call(
        functools.partial(_dsa_proj_kernel, scale_q=hd ** -0.5, scale_w=(IDX_HEADS * IDX_DIM) ** -0.5),
        grid=(t // tm,),
        in_specs=[pl.BlockSpec((tm, d), lambda i: (i, 0))] + [_full(a.shape) for a in proj_args[1:]],
        out_specs=[pl.BlockSpec((nq, heads * Q_BLOCK, lat), lambda i: (i, 0, 0)),
                   pl.BlockSpec((tm, lat), lambda i: (i, 0)),
                   pl.BlockSpec((nq, IDX_HEADS * Q_BLOCK, IDX_DIM), lambda i: (i, 0, 0)),
                   pl.BlockSpec((tm, IDX_DIM), lambda i: (i, 0)),
                   pl.BlockSpec((tm, IDX_HEADS), lambda i: (i, 0))],
        out_shape=[jax.ShapeDtypeStruct((nqb, heads * Q_BLOCK, lat), BF16),
                   jax.ShapeDtypeStruct((t, lat), BF16),
                   jax.ShapeDtypeStruct((nqb, IDX_HEADS * Q_BLOCK, IDX_DIM), BF16),
                   jax.ShapeDtypeStruct((t, IDX_DIM), BF16),
                   jax.ShapeDtypeStruct((t, IDX_HEADS), F32)],
        compiler_params=_cparams(("parallel",)),
        name="dsa_proj",
    )(*proj_args)

    nq_seq = seq // Q_BLOCK
    rows = heads * Q_BLOCK
    o = pl.pallas_call(
        functools.partial(_dsa_attn_kernel, k_sel=k_sel, heads=heads),
        grid=(batch, nq_seq),
        in_specs=[pl.BlockSpec((None, IDX_HEADS * Q_BLOCK, IDX_DIM), lambda b, q: (b * nq_seq + q, 0, 0)),
                  pl.BlockSpec((Q_BLOCK, IDX_HEADS), lambda b, q: (b * nq_seq + q, 0)),
                  pl.BlockSpec((None, rows, lat), lambda b, q: (b * nq_seq + q, 0, 0)),
                  pl.BlockSpec((None, seq, IDX_DIM), lambda b, q: (b, 0, 0)),
                  pl.BlockSpec((None, seq, lat), lambda b, q: (b, 0, 0))],
        out_specs=pl.BlockSpec((None, rows, lat), lambda b, q: (b * nq_seq + q, 0, 0)),
        out_shape=jax.ShapeDtypeStruct((nqb, rows, lat), BF16),
        scratch_shapes=[pltpu.VMEM((seq // kc, Q_BLOCK, kc), I32),
                        pltpu.VMEM((rows, lat), F32),
                        pltpu.VMEM((rows, 1), F32),
                        pltpu.VMEM((rows, 1), F32),
                        pltpu.VMEM((rows, 1), F32),
                        pltpu.VMEM((rows, kc), BF16)],
        compiler_params=_cparams(("parallel", "arbitrary")),
        name="dsa_attn",
    )(qidx, wi, qlat, kidx.reshape(batch, seq, IDX_DIM), ckv.reshape(batch, seq, lat))

    wr_hi, wr_lo, br = router
    out_args = (o, xt, w_uv.astype(BF16), w_out.astype(BF16), ln_g[None], ln_b[None], wr_hi, wr_lo, br)
    return pl.pallas_call(
        functools.partial(_dsa_out_kernel, alpha=alpha),
        grid=(t // tm,),
        in_specs=[pl.BlockSpec((nq, rows, lat), lambda i: (i, 0, 0)),
                  pl.BlockSpec((tm, d), lambda i: (i, 0))] + [_full(a.shape) for a in out_args[2:]],
        out_specs=pl.BlockSpec((tm, d + AUX), lambda i: (i, 0)),
        out_shape=jax.ShapeDtypeStruct((t, d + AUX), F32),
        scratch_shapes=[pltpu.VMEM((tm, qw), BF16)],
        compiler_params=_cparams(("parallel",)),
        name="dsa_out",
    )(*out_args)


def _row_gather_step(i, n, idx_hbm, src_hbm, idx_smem, buf, isem, rsem):
    tm = buf.shape[1]
    slot = i % 2
    nslot = 1 - slot

    def idx_copy(t, s):
        return pltpu.make_async_copy(idx_hbm.at[t], idx_smem.at[s], isem.at[s])

    def start_rows(s):
        def body(r, carry):
            tok = idx_smem[s, r]
            pltpu.make_async_copy(src_hbm.at[pl.ds(tok, 1)], buf.at[s, pl.ds(r, 1)], rsem.at[s]).start()
            return carry
        lax.fori_loop(0, tm, body, 0, unroll=8)

    def wait_rows(s):
        def body(r, carry):
            pltpu.make_async_copy(src_hbm.at[pl.ds(0, 1)], buf.at[s, pl.ds(r, 1)], rsem.at[s]).wait()
            return carry
        lax.fori_loop(0, tm, body, 0, unroll=8)

    @pl.when(i == 0)
    def _():
        first = idx_copy(0, 0)
        first.start()
        first.wait()
        start_rows(0)

        @pl.when(n > 1)
        def _():
            idx_copy(1, 1).start()

    @pl.when(i + 1 < n)
    def _():
        idx_copy(i + 1, nslot).wait()
        start_rows(nslot)

    @pl.when(i + 2 < n)
    def _():
        idx_copy(i + 2, slot).start()

    @pl.when(i < n)
    def _():
        wait_rows(slot)


_GATHER_SCRATCH = lambda tm, width: [pltpu.SMEM((2, tm), I32), pltpu.VMEM((2, tm, width), F32),
                                     pltpu.SemaphoreType.DMA((2,)), pltpu.SemaphoreType.DMA((2,))]


def _moe_kernel(tcls_ref, nused_ref, src_ref, xa_ref, win_ref, wout_ref, ys_ref,
                idx_smem, xbuf, isem, rsem):
    i = pl.program_id(0)
    n = nused_ref[0]
    d = ys_ref.shape[1]
    ff = wout_ref.shape[1]
    _row_gather_step(i, n, src_ref, xa_ref, idx_smem, xbuf, isem, rsem)

    @pl.when(i < n)
    def _():
        cls = tcls_ref[i]
        xs = xbuf[i % 2]
        xb = xs[:, :d].astype(BF16)
        y = None
        for e, gate in (((cls >> 3) & 7, xs[:, d + 1:d + 2]), (cls & 7, xs[:, d + 2:d + 3])):
            h = _dot(xb, win_ref[e])
            a = h[:, :ff]
            act = (a * jax.nn.sigmoid(a) * h[:, ff:] * gate).astype(BF16)
            ye = _dot(act, wout_ref[e])
            y = ye if y is None else y + ye
        ys_ref[...] = y

    @pl.when(i >= n)
    def _():
        ys_ref[...] = jnp.zeros(ys_ref.shape, F32)


def _post_kernel(pos_ref, ys_ref, xa_ref, p_ref, ln_g_ref, ln_b_ref, pw_ref, gw_ref, gb_ref, out_ref,
                 idx_smem, ybuf, isem, rsem, *, alpha, n_tiles):
    i = pl.program_id(0)
    _row_gather_step(i, n_tiles, pos_ref, ys_ref, idx_smem, ybuf, isem, rsem)
    x2 = _layer_norm(alpha * xa_ref[...] + ybuf[i % 2], ln_g_ref[...], ln_b_ref[...])
    gate = jax.nn.sigmoid(_dot(x2.astype(BF16), gw_ref[...]) + gb_ref[...])
    out_ref[...] = x2 + _dot(p_ref[...].astype(BF16), pw_ref[...]) * gate


def _moe_and_post(x1a, p_i, e_w_in, e_w_out, ln_g, ln_b, ple_w, ple_gw, ple_gb, alpha, tm_moe, tm_post):
    t, da = x1a.shape
    d = da - AUX
    n_exp, _, ff2 = e_w_in.shape
    epg = n_exp // N_EGROUPS
    n_cls = N_EGROUPS * 64

    cls = x1a[:, d].astype(I32)
    order = jnp.argsort(cls).astype(I32)
    scls = cls[order]
    starts = jnp.searchsorted(scls, jnp.arange(n_cls, dtype=I32), side="left").astype(I32)
    counts = jnp.diff(jnp.concatenate([starts, jnp.array([t], I32)]))
    tiles_per = (counts + tm_moe - 1) // tm_moe
    tile_end = jnp.cumsum(tiles_per)
    tile_off = tile_end - tiles_per
    n_pairs = N_EGROUPS * (epg * (epg - 1) // 2)
    nt = t // tm_moe + n_pairs
    ppos = tile_off[scls] * tm_moe + (jnp.arange(t, dtype=I32) - starts[scls])
    src = jnp.zeros((nt * tm_moe,), I32).at[ppos].set(order).reshape(nt, tm_moe)
    pos = jnp.zeros((t,), I32).at[order].set(ppos).reshape(t // tm_post, tm_post)
    n_used = tile_end[-1:].astype(I32)
    tile_cls = jnp.searchsorted(tile_end, jnp.arange(nt, dtype=I32), side="right").astype(I32)
    tile_cls = jnp.minimum(tile_cls, scls[-1])

    w_in = e_w_in.astype(BF16).reshape(N_EGROUPS, epg, d, ff2)
    w_out = e_w_out.astype(BF16).reshape(N_EGROUPS, epg, ff2 // 2, d)
    ys = pl.pallas_call(
        _moe_kernel,
        grid_spec=pltpu.PrefetchScalarGridSpec(
            num_scalar_prefetch=2,
            grid=(nt,),
            in_specs=[pl.BlockSpec(memory_space=pl.ANY),
                      pl.BlockSpec(memory_space=pl.ANY),
                      pl.BlockSpec((None, epg, d, ff2), lambda i, tc, nu: (tc[i] >> 6, 0, 0, 0)),
                      pl.BlockSpec((None, epg, ff2 // 2, d), lambda i, tc, nu: (tc[i] >> 6, 0, 0, 0))],
            out_specs=pl.BlockSpec((tm_moe, d), lambda i, tc, nu: (i, 0)),
            scratch_shapes=_GATHER_SCRATCH(tm_moe, da)),
        out_shape=jax.ShapeDtypeStruct((nt * tm_moe, d), F32),
        compiler_params=_cparams(("arbitrary",)),
        name="moe",
    )(tile_cls, n_used, src, x1a, w_in, w_out)

    n_tiles = t // tm_post
    post_args = (pos, ys, x1a, p_i, ln_g[None], ln_b[None], ple_w.astype(BF16), ple_gw.astype(BF16), ple_gb[None])
    return pl.pallas_call(
        functools.partial(_post_kernel, alpha=alpha, n_tiles=n_tiles),
        grid=(n_tiles,),
        in_specs=[pl.BlockSpec(memory_space=pl.ANY),
                  pl.BlockSpec(memory_space=pl.ANY),
                  pl.BlockSpec((tm_post, d), lambda i: (i, 0)),
                  pl.BlockSpec((tm_post, p_i.shape[1]), lambda i: (i, 0))] + [_full(a.shape) for a in post_args[4:]],
        out_specs=pl.BlockSpec((tm_post, d), lambda i: (i, 0)),
        out_shape=jax.ShapeDtypeStruct((t, d), F32),
        scratch_shapes=_GATHER_SCRATCH(tm_post, d),
        compiler_params=_cparams(("arbitrary",)),
        name="post",
    )(*post_args)


def _router_params(wg, bg, we, be):
    d = wg.shape[0]
    w = jnp.zeros((d, ROUTE_COLS), F32).at[:, :N_EGROUPS].set(wg).at[:, 8:8 + we.shape[1]].set(we)
    b = jnp.zeros((1, ROUTE_COLS), F32).at[0, :N_EGROUPS].set(bg).at[0, 8:8 + be.shape[0]].set(be)
    hi = w.astype(BF16)
    return hi, (w - hi.astype(F32)).astype(BF16), b


def _tile(t, want):
    while t % want:
        want //= 2
    return want


def kernel(x, p, a_w_in, a_b_in, a_vn_g, a_vn_b, a_w_s, a_b_s, a_w_out, b_w_in, b_kv_g, b_w_uk, b_w_uv, b_w_out,
           ln1_g, ln1_b, ln2_g, ln2_b, r_wg, r_bg, r_we, r_be, e_w_in, e_w_out, ple_w, ple_gw, ple_gb):
    batch, seq, d = x.shape
    t = batch * seq
    depth = p.shape[0]
    alpha = (2 * depth) ** 0.25
    chunk = a_w_s.shape[-1]
    xt = x.reshape(t, d)
    pt = p.reshape(depth, t, p.shape[-1])
    for i in range(depth):
        j = i // 2
        router = _router_params(r_wg[i], r_bg[i], r_we[i], r_be[i])
        if i % 2 == 0:
            x1a = _gmlp_layer(xt, a_w_in[j], a_b_in[j], a_vn_g[j], a_vn_b[j], a_w_s[j], a_b_s[j], a_w_out[j],
                              ln1_g[i], ln1_b[i], router, alpha, max(chunk, _tile(t, 512)))
        else:
            x1a = _dsa_layer(xt, batch, b_w_in[j], b_kv_g[j], b_w_uk[j], b_w_uv[j], b_w_out[j],
                             ln1_g[i], ln1_b[i], router, alpha, _tile(t, 256))
        xt = _moe_and_post(x1a, pt[i], e_w_in[i], e_w_out[i], ln2_g[i], ln2_b[i],
                           ple_w[i], ple_gw[i], ple_gb[i], alpha, 128, _tile(t, 256))
    return xt.reshape(batch, seq, d)
```

```python
import functools

import jax
import jax.numpy as jnp
from jax import lax
from jax.experimental import pallas as pl
from jax.experimental.pallas import tpu as pltpu

F32, BF16, I32 = jnp.float32, jnp.bfloat16, jnp.int32

IDX_HEADS = 8
IDX_DIM = 64
TOPK_MAX = 256
N_EGROUPS = 4
LN_EPS = 1e-5
RMS_EPS = 1e-6

LANES = 128
Q_BLOCK = 128
KEY_CHUNK = 512
COUNT_SLAB = 1024
COUNT_ACC = 64
LOG2E = 1.4426950408889634
VMEM_LIMIT = 48 * 1024 * 1024

ROUTE_COLS = LANES
AUX = LANES
INT_MIN = -(2 ** 31)
NEG_BIG = -1e30


def _cparams(sem, vmem=VMEM_LIMIT):
    return pltpu.CompilerParams(dimension_semantics=sem, vmem_limit_bytes=vmem)


def _full(shape):
    n = len(shape)
    return pl.BlockSpec(shape, lambda *_: (0,) * n)


def _dot(a, b):
    return jnp.dot(a, b, preferred_element_type=F32)


def _dot_t(a, b):
    return lax.dot_general(a, b, (((1,), (1,)), ((), ())), preferred_element_type=F32)


def _layer_norm(x, g, b):
    mu = jnp.mean(x, axis=-1, keepdims=True)
    xc = x - mu
    var = jnp.mean(xc * xc, axis=-1, keepdims=True)
    return xc * lax.rsqrt(var + LN_EPS) * g + b


def _route_rows(lg_t):
    g = [lg_t[i:i + 1] for i in range(N_EGROUPS)]
    gmax = functools.reduce(jnp.maximum, g)
    gsel = jnp.where(g[0] >= gmax, 0.0, jnp.where(g[1] >= gmax, 1.0, jnp.where(g[2] >= gmax, 2.0, 3.0)))
    den = functools.reduce(lambda a, b: a + b, [jnp.exp(gi - gmax) for gi in g])
    p_g = 1.0 / den
    el = lg_t[8:16]
    for gi in range(1, N_EGROUPS):
        el = jnp.where(gsel == float(gi), lg_t[8 + 8 * gi:16 + 8 * gi], el)
    eidx = lax.broadcasted_iota(I32, el.shape, 0).astype(F32)
    m1 = jnp.max(el, axis=0, keepdims=True)
    i1 = jnp.min(jnp.where(el == m1, eidx, 8.0), axis=0, keepdims=True)
    el2 = jnp.where(eidx == i1, -jnp.inf, el)
    m2 = jnp.max(el2, axis=0, keepdims=True)
    i2 = jnp.min(jnp.where(el2 == m2, eidx, 8.0), axis=0, keepdims=True)
    r = jnp.exp(m2 - m1)
    inv = 1.0 / (1.0 + r)
    gate1 = p_g * inv
    gate2 = p_g * r * inv
    first_lo = i1 < i2
    lo = jnp.minimum(i1, i2)
    hi = jnp.maximum(i1, i2)
    cls = gsel * 64.0 + lo * 8.0 + hi
    return cls, jnp.where(first_lo, gate1, gate2), jnp.where(first_lo, gate2, gate1)


def _norm_and_route(y, ln_g, ln_b, wr_hi, wr_lo, br, out_ref):
    tm, d = y.shape
    x1 = _layer_norm(y, ln_g, ln_b)
    out_ref[:, :d] = x1
    x_hi = x1.astype(BF16)
    x_lo = (x1 - x_hi.astype(F32)).astype(BF16)
    lg = _dot(x_hi, wr_hi) + _dot(x_lo, wr_hi) + _dot(x_hi, wr_lo) + br
    cls, g_lo, g_hi = _route_rows(lg.T)
    row = lax.broadcasted_iota(I32, (8, tm), 0)
    packed = jnp.where(row == 0, cls, jnp.where(row == 1, g_lo, jnp.where(row == 2, g_hi, 0.0)))
    packed = jnp.concatenate([packed, jnp.zeros((AUX - 8, tm), F32)], axis=0)
    out_ref[:, d:] = packed.T


def _gmlp_kernel(x_ref, w_in_ref, b_in_ref, vn_g_ref, vn_b_ref, w_s_ref, b_st_ref, w_out_ref,
                 ln_g_ref, ln_b_ref, wr_hi_ref, wr_lo_ref, br_ref, out_ref, gated_ref, *, alpha):
    x = x_ref[...]
    tm, _ = x.shape
    groups, chunk, _ = w_s_ref.shape
    z = _dot(x.astype(BF16), w_in_ref[...]) + b_in_ref[...]
    z = 0.5 * z * (1.0 + lax.erf(z * (2.0 ** -0.5)))
    half = z.shape[1] // 2
    gd = half // groups
    u = z[:, :half]
    v = _layer_norm(z[:, half:], vn_g_ref[...], vn_b_ref[...]).astype(BF16)
    r = lax.broadcasted_iota(I32, (chunk, chunk), 0)
    c = lax.broadcasted_iota(I32, (chunk, chunk), 1)
    causal = r >= c
    for g in range(groups):
        w_c = jnp.where(causal, w_s_ref[g], 0.0).astype(BF16)
        bias = b_st_ref[:, g:g + 1]
        for ci in range(tm // chunk):
            rows = slice(ci * chunk, (ci + 1) * chunk)
            cols = slice(g * gd, (g + 1) * gd)
            s = _dot(w_c, v[rows, cols]) + bias
            gated_ref[rows, cols] = (u[rows, cols] * s).astype(BF16)
    mix = _dot(gated_ref[...], w_out_ref[...])
    _norm_and_route(alpha * x + mix, ln_g_ref[...], ln_b_ref[...], wr_hi_ref[...], wr_lo_ref[...],
                    br_ref[...], out_ref)


def _gmlp_layer(xt, w_in, b_in, vn_g, vn_b, w_s, b_s, w_out, ln_g, ln_b, router, alpha, tm):
    t, d = xt.shape
    half = w_out.shape[0]
    wr_hi, wr_lo, br = router
    args = (xt, w_in.astype(BF16), b_in[None], vn_g[None], vn_b[None], w_s, b_s.T, w_out.astype(BF16),
            ln_g[None], ln_b[None], wr_hi, wr_lo, br)
    in_specs = [pl.BlockSpec((tm, d), lambda i: (i, 0))] + [_full(a.shape) for a in args[1:]]
    return pl.pallas_call(
        functools.partial(_gmlp_kernel, alpha=alpha),
        grid=(t // tm,),
        in_specs=in_specs,
        out_specs=pl.BlockSpec((tm, d + AUX), lambda i: (i, 0)),
        out_shape=jax.ShapeDtypeStruct((t, d + AUX), F32),
        scratch_shapes=[pltpu.VMEM((tm, half), BF16)],
        compiler_params=_cparams(("parallel",)),
        name="gmlp",
    )(*args)


def _dsa_proj_kernel(x_ref, wq_ref, wc_ref, wqi_ref, wkw_ref, kvg_ref, wuk_t_ref,
                     qlat_ref, ckv_ref, ckv_t_ref, qidx_ref, kidx_ref, wi_t_ref, *, scale_q, scale_w):
    xb = x_ref[...].astype(BF16)
    nq = qlat_ref.shape[0]
    kc = ckv_t_ref.shape[2]
    heads, hd, _ = wuk_t_ref.shape
    q = _dot(xb, wq_ref[...])
    for h in range(heads):
        ql = (_dot(q[:, h * hd:(h + 1) * hd].astype(BF16), wuk_t_ref[h]) * scale_q).astype(BF16)
        for b in range(nq):
            qlat_ref[b, h * Q_BLOCK:(h + 1) * Q_BLOCK, :] = ql[b * Q_BLOCK:(b + 1) * Q_BLOCK]
    c = _dot(xb, wc_ref[...])
    ms = jnp.mean(c * c, axis=-1, keepdims=True)
    c = c * lax.rsqrt(ms + RMS_EPS) * kvg_ref[...]
    ckv_ref[...] = c.astype(BF16)
    for b in range(ckv_t_ref.shape[0]):
        ckv_t_ref[b] = c[b * kc:(b + 1) * kc].T.astype(BF16)
    qi = _dot(xb, wqi_ref[...]).astype(BF16)
    for h in range(IDX_HEADS):
        for b in range(nq):
            qidx_ref[b, h * Q_BLOCK:(h + 1) * Q_BLOCK, :] = qi[b * Q_BLOCK:(b + 1) * Q_BLOCK,
                                                               h * IDX_DIM:(h + 1) * IDX_DIM]
    kw = _dot(xb, wkw_ref[...])
    kidx_ref[...] = kw[:, :IDX_DIM].astype(BF16)
    wi_t_ref[...] = kw.T[IDX_DIM:IDX_DIM + IDX_HEADS] * scale_w


def _dsa_attn_kernel(qidx_ref, wi_t_ref, qlat_ref, kidx_ref, ckv_ref, ckv_t_ref, o_ref,
                     key_ref, acc_ref, m_ref, l_ref, *, k_sel, heads):
    qb = pl.program_id(1)
    kc = ckv_t_ref.shape[2]
    slab = key_ref.shape[1]
    cps = slab // kc
    nq = Q_BLOCK
    pair = 2 * nq
    n_pairs = heads // 2
    n_chunks = (qb * nq + nq + kc - 1) // kc
    n_slabs = (n_chunks + cps - 1) // cps

    def key_chunk(j):
        return key_ref.at[j // cps, pl.ds(pl.multiple_of((j % cps) * kc, kc), kc), :]

    q_pos = qb * nq + lax.broadcasted_iota(I32, (kc, nq), 1)
    k_off = lax.broadcasted_iota(I32, (kc, nq), 0)

    def pad_chunk(j, carry):
        key_chunk(j)[...] = jnp.full((kc, nq), INT_MIN, I32)
        return carry

    lax.fori_loop(n_chunks, n_slabs * cps, pad_chunk, 0)

    def score_chunk(j, carry):
        kch = kidx_ref[pl.ds(pl.multiple_of(j * kc, kc), kc), :]
        tot = None
        for p in range(IDX_HEADS // 2):
            sc = _dot_t(kch, qidx_ref[p * pair:(p + 1) * pair, :])
            for hh in range(2):
                h = 2 * p + hh
                r = jnp.maximum(sc[:, hh * nq:(hh + 1) * nq], 0.0) * wi_t_ref[h:h + 1, :]
                tot = r if tot is None else tot + r
        tot = jnp.where(tot == 0.0, 0.0, tot)
        bits = lax.bitcast_convert_type(tot, I32)
        key = bits ^ ((bits >> 31) & 0x7FFFFFFF)
        key_chunk(j)[...] = jnp.where(k_off + j * kc <= q_pos, key, INT_MIN)
        return carry

    lax.fori_loop(0, n_chunks, score_chunk, 0)

    def count_ge(t):
        tb = jnp.broadcast_to(t, (slab, nq))

        def body(s, acc):
            hit = jnp.where(key_ref[s] >= tb, 1.0, 0.0)
            return acc + jnp.sum(hit.reshape(slab // COUNT_ACC, COUNT_ACC, nq), axis=0)

        acc = lax.fori_loop(0, n_slabs, body, jnp.zeros((COUNT_ACC, nq), F32))
        return jnp.sum(acc, axis=0, keepdims=True)

    def bit_step(b, c):
        t = c + jnp.left_shift(jnp.int32(1), 31 - b)
        return jnp.where(count_ge(t) >= float(k_sel), t, c)

    c_sel = lax.fori_loop(0, 32, bit_step, jnp.full((1, nq), INT_MIN, I32))

    n_ge = count_ge(c_sel)
    has_tie = jnp.max(n_ge) > float(k_sel)

    @pl.when(has_tie)
    def _():
        need = float(k_sel) - count_ge(c_sel + 1)
        tie_col = n_ge > float(k_sel)
        cb = jnp.broadcast_to(c_sel, (kc, nq))
        lower = (lax.broadcasted_iota(I32, (kc, kc), 1) < lax.broadcasted_iota(I32, (kc, kc), 0))
        lower = jnp.where(lower, 1.0, 0.0).astype(BF16)

        def fix(j, seen):
            blk = key_chunk(j)[...]
            eq = blk == cb
            eq_f = jnp.where(eq, 1.0, 0.0)
            rank = seen + _dot(lower, eq_f.astype(BF16))
            drop = eq & tie_col & (rank >= need)
            key_chunk(j)[...] = jnp.where(drop, INT_MIN, blk)
            return seen + jnp.sum(eq_f, axis=0, keepdims=True)

        lax.fori_loop(0, n_chunks, fix, jnp.zeros((1, nq), F32))

    thr = jnp.broadcast_to(jnp.maximum(c_sel, INT_MIN + 1), (kc, nq))
    m_ref[...] = jnp.full(m_ref.shape, NEG_BIG, F32)
    l_ref[...] = jnp.zeros(l_ref.shape, F32)
    acc_ref[...] = jnp.zeros(acc_ref.shape, F32)

    def att_chunk(j, carry):
        ck = ckv_ref[pl.ds(pl.multiple_of(j * kc, kc), kc), :]
        ck_t = ckv_t_ref[j]
        bias = jnp.where(key_chunk(j)[...] >= thr, 0.0, NEG_BIG)
        bias = jnp.concatenate([bias, bias], axis=1)
        for p in range(n_pairs):
            lg = _dot_t(ck, qlat_ref[p * pair:(p + 1) * pair, :]) + bias
            m_old = m_ref[p]
            m_new = jnp.maximum(m_old, jnp.max(lg, axis=0, keepdims=True))
            pr = jnp.exp2(lg - m_new)
            alpha = jnp.exp2(m_old - m_new)
            l_ref[p] = alpha * l_ref[p] + jnp.sum(pr, axis=0, keepdims=True)
            m_ref[p] = m_new
            acc_ref[p] = alpha * acc_ref[p] + _dot(ck_t, pr.astype(BF16))
        return carry

    lax.fori_loop(0, n_chunks, att_chunk, 0)
    for p in range(n_pairs):
        o_ref[p * pair:(p + 1) * pair, :] = (acc_ref[p] * (1.0 / l_ref[p])).T.astype(o_ref.dtype)


def _dsa_out_kernel(o_ref, x_ref, wuv_ref, wout_ref, ln_g_ref, ln_b_ref, wr_hi_ref, wr_lo_ref, br_ref,
                    out_ref, o2_ref, *, alpha):
    nq = o_ref.shape[0]
    heads, _, hd = wuv_ref.shape
    for b in range(nq):
        for h in range(heads):
            oh = o_ref[b, h * Q_BLOCK:(h + 1) * Q_BLOCK, :]
            o2_ref[b * Q_BLOCK:(b + 1) * Q_BLOCK, h * hd:(h + 1) * hd] = _dot(oh, wuv_ref[h]).astype(BF16)
    mix = _dot(o2_ref[...], wout_ref[...])
    _norm_and_route(alpha * x_ref[...] + mix, ln_g_ref[...], ln_b_ref[...], wr_hi_ref[...], wr_lo_ref[...],
                    br_ref[...], out_ref)


def _dsa_layer(xt, batch, w_in, kv_g, w_uk, w_uv, w_out, ln_g, ln_b, router, alpha, tm):
    t, d = xt.shape
    seq = t // batch
    heads, lat, hd = w_uk.shape
    qw = heads * hd
    nqb = t // Q_BLOCK
    nq = tm // Q_BLOCK
    k_sel = min(TOPK_MAX, seq // 4)
    kc = min(KEY_CHUNK, seq)
    slab = min(COUNT_SLAB, seq)
    w_in = w_in.astype(BF16)
    o1, o2, o3 = qw, qw + lat, qw + lat + IDX_HEADS * IDX_DIM
    wkw = jnp.pad(w_in[:, o3:], ((0, 0), (0, LANES - (w_in.shape[1] - o3))))
    proj_args = (xt, w_in[:, :o1], w_in[:, o1:o2], w_in[:, o2:o3], wkw, kv_g[None],
                 jnp.swapaxes(w_uk, 1, 2).astype(BF16))
    qlat, ckv, ckv_t, qidx, kidx, wi_t = pl.pallas_call(
        functools.partial(_dsa_proj_kernel, scale_q=hd ** -0.5 * LOG2E, scale_w=(IDX_HEADS * IDX_DIM) ** -0.5),
        grid=(t // tm,),
        in_specs=[pl.BlockSpec((tm, d), lambda i: (i, 0))] + [_full(a.shape) for a in proj_args[1:]],
        out_specs=[pl.BlockSpec((nq, heads * Q_BLOCK, lat), lambda i: (i, 0, 0)),
                   pl.BlockSpec((tm, lat), lambda i: (i, 0)),
                   pl.BlockSpec((tm // kc, lat, kc), lambda i: (i, 0, 0)),
                   pl.BlockSpec((nq, IDX_HEADS * Q_BLOCK, IDX_DIM), lambda i: (i, 0, 0)),
                   pl.BlockSpec((tm, IDX_DIM), lambda i: (i, 0)),
                   pl.BlockSpec((IDX_HEADS, tm), lambda i: (0, i))],
        out_shape=[jax.ShapeDtypeStruct((nqb, heads * Q_BLOCK, lat), BF16),
                   jax.ShapeDtypeStruct((t, lat), BF16),
                   jax.ShapeDtypeStruct((t // kc, lat, kc), BF16),
                   jax.ShapeDtypeStruct((nqb, IDX_HEADS * Q_BLOCK, IDX_DIM), BF16),
                   jax.ShapeDtypeStruct((t, IDX_DIM), BF16),
                   jax.ShapeDtypeStruct((IDX_HEADS, t), F32)],
        compiler_params=_cparams(("parallel",)),
        name="dsa_proj",
    )(*proj_args)

    nq_seq = seq // Q_BLOCK
    n_kc = seq // kc
    rows = heads * Q_BLOCK
    o = pl.pallas_call(
        functools.partial(_dsa_attn_kernel, k_sel=k_sel, heads=heads),
        grid=(batch, nq_seq),
        in_specs=[pl.BlockSpec((None, IDX_HEADS * Q_BLOCK, IDX_DIM), lambda b, q: (b * nq_seq + q, 0, 0)),
                  pl.BlockSpec((IDX_HEADS, Q_BLOCK), lambda b, q: (0, b * nq_seq + q)),
                  pl.BlockSpec((None, rows, lat), lambda b, q: (b * nq_seq + q, 0, 0)),
                  pl.BlockSpec((None, seq, IDX_DIM), lambda b, q: (b, 0, 0)),
                  pl.BlockSpec((None, seq, lat), lambda b, q: (b, 0, 0)),
                  pl.BlockSpec((None, n_kc, lat, kc), lambda b, q: (b, 0, 0, 0))],
        out_specs=pl.BlockSpec((None, rows, lat), lambda b, q: (b * nq_seq + q, 0, 0)),
        out_shape=jax.ShapeDtypeStruct((nqb, rows, lat), BF16),
        scratch_shapes=[pltpu.VMEM((seq // slab, slab, Q_BLOCK), I32),
                        pltpu.VMEM((heads // 2, lat, 2 * Q_BLOCK), F32),
                        pltpu.VMEM((heads // 2, 1, 2 * Q_BLOCK), F32),
                        pltpu.VMEM((heads // 2, 1, 2 * Q_BLOCK), F32)],
        compiler_params=_cparams(("parallel", "arbitrary")),
        name="dsa_attn",
    )(qidx, wi_t, qlat, kidx.reshape(batch, seq, IDX_DIM), ckv.reshape(batch, seq, lat),
      ckv_t.reshape(batch, n_kc, lat, kc))

    wr_hi, wr_lo, br = router
    out_args = (o, xt, w_uv.astype(BF16), w_out.astype(BF16), ln_g[None], ln_b[None], wr_hi, wr_lo, br)
    return pl.pallas_call(
        functools.partial(_dsa_out_kernel, alpha=alpha),
        grid=(t // tm,),
        in_specs=[pl.BlockSpec((nq, rows, lat), lambda i: (i, 0, 0)),
                  pl.BlockSpec((tm, d), lambda i: (i, 0))] + [_full(a.shape) for a in out_args[2:]],
        out_specs=pl.BlockSpec((tm, d + AUX), lambda i: (i, 0)),
        out_shape=jax.ShapeDtypeStruct((t, d + AUX), F32),
        scratch_shapes=[pltpu.VMEM((tm, qw), BF16)],
        compiler_params=_cparams(("parallel",)),
        name="dsa_out",
    )(*out_args)


def _row_gather_step(i, n, idx_hbm, src_hbm, idx_smem, buf, isem, rsem):
    tm = buf.shape[1]
    slot = i % 2
    nslot = 1 - slot

    def idx_copy(t, s):
        return pltpu.make_async_copy(idx_hbm.at[t], idx_smem.at[s], isem.at[s])

    def start_rows(s):
        def body(r, carry):
            tok = idx_smem[s, r]
            pltpu.make_async_copy(src_hbm.at[pl.ds(tok, 1)], buf.at[s, pl.ds(r, 1)], rsem.at[s]).start()
            return carry
        lax.fori_loop(0, tm, body, 0, unroll=8)

    def wait_rows(s):
        def body(r, carry):
            pltpu.make_async_copy(src_hbm.at[pl.ds(0, 1)], buf.at[s, pl.ds(r, 1)], rsem.at[s]).wait()
            return carry
        lax.fori_loop(0, tm, body, 0, unroll=8)

    @pl.when(i == 0)
    def _():
        first = idx_copy(0, 0)
        first.start()
        first.wait()
        start_rows(0)

        @pl.when(n > 1)
        def _():
            idx_copy(1, 1).start()

    @pl.when(i + 1 < n)
    def _():
        idx_copy(i + 1, nslot).wait()
        start_rows(nslot)

    @pl.when(i + 2 < n)
    def _():
        idx_copy(i + 2, slot).start()

    @pl.when(i < n)
    def _():
        wait_rows(slot)


_GATHER_SCRATCH = lambda tm, width: [pltpu.SMEM((2, tm), I32), pltpu.VMEM((2, tm, width), F32),
                                     pltpu.SemaphoreType.DMA((2,)), pltpu.SemaphoreType.DMA((2,))]


def _moe_kernel(tcls_ref, nused_ref, src_ref, xa_ref, win_ref, wout_ref, ys_ref,
                idx_smem, xbuf, isem, rsem):
    i = pl.program_id(0)
    n = nused_ref[0]
    d = ys_ref.shape[1]
    ff = wout_ref.shape[1]
    _row_gather_step(i, n, src_ref, xa_ref, idx_smem, xbuf, isem, rsem)

    @pl.when(i < n)
    def _():
        cls = tcls_ref[i]
        xs = xbuf[i % 2]
        xb = xs[:, :d].astype(BF16)
        y = None
        for e, gate in (((cls >> 3) & 7, xs[:, d + 1:d + 2]), (cls & 7, xs[:, d + 2:d + 3])):
            h = _dot(xb, win_ref[e])
            a = h[:, :ff]
            act = (a * jax.nn.sigmoid(a) * h[:, ff:] * gate).astype(BF16)
            ye = _dot(act, wout_ref[e])
            y = ye if y is None else y + ye
        ys_ref[...] = y

    @pl.when(i >= n)
    def _():
        ys_ref[...] = jnp.zeros(ys_ref.shape, F32)


def _post_kernel(pos_ref, ys_ref, xa_ref, p_ref, ln_g_ref, ln_b_ref, pw_ref, gw_ref, gb_ref, out_ref,
                 idx_smem, ybuf, isem, rsem, *, alpha, n_tiles):
    i = pl.program_id(0)
    _row_gather_step(i, n_tiles, pos_ref, ys_ref, idx_smem, ybuf, isem, rsem)
    x2 = _layer_norm(alpha * xa_ref[...] + ybuf[i % 2], ln_g_ref[...], ln_b_ref[...])
    gate = jax.nn.sigmoid(_dot(x2.astype(BF16), gw_ref[...]) + gb_ref[...])
    out_ref[...] = x2 + _dot(p_ref[...].astype(BF16), pw_ref[...]) * gate


def _moe_and_post(x1a, p_i, e_w_in, e_w_out, ln_g, ln_b, ple_w, ple_gw, ple_gb, alpha, tm_moe, tm_post):
    t, da = x1a.shape
    d = da - AUX
    n_exp, _, ff2 = e_w_in.shape
    epg = n_exp // N_EGROUPS
    n_cls = N_EGROUPS * 64

    cls = x1a[:, d].astype(I32)
    order = jnp.argsort(cls).astype(I32)
    scls = cls[order]
    starts = jnp.searchsorted(scls, jnp.arange(n_cls, dtype=I32), side="left").astype(I32)
    counts = jnp.diff(jnp.concatenate([starts, jnp.array([t], I32)]))
    tiles_per = (counts + tm_moe - 1) // tm_moe
    tile_end = jnp.cumsum(tiles_per)
    tile_off = tile_end - tiles_per
    n_pairs = N_EGROUPS * (epg * (epg - 1) // 2)
    nt = t // tm_moe + n_pairs
    ppos = tile_off[scls] * tm_moe + (jnp.arange(t, dtype=I32) - starts[scls])
    src = jnp.zeros((nt * tm_moe,), I32).at[ppos].set(order).reshape(nt, tm_moe)
    pos = jnp.zeros((t,), I32).at[order].set(ppos).reshape(t // tm_post, tm_post)
    n_used = tile_end[-1:].astype(I32)
    tile_cls = jnp.searchsorted(tile_end, jnp.arange(nt, dtype=I32), side="right").astype(I32)
    tile_cls = jnp.minimum(tile_cls, scls[-1])

    w_in = e_w_in.astype(BF16).reshape(N_EGROUPS, epg, d, ff2)
    w_out = e_w_out.astype(BF16).reshape(N_EGROUPS, epg, ff2 // 2, d)
    ys = pl.pallas_call(
        _moe_kernel,
        grid_spec=pltpu.PrefetchScalarGridSpec(
            num_scalar_prefetch=2,
            grid=(nt,),
            in_specs=[pl.BlockSpec(memory_space=pl.ANY),
                      pl.BlockSpec(memory_space=pl.ANY),
                      pl.BlockSpec((None, epg, d, ff2), lambda i, tc, nu: (tc[i] >> 6, 0, 0, 0)),
                      pl.BlockSpec((None, epg, ff2 // 2, d), lambda i, tc, nu: (tc[i] >> 6, 0, 0, 0))],
            out_specs=pl.BlockSpec((tm_moe, d), lambda i, tc, nu: (i, 0)),
            scratch_shapes=_GATHER_SCRATCH(tm_moe, da)),
        out_shape=jax.ShapeDtypeStruct((nt * tm_moe, d), F32),
        compiler_params=_cparams(("arbitrary",)),
        name="moe",
    )(tile_cls, n_used, src, x1a, w_in, w_out)

    n_tiles = t // tm_post
    post_args = (pos, ys, x1a, p_i, ln_g[None], ln_b[None], ple_w.astype(BF16), ple_gw.astype(BF16), ple_gb[None])
    return pl.pallas_call(
        functools.partial(_post_kernel, alpha=alpha, n_tiles=n_tiles),
        grid=(n_tiles,),
        in_specs=[pl.BlockSpec(memory_space=pl.ANY),
                  pl.BlockSpec(memory_space=pl.ANY),
                  pl.BlockSpec((tm_post, d), lambda i: (i, 0)),
                  pl.BlockSpec((tm_post, p_i.shape[1]), lambda i: (i, 0))] + [_full(a.shape) for a in post_args[4:]],
        out_specs=pl.BlockSpec((tm_post, d), lambda i: (i, 0)),
        out_shape=jax.ShapeDtypeStruct((t, d), F32),
        scratch_shapes=_GATHER_SCRATCH(tm_post, d),
        compiler_params=_cparams(("arbitrary",)),
        name="post",
    )(*post_args)


def _router_params(wg, bg, we, be):
    d = wg.shape[0]
    w = jnp.zeros((d, ROUTE_COLS), F32).at[:, :N_EGROUPS].set(wg).at[:, 8:8 + we.shape[1]].set(we)
    b = jnp.zeros((1, ROUTE_COLS), F32).at[0, :N_EGROUPS].set(bg).at[0, 8:8 + be.shape[0]].set(be)
    hi = w.astype(BF16)
    return hi, (w - hi.astype(F32)).astype(BF16), b


def _tile(t, want):
    while t % want:
        want //= 2
    return want


def kernel(x, p, a_w_in, a_b_in, a_vn_g, a_vn_b, a_w_s, a_b_s, a_w_out, b_w_in, b_kv_g, b_w_uk, b_w_uv, b_w_out,
           ln1_g, ln1_b, ln2_g, ln2_b, r_wg, r_bg, r_we, r_be, e_w_in, e_w_out, ple_w, ple_gw, ple_gb):
    batch, seq, d = x.shape
    t = batch * seq
    depth = p.shape[0]
    alpha = (2 * depth) ** 0.25
    chunk = a_w_s.shape[-1]
    xt = x.reshape(t, d)
    pt = p.reshape(depth, t, p.shape[-1])
    for i in range(depth):
        j = i // 2
        router = _router_params(r_wg[i], r_bg[i], r_we[i], r_be[i])
        if i % 2 == 0:
            x1a = _gmlp_layer(xt, a_w_in[j], a_b_in[j], a_vn_g[j], a_vn_b[j], a_w_s[j], a_b_s[j], a_w_out[j],
                              ln1_g[i], ln1_b[i], router, alpha, max(chunk, _tile(t, 512)))
        else:
            x1a = _dsa_layer(xt, batch, b_w_in[j], b_kv_g[j], b_w_uk[j], b_w_uv[j], b_w_out[j],
                             ln1_g[i], ln1_b[i], router, alpha, _tile(t, 512))
        xt = _moe_and_post(x1a, pt[i], e_w_in[i], e_w_out[i], ln2_g[i], ln2_b[i],
                           ple_w[i], ple_gw[i], ple_gb[i], alpha, 128, _tile(t, 256))
    return xt.reshape(batch, seq, d)
```

```python
import functools

import jax
import jax.numpy as jnp
from jax import lax
from jax.experimental import pallas as pl
from jax.experimental.pallas import tpu as pltpu

F32, BF16, I32 = jnp.float32, jnp.bfloat16, jnp.int32

IDX_HEADS = 8
IDX_DIM = 64
TOPK_MAX = 256
N_EGROUPS = 4
LN_EPS = 1e-5
RMS_EPS = 1e-6

LANES = 128
Q_BLOCK = 128
KEY_CHUNK = 512
COUNT_SLAB = 1024
COUNT_ACC = 64
LOG2E = 1.4426950408889634
VMEM_LIMIT = 48 * 1024 * 1024

ROUTE_COLS = LANES
ROUTE_ROWS = 8
N_CLASS = 64 * N_EGROUPS
Y_ROWS = 8
X_ROWS = Y_ROWS + 1
PLAN_BLOCK = 256
MOE_TILE = 128
INT_MIN = -(2 ** 31)
NEG_BIG = -1e30


def _cparams(sem, vmem=VMEM_LIMIT):
    return pltpu.CompilerParams(dimension_semantics=sem, vmem_limit_bytes=vmem)


def _full(shape):
    n = len(shape)
    return pl.BlockSpec(shape, lambda *_: (0,) * n)


def _dot(a, b):
    return jnp.dot(a, b, preferred_element_type=F32)


def _dot_t(a, b):
    return lax.dot_general(a, b, (((1,), (1,)), ((), ())), preferred_element_type=F32)


def _layer_norm(x, g, b):
    mu = jnp.mean(x, axis=-1, keepdims=True)
    xc = x - mu
    var = jnp.mean(xc * xc, axis=-1, keepdims=True)
    return xc * lax.rsqrt(var + LN_EPS) * g + b


def _route_rows(lg_t):
    g = [lg_t[i:i + 1] for i in range(N_EGROUPS)]
    gmax = functools.reduce(jnp.maximum, g)
    gsel = jnp.where(g[0] >= gmax, 0.0, jnp.where(g[1] >= gmax, 1.0, jnp.where(g[2] >= gmax, 2.0, 3.0)))
    den = functools.reduce(lambda a, b: a + b, [jnp.exp(gi - gmax) for gi in g])
    p_g = 1.0 / den
    el = lg_t[8:16]
    for gi in range(1, N_EGROUPS):
        el = jnp.where(gsel == float(gi), lg_t[8 + 8 * gi:16 + 8 * gi], el)
    eidx = lax.broadcasted_iota(I32, el.shape, 0).astype(F32)
    m1 = jnp.max(el, axis=0, keepdims=True)
    i1 = jnp.min(jnp.where(el == m1, eidx, 8.0), axis=0, keepdims=True)
    el2 = jnp.where(eidx == i1, -jnp.inf, el)
    m2 = jnp.max(el2, axis=0, keepdims=True)
    i2 = jnp.min(jnp.where(el2 == m2, eidx, 8.0), axis=0, keepdims=True)
    r = jnp.exp(m2 - m1)
    inv = 1.0 / (1.0 + r)
    gate1 = p_g * inv
    gate2 = p_g * r * inv
    first_lo = i1 < i2
    lo = jnp.minimum(i1, i2)
    hi = jnp.maximum(i1, i2)
    cls = gsel * 64.0 + lo * 8.0 + hi
    return cls, jnp.where(first_lo, gate1, gate2), jnp.where(first_lo, gate2, gate1)


def _norm_and_route(y, ln_g, ln_b, wr_hi, wr_lo, br, x1_ref, route_ref):
    tm, _ = y.shape
    x1 = _layer_norm(y, ln_g, ln_b)
    x1_ref[...] = x1
    x_hi = x1.astype(BF16)
    x_lo = (x1 - x_hi.astype(F32)).astype(BF16)
    lg = _dot(x_hi, wr_hi) + _dot(x_lo, wr_hi) + _dot(x_hi, wr_lo) + br
    cls, g_lo, g_hi = _route_rows(lg.T)
    row = lax.broadcasted_iota(I32, (ROUTE_ROWS, tm), 0)
    route_ref[...] = jnp.where(row == 0, cls, jnp.where(row == 1, g_lo, jnp.where(row == 2, g_hi, 0.0)))


def _mixer_out_specs(tm, d):
    return [pl.BlockSpec((tm, d), lambda i: (i, 0)), pl.BlockSpec((ROUTE_ROWS, tm), lambda i: (0, i))]


def _mixer_out_shape(t, d):
    return [jax.ShapeDtypeStruct((t, d), F32), jax.ShapeDtypeStruct((ROUTE_ROWS, t), F32)]


def _gmlp_kernel(x_ref, w_in_ref, b_in_ref, vn_g_ref, vn_b_ref, w_s_ref, b_st_ref, w_out_ref,
                 ln_g_ref, ln_b_ref, wr_hi_ref, wr_lo_ref, br_ref, x1_ref, route_ref, gated_ref, *, alpha):
    x = x_ref[...]
    tm, _ = x.shape
    groups, chunk, _ = w_s_ref.shape
    z = _dot(x.astype(BF16), w_in_ref[...]) + b_in_ref[...]
    z = 0.5 * z * (1.0 + lax.erf(z * (2.0 ** -0.5)))
    half = z.shape[1] // 2
    gd = half // groups
    u = z[:, :half]
    v = _layer_norm(z[:, half:], vn_g_ref[...], vn_b_ref[...]).astype(BF16)
    r = lax.broadcasted_iota(I32, (chunk, chunk), 0)
    c = lax.broadcasted_iota(I32, (chunk, chunk), 1)
    causal = r >= c
    for g in range(groups):
        w_c = jnp.where(causal, w_s_ref[g], 0.0).astype(BF16)
        bias = b_st_ref[:, g:g + 1]
        for ci in range(tm // chunk):
            rows = slice(ci * chunk, (ci + 1) * chunk)
            cols = slice(g * gd, (g + 1) * gd)
            s = _dot(w_c, v[rows, cols]) + bias
            gated_ref[rows, cols] = (u[rows, cols] * s).astype(BF16)
    mix = _dot(gated_ref[...], w_out_ref[...])
    _norm_and_route(alpha * x + mix, ln_g_ref[...], ln_b_ref[...], wr_hi_ref[...], wr_lo_ref[...],
                    br_ref[...], x1_ref, route_ref)


def _gmlp_layer(xt, w_in, b_in, vn_g, vn_b, w_s, b_s, w_out, ln_g, ln_b, router, alpha, tm):
    t, d = xt.shape
    half = w_out.shape[0]
    wr_hi, wr_lo, br = router
    args = (xt, w_in.astype(BF16), b_in[None], vn_g[None], vn_b[None], w_s, b_s.T, w_out.astype(BF16),
            ln_g[None], ln_b[None], wr_hi, wr_lo, br)
    in_specs = [pl.BlockSpec((tm, d), lambda i: (i, 0))] + [_full(a.shape) for a in args[1:]]
    return pl.pallas_call(
        functools.partial(_gmlp_kernel, alpha=alpha),
        grid=(t // tm,),
        in_specs=in_specs,
        out_specs=_mixer_out_specs(tm, d),
        out_shape=_mixer_out_shape(t, d),
        scratch_shapes=[pltpu.VMEM((tm, half), BF16)],
        compiler_params=_cparams(("parallel",)),
        name="gmlp",
    )(*args)


def _dsa_proj_kernel(x_ref, wq_ref, wc_ref, wqi_ref, wkw_ref, kvg_ref, wuk_t_ref,
                     qlat_ref, ckv_ref, ckv_t_ref, qidx_ref, kidx_ref, wi_t_ref, *, scale_q, scale_w):
    xb = x_ref[...].astype(BF16)
    nq = qlat_ref.shape[0]
    kc = ckv_t_ref.shape[2]
    heads, hd, _ = wuk_t_ref.shape
    q = _dot(xb, wq_ref[...])
    for h in range(heads):
        ql = (_dot(q[:, h * hd:(h + 1) * hd].astype(BF16), wuk_t_ref[h]) * scale_q).astype(BF16)
        for b in range(nq):
            qlat_ref[b, h * Q_BLOCK:(h + 1) * Q_BLOCK, :] = ql[b * Q_BLOCK:(b + 1) * Q_BLOCK]
    c = _dot(xb, wc_ref[...])
    ms = jnp.mean(c * c, axis=-1, keepdims=True)
    c = c * lax.rsqrt(ms + RMS_EPS) * kvg_ref[...]
    ckv_ref[...] = c.astype(BF16)
    for b in range(ckv_t_ref.shape[0]):
        ckv_t_ref[b] = c[b * kc:(b + 1) * kc].T.astype(BF16)
    qi = _dot(xb, wqi_ref[...]).astype(BF16)
    for h in range(IDX_HEADS):
        for b in range(nq):
            qidx_ref[b, h * Q_BLOCK:(h + 1) * Q_BLOCK, :] = qi[b * Q_BLOCK:(b + 1) * Q_BLOCK,
                                                               h * IDX_DIM:(h + 1) * IDX_DIM]
    kw = _dot(xb, wkw_ref[...])
    kidx_ref[...] = kw[:, :IDX_DIM].astype(BF16)
    wi_t_ref[...] = kw.T[IDX_DIM:IDX_DIM + IDX_HEADS] * scale_w


def _dsa_attn_kernel(qidx_ref, wi_t_ref, qlat_ref, kidx_ref, ckv_ref, ckv_t_ref, o_ref,
                     key_ref, acc_ref, m_ref, l_ref, *, k_sel, heads):
    qb = pl.program_id(1)
    kc = ckv_t_ref.shape[2]
    slab = key_ref.shape[1]
    cps = slab // kc
    nq = Q_BLOCK
    pair = 2 * nq
    n_pairs = heads // 2
    n_chunks = (qb * nq + nq + kc - 1) // kc
    n_slabs = (n_chunks + cps - 1) // cps

    def key_chunk(j):
        return key_ref.at[j // cps, pl.ds(pl.multiple_of((j % cps) * kc, kc), kc), :]

    q_pos = qb * nq + lax.broadcasted_iota(I32, (kc, nq), 1)
    k_off = lax.broadcasted_iota(I32, (kc, nq), 0)

    def pad_chunk(j, carry):
        key_chunk(j)[...] = jnp.full((kc, nq), jnp.nan, F32)
        return carry

    lax.fori_loop(n_chunks, n_slabs * cps, pad_chunk, 0)

    def score_chunk(j, carry):
        kch = kidx_ref[pl.ds(pl.multiple_of(j * kc, kc), kc), :]
        tot = None
        for p in range(IDX_HEADS // 2):
            sc = _dot_t(kch, qidx_ref[p * pair:(p + 1) * pair, :])
            for hh in range(2):
                h = 2 * p + hh
                r = jnp.maximum(sc[:, hh * nq:(hh + 1) * nq], 0.0) * wi_t_ref[h:h + 1, :]
                tot = r if tot is None else tot + r
        key_chunk(j)[...] = jnp.where(k_off + j * kc <= q_pos, tot, jnp.nan)
        return carry

    lax.fori_loop(0, n_chunks, score_chunk, 0)

    def count(t, strict=False):
        tb = jnp.broadcast_to(t, (slab, nq))

        def body(s, acc):
            blk = key_ref[s]
            hit = jnp.where(blk > tb if strict else blk >= tb, 1.0, 0.0)
            return acc + jnp.sum(hit.reshape(slab // COUNT_ACC, COUNT_ACC, nq), axis=0)

        acc = lax.fori_loop(0, n_slabs, body, jnp.zeros((COUNT_ACC, nq), F32))
        return jnp.sum(acc, axis=0, keepdims=True)

    def decode(code):
        return lax.bitcast_convert_type(code ^ ((code >> 31) & 0x7FFFFFFF), F32)

    def bit_step(b, c):
        t = c + jnp.left_shift(jnp.int32(1), 31 - b)
        return jnp.where(count(decode(t)) >= float(k_sel), t, c)

    c_sel = lax.fori_loop(0, 32, bit_step, jnp.full((1, nq), INT_MIN, I32))
    c_sel = jnp.where(c_sel == INT_MIN, -jnp.inf, decode(c_sel))

    n_ge = count(c_sel)
    has_tie = jnp.max(n_ge) > float(k_sel)

    @pl.when(has_tie)
    def _():
        need = float(k_sel) - count(c_sel, strict=True)
        tie_col = n_ge > float(k_sel)
        cb = jnp.broadcast_to(c_sel, (kc, nq))
        lower = (lax.broadcasted_iota(I32, (kc, kc), 1) < lax.broadcasted_iota(I32, (kc, kc), 0))
        lower = jnp.where(lower, 1.0, 0.0).astype(BF16)

        def fix(j, seen):
            blk = key_chunk(j)[...]
            eq = blk == cb
            eq_f = jnp.where(eq, 1.0, 0.0)
            rank = seen + _dot(lower, eq_f.astype(BF16))
            drop = eq & tie_col & (rank >= need)
            key_chunk(j)[...] = jnp.where(drop, jnp.nan, blk)
            return seen + jnp.sum(eq_f, axis=0, keepdims=True)

        lax.fori_loop(0, n_chunks, fix, jnp.zeros((1, nq), F32))

    thr = jnp.broadcast_to(c_sel, (kc, nq))
    m_ref[...] = jnp.full(m_ref.shape, NEG_BIG, F32)
    l_ref[...] = jnp.zeros(l_ref.shape, F32)
    acc_ref[...] = jnp.zeros(acc_ref.shape, F32)

    def att_chunk(j, carry):
        ck = ckv_ref[pl.ds(pl.multiple_of(j * kc, kc), kc), :]
        ck_t = ckv_t_ref[j]
        bias = jnp.where(key_chunk(j)[...] >= thr, 0.0, NEG_BIG)
        bias = jnp.concatenate([bias, bias], axis=1)
        for p in range(n_pairs):
            lg = _dot_t(ck, qlat_ref[p * pair:(p + 1) * pair, :]) + bias
            m_old = m_ref[p]
            m_new = jnp.maximum(m_old, jnp.max(lg, axis=0, keepdims=True))
            pr = jnp.exp2(lg - m_new)
            alpha = jnp.exp2(m_old - m_new)
            l_ref[p] = alpha * l_ref[p] + jnp.sum(pr, axis=0, keepdims=True)
            m_ref[p] = m_new
            acc_ref[p] = alpha * acc_ref[p] + _dot(ck_t, pr.astype(BF16))
        return carry

    lax.fori_loop(0, n_chunks, att_chunk, 0)
    for p in range(n_pairs):
        o_ref[p * pair:(p + 1) * pair, :] = (acc_ref[p] * (1.0 / l_ref[p])).T.astype(o_ref.dtype)


def _dsa_out_kernel(o_ref, x_ref, wuv_ref, wout_ref, ln_g_ref, ln_b_ref, wr_hi_ref, wr_lo_ref, br_ref,
                    x1_ref, route_ref, o2_ref, *, alpha):
    nq = o_ref.shape[0]
    heads, _, hd = wuv_ref.shape
    for b in range(nq):
        for h in range(heads):
            oh = o_ref[b, h * Q_BLOCK:(h + 1) * Q_BLOCK, :]
            o2_ref[b * Q_BLOCK:(b + 1) * Q_BLOCK, h * hd:(h + 1) * hd] = _dot(oh, wuv_ref[h]).astype(BF16)
    mix = _dot(o2_ref[...], wout_ref[...])
    _norm_and_route(alpha * x_ref[...] + mix, ln_g_ref[...], ln_b_ref[...], wr_hi_ref[...], wr_lo_ref[...],
                    br_ref[...], x1_ref, route_ref)


def _dsa_layer(xt, batch, w_in, kv_g, w_uk, w_uv, w_out, ln_g, ln_b, router, alpha, tm):
    t, d = xt.shape
    seq = t // batch
    heads, lat, hd = w_uk.shape
    qw = heads * hd
    nqb = t // Q_BLOCK
    nq = tm // Q_BLOCK
    k_sel = min(TOPK_MAX, seq // 4)
    kc = min(KEY_CHUNK, seq)
    slab = min(COUNT_SLAB, seq)
    w_in = w_in.astype(BF16)
    o1, o2, o3 = qw, qw + lat, qw + lat + IDX_HEADS * IDX_DIM
    wkw = jnp.pad(w_in[:, o3:], ((0, 0), (0, LANES - (w_in.shape[1] - o3))))
    proj_args = (xt, w_in[:, :o1], w_in[:, o1:o2], w_in[:, o2:o3], wkw, kv_g[None],
                 jnp.swapaxes(w_uk, 1, 2).astype(BF16))
    qlat, ckv, ckv_t, qidx, kidx, wi_t = pl.pallas_call(
        functools.partial(_dsa_proj_kernel, scale_q=hd ** -0.5 * LOG2E, scale_w=(IDX_HEADS * IDX_DIM) ** -0.5),
        grid=(t // tm,),
        in_specs=[pl.BlockSpec((tm, d), lambda i: (i, 0))] + [_full(a.shape) for a in proj_args[1:]],
        out_specs=[pl.BlockSpec((nq, heads * Q_BLOCK, lat), lambda i: (i, 0, 0)),
                   pl.BlockSpec((tm, lat), lambda i: (i, 0)),
                   pl.BlockSpec((tm // kc, lat, kc), lambda i: (i, 0, 0)),
                   pl.BlockSpec((nq, IDX_HEADS * Q_BLOCK, IDX_DIM), lambda i: (i, 0, 0)),
                   pl.BlockSpec((tm, IDX_DIM), lambda i: (i, 0)),
                   pl.BlockSpec((IDX_HEADS, tm), lambda i: (0, i))],
        out_shape=[jax.ShapeDtypeStruct((nqb, heads * Q_BLOCK, lat), BF16),
                   jax.ShapeDtypeStruct((t, lat), BF16),
                   jax.ShapeDtypeStruct((t // kc, lat, kc), BF16),
                   jax.ShapeDtypeStruct((nqb, IDX_HEADS * Q_BLOCK, IDX_DIM), BF16),
                   jax.ShapeDtypeStruct((t, IDX_DIM), BF16),
                   jax.ShapeDtypeStruct((IDX_HEADS, t), F32)],
        compiler_params=_cparams(("parallel",)),
        name="dsa_proj",
    )(*proj_args)

    nq_seq = seq // Q_BLOCK
    n_kc = seq // kc
    rows = heads * Q_BLOCK
    o = pl.pallas_call(
        functools.partial(_dsa_attn_kernel, k_sel=k_sel, heads=heads),
        grid=(batch, nq_seq),
        in_specs=[pl.BlockSpec((None, IDX_HEADS * Q_BLOCK, IDX_DIM), lambda b, q: (b * nq_seq + q, 0, 0)),
                  pl.BlockSpec((IDX_HEADS, Q_BLOCK), lambda b, q: (0, b * nq_seq + q)),
                  pl.BlockSpec((None, rows, lat), lambda b, q: (b * nq_seq + q, 0, 0)),
                  pl.BlockSpec((None, seq, IDX_DIM), lambda b, q: (b, 0, 0)),
                  pl.BlockSpec((None, seq, lat), lambda b, q: (b, 0, 0)),
                  pl.BlockSpec((None, n_kc, lat, kc), lambda b, q: (b, 0, 0, 0))],
        out_specs=pl.BlockSpec((None, rows, lat), lambda b, q: (b * nq_seq + q, 0, 0)),
        out_shape=jax.ShapeDtypeStruct((nqb, rows, lat), BF16),
        scratch_shapes=[pltpu.VMEM((seq // slab, slab, Q_BLOCK), F32),
                        pltpu.VMEM((heads // 2, lat, 2 * Q_BLOCK), F32),
                        pltpu.VMEM((heads // 2, 1, 2 * Q_BLOCK), F32),
                        pltpu.VMEM((heads // 2, 1, 2 * Q_BLOCK), F32)],
        compiler_params=_cparams(("parallel", "arbitrary")),
        name="dsa_attn",
    )(qidx, wi_t, qlat, kidx.reshape(batch, seq, IDX_DIM), ckv.reshape(batch, seq, lat),
      ckv_t.reshape(batch, n_kc, lat, kc))

    wr_hi, wr_lo, br = router
    out_args = (o, xt, w_uv.astype(BF16), w_out.astype(BF16), ln_g[None], ln_b[None], wr_hi, wr_lo, br)
    return pl.pallas_call(
        functools.partial(_dsa_out_kernel, alpha=alpha),
        grid=(t // tm,),
        in_specs=[pl.BlockSpec((nq, rows, lat), lambda i: (i, 0, 0)),
                  pl.BlockSpec((tm, d), lambda i: (i, 0))] + [_full(a.shape) for a in out_args[2:]],
        out_specs=_mixer_out_specs(tm, d),
        out_shape=_mixer_out_shape(t, d),
        scratch_shapes=[pltpu.VMEM((tm, qw), BF16)],
        compiler_params=_cparams(("parallel",)),
        name="dsa_out",
    )(*out_args)


DMA_UNROLL = 8


def _start_token_moves(n, copy_of):
    def body(g, carry):
        for u in range(DMA_UNROLL):
            copy_of(g * DMA_UNROLL + u).start(priority=u % 2)
        return carry
    lax.fori_loop(0, n // DMA_UNROLL, body, 0)


def _tok(ref, i, rows):
    return ref.at[pl.ds(i * rows, rows)]


def _rows_of(ref, j, n, rows):
    return ref.at[pl.ds(j, n, stride=rows), :]


def _token_gather_step(i, n, idx_hbm, src_hbm, idx_smem, buf, isem, rsem):
    tm = idx_smem.shape[2]
    rows = buf.shape[1] // tm
    slot = i % 2
    nslot = 1 - slot

    def idx_copy(t, s):
        return pltpu.make_async_copy(idx_hbm.at[t], idx_smem.at[s], isem.at[s])

    def start_tiles(s):
        _start_token_moves(tm, lambda r: pltpu.make_async_copy(
            _tok(src_hbm, idx_smem[s, 0, r], rows), _tok(buf.at[s], r, rows), rsem.at[s]))

    @pl.when(i == 0)
    def _():
        first = idx_copy(0, 0)
        first.start()
        first.wait()
        start_tiles(0)

        @pl.when(n > 1)
        def _():
            idx_copy(1, 1).start()

    @pl.when(i + 1 < n)
    def _():
        idx_copy(i + 1, nslot).wait()
        start_tiles(nslot)

    @pl.when(i + 2 < n)
    def _():
        idx_copy(i + 2, slot).start()

    pltpu.make_async_copy(src_hbm.at[pl.ds(0, tm * rows)], buf.at[slot], rsem.at[slot]).wait()


def _plan_kernel(route_ref, x_ref, xs_in_ref, pos_ref, tinfo_ref, xs_ref,
                 hist_ref, off_ref, carry_ref, stage_ref, ppos_vmem, ppos_smem, ssem, psem, *, tile):
    del xs_in_ref
    ps = pl.program_id(0)
    b = pl.program_id(1)
    nb = pl.num_programs(1)
    n_cls, pb = hist_ref.shape
    kt = tinfo_ref.shape[1]
    cls = route_ref[0:1, :].astype(I32)
    cid = lax.broadcasted_iota(I32, (n_cls, pb), 0)
    onehot = jnp.where(cid == cls, 1.0, 0.0)

    @pl.when(ps == 0)
    def _():
        @pl.when(b == 0)
        def _():
            hist_ref[...] = jnp.zeros(hist_ref.shape, F32)
        hist_ref[...] += onehot

    @pl.when((ps == 1) & (b == 0))
    def _():
        counts = jnp.sum(hist_ref[...], axis=1, keepdims=True)
        tiles = jnp.floor((counts + float(tile - 1)) * (1.0 / tile))
        lower = lax.broadcasted_iota(I32, (n_cls, n_cls), 1) < lax.broadcasted_iota(I32, (n_cls, n_cls), 0)
        lower = jnp.where(lower, 1.0, 0.0).astype(BF16)
        off = _dot(lower, jnp.broadcast_to(tiles, (n_cls, LANES)).astype(BF16))[:, :1]
        off_ref[...] = off * float(tile)
        carry_ref[...] = jnp.zeros(carry_ref.shape, F32)
        k = lax.broadcasted_iota(I32, (n_cls, kt), 1).astype(F32)
        cid_f = lax.broadcasted_iota(I32, (n_cls, kt), 0).astype(F32)
        mine = (k >= off) & (k < off + tiles)
        tcls = jnp.sum(jnp.where(mine, cid_f, 0.0), axis=0, keepdims=True)
        n_used = jnp.sum(tiles, axis=0, keepdims=True)
        last = jnp.max(jnp.where(tiles > 0.0, cid_f[:, :1], 0.0), axis=0, keepdims=True)
        tcls = jnp.where(k[:1] >= n_used, last, tcls)
        row = lax.broadcasted_iota(I32, (ROUTE_ROWS, kt), 0)
        tinfo_ref[...] = jnp.where(row == 0, tcls, jnp.where(row == 1, n_used, 0.0)).astype(I32)

    @pl.when(ps == 1)
    def _():
        upper = lax.broadcasted_iota(I32, (pb, pb), 0) < lax.broadcasted_iota(I32, (pb, pb), 1)
        upper = jnp.where(upper, 1.0, 0.0).astype(BF16)
        before = _dot(onehot.astype(BF16), upper)
        ppos = jnp.sum(onehot * (before + carry_ref[...] + off_ref[...]), axis=0, keepdims=True).astype(I32)
        carry_ref[...] += jnp.sum(onehot, axis=1, keepdims=True)
        pos_ref[...] = ppos
        ppos_vmem[...] = ppos
        to_smem = pltpu.make_async_copy(ppos_vmem, ppos_smem, psem)
        to_smem.start()

        slot = b % 2
        stage = stage_ref.at[slot]

        def drain(s):
            pltpu.make_async_copy(stage_ref.at[s], xs_ref.at[pl.ds(0, pb * X_ROWS)], ssem.at[s]).wait()

        @pl.when(b >= 2)
        def _():
            drain(slot)

        for j in range(Y_ROWS):
            _rows_of(stage, j, pb, X_ROWS)[...] = x_ref[:, j * LANES:(j + 1) * LANES]
        record = jnp.concatenate([route_ref[...], jnp.zeros((LANES - ROUTE_ROWS, pb), F32)], axis=0).T
        _rows_of(stage, Y_ROWS, pb, X_ROWS)[...] = record

        to_smem.wait()
        _start_token_moves(pb, lambda r: pltpu.make_async_copy(
            _tok(stage, r, X_ROWS), _tok(xs_ref, ppos_smem[0, r], X_ROWS), ssem.at[slot]))

        @pl.when(b == nb - 1)
        def _():
            drain(slot)

            @pl.when(nb > 1)
            def _():
                drain(1 - slot)


def _moe_kernel(tcls_ref, nused_ref, xs_ref, win_ref, wout_ref, ys_ref):
    i = pl.program_id(0)
    tm = xs_ref.shape[0] // X_ROWS
    ff = wout_ref.shape[1]

    @pl.when(i < nused_ref[0])
    def _():
        cls = tcls_ref[i]
        xb = jnp.concatenate([_rows_of(xs_ref, j, tm, X_ROWS)[...].astype(BF16) for j in range(Y_ROWS)], axis=1)
        record = _rows_of(xs_ref, Y_ROWS, tm, X_ROWS)[...]
        y = None
        for e, gate in (((cls >> 3) & 7, record[:, 1:2]), (cls & 7, record[:, 2:3])):
            h = _dot(xb, win_ref[e])
            a = h[:, :ff]
            act = (a * jax.nn.sigmoid(a) * h[:, ff:] * gate).astype(BF16)
            ye = _dot(act, wout_ref[e])
            y = ye if y is None else y + ye
        for j in range(Y_ROWS):
            _rows_of(ys_ref, j, tm, Y_ROWS)[...] = y[:, j * LANES:(j + 1) * LANES]

    @pl.when(i >= nused_ref[0])
    def _():
        ys_ref[...] = jnp.zeros(ys_ref.shape, F32)


def _post_kernel(pos_ref, ys_ref, x1_ref, p_ref, ln_g_ref, ln_b_ref, pw_ref, gw_ref, gb_ref, out_ref,
                 idx_smem, ybuf, isem, rsem, *, alpha, n_blocks):
    i = pl.program_id(0)
    tm = x1_ref.shape[0]
    _token_gather_step(i, n_blocks, pos_ref, ys_ref, idx_smem, ybuf, isem, rsem)
    yb = ybuf.at[i % 2]
    ffn = jnp.concatenate([_rows_of(yb, j, tm, Y_ROWS)[...] for j in range(Y_ROWS)], axis=1)
    x2 = _layer_norm(alpha * x1_ref[...] + ffn, ln_g_ref[...], ln_b_ref[...])
    gate = jax.nn.sigmoid(_dot(x2.astype(BF16), gw_ref[...]) + gb_ref[...])
    out_ref[...] = x2 + _dot(p_ref[...].astype(BF16), pw_ref[...]) * gate


def _moe_and_post(x1, route, p_i, e_w_in, e_w_out, ln_g, ln_b, ple_w, ple_gw, ple_gb, alpha):
    t, d = x1.shape
    assert d == LANES * Y_ROWS
    n_exp, _, ff2 = e_w_in.shape
    epg = n_exp // N_EGROUPS
    pb = _tile(t, PLAN_BLOCK)
    nb = t // pb
    nt = t // MOE_TILE + N_EGROUPS * (epg * (epg - 1) // 2)
    kt = -(-nt // LANES) * LANES
    n_sorted = nt * MOE_TILE

    pos, tinfo, xs = pl.pallas_call(
        functools.partial(_plan_kernel, tile=MOE_TILE),
        grid=(2, nb),
        in_specs=[pl.BlockSpec((ROUTE_ROWS, pb), lambda ps, b: (0, b)),
                  pl.BlockSpec((pb, d), lambda ps, b: (ps * b, 0)),
                  pl.BlockSpec(memory_space=pl.ANY)],
        out_specs=[pl.BlockSpec((None, 1, pb), lambda ps, b: (ps * b, 0, 0)),
                   pl.BlockSpec((ROUTE_ROWS, kt), lambda ps, b: (0, 0)),
                   pl.BlockSpec(memory_space=pl.ANY)],
        out_shape=[jax.ShapeDtypeStruct((nb, 1, pb), I32),
                   jax.ShapeDtypeStruct((ROUTE_ROWS, kt), I32),
                   jax.ShapeDtypeStruct((n_sorted * X_ROWS, LANES), F32)],
        scratch_shapes=[pltpu.VMEM((N_CLASS, pb), F32),
                        pltpu.VMEM((N_CLASS, 1), F32),
                        pltpu.VMEM((N_CLASS, 1), F32),
                        pltpu.VMEM((2, pb * X_ROWS, LANES), F32),
                        pltpu.VMEM((1, pb), I32),
                        pltpu.SMEM((1, pb), I32),
                        pltpu.SemaphoreType.DMA((2,)),
                        pltpu.SemaphoreType.DMA(())],
        input_output_aliases={2: 2},
        compiler_params=_cparams(("arbitrary", "arbitrary")),
        name="plan",
    )(route, x1, jnp.zeros((n_sorted * X_ROWS, LANES), F32))

    w_in = e_w_in.astype(BF16).reshape(N_EGROUPS, epg, d, ff2)
    w_out = e_w_out.astype(BF16).reshape(N_EGROUPS, epg, ff2 // 2, d)
    ys = pl.pallas_call(
        _moe_kernel,
        grid_spec=pltpu.PrefetchScalarGridSpec(
            num_scalar_prefetch=2,
            grid=(nt,),
            in_specs=[pl.BlockSpec((MOE_TILE * X_ROWS, LANES), lambda i, tc, nu: (i, 0)),
                      pl.BlockSpec((None, epg, d, ff2), lambda i, tc, nu: (tc[i] >> 6, 0, 0, 0)),
                      pl.BlockSpec((None, epg, ff2 // 2, d), lambda i, tc, nu: (tc[i] >> 6, 0, 0, 0))],
            out_specs=pl.BlockSpec((MOE_TILE * Y_ROWS, LANES), lambda i, tc, nu: (i, 0))),
        out_shape=jax.ShapeDtypeStruct((n_sorted * Y_ROWS, LANES), F32),
        compiler_params=_cparams(("arbitrary",)),
        name="moe",
    )(tinfo[0, :nt], tinfo[1, :1], xs, w_in, w_out)

    post_args = (pos, ys, x1, p_i, ln_g[None], ln_b[None], ple_w.astype(BF16), ple_gw.astype(BF16), ple_gb[None])
    return pl.pallas_call(
        functools.partial(_post_kernel, alpha=alpha, n_blocks=nb),
        grid=(nb,),
        in_specs=[pl.BlockSpec(memory_space=pl.ANY),
                  pl.BlockSpec(memory_space=pl.ANY),
                  pl.BlockSpec((pb, d), lambda i: (i, 0)),
                  pl.BlockSpec((pb, p_i.shape[1]), lambda i: (i, 0))] + [_full(a.shape) for a in post_args[4:]],
        out_specs=pl.BlockSpec((pb, d), lambda i: (i, 0)),
        out_shape=jax.ShapeDtypeStruct((t, d), F32),
        scratch_shapes=[pltpu.SMEM((2, 1, pb), I32), pltpu.VMEM((2, pb * Y_ROWS, LANES), F32),
                        pltpu.SemaphoreType.DMA((2,)), pltpu.SemaphoreType.DMA((2,))],
        compiler_params=_cparams(("arbitrary",)),
        name="post",
    )(*post_args)


def _router_params(wg, bg, we, be):
    d = wg.shape[0]
    w = jnp.zeros((d, ROUTE_COLS), F32).at[:, :N_EGROUPS].set(wg).at[:, 8:8 + we.shape[1]].set(we)
    b = jnp.zeros((1, ROUTE_COLS), F32).at[0, :N_EGROUPS].set(bg).at[0, 8:8 + be.shape[0]].set(be)
    hi = w.astype(BF16)
    return hi, (w - hi.astype(F32)).astype(BF16), b


def _tile(t, want):
    while t % want:
        want //= 2
    return want


def kernel(x, p, a_w_in, a_b_in, a_vn_g, a_vn_b, a_w_s, a_b_s, a_w_out, b_w_in, b_kv_g, b_w_uk, b_w_uv, b_w_out,
           ln1_g, ln1_b, ln2_g, ln2_b, r_wg, r_bg, r_we, r_be, e_w_in, e_w_out, ple_w, ple_gw, ple_gb):
    batch, seq, d = x.shape
    t = batch * seq
    depth = p.shape[0]
    alpha = (2 * depth) ** 0.25
    chunk = a_w_s.shape[-1]
    xt = x.reshape(t, d)
    pt = p.reshape(depth, t, p.shape[-1])
    for i in range(depth):
        j = i // 2
        router = _router_params(r_wg[i], r_bg[i], r_we[i], r_be[i])
        if i % 2 == 0:
            x1, route = _gmlp_layer(xt, a_w_in[j], a_b_in[j], a_vn_g[j], a_vn_b[j], a_w_s[j], a_b_s[j],
                                    a_w_out[j], ln1_g[i], ln1_b[i], router, alpha, max(chunk, _tile(t, 512)))
        else:
            x1, route = _dsa_layer(xt, batch, b_w_in[j], b_kv_g[j], b_w_uk[j], b_w_uv[j], b_w_out[j],
                                   ln1_g[i], ln1_b[i], router, alpha, _tile(t, 512))
        xt = _moe_and_post(x1, route, pt[i], e_w_in[i], e_w_out[i], ln2_g[i], ln2_b[i],
                           ple_w[i], ple_gw[i], ple_gb[i], alpha)
    return xt.reshape(batch, seq, d)
```

```python
import functools

import jax
import jax.numpy as jnp
from jax import lax
from jax.experimental import pallas as pl
from jax.experimental.pallas import tpu as pltpu

F32, BF16, I32 = jnp.float32, jnp.bfloat16, jnp.int32

IDX_HEADS = 8
IDX_DIM = 64
TOPK_MAX = 256
N_EGROUPS = 4
LN_EPS = 1e-5
RMS_EPS = 1e-6

LANES = 128
Q_BLOCK = 128
KEY_CHUNK = 512
COUNT_SLAB = 1024
COUNT_ACC = 64
LOG2E = 1.4426950408889634
VMEM_LIMIT = 48 * 1024 * 1024

ROUTE_COLS = LANES
ROUTE_ROWS = 8
N_CLASS = 64 * N_EGROUPS
Y_ROWS = 8
X_ROWS = Y_ROWS + 1
PLAN_BLOCK = 256
MOE_TILE = 128
INT_MIN = -(2 ** 31)
NEG_BIG = -1e30


def _cparams(sem, vmem=VMEM_LIMIT):
    return pltpu.CompilerParams(dimension_semantics=sem, vmem_limit_bytes=vmem)


def _full(shape):
    n = len(shape)
    return pl.BlockSpec(shape, lambda *_: (0,) * n)


def _dot(a, b):
    return jnp.dot(a, b, preferred_element_type=F32)


def _dot_t(a, b):
    return lax.dot_general(a, b, (((1,), (1,)), ((), ())), preferred_element_type=F32)


def _layer_norm(x, g, b):
    mu = jnp.mean(x, axis=-1, keepdims=True)
    xc = x - mu
    var = jnp.mean(xc * xc, axis=-1, keepdims=True)
    return xc * lax.rsqrt(var + LN_EPS) * g + b


def _route_rows(lg_t):
    g = [lg_t[i:i + 1] for i in range(N_EGROUPS)]
    gmax = functools.reduce(jnp.maximum, g)
    gsel = jnp.where(g[0] >= gmax, 0.0, jnp.where(g[1] >= gmax, 1.0, jnp.where(g[2] >= gmax, 2.0, 3.0)))
    den = functools.reduce(lambda a, b: a + b, [jnp.exp(gi - gmax) for gi in g])
    p_g = 1.0 / den
    el = lg_t[8:16]
    for gi in range(1, N_EGROUPS):
        el = jnp.where(gsel == float(gi), lg_t[8 + 8 * gi:16 + 8 * gi], el)
    eidx = lax.broadcasted_iota(I32, el.shape, 0).astype(F32)
    m1 = jnp.max(el, axis=0, keepdims=True)
    i1 = jnp.min(jnp.where(el == m1, eidx, 8.0), axis=0, keepdims=True)
    el2 = jnp.where(eidx == i1, -jnp.inf, el)
    m2 = jnp.max(el2, axis=0, keepdims=True)
    i2 = jnp.min(jnp.where(el2 == m2, eidx, 8.0), axis=0, keepdims=True)
    r = jnp.exp(m2 - m1)
    inv = 1.0 / (1.0 + r)
    gate1 = p_g * inv
    gate2 = p_g * r * inv
    first_lo = i1 < i2
    lo = jnp.minimum(i1, i2)
    hi = jnp.maximum(i1, i2)
    cls = gsel * 64.0 + lo * 8.0 + hi
    return cls, jnp.where(first_lo, gate1, gate2), jnp.where(first_lo, gate2, gate1)


def _norm_and_route(y, ln_g, ln_b, wr_hi, wr_lo, br, x1_ref, route_ref):
    tm, _ = y.shape
    x1 = _layer_norm(y, ln_g, ln_b)
    x1_ref[...] = x1
    x_hi = x1.astype(BF16)
    x_lo = (x1 - x_hi.astype(F32)).astype(BF16)
    lg = _dot(x_hi, wr_hi) + _dot(x_lo, wr_hi) + _dot(x_hi, wr_lo) + br
    cls, g_lo, g_hi = _route_rows(lg.T)
    row = lax.broadcasted_iota(I32, (ROUTE_ROWS, tm), 0)
    route_ref[...] = jnp.where(row == 0, cls, jnp.where(row == 1, g_lo, jnp.where(row == 2, g_hi, 0.0)))


def _mixer_out_specs(tm, d):
    return [pl.BlockSpec((tm, d), lambda i: (i, 0)), pl.BlockSpec((ROUTE_ROWS, tm), lambda i: (0, i))]


def _mixer_out_shape(t, d):
    return [jax.ShapeDtypeStruct((t, d), F32), jax.ShapeDtypeStruct((ROUTE_ROWS, t), F32)]


def _gmlp_kernel(x_ref, w_in_ref, b_in_ref, vn_g_ref, vn_b_ref, w_s_ref, b_st_ref, w_out_ref,
                 ln_g_ref, ln_b_ref, wr_hi_ref, wr_lo_ref, br_ref, x1_ref, route_ref, gated_ref, *, alpha):
    x = x_ref[...]
    tm, _ = x.shape
    groups, chunk, _ = w_s_ref.shape
    z = _dot(x.astype(BF16), w_in_ref[...]) + b_in_ref[...]
    z = 0.5 * z * (1.0 + lax.erf(z * (2.0 ** -0.5)))
    half = z.shape[1] // 2
    gd = half // groups
    u = z[:, :half]
    v = _layer_norm(z[:, half:], vn_g_ref[...], vn_b_ref[...]).astype(BF16)
    r = lax.broadcasted_iota(I32, (chunk, chunk), 0)
    c = lax.broadcasted_iota(I32, (chunk, chunk), 1)
    causal = r >= c
    for g in range(groups):
        w_c = jnp.where(causal, w_s_ref[g], 0.0).astype(BF16)
        bias = b_st_ref[:, g:g + 1]
        for ci in range(tm // chunk):
            rows = slice(ci * chunk, (ci + 1) * chunk)
            cols = slice(g * gd, (g + 1) * gd)
            s = _dot(w_c, v[rows, cols]) + bias
            gated_ref[rows, cols] = (u[rows, cols] * s).astype(BF16)
    mix = _dot(gated_ref[...], w_out_ref[...])
    _norm_and_route(alpha * x + mix, ln_g_ref[...], ln_b_ref[...], wr_hi_ref[...], wr_lo_ref[...],
                    br_ref[...], x1_ref, route_ref)


def _gmlp_layer(xt, w_in, b_in, vn_g, vn_b, w_s, b_s, w_out, ln_g, ln_b, router, alpha, tm):
    t, d = xt.shape
    half = w_out.shape[0]
    wr_hi, wr_lo, br = router
    args = (xt, w_in.astype(BF16), b_in[None], vn_g[None], vn_b[None], w_s, b_s.T, w_out.astype(BF16),
            ln_g[None], ln_b[None], wr_hi, wr_lo, br)
    in_specs = [pl.BlockSpec((tm, d), lambda i: (i, 0))] + [_full(a.shape) for a in args[1:]]
    return pl.pallas_call(
        functools.partial(_gmlp_kernel, alpha=alpha),
        grid=(t // tm,),
        in_specs=in_specs,
        out_specs=_mixer_out_specs(tm, d),
        out_shape=_mixer_out_shape(t, d),
        scratch_shapes=[pltpu.VMEM((tm, half), BF16)],
        compiler_params=_cparams(("parallel",)),
        name="gmlp",
    )(*args)


def _dsa_proj_kernel(x_ref, wq_ref, wc_ref, wqi_ref, wkw_ref, kvg_ref, wuk_t_ref,
                     qlat_ref, ckv_ref, ckv_t_ref, qidx_ref, kidx_ref, wi_t_ref, *, scale_q, scale_w):
    xb = x_ref[...].astype(BF16)
    nq = qlat_ref.shape[0]
    kc = ckv_t_ref.shape[2]
    heads, hd, _ = wuk_t_ref.shape
    q = _dot(xb, wq_ref[...])
    for h in range(heads):
        ql = (_dot(q[:, h * hd:(h + 1) * hd].astype(BF16), wuk_t_ref[h]) * scale_q).astype(BF16)
        for b in range(nq):
            qlat_ref[b, h * Q_BLOCK:(h + 1) * Q_BLOCK, :] = ql[b * Q_BLOCK:(b + 1) * Q_BLOCK]
    c = _dot(xb, wc_ref[...])
    ms = jnp.mean(c * c, axis=-1, keepdims=True)
    c = c * lax.rsqrt(ms + RMS_EPS) * kvg_ref[...]
    ckv_ref[...] = c.astype(BF16)
    for b in range(ckv_t_ref.shape[0]):
        ckv_t_ref[b] = c[b * kc:(b + 1) * kc].T.astype(BF16)
    qi = _dot(xb, wqi_ref[...]).astype(BF16)
    for h in range(IDX_HEADS):
        for b in range(nq):
            qidx_ref[b, h * Q_BLOCK:(h + 1) * Q_BLOCK, :] = qi[b * Q_BLOCK:(b + 1) * Q_BLOCK,
                                                               h * IDX_DIM:(h + 1) * IDX_DIM]
    kw = _dot(xb, wkw_ref[...])
    kidx_ref[...] = kw[:, :IDX_DIM].astype(BF16)
    wi_t_ref[...] = kw.T[IDX_DIM:IDX_DIM + IDX_HEADS] * scale_w


def _dsa_attn_kernel(qidx_ref, wi_t_ref, qlat_ref, kidx_ref, ckv_ref, ckv_t_ref, o_ref,
                     key_ref, acc_ref, m_ref, l_ref, a_ref, p_ref, *, k_sel, heads):
    qb = pl.program_id(1)
    kc = ckv_t_ref.shape[2]
    slab = key_ref.shape[1]
    cps = slab // kc
    nq = Q_BLOCK
    pair = 2 * nq
    n_pairs = heads // 2
    n_chunks = (qb * nq + nq + kc - 1) // kc
    n_slabs = (n_chunks + cps - 1) // cps

    def key_chunk(j):
        return key_ref.at[j // cps, pl.ds(pl.multiple_of((j % cps) * kc, kc), kc), :]

    q_pos = qb * nq + lax.broadcasted_iota(I32, (kc, nq), 1)
    k_off = lax.broadcasted_iota(I32, (kc, nq), 0)

    def pad_chunk(j, carry):
        key_chunk(j)[...] = jnp.full((kc, nq), jnp.nan, F32)
        return carry

    lax.fori_loop(n_chunks, n_slabs * cps, pad_chunk, 0)

    def score_chunk(j, carry):
        kch = kidx_ref[pl.ds(pl.multiple_of(j * kc, kc), kc), :]
        tot = None
        for p in range(IDX_HEADS // 2):
            sc = _dot_t(kch, qidx_ref[p * pair:(p + 1) * pair, :])
            for hh in range(2):
                h = 2 * p + hh
                r = jnp.maximum(sc[:, hh * nq:(hh + 1) * nq], 0.0) * wi_t_ref[h:h + 1, :]
                tot = r if tot is None else tot + r
        key_chunk(j)[...] = jnp.where(k_off + j * kc <= q_pos, tot, jnp.nan)
        return carry

    lax.fori_loop(0, n_chunks, score_chunk, 0)

    def count(t, strict=False):
        tb = jnp.broadcast_to(t, (slab, nq))

        def body(s, acc):
            blk = key_ref[s]
            hit = jnp.where(blk > tb if strict else blk >= tb, 1.0, 0.0)
            return acc + jnp.sum(hit.reshape(slab // COUNT_ACC, COUNT_ACC, nq), axis=0)

        acc = lax.fori_loop(0, n_slabs, body, jnp.zeros((COUNT_ACC, nq), F32))
        return jnp.sum(acc, axis=0, keepdims=True)

    def decode(code):
        return lax.bitcast_convert_type(code ^ ((code >> 31) & 0x7FFFFFFF), F32)

    def bit_step(b, state):
        c, n_at = state
        t = c + jnp.left_shift(jnp.int32(1), 31 - b)
        n_t = count(decode(t))
        take = n_t >= float(k_sel)
        return jnp.where(take, t, c), jnp.where(take, n_t, n_at)

    c_sel, n_ge = lax.fori_loop(0, 32, bit_step, (jnp.full((1, nq), INT_MIN, I32), jnp.zeros((1, nq), F32)))
    c_sel = jnp.where(c_sel == INT_MIN, -jnp.inf, decode(c_sel))

    has_tie = jnp.max(n_ge) > float(k_sel)

    @pl.when(has_tie)
    def _():
        need = float(k_sel) - count(c_sel, strict=True)
        tie_col = n_ge > float(k_sel)
        cb = jnp.broadcast_to(c_sel, (kc, nq))
        lower = (lax.broadcasted_iota(I32, (kc, kc), 1) < lax.broadcasted_iota(I32, (kc, kc), 0))
        lower = jnp.where(lower, 1.0, 0.0).astype(BF16)

        def fix(j, seen):
            blk = key_chunk(j)[...]
            eq = blk == cb
            eq_f = jnp.where(eq, 1.0, 0.0)
            rank = seen + _dot(lower, eq_f.astype(BF16))
            drop = eq & tie_col & (rank >= need)
            key_chunk(j)[...] = jnp.where(drop, jnp.nan, blk)
            return seen + jnp.sum(eq_f, axis=0, keepdims=True)

        lax.fori_loop(0, n_chunks, fix, jnp.zeros((1, nq), F32))

    thr = jnp.broadcast_to(c_sel, (kc, nq))
    m_ref[...] = jnp.full(m_ref.shape, NEG_BIG, F32)
    l_ref[...] = jnp.zeros(l_ref.shape, F32)
    acc_ref[...] = jnp.zeros(acc_ref.shape, F32)

    def softmax_chunk(j):
        ck = ckv_ref[pl.ds(pl.multiple_of(j * kc, kc), kc), :]
        bias = jnp.where(key_chunk(j)[...] >= thr, 0.0, NEG_BIG)
        bias = jnp.concatenate([bias, bias], axis=1)
        slot = j % 2
        for p in range(n_pairs):
            lg = _dot_t(ck, qlat_ref[p * pair:(p + 1) * pair, :]) + bias
            m_old = m_ref[p]
            m_new = jnp.maximum(m_old, jnp.max(lg, axis=0, keepdims=True))
            pr = jnp.exp2(lg - m_new)
            alpha = jnp.exp2(m_old - m_new)
            l_ref[p] = alpha * l_ref[p] + jnp.sum(pr, axis=0, keepdims=True)
            m_ref[p] = m_new
            a_ref[slot, p] = alpha
            p_ref[slot, p] = pr.astype(BF16)

    def accumulate_chunk(j):
        ck_t = ckv_t_ref[j]
        slot = j % 2
        for p in range(n_pairs):
            acc_ref[p] = a_ref[slot, p] * acc_ref[p] + _dot(ck_t, p_ref[slot, p])

    def att_step(j, carry):
        accumulate_chunk(j - 1)
        softmax_chunk(j)
        return carry

    softmax_chunk(0)
    lax.fori_loop(1, n_chunks, att_step, 0)
    accumulate_chunk(n_chunks - 1)
    for p in range(n_pairs):
        o_ref[p * pair:(p + 1) * pair, :] = (acc_ref[p] * (1.0 / l_ref[p])).T.astype(o_ref.dtype)


def _dsa_out_kernel(o_ref, x_ref, wuv_ref, wout_ref, ln_g_ref, ln_b_ref, wr_hi_ref, wr_lo_ref, br_ref,
                    x1_ref, route_ref, o2_ref, *, alpha):
    nq = o_ref.shape[0]
    heads, _, hd = wuv_ref.shape
    for b in range(nq):
        for h in range(heads):
            oh = o_ref[b, h * Q_BLOCK:(h + 1) * Q_BLOCK, :]
            o2_ref[b * Q_BLOCK:(b + 1) * Q_BLOCK, h * hd:(h + 1) * hd] = _dot(oh, wuv_ref[h]).astype(BF16)
    mix = _dot(o2_ref[...], wout_ref[...])
    _norm_and_route(alpha * x_ref[...] + mix, ln_g_ref[...], ln_b_ref[...], wr_hi_ref[...], wr_lo_ref[...],
                    br_ref[...], x1_ref, route_ref)


def _dsa_layer(xt, batch, w_in, kv_g, w_uk, w_uv, w_out, ln_g, ln_b, router, alpha, tm):
    t, d = xt.shape
    seq = t // batch
    heads, lat, hd = w_uk.shape
    qw = heads * hd
    nqb = t // Q_BLOCK
    nq = tm // Q_BLOCK
    k_sel = min(TOPK_MAX, seq // 4)
    kc = min(KEY_CHUNK, seq)
    slab = min(COUNT_SLAB, seq)
    w_in = w_in.astype(BF16)
    o1, o2, o3 = qw, qw + lat, qw + lat + IDX_HEADS * IDX_DIM
    wkw = jnp.pad(w_in[:, o3:], ((0, 0), (0, LANES - (w_in.shape[1] - o3))))
    proj_args = (xt, w_in[:, :o1], w_in[:, o1:o2], w_in[:, o2:o3], wkw, kv_g[None],
                 jnp.swapaxes(w_uk, 1, 2).astype(BF16))
    qlat, ckv, ckv_t, qidx, kidx, wi_t = pl.pallas_call(
        functools.partial(_dsa_proj_kernel, scale_q=hd ** -0.5 * LOG2E, scale_w=(IDX_HEADS * IDX_DIM) ** -0.5),
        grid=(t // tm,),
        in_specs=[pl.BlockSpec((tm, d), lambda i: (i, 0))] + [_full(a.shape) for a in proj_args[1:]],
        out_specs=[pl.BlockSpec((nq, heads * Q_BLOCK, lat), lambda i: (i, 0, 0)),
                   pl.BlockSpec((tm, lat), lambda i: (i, 0)),
                   pl.BlockSpec((tm // kc, lat, kc), lambda i: (i, 0, 0)),
                   pl.BlockSpec((nq, IDX_HEADS * Q_BLOCK, IDX_DIM), lambda i: (i, 0, 0)),
                   pl.BlockSpec((tm, IDX_DIM), lambda i: (i, 0)),
                   pl.BlockSpec((IDX_HEADS, tm), lambda i: (0, i))],
        out_shape=[jax.ShapeDtypeStruct((nqb, heads * Q_BLOCK, lat), BF16),
                   jax.ShapeDtypeStruct((t, lat), BF16),
                   jax.ShapeDtypeStruct((t // kc, lat, kc), BF16),
                   jax.ShapeDtypeStruct((nqb, IDX_HEADS * Q_BLOCK, IDX_DIM), BF16),
                   jax.ShapeDtypeStruct((t, IDX_DIM), BF16),
                   jax.ShapeDtypeStruct((IDX_HEADS, t), F32)],
        compiler_params=_cparams(("parallel",)),
        name="dsa_proj",
    )(*proj_args)

    nq_seq = seq // Q_BLOCK
    n_kc = seq // kc
    rows = heads * Q_BLOCK
    o = pl.pallas_call(
        functools.partial(_dsa_attn_kernel, k_sel=k_sel, heads=heads),
        grid=(batch, nq_seq),
        in_specs=[pl.BlockSpec((None, IDX_HEADS * Q_BLOCK, IDX_DIM), lambda b, q: (b * nq_seq + q, 0, 0)),
                  pl.BlockSpec((IDX_HEADS, Q_BLOCK), lambda b, q: (0, b * nq_seq + q)),
                  pl.BlockSpec((None, rows, lat), lambda b, q: (b * nq_seq + q, 0, 0)),
                  pl.BlockSpec((None, seq, IDX_DIM), lambda b, q: (b, 0, 0)),
                  pl.BlockSpec((None, seq, lat), lambda b, q: (b, 0, 0)),
                  pl.BlockSpec((None, n_kc, lat, kc), lambda b, q: (b, 0, 0, 0))],
        out_specs=pl.BlockSpec((None, rows, lat), lambda b, q: (b * nq_seq + q, 0, 0)),
        out_shape=jax.ShapeDtypeStruct((nqb, rows, lat), BF16),
        scratch_shapes=[pltpu.VMEM((seq // slab, slab, Q_BLOCK), F32),
                        pltpu.VMEM((heads // 2, lat, 2 * Q_BLOCK), F32),
                        pltpu.VMEM((heads // 2, 1, 2 * Q_BLOCK), F32),
                        pltpu.VMEM((heads // 2, 1, 2 * Q_BLOCK), F32),
                        pltpu.VMEM((2, heads // 2, 1, 2 * Q_BLOCK), F32),
                        pltpu.VMEM((2, heads // 2, kc, 2 * Q_BLOCK), BF16)],
        compiler_params=_cparams(("parallel", "arbitrary")),
        name="dsa_attn",
    )(qidx, wi_t, qlat, kidx.reshape(batch, seq, IDX_DIM), ckv.reshape(batch, seq, lat),
      ckv_t.reshape(batch, n_kc, lat, kc))

    wr_hi, wr_lo, br = router
    out_args = (o, xt, w_uv.astype(BF16), w_out.astype(BF16), ln_g[None], ln_b[None], wr_hi, wr_lo, br)
    return pl.pallas_call(
        functools.partial(_dsa_out_kernel, alpha=alpha),
        grid=(t // tm,),
        in_specs=[pl.BlockSpec((nq, rows, lat), lambda i: (i, 0, 0)),
                  pl.BlockSpec((tm, d), lambda i: (i, 0))] + [_full(a.shape) for a in out_args[2:]],
        out_specs=_mixer_out_specs(tm, d),
        out_shape=_mixer_out_shape(t, d),
        scratch_shapes=[pltpu.VMEM((tm, qw), BF16)],
        compiler_params=_cparams(("parallel",)),
        name="dsa_out",
    )(*out_args)


DMA_UNROLL = 8


def _start_token_moves(n, copy_of):
    def body(g, carry):
        for u in range(DMA_UNROLL):
            copy_of(g * DMA_UNROLL + u).start(priority=u % 2)
        return carry
    lax.fori_loop(0, n // DMA_UNROLL, body, 0)


def _tok(ref, i, rows):
    return ref.at[pl.ds(i * rows, rows)]


def _rows_of(ref, j, n, rows):
    return ref.at[pl.ds(j, n, stride=rows), :]


def _token_gather_step(i, n, idx_hbm, src_hbm, idx_smem, buf, isem, rsem):
    tm = idx_smem.shape[2]
    rows = buf.shape[1] // tm
    slot = i % 2
    nslot = 1 - slot

    def idx_copy(t, s):
        return pltpu.make_async_copy(idx_hbm.at[t], idx_smem.at[s], isem.at[s])

    def start_tiles(s):
        _start_token_moves(tm, lambda r: pltpu.make_async_copy(
            _tok(src_hbm, idx_smem[s, 0, r], rows), _tok(buf.at[s], r, rows), rsem.at[s]))

    @pl.when(i == 0)
    def _():
        first = idx_copy(0, 0)
        first.start()
        first.wait()
        start_tiles(0)

        @pl.when(n > 1)
        def _():
            idx_copy(1, 1).start()

    @pl.when(i + 1 < n)
    def _():
        idx_copy(i + 1, nslot).wait()
        start_tiles(nslot)

    @pl.when(i + 2 < n)
    def _():
        idx_copy(i + 2, slot).start()

    pltpu.make_async_copy(src_hbm.at[pl.ds(0, tm * rows)], buf.at[slot], rsem.at[slot]).wait()


def _plan_kernel(route_ref, x_ref, xs_in_ref, pos_ref, tinfo_ref, xs_ref,
                 hist_ref, off_ref, carry_ref, stage_ref, ppos_vmem, ppos_smem, ssem, psem, *, tile):
    del xs_in_ref
    ps = pl.program_id(0)
    b = pl.program_id(1)
    nb = pl.num_programs(1)
    n_cls, pb = hist_ref.shape
    kt = tinfo_ref.shape[1]
    cls = route_ref[0:1, :].astype(I32)
    cid = lax.broadcasted_iota(I32, (n_cls, pb), 0)
    onehot = jnp.where(cid == cls, 1.0, 0.0)

    @pl.when(ps == 0)
    def _():
        @pl.when(b == 0)
        def _():
            hist_ref[...] = jnp.zeros(hist_ref.shape, F32)
        hist_ref[...] += onehot

    @pl.when((ps == 1) & (b == 0))
    def _():
        counts = jnp.sum(hist_ref[...], axis=1, keepdims=True)
        tiles = jnp.floor((counts + float(tile - 1)) * (1.0 / tile))
        lower = lax.broadcasted_iota(I32, (n_cls, n_cls), 1) < lax.broadcasted_iota(I32, (n_cls, n_cls), 0)
        lower = jnp.where(lower, 1.0, 0.0).astype(BF16)
        off = _dot(lower, jnp.broadcast_to(tiles, (n_cls, LANES)).astype(BF16))[:, :1]
        off_ref[...] = off * float(tile)
        carry_ref[...] = jnp.zeros(carry_ref.shape, F32)
        k = lax.broadcasted_iota(I32, (n_cls, kt), 1).astype(F32)
        cid_f = lax.broadcasted_iota(I32, (n_cls, kt), 0).astype(F32)
        mine = (k >= off) & (k < off + tiles)
        tcls = jnp.sum(jnp.where(mine, cid_f, 0.0), axis=0, keepdims=True)
        n_used = jnp.sum(tiles, axis=0, keepdims=True)
        last = jnp.max(jnp.where(tiles > 0.0, cid_f[:, :1], 0.0), axis=0, keepdims=True)
        tcls = jnp.where(k[:1] >= n_used, last, tcls)
        row = lax.broadcasted_iota(I32, (ROUTE_ROWS, kt), 0)
        tinfo_ref[...] = jnp.where(row == 0, tcls, jnp.where(row == 1, n_used, 0.0)).astype(I32)

    @pl.when(ps == 1)
    def _():
        upper = lax.broadcasted_iota(I32, (pb, pb), 0) < lax.broadcasted_iota(I32, (pb, pb), 1)
        upper = jnp.where(upper, 1.0, 0.0).astype(BF16)
        before = _dot(onehot.astype(BF16), upper)
        ppos = jnp.sum(onehot * (before + carry_ref[...] + off_ref[...]), axis=0, keepdims=True).astype(I32)
        carry_ref[...] += jnp.sum(onehot, axis=1, keepdims=True)
        pos_ref[...] = ppos
        ppos_vmem[...] = ppos
        to_smem = pltpu.make_async_copy(ppos_vmem, ppos_smem, psem)
        to_smem.start()

        slot = b % 2
        stage = stage_ref.at[slot]

        def drain(s):
            pltpu.make_async_copy(stage_ref.at[s], xs_ref.at[pl.ds(0, pb * X_ROWS)], ssem.at[s]).wait()

        @pl.when(b >= 2)
        def _():
            drain(slot)

        for j in range(Y_ROWS):
            _rows_of(stage, j, pb, X_ROWS)[...] = x_ref[:, j * LANES:(j + 1) * LANES]
        record = jnp.concatenate([route_ref[...], jnp.zeros((LANES - ROUTE_ROWS, pb), F32)], axis=0).T
        _rows_of(stage, Y_ROWS, pb, X_ROWS)[...] = record

        to_smem.wait()
        _start_token_moves(pb, lambda r: pltpu.make_async_copy(
            _tok(stage, r, X_ROWS), _tok(xs_ref, ppos_smem[0, r], X_ROWS), ssem.at[slot]))

        @pl.when(b == nb - 1)
        def _():
            drain(slot)

            @pl.when(nb > 1)
            def _():
                drain(1 - slot)


def _moe_kernel(tcls_ref, nused_ref, xs_ref, win_ref, wout_ref, ys_ref):
    i = pl.program_id(0)
    tm = xs_ref.shape[0] // X_ROWS
    ff = wout_ref.shape[1]

    @pl.when(i < nused_ref[0])
    def _():
        cls = tcls_ref[i]
        xb = jnp.concatenate([_rows_of(xs_ref, j, tm, X_ROWS)[...].astype(BF16) for j in range(Y_ROWS)], axis=1)
        record = _rows_of(xs_ref, Y_ROWS, tm, X_ROWS)[...]
        y = None
        for e, gate in (((cls >> 3) & 7, record[:, 1:2]), (cls & 7, record[:, 2:3])):
            h = _dot(xb, win_ref[e])
            a = h[:, :ff]
            act = (a * jax.nn.sigmoid(a) * h[:, ff:] * gate).astype(BF16)
            ye = _dot(act, wout_ref[e])
            y = ye if y is None else y + ye
        for j in range(Y_ROWS):
            _rows_of(ys_ref, j, tm, Y_ROWS)[...] = y[:, j * LANES:(j + 1) * LANES]

    @pl.when(i >= nused_ref[0])
    def _():
        ys_ref[...] = jnp.zeros(ys_ref.shape, F32)


def _post_kernel(pos_ref, ys_ref, x1_ref, p_ref, ln_g_ref, ln_b_ref, pw_ref, gw_ref, gb_ref, out_ref,
                 idx_smem, ybuf, isem, rsem, *, alpha, n_blocks):
    i = pl.program_id(0)
    tm = x1_ref.shape[0]
    _token_gather_step(i, n_blocks, pos_ref, ys_ref, idx_smem, ybuf, isem, rsem)
    yb = ybuf.at[i % 2]
    ffn = jnp.concatenate([_rows_of(yb, j, tm, Y_ROWS)[...] for j in range(Y_ROWS)], axis=1)
    x2 = _layer_norm(alpha * x1_ref[...] + ffn, ln_g_ref[...], ln_b_ref[...])
    gate = jax.nn.sigmoid(_dot(x2.astype(BF16), gw_ref[...]) + gb_ref[...])
    out_ref[...] = x2 + _dot(p_ref[...].astype(BF16), pw_ref[...]) * gate


def _moe_and_post(x1, route, p_i, e_w_in, e_w_out, ln_g, ln_b, ple_w, ple_gw, ple_gb, alpha):
    t, d = x1.shape
    assert d == LANES * Y_ROWS
    n_exp, _, ff2 = e_w_in.shape
    epg = n_exp // N_EGROUPS
    pb = _tile(t, PLAN_BLOCK)
    nb = t // pb
    nt = t // MOE_TILE + N_EGROUPS * (epg * (epg - 1) // 2)
    kt = -(-nt // LANES) * LANES
    n_sorted = nt * MOE_TILE

    pos, tinfo, xs = pl.pallas_call(
        functools.partial(_plan_kernel, tile=MOE_TILE),
        grid=(2, nb),
        in_specs=[pl.BlockSpec((ROUTE_ROWS, pb), lambda ps, b: (0, b)),
                  pl.BlockSpec((pb, d), lambda ps, b: (ps * b, 0)),
                  pl.BlockSpec(memory_space=pl.ANY)],
        out_specs=[pl.BlockSpec((None, 1, pb), lambda ps, b: (ps * b, 0, 0)),
                   pl.BlockSpec((ROUTE_ROWS, kt), lambda ps, b: (0, 0)),
                   pl.BlockSpec(memory_space=pl.ANY)],
        out_shape=[jax.ShapeDtypeStruct((nb, 1, pb), I32),
                   jax.ShapeDtypeStruct((ROUTE_ROWS, kt), I32),
                   jax.ShapeDtypeStruct((n_sorted * X_ROWS, LANES), F32)],
        scratch_shapes=[pltpu.VMEM((N_CLASS, pb), F32),
                        pltpu.VMEM((N_CLASS, 1), F32),
                        pltpu.VMEM((N_CLASS, 1), F32),
                        pltpu.VMEM((2, pb * X_ROWS, LANES), F32),
                        pltpu.VMEM((1, pb), I32),
                        pltpu.SMEM((1, pb), I32),
                        pltpu.SemaphoreType.DMA((2,)),
                        pltpu.SemaphoreType.DMA(())],
        input_output_aliases={2: 2},
        compiler_params=_cparams(("arbitrary", "arbitrary")),
        name="plan",
    )(route, x1, jnp.zeros((n_sorted * X_ROWS, LANES), F32))

    w_in = e_w_in.astype(BF16).reshape(N_EGROUPS, epg, d, ff2)
    w_out = e_w_out.astype(BF16).reshape(N_EGROUPS, epg, ff2 // 2, d)
    ys = pl.pallas_call(
        _moe_kernel,
        grid_spec=pltpu.PrefetchScalarGridSpec(
            num_scalar_prefetch=2,
            grid=(nt,),
            in_specs=[pl.BlockSpec((MOE_TILE * X_ROWS, LANES), lambda i, tc, nu: (i, 0)),
                      pl.BlockSpec((None, epg, d, ff2), lambda i, tc, nu: (tc[i] >> 6, 0, 0, 0)),
                      pl.BlockSpec((None, epg, ff2 // 2, d), lambda i, tc, nu: (tc[i] >> 6, 0, 0, 0))],
            out_specs=pl.BlockSpec((MOE_TILE * Y_ROWS, LANES), lambda i, tc, nu: (i, 0))),
        out_shape=jax.ShapeDtypeStruct((n_sorted * Y_ROWS, LANES), F32),
        compiler_params=_cparams(("arbitrary",)),
        name="moe",
    )(tinfo[0, :nt], tinfo[1, :1], xs, w_in, w_out)

    post_args = (pos, ys, x1, p_i, ln_g[None], ln_b[None], ple_w.astype(BF16), ple_gw.astype(BF16), ple_gb[None])
    return pl.pallas_call(
        functools.partial(_post_kernel, alpha=alpha, n_blocks=nb),
        grid=(nb,),
        in_specs=[pl.BlockSpec(memory_space=pl.ANY),
                  pl.BlockSpec(memory_space=pl.ANY),
                  pl.BlockSpec((pb, d), lambda i: (i, 0)),
                  pl.BlockSpec((pb, p_i.shape[1]), lambda i: (i, 0))] + [_full(a.shape) for a in post_args[4:]],
        out_specs=pl.BlockSpec((pb, d), lambda i: (i, 0)),
        out_shape=jax.ShapeDtypeStruct((t, d), F32),
        scratch_shapes=[pltpu.SMEM((2, 1, pb), I32), pltpu.VMEM((2, pb * Y_ROWS, LANES), F32),
                        pltpu.SemaphoreType.DMA((2,)), pltpu.SemaphoreType.DMA((2,))],
        compiler_params=_cparams(("arbitrary",)),
        name="post",
    )(*post_args)


def _router_params(wg, bg, we, be):
    d = wg.shape[0]
    w = jnp.zeros((d, ROUTE_COLS), F32).at[:, :N_EGROUPS].set(wg).at[:, 8:8 + we.shape[1]].set(we)
    b = jnp.zeros((1, ROUTE_COLS), F32).at[0, :N_EGROUPS].set(bg).at[0, 8:8 + be.shape[0]].set(be)
    hi = w.astype(BF16)
    return hi, (w - hi.astype(F32)).astype(BF16), b


def _tile(t, want):
    while t % want:
        want //= 2
    return want


def kernel(x, p, a_w_in, a_b_in, a_vn_g, a_vn_b, a_w_s, a_b_s, a_w_out, b_w_in, b_kv_g, b_w_uk, b_w_uv, b_w_out,
           ln1_g, ln1_b, ln2_g, ln2_b, r_wg, r_bg, r_we, r_be, e_w_in, e_w_out, ple_w, ple_gw, ple_gb):
    batch, seq, d = x.shape
    t = batch * seq
    depth = p.shape[0]
    alpha = (2 * depth) ** 0.25
    chunk = a_w_s.shape[-1]
    xt = x.reshape(t, d)
    pt = p.reshape(depth, t, p.shape[-1])
    for i in range(depth):
        j = i // 2
        router = _router_params(r_wg[i], r_bg[i], r_we[i], r_be[i])
        if i % 2 == 0:
            x1, route = _gmlp_layer(xt, a_w_in[j], a_b_in[j], a_vn_g[j], a_vn_b[j], a_w_s[j], a_b_s[j],
                                    a_w_out[j], ln1_g[i], ln1_b[i], router, alpha, max(chunk, _tile(t, 512)))
        else:
            x1, route = _dsa_layer(xt, batch, b_w_in[j], b_kv_g[j], b_w_uk[j], b_w_uv[j], b_w_out[j],
                                   ln1_g[i], ln1_b[i], router, alpha, _tile(t, 512))
        xt = _moe_and_post(x1, route, pt[i], e_w_in[i], e_w_out[i], ln2_g[i], ln2_b[i],
                           ple_w[i], ple_gw[i], ple_gb[i], alpha)
    return xt.reshape(batch, seq, d)
```

```python
import functools

import jax
import jax.numpy as jnp
from jax import lax
from jax.experimental import pallas as pl
from jax.experimental.pallas import tpu as pltpu

F32, BF16, I32 = jnp.float32, jnp.bfloat16, jnp.int32

IDX_HEADS = 8
IDX_DIM = 64
TOPK_MAX = 256
N_EGROUPS = 4
LN_EPS = 1e-5
RMS_EPS = 1e-6

LANES = 128
Q_BLOCK = 128
KEY_CHUNK = 512
COUNT_SLAB = 1024
COUNT_ACC = 64
LOG2E = 1.4426950408889634
VMEM_LIMIT = 48 * 1024 * 1024

ROUTE_COLS = LANES
ROUTE_ROWS = 8
N_CLASS = 64 * N_EGROUPS
Y_ROWS = 8
X_ROWS = Y_ROWS + 1
PLAN_BLOCK = 256
MOE_TILE = 128
INT_MIN = -(2 ** 31)
NEG_BIG = -1e30


def _cparams(sem, vmem=VMEM_LIMIT):
    return pltpu.CompilerParams(dimension_semantics=sem, vmem_limit_bytes=vmem)


def _full(shape):
    n = len(shape)
    return pl.BlockSpec(shape, lambda *_: (0,) * n)


def _dot(a, b):
    return jnp.dot(a, b, preferred_element_type=F32)


def _dot_t(a, b):
    return lax.dot_general(a, b, (((1,), (1,)), ((), ())), preferred_element_type=F32)


def _layer_norm(x, g, b):
    mu = jnp.mean(x, axis=-1, keepdims=True)
    xc = x - mu
    var = jnp.mean(xc * xc, axis=-1, keepdims=True)
    return xc * lax.rsqrt(var + LN_EPS) * g + b


def _route_rows(lg_t):
    g = [lg_t[i:i + 1] for i in range(N_EGROUPS)]
    gmax = functools.reduce(jnp.maximum, g)
    gsel = jnp.where(g[0] >= gmax, 0.0, jnp.where(g[1] >= gmax, 1.0, jnp.where(g[2] >= gmax, 2.0, 3.0)))
    den = functools.reduce(lambda a, b: a + b, [jnp.exp(gi - gmax) for gi in g])
    p_g = 1.0 / den
    el = lg_t[8:16]
    for gi in range(1, N_EGROUPS):
        el = jnp.where(gsel == float(gi), lg_t[8 + 8 * gi:16 + 8 * gi], el)
    eidx = lax.broadcasted_iota(I32, el.shape, 0).astype(F32)
    m1 = jnp.max(el, axis=0, keepdims=True)
    i1 = jnp.min(jnp.where(el == m1, eidx, 8.0), axis=0, keepdims=True)
    el2 = jnp.where(eidx == i1, -jnp.inf, el)
    m2 = jnp.max(el2, axis=0, keepdims=True)
    i2 = jnp.min(jnp.where(el2 == m2, eidx, 8.0), axis=0, keepdims=True)
    r = jnp.exp(m2 - m1)
    inv = 1.0 / (1.0 + r)
    gate1 = p_g * inv
    gate2 = p_g * r * inv
    first_lo = i1 < i2
    lo = jnp.minimum(i1, i2)
    hi = jnp.maximum(i1, i2)
    cls = gsel * 64.0 + lo * 8.0 + hi
    return cls, jnp.where(first_lo, gate1, gate2), jnp.where(first_lo, gate2, gate1)


def _norm_and_route(y, ln_g, ln_b, wr_hi, wr_lo, br, x1_ref, route_ref):
    tm, _ = y.shape
    x1 = _layer_norm(y, ln_g, ln_b)
    x1_ref[...] = x1
    x_hi = x1.astype(BF16)
    x_lo = (x1 - x_hi.astype(F32)).astype(BF16)
    lg = _dot(x_hi, wr_hi) + _dot(x_lo, wr_hi) + _dot(x_hi, wr_lo) + br
    cls, g_lo, g_hi = _route_rows(lg.T)
    row = lax.broadcasted_iota(I32, (ROUTE_ROWS, tm), 0)
    route_ref[...] = jnp.where(row == 0, cls, jnp.where(row == 1, g_lo, jnp.where(row == 2, g_hi, 0.0)))


def _mixer_out_specs(tm, d):
    return [pl.BlockSpec((tm, d), lambda i: (i, 0)), pl.BlockSpec((ROUTE_ROWS, tm), lambda i: (0, i))]


def _mixer_out_shape(t, d):
    return [jax.ShapeDtypeStruct((t, d), F32), jax.ShapeDtypeStruct((ROUTE_ROWS, t), F32)]


def _gmlp_kernel(x_ref, w_in_ref, b_in_ref, vn_g_ref, vn_b_ref, w_s_ref, b_st_ref, w_out_ref,
                 ln_g_ref, ln_b_ref, wr_hi_ref, wr_lo_ref, br_ref, x1_ref, route_ref, gated_ref, *, alpha):
    x = x_ref[...]
    tm, _ = x.shape
    groups, chunk, _ = w_s_ref.shape
    z = _dot(x.astype(BF16), w_in_ref[...]) + b_in_ref[...]
    z = 0.5 * z * (1.0 + lax.erf(z * (2.0 ** -0.5)))
    half = z.shape[1] // 2
    gd = half // groups
    u = z[:, :half]
    v = _layer_norm(z[:, half:], vn_g_ref[...], vn_b_ref[...]).astype(BF16)
    r = lax.broadcasted_iota(I32, (chunk, chunk), 0)
    c = lax.broadcasted_iota(I32, (chunk, chunk), 1)
    causal = r >= c
    for g in range(groups):
        w_c = jnp.where(causal, w_s_ref[g], 0.0).astype(BF16)
        bias = b_st_ref[:, g:g + 1]
        for ci in range(tm // chunk):
            rows = slice(ci * chunk, (ci + 1) * chunk)
            cols = slice(g * gd, (g + 1) * gd)
            s = _dot(w_c, v[rows, cols]) + bias
            gated_ref[rows, cols] = (u[rows, cols] * s).astype(BF16)
    mix = _dot(gated_ref[...], w_out_ref[...])
    _norm_and_route(alpha * x + mix, ln_g_ref[...], ln_b_ref[...], wr_hi_ref[...], wr_lo_ref[...],
                    br_ref[...], x1_ref, route_ref)


def _gmlp_layer(xt, w_in, b_in, vn_g, vn_b, w_s, b_s, w_out, ln_g, ln_b, router, alpha, tm):
    t, d = xt.shape
    half = w_out.shape[0]
    wr_hi, wr_lo, br = router
    args = (xt, w_in.astype(BF16), b_in[None], vn_g[None], vn_b[None], w_s, b_s.T, w_out.astype(BF16),
            ln_g[None], ln_b[None], wr_hi, wr_lo, br)
    in_specs = [pl.BlockSpec((tm, d), lambda i: (i, 0))] + [_full(a.shape) for a in args[1:]]
    return pl.pallas_call(
        functools.partial(_gmlp_kernel, alpha=alpha),
        grid=(t // tm,),
        in_specs=in_specs,
        out_specs=_mixer_out_specs(tm, d),
        out_shape=_mixer_out_shape(t, d),
        scratch_shapes=[pltpu.VMEM((tm, half), BF16)],
        compiler_params=_cparams(("parallel",)),
        name="gmlp",
    )(*args)


def _dsa_proj_kernel(x_ref, wq_ref, wc_ref, wqi_ref, wkw_ref, kvg_ref, wuk_t_ref,
                     qlat_ref, ckv_ref, ckv_t_ref, qidx_ref, kidx_ref, wi_t_ref, *, scale_q, scale_w):
    xb = x_ref[...].astype(BF16)
    nq = qlat_ref.shape[0]
    kc = ckv_t_ref.shape[2]
    heads, hd, _ = wuk_t_ref.shape
    q = _dot(xb, wq_ref[...])
    for h in range(heads):
        ql = (_dot(q[:, h * hd:(h + 1) * hd].astype(BF16), wuk_t_ref[h]) * scale_q).astype(BF16)
        for b in range(nq):
            qlat_ref[b, h * Q_BLOCK:(h + 1) * Q_BLOCK, :] = ql[b * Q_BLOCK:(b + 1) * Q_BLOCK]
    c = _dot(xb, wc_ref[...])
    ms = jnp.mean(c * c, axis=-1, keepdims=True)
    c = c * lax.rsqrt(ms + RMS_EPS) * kvg_ref[...]
    ckv_ref[...] = c.astype(BF16)
    for b in range(ckv_t_ref.shape[0]):
        ckv_t_ref[b] = c[b * kc:(b + 1) * kc].T.astype(BF16)
    qi = _dot(xb, wqi_ref[...]).astype(BF16)
    for h in range(IDX_HEADS):
        for b in range(nq):
            qidx_ref[b, h * Q_BLOCK:(h + 1) * Q_BLOCK, :] = qi[b * Q_BLOCK:(b + 1) * Q_BLOCK,
                                                               h * IDX_DIM:(h + 1) * IDX_DIM]
    kw = _dot(xb, wkw_ref[...])
    kidx_ref[...] = kw[:, :IDX_DIM].astype(BF16)
    wi_t_ref[...] = kw.T[IDX_DIM:IDX_DIM + IDX_HEADS] * scale_w


def _dsa_attn_kernel(qidx_ref, wi_t_ref, qlat_ref, kidx_ref, ckv_ref, ckv_t_ref, o_ref,
                     key_ref, acc_ref, m_ref, l_ref, a_ref, p_ref, *, k_sel, heads):
    qb = pl.program_id(1)
    kc = ckv_t_ref.shape[2]
    slab = key_ref.shape[1]
    cps = slab // kc
    nq = Q_BLOCK
    pair = 2 * nq
    n_pairs = heads // 2
    n_chunks = (qb * nq + nq + kc - 1) // kc
    n_slabs = (n_chunks + cps - 1) // cps

    def key_chunk(j):
        return key_ref.at[j // cps, pl.ds(pl.multiple_of((j % cps) * kc, kc), kc), :]

    q_pos = qb * nq + lax.broadcasted_iota(I32, (kc, nq), 1)
    k_off = lax.broadcasted_iota(I32, (kc, nq), 0)

    def pad_chunk(j, carry):
        key_chunk(j)[...] = jnp.full((kc, nq), jnp.nan, F32)
        return carry

    lax.fori_loop(n_chunks, n_slabs * cps, pad_chunk, 0)

    def score_chunk(j, carry):
        kch = kidx_ref[pl.ds(pl.multiple_of(j * kc, kc), kc), :]
        tot = None
        for p in range(IDX_HEADS // 2):
            sc = _dot_t(kch, qidx_ref[p * pair:(p + 1) * pair, :])
            for hh in range(2):
                h = 2 * p + hh
                r = jnp.maximum(sc[:, hh * nq:(hh + 1) * nq], 0.0) * wi_t_ref[h:h + 1, :]
                tot = r if tot is None else tot + r
        key_chunk(j)[...] = jnp.where(k_off + j * kc <= q_pos, tot, jnp.nan)
        return carry

    lax.fori_loop(0, n_chunks, score_chunk, 0)

    def count(t, strict=False):
        tb = jnp.broadcast_to(t, (slab, nq))

        def body(s, acc):
            blk = key_ref[s]
            hit = jnp.where(blk > tb if strict else blk >= tb, 1.0, 0.0)
            return acc + jnp.sum(hit.reshape(slab // COUNT_ACC, COUNT_ACC, nq), axis=0)

        acc = lax.fori_loop(0, n_slabs, body, jnp.zeros((COUNT_ACC, nq), F32))
        return jnp.sum(acc, axis=0, keepdims=True)

    def decode(code):
        return lax.bitcast_convert_type(code ^ ((code >> 31) & 0x7FFFFFFF), F32)

    def bit_step(b, state):
        c, n_at = state
        t = c + jnp.left_shift(jnp.int32(1), 31 - b)
        n_t = count(decode(t))
        take = n_t >= float(k_sel)
        return jnp.where(take, t, c), jnp.where(take, n_t, n_at)

    c_sel, n_ge = lax.fori_loop(0, 32, bit_step, (jnp.full((1, nq), INT_MIN, I32), jnp.zeros((1, nq), F32)))
    c_sel = jnp.where(c_sel == INT_MIN, -jnp.inf, decode(c_sel))

    has_tie = jnp.max(n_ge) > float(k_sel)

    @pl.when(has_tie)
    def _():
        need = float(k_sel) - count(c_sel, strict=True)
        tie_col = n_ge > float(k_sel)
        cb = jnp.broadcast_to(c_sel, (kc, nq))
        lower = (lax.broadcasted_iota(I32, (kc, kc), 1) < lax.broadcasted_iota(I32, (kc, kc), 0))
        lower = jnp.where(lower, 1.0, 0.0).astype(BF16)

        def fix(j, seen):
            blk = key_chunk(j)[...]
            eq = blk == cb
            eq_f = jnp.where(eq, 1.0, 0.0)
            rank = seen + _dot(lower, eq_f.astype(BF16))
            drop = eq & tie_col & (rank >= need)
            key_chunk(j)[...] = jnp.where(drop, jnp.nan, blk)
            return seen + jnp.sum(eq_f, axis=0, keepdims=True)

        lax.fori_loop(0, n_chunks, fix, jnp.zeros((1, nq), F32))

    thr = jnp.broadcast_to(c_sel, (kc, nq))
    m_ref[...] = jnp.full(m_ref.shape, NEG_BIG, F32)
    l_ref[...] = jnp.zeros(l_ref.shape, F32)
    acc_ref[...] = jnp.zeros(acc_ref.shape, F32)

    def softmax_chunk(j):
        ck = ckv_ref[pl.ds(pl.multiple_of(j * kc, kc), kc), :]
        bias = jnp.where(key_chunk(j)[...] >= thr, 0.0, NEG_BIG)
        bias = jnp.concatenate([bias, bias], axis=1)
        slot = j % 2
        for p in range(n_pairs):
            lg = _dot_t(ck, qlat_ref[p * pair:(p + 1) * pair, :]) + bias
            m_old = m_ref[p]
            m_new = jnp.maximum(m_old, jnp.max(lg, axis=0, keepdims=True))
            pr = jnp.exp2(lg - m_new)
            alpha = jnp.exp2(m_old - m_new)
            l_ref[p] = alpha * l_ref[p] + jnp.sum(pr, axis=0, keepdims=True)
            m_ref[p] = m_new
            a_ref[slot, p] = alpha
            p_ref[slot, p] = pr.astype(BF16)

    def accumulate_chunk(j):
        ck_t = ckv_t_ref[j]
        slot = j % 2
        for p in range(n_pairs):
            acc_ref[p] = a_ref[slot, p] * acc_ref[p] + _dot(ck_t, p_ref[slot, p])

    def att_step(j, carry):
        accumulate_chunk(j - 1)
        softmax_chunk(j)
        return carry

    softmax_chunk(0)
    lax.fori_loop(1, n_chunks, att_step, 0)
    accumulate_chunk(n_chunks - 1)
    for p in range(n_pairs):
        o_ref[p * pair:(p + 1) * pair, :] = (acc_ref[p] * (1.0 / l_ref[p])).T.astype(o_ref.dtype)


def _dsa_out_kernel(o_ref, x_ref, wuv_ref, wout_ref, ln_g_ref, ln_b_ref, wr_hi_ref, wr_lo_ref, br_ref,
                    x1_ref, route_ref, o2_ref, *, alpha):
    nq = o_ref.shape[0]
    heads, _, hd = wuv_ref.shape
    for b in range(nq):
        for h in range(heads):
            oh = o_ref[b, h * Q_BLOCK:(h + 1) * Q_BLOCK, :]
            o2_ref[b * Q_BLOCK:(b + 1) * Q_BLOCK, h * hd:(h + 1) * hd] = _dot(oh, wuv_ref[h]).astype(BF16)
    mix = _dot(o2_ref[...], wout_ref[...])
    _norm_and_route(alpha * x_ref[...] + mix, ln_g_ref[...], ln_b_ref[...], wr_hi_ref[...], wr_lo_ref[...],
                    br_ref[...], x1_ref, route_ref)


def _dsa_layer(xt, batch, w_in, kv_g, w_uk, w_uv, w_out, ln_g, ln_b, router, alpha, tm):
    t, d = xt.shape
    seq = t // batch
    heads, lat, hd = w_uk.shape
    qw = heads * hd
    nqb = t // Q_BLOCK
    nq = tm // Q_BLOCK
    k_sel = min(TOPK_MAX, seq // 4)
    kc = min(KEY_CHUNK, seq)
    slab = min(COUNT_SLAB, seq)
    w_in = w_in.astype(BF16)
    o1, o2, o3 = qw, qw + lat, qw + lat + IDX_HEADS * IDX_DIM
    wkw = jnp.pad(w_in[:, o3:], ((0, 0), (0, LANES - (w_in.shape[1] - o3))))
    proj_args = (xt, w_in[:, :o1], w_in[:, o1:o2], w_in[:, o2:o3], wkw, kv_g[None],
                 jnp.swapaxes(w_uk, 1, 2).astype(BF16))
    qlat, ckv, ckv_t, qidx, kidx, wi_t = pl.pallas_call(
        functools.partial(_dsa_proj_kernel, scale_q=hd ** -0.5 * LOG2E, scale_w=(IDX_HEADS * IDX_DIM) ** -0.5),
        grid=(t // tm,),
        in_specs=[pl.BlockSpec((tm, d), lambda i: (i, 0))] + [_full(a.shape) for a in proj_args[1:]],
        out_specs=[pl.BlockSpec((nq, heads * Q_BLOCK, lat), lambda i: (i, 0, 0)),
                   pl.BlockSpec((tm, lat), lambda i: (i, 0)),
                   pl.BlockSpec((tm // kc, lat, kc), lambda i: (i, 0, 0)),
                   pl.BlockSpec((nq, IDX_HEADS * Q_BLOCK, IDX_DIM), lambda i: (i, 0, 0)),
                   pl.BlockSpec((tm, IDX_DIM), lambda i: (i, 0)),
                   pl.BlockSpec((IDX_HEADS, tm), lambda i: (0, i))],
        out_shape=[jax.ShapeDtypeStruct((nqb, heads * Q_BLOCK, lat), BF16),
                   jax.ShapeDtypeStruct((t, lat), BF16),
                   jax.ShapeDtypeStruct((t // kc, lat, kc), BF16),
                   jax.ShapeDtypeStruct((nqb, IDX_HEADS * Q_BLOCK, IDX_DIM), BF16),
                   jax.ShapeDtypeStruct((t, IDX_DIM), BF16),
                   jax.ShapeDtypeStruct((IDX_HEADS, t), F32)],
        compiler_params=_cparams(("parallel",)),
        name="dsa_proj",
    )(*proj_args)

    nq_seq = seq // Q_BLOCK
    n_kc = seq // kc
    rows = heads * Q_BLOCK
    o = pl.pallas_call(
        functools.partial(_dsa_attn_kernel, k_sel=k_sel, heads=heads),
        grid=(batch, nq_seq),
        in_specs=[pl.BlockSpec((None, IDX_HEADS * Q_BLOCK, IDX_DIM), lambda b, q: (b * nq_seq + q, 0, 0)),
                  pl.BlockSpec((IDX_HEADS, Q_BLOCK), lambda b, q: (0, b * nq_seq + q)),
                  pl.BlockSpec((None, rows, lat), lambda b, q: (b * nq_seq + q, 0, 0)),
                  pl.BlockSpec((None, seq, IDX_DIM), lambda b, q: (b, 0, 0)),
                  pl.BlockSpec((None, seq, lat), lambda b, q: (b, 0, 0)),
                  pl.BlockSpec((None, n_kc, lat, kc), lambda b, q: (b, 0, 0, 0))],
        out_specs=pl.BlockSpec((None, rows, lat), lambda b, q: (b * nq_seq + q, 0, 0)),
        out_shape=jax.ShapeDtypeStruct((nqb, rows, lat), BF16),
        scratch_shapes=[pltpu.VMEM((seq // slab, slab, Q_BLOCK), F32),
                        pltpu.VMEM((heads // 2, lat, 2 * Q_BLOCK), F32),
                        pltpu.VMEM((heads // 2, 1, 2 * Q_BLOCK), F32),
                        pltpu.VMEM((heads // 2, 1, 2 * Q_BLOCK), F32),
                        pltpu.VMEM((2, heads // 2, 1, 2 * Q_BLOCK), F32),
                        pltpu.VMEM((2, heads // 2, kc, 2 * Q_BLOCK), BF16)],
        compiler_params=_cparams(("parallel", "arbitrary")),
        name="dsa_attn",
    )(qidx, wi_t, qlat, kidx.reshape(batch, seq, IDX_DIM), ckv.reshape(batch, seq, lat),
      ckv_t.reshape(batch, n_kc, lat, kc))

    wr_hi, wr_lo, br = router
    out_args = (o, xt, w_uv.astype(BF16), w_out.astype(BF16), ln_g[None], ln_b[None], wr_hi, wr_lo, br)
    return pl.pallas_call(
        functools.partial(_dsa_out_kernel, alpha=alpha),
        grid=(t // tm,),
        in_specs=[pl.BlockSpec((nq, rows, lat), lambda i: (i, 0, 0)),
                  pl.BlockSpec((tm, d), lambda i: (i, 0))] + [_full(a.shape) for a in out_args[2:]],
        out_specs=_mixer_out_specs(tm, d),
        out_shape=_mixer_out_shape(t, d),
        scratch_shapes=[pltpu.VMEM((tm, qw), BF16)],
        compiler_params=_cparams(("parallel",)),
        name="dsa_out",
    )(*out_args)


DMA_UNROLL = 8


def _start_token_moves(n, copy_of):
    def body(g, carry):
        for u in range(DMA_UNROLL):
            copy_of(g * DMA_UNROLL + u).start(priority=u % 2)
        return carry
    lax.fori_loop(0, n // DMA_UNROLL, body, 0)


def _tok(ref, i, rows):
    return ref.at[pl.ds(i * rows, rows)]


def _rows_of(ref, j, n, rows):
    return ref.at[pl.ds(j, n, stride=rows), :]


class _TokenGather:
    def __init__(self, i, n, idx_hbm, src_hbm, idx_smem, buf, isem, rsem):
        self.i, self.n = i, n
        self.idx_hbm, self.src_hbm, self.idx_smem, self.buf, self.isem, self.rsem = (
            idx_hbm, src_hbm, idx_smem, buf, isem, rsem)
        self.tm = idx_smem.shape[2]
        self.rows = buf.shape[1] // self.tm
        self.slot = i % 2

    def _idx_copy(self, blk, s):
        return pltpu.make_async_copy(self.idx_hbm.at[blk], self.idx_smem.at[s], self.isem.at[s])

    def _token_copy(self, s, r):
        return pltpu.make_async_copy(_tok(self.src_hbm, self.idx_smem[s, 0, r], self.rows),
                                     _tok(self.buf.at[s], r, self.rows), self.rsem.at[s])

    def _wait_tokens(self, s):
        pltpu.make_async_copy(self.src_hbm.at[pl.ds(0, self.tm * self.rows)], self.buf.at[s], self.rsem.at[s]).wait()

    def arrive(self):
        @pl.when(self.i == 0)
        def _():
            first = self._idx_copy(0, 0)
            first.start()
            first.wait()
            _start_token_moves(self.tm, lambda r: self._token_copy(0, r))
            self._idx_copy(min(1, self.n - 1), 1).start()

        self._idx_copy(jnp.minimum(self.i + 1, self.n - 1), 1 - self.slot).wait()
        self._wait_tokens(self.slot)

    def prefetch(self):
        nslot = 1 - self.slot
        for r in range(self.tm):
            self._token_copy(nslot, r).start(priority=r % 2)
        self._idx_copy(jnp.minimum(self.i + 2, self.n - 1), self.slot).start()

    def finish(self):
        @pl.when(self.i == self.n - 1)
        def _():
            self._wait_tokens(1 - self.slot)
            self._idx_copy(self.n - 1, self.slot).wait()


def _plan_kernel(route_ref, x_ref, pos_ref, tinfo_ref, xs_ref,
                 hist_ref, off_ref, carry_ref, stage_ref, ppos_vmem, ppos_smem, fill_vmem, fill_smem, zero_ref,
                 ssem, psem, zsem, *, tile):
    ps = pl.program_id(0)
    b = pl.program_id(1)
    nb = pl.num_programs(1)
    n_cls, pb = hist_ref.shape
    kt = tinfo_ref.shape[1]
    cls = route_ref[0:1, :].astype(I32)
    cid = lax.broadcasted_iota(I32, (n_cls, pb), 0)
    onehot = jnp.where(cid == cls, 1.0, 0.0)

    def zero_fill(start):
        def piece(tok0, n_tok):
            cp = pltpu.make_async_copy(zero_ref.at[pl.ds(0, n_tok * X_ROWS)],
                                       xs_ref.at[pl.ds(tok0 * X_ROWS, n_tok * X_ROWS)], zsem)
            cp.start() if start else cp.wait()

        def per_class(c, carry):
            first, n = fill_smem[0, c], fill_smem[1, c]
            bit = tile // 2
            while bit:
                @pl.when((n & bit) != 0)
                def _(bit=bit):
                    piece(first + (n & -(2 * bit)), bit)
                bit //= 2
            return carry

        lax.fori_loop(0, n_cls, per_class, 0)

        def per_tile(k, carry):
            piece(k * tile, tile)
            return carry

        lax.fori_loop(fill_smem[2, 0], xs_ref.shape[0] // (tile * X_ROWS), per_tile, 0)

    @pl.when(ps == 0)
    def _():
        @pl.when(b == 0)
        def _():
            hist_ref[...] = jnp.zeros(hist_ref.shape, F32)
            zero_ref[...] = jnp.zeros(zero_ref.shape, F32)
        hist_ref[...] += onehot

    @pl.when((ps == 1) & (b == 0))
    def _():
        counts = jnp.sum(hist_ref[...], axis=1, keepdims=True)
        tiles = jnp.floor((counts + float(tile - 1)) * (1.0 / tile))
        lower = lax.broadcasted_iota(I32, (n_cls, n_cls), 1) < lax.broadcasted_iota(I32, (n_cls, n_cls), 0)
        lower = jnp.where(lower, 1.0, 0.0).astype(BF16)
        off = _dot(lower, jnp.broadcast_to(tiles, (n_cls, LANES)).astype(BF16))[:, :1]
        off_ref[...] = off * float(tile)
        carry_ref[...] = jnp.zeros(carry_ref.shape, F32)
        k = lax.broadcasted_iota(I32, (n_cls, kt), 1).astype(F32)
        cid_f = lax.broadcasted_iota(I32, (n_cls, kt), 0).astype(F32)
        mine = (k >= off) & (k < off + tiles)
        tcls = jnp.sum(jnp.where(mine, cid_f, 0.0), axis=0, keepdims=True)
        n_used = jnp.sum(tiles, axis=0, keepdims=True)
        last = jnp.max(jnp.where(tiles > 0.0, cid_f[:, :1], 0.0), axis=0, keepdims=True)
        tcls = jnp.where(k[:1] >= n_used, last, tcls)
        row = lax.broadcasted_iota(I32, (ROUTE_ROWS, kt), 0)
        tinfo_ref[...] = jnp.where(row == 0, tcls, jnp.where(row == 1, n_used, 0.0)).astype(I32)
        as_row = lambda col: jnp.broadcast_to(col, (n_cls, LANES)).T[0:1]
        frow = lax.broadcasted_iota(I32, fill_vmem.shape, 0)
        fill_vmem[...] = jnp.where(frow == 0, as_row(off * float(tile) + counts),
                                   jnp.where(frow == 1, as_row(tiles * float(tile) - counts), n_used)).astype(I32)
        to_fill = pltpu.make_async_copy(fill_vmem, fill_smem, psem)
        to_fill.start()
        to_fill.wait()
        zero_fill(start=True)

    @pl.when(ps == 1)
    def _():
        upper = lax.broadcasted_iota(I32, (pb, pb), 0) < lax.broadcasted_iota(I32, (pb, pb), 1)
        upper = jnp.where(upper, 1.0, 0.0).astype(BF16)
        before = _dot(onehot.astype(BF16), upper)
        ppos = jnp.sum(onehot * (before + carry_ref[...] + off_ref[...]), axis=0, keepdims=True).astype(I32)
        carry_ref[...] += jnp.sum(onehot, axis=1, keepdims=True)
        pos_ref[...] = ppos
        ppos_vmem[...] = ppos
        to_smem = pltpu.make_async_copy(ppos_vmem, ppos_smem, psem)
        to_smem.start()

        slot = b % 2
        stage = stage_ref.at[slot]

        def drain(s):
            pltpu.make_async_copy(stage_ref.at[s], xs_ref.at[pl.ds(0, pb * X_ROWS)], ssem.at[s]).wait()

        @pl.when(b >= 2)
        def _():
            drain(slot)

        for j in range(Y_ROWS):
            _rows_of(stage, j, pb, X_ROWS)[...] = x_ref[:, j * LANES:(j + 1) * LANES]
        record = jnp.concatenate([route_ref[...], jnp.zeros((LANES - ROUTE_ROWS, pb), F32)], axis=0).T
        _rows_of(stage, Y_ROWS, pb, X_ROWS)[...] = record

        to_smem.wait()
        _start_token_moves(pb, lambda r: pltpu.make_async_copy(
            _tok(stage, r, X_ROWS), _tok(xs_ref, ppos_smem[0, r], X_ROWS), ssem.at[slot]))

        @pl.when(b == nb - 1)
        def _():
            drain(slot)

            @pl.when(nb > 1)
            def _():
                drain(1 - slot)

            zero_fill(start=False)


def _moe_kernel(tcls_ref, nused_ref, xs_ref, win_ref, wout_ref, ys_ref):
    i = pl.program_id(0)
    tm = xs_ref.shape[0] // X_ROWS
    ff = wout_ref.shape[1]

    @pl.when(i < nused_ref[0])
    def _():
        cls = tcls_ref[i]
        xb = jnp.concatenate([_rows_of(xs_ref, j, tm, X_ROWS)[...].astype(BF16) for j in range(Y_ROWS)], axis=1)
        record = _rows_of(xs_ref, Y_ROWS, tm, X_ROWS)[...]
        y = None
        for e, gate in (((cls >> 3) & 7, record[:, 1:2]), (cls & 7, record[:, 2:3])):
            h = _dot(xb, win_ref[e])
            a = h[:, :ff]
            act = (a * jax.nn.sigmoid(a) * h[:, ff:] * gate).astype(BF16)
            ye = _dot(act, wout_ref[e])
            y = ye if y is None else y + ye
        for j in range(Y_ROWS):
            _rows_of(ys_ref, j, tm, Y_ROWS)[...] = y[:, j * LANES:(j + 1) * LANES]

    @pl.when(i >= nused_ref[0])
    def _():
        ys_ref[...] = jnp.zeros(ys_ref.shape, F32)


def _post_kernel(pos_ref, ys_ref, x1_ref, p_ref, ln_g_ref, ln_b_ref, pw_ref, gw_ref, gb_ref, out_ref,
                 idx_smem, ybuf, isem, rsem, *, alpha, n_blocks):
    i = pl.program_id(0)
    tm = x1_ref.shape[0]
    gather = _TokenGather(i, n_blocks, pos_ref, ys_ref, idx_smem, ybuf, isem, rsem)
    gather.arrive()
    gather.prefetch()
    yb = ybuf.at[i % 2]
    ffn = jnp.concatenate([_rows_of(yb, j, tm, Y_ROWS)[...] for j in range(Y_ROWS)], axis=1)
    x2 = _layer_norm(alpha * x1_ref[...] + ffn, ln_g_ref[...], ln_b_ref[...])
    gate = jax.nn.sigmoid(_dot(x2.astype(BF16), gw_ref[...]) + gb_ref[...])
    out_ref[...] = x2 + _dot(p_ref[...].astype(BF16), pw_ref[...]) * gate
    gather.finish()


def _moe_and_post(x1, route, p_i, e_w_in, e_w_out, ln_g, ln_b, ple_w, ple_gw, ple_gb, alpha):
    t, d = x1.shape
    assert d == LANES * Y_ROWS
    n_exp, _, ff2 = e_w_in.shape
    epg = n_exp // N_EGROUPS
    pb = _tile(t, PLAN_BLOCK)
    nb = t // pb
    nt = t // MOE_TILE + N_EGROUPS * (epg * (epg - 1) // 2)
    kt = -(-nt // LANES) * LANES
    n_sorted = nt * MOE_TILE

    pos, tinfo, xs = pl.pallas_call(
        functools.partial(_plan_kernel, tile=MOE_TILE),
        grid=(2, nb),
        in_specs=[pl.BlockSpec((ROUTE_ROWS, pb), lambda ps, b: (0, b)),
                  pl.BlockSpec((pb, d), lambda ps, b: (ps * b, 0))],
        out_specs=[pl.BlockSpec((None, 1, pb), lambda ps, b: (ps * b, 0, 0)),
                   pl.BlockSpec((ROUTE_ROWS, kt), lambda ps, b: (0, 0)),
                   pl.BlockSpec(memory_space=pl.ANY)],
        out_shape=[jax.ShapeDtypeStruct((nb, 1, pb), I32),
                   jax.ShapeDtypeStruct((ROUTE_ROWS, kt), I32),
                   jax.ShapeDtypeStruct((n_sorted * X_ROWS, LANES), F32)],
        scratch_shapes=[pltpu.VMEM((N_CLASS, pb), F32),
                        pltpu.VMEM((N_CLASS, 1), F32),
                        pltpu.VMEM((N_CLASS, 1), F32),
                        pltpu.VMEM((2, pb * X_ROWS, LANES), F32),
                        pltpu.VMEM((1, pb), I32),
                        pltpu.SMEM((1, pb), I32),
                        pltpu.VMEM((ROUTE_ROWS, N_CLASS), I32),
                        pltpu.SMEM((ROUTE_ROWS, N_CLASS), I32),
                        pltpu.VMEM((MOE_TILE * X_ROWS, LANES), F32),
                        pltpu.SemaphoreType.DMA((2,)),
                        pltpu.SemaphoreType.DMA(()),
                        pltpu.SemaphoreType.DMA(())],
        compiler_params=_cparams(("arbitrary", "arbitrary")),
        name="plan",
    )(route, x1)

    w_in = e_w_in.reshape(N_EGROUPS, epg, d, ff2)
    w_out = e_w_out.reshape(N_EGROUPS, epg, ff2 // 2, d)
    ys = pl.pallas_call(
        _moe_kernel,
        grid_spec=pltpu.PrefetchScalarGridSpec(
            num_scalar_prefetch=2,
            grid=(nt,),
            in_specs=[pl.BlockSpec((MOE_TILE * X_ROWS, LANES), lambda i, tc, nu: (i, 0)),
                      pl.BlockSpec((None, epg, d, ff2), lambda i, tc, nu: (tc[i] >> 6, 0, 0, 0),
                                   pipeline_mode=pl.Buffered(1)),
                      pl.BlockSpec((None, epg, ff2 // 2, d), lambda i, tc, nu: (tc[i] >> 6, 0, 0, 0),
                                   pipeline_mode=pl.Buffered(1))],
            out_specs=pl.BlockSpec((MOE_TILE * Y_ROWS, LANES), lambda i, tc, nu: (i, 0))),
        out_shape=jax.ShapeDtypeStruct((n_sorted * Y_ROWS, LANES), F32),
        compiler_params=_cparams(("arbitrary",)),
        name="moe",
    )(tinfo[0, :nt], tinfo[1, :1], xs, w_in, w_out)

    post_args = (pos, ys, x1, p_i, ln_g[None], ln_b[None], ple_w.astype(BF16), ple_gw.astype(BF16), ple_gb[None])
    return pl.pallas_call(
        functools.partial(_post_kernel, alpha=alpha, n_blocks=nb),
        grid=(nb,),
        in_specs=[pl.BlockSpec(memory_space=pl.ANY),
                  pl.BlockSpec(memory_space=pl.ANY),
                  pl.BlockSpec((pb, d), lambda i: (i, 0)),
                  pl.BlockSpec((pb, p_i.shape[1]), lambda i: (i, 0))] + [_full(a.shape) for a in post_args[4:]],
        out_specs=pl.BlockSpec((pb, d), lambda i: (i, 0)),
        out_shape=jax.ShapeDtypeStruct((t, d), F32),
        scratch_shapes=[pltpu.SMEM((2, 1, pb), I32), pltpu.VMEM((2, pb * Y_ROWS, LANES), F32),
                        pltpu.SemaphoreType.DMA((2,)), pltpu.SemaphoreType.DMA((2,))],
        compiler_params=_cparams(("arbitrary",)),
        name="post",
    )(*post_args)


def _router_params(wg, bg, we, be):
    d = wg.shape[0]
    w = jnp.zeros((d, ROUTE_COLS), F32).at[:, :N_EGROUPS].set(wg).at[:, 8:8 + we.shape[1]].set(we)
    b = jnp.zeros((1, ROUTE_COLS), F32).at[0, :N_EGROUPS].set(bg).at[0, 8:8 + be.shape[0]].set(be)
    hi = w.astype(BF16)
    return hi, (w - hi.astype(F32)).astype(BF16), b


def _tile(t, want):
    while t % want:
        want //= 2
    return want


def kernel(x, p, a_w_in, a_b_in, a_vn_g, a_vn_b, a_w_s, a_b_s, a_w_out, b_w_in, b_kv_g, b_w_uk, b_w_uv, b_w_out,
           ln1_g, ln1_b, ln2_g, ln2_b, r_wg, r_bg, r_we, r_be, e_w_in, e_w_out, ple_w, ple_gw, ple_gb):
    batch, seq, d = x.shape
    t = batch * seq
    depth = p.shape[0]
    alpha = (2 * depth) ** 0.25
    chunk = a_w_s.shape[-1]
    xt = x.reshape(t, d)
    pt = p.reshape(depth, t, p.shape[-1])
    for i in range(depth):
        j = i // 2
        router = _router_params(r_wg[i], r_bg[i], r_we[i], r_be[i])
        if i % 2 == 0:
            x1, route = _gmlp_layer(xt, a_w_in[j], a_b_in[j], a_vn_g[j], a_vn_b[j], a_w_s[j], a_b_s[j],
                                    a_w_out[j], ln1_g[i], ln1_b[i], router, alpha, max(chunk, _tile(t, 512)))
        else:
            x1, route = _dsa_layer(xt, batch, b_w_in[j], b_kv_g[j], b_w_uk[j], b_w_uv[j], b_w_out[j],
                                   ln1_g[i], ln1_b[i], router, alpha, _tile(t, 512))
        xt = _moe_and_post(x1, route, pt[i], e_w_in[i], e_w_out[i], ln2_g[i], ln2_b[i],
                           ple_w[i], ple_gw[i], ple_gb[i], alpha)
    return xt.reshape(batch, seq, d)
```

```python
import functools

import jax
import jax.numpy as jnp
from jax import lax
from jax.experimental import pallas as pl
from jax.experimental.pallas import tpu as pltpu

F32, BF16, I32 = jnp.float32, jnp.bfloat16, jnp.int32

IDX_HEADS = 8
IDX_DIM = 64
TOPK_MAX = 256
N_EGROUPS = 4
LN_EPS = 1e-5
RMS_EPS = 1e-6

LANES = 128
Q_BLOCK = 128
KEY_CHUNK = 512
COUNT_SLAB = 1024
COUNT_ACC = 64
LOG2E = 1.4426950408889634
VMEM_LIMIT = 48 * 1024 * 1024

ROUTE_COLS = LANES
ROUTE_ROWS = 8
N_CLASS = 64 * N_EGROUPS
Y_ROWS = 8
X_ROWS = Y_ROWS + 1
PLAN_BLOCK = 256
MOE_TILE = 128
INT_MIN = -(2 ** 31)
NEG_BIG = -1e30


def _cparams(sem, vmem=VMEM_LIMIT):
    return pltpu.CompilerParams(dimension_semantics=sem, vmem_limit_bytes=vmem)


def _full(shape):
    n = len(shape)
    return pl.BlockSpec(shape, lambda *_: (0,) * n)


def _dot(a, b):
    return jnp.dot(a, b, preferred_element_type=F32)


def _dot_t(a, b):
    return lax.dot_general(a, b, (((1,), (1,)), ((), ())), preferred_element_type=F32)


def _layer_norm(x, g, b):
    mu = jnp.mean(x, axis=-1, keepdims=True)
    xc = x - mu
    var = jnp.mean(xc * xc, axis=-1, keepdims=True)
    return xc * lax.rsqrt(var + LN_EPS) * g + b


def _route_rows(lg_t):
    g = [lg_t[i:i + 1] for i in range(N_EGROUPS)]
    gmax = functools.reduce(jnp.maximum, g)
    gsel = jnp.where(g[0] >= gmax, 0.0, jnp.where(g[1] >= gmax, 1.0, jnp.where(g[2] >= gmax, 2.0, 3.0)))
    den = functools.reduce(lambda a, b: a + b, [jnp.exp(gi - gmax) for gi in g])
    p_g = 1.0 / den
    el = lg_t[8:16]
    for gi in range(1, N_EGROUPS):
        el = jnp.where(gsel == float(gi), lg_t[8 + 8 * gi:16 + 8 * gi], el)
    eidx = lax.broadcasted_iota(I32, el.shape, 0).astype(F32)
    m1 = jnp.max(el, axis=0, keepdims=True)
    i1 = jnp.min(jnp.where(el == m1, eidx, 8.0), axis=0, keepdims=True)
    el2 = jnp.where(eidx == i1, -jnp.inf, el)
    m2 = jnp.max(el2, axis=0, keepdims=True)
    i2 = jnp.min(jnp.where(el2 == m2, eidx, 8.0), axis=0, keepdims=True)
    r = jnp.exp(m2 - m1)
    inv = 1.0 / (1.0 + r)
    gate1 = p_g * inv
    gate2 = p_g * r * inv
    first_lo = i1 < i2
    lo = jnp.minimum(i1, i2)
    hi = jnp.maximum(i1, i2)
    cls = gsel * 64.0 + lo * 8.0 + hi
    return cls, jnp.where(first_lo, gate1, gate2), jnp.where(first_lo, gate2, gate1)


def _norm_and_route(y, ln_g, ln_b, wr_hi, wr_lo, br, x1_ref, route_ref):
    tm, _ = y.shape
    x1 = _layer_norm(y, ln_g, ln_b)
    x1_ref[...] = x1
    x_hi = x1.astype(BF16)
    x_lo = (x1 - x_hi.astype(F32)).astype(BF16)
    lg = _dot(x_hi, wr_hi) + _dot(x_lo, wr_hi) + _dot(x_hi, wr_lo) + br
    cls, g_lo, g_hi = _route_rows(lg.T)
    row = lax.broadcasted_iota(I32, (ROUTE_ROWS, tm), 0)
    route_ref[...] = jnp.where(row == 0, cls, jnp.where(row == 1, g_lo, jnp.where(row == 2, g_hi, 0.0)))


def _mixer_out_specs(tm, d):
    return [pl.BlockSpec((tm, d), lambda i: (i, 0)), pl.BlockSpec((ROUTE_ROWS, tm), lambda i: (0, i))]


def _mixer_out_shape(t, d):
    return [jax.ShapeDtypeStruct((t, d), F32), jax.ShapeDtypeStruct((ROUTE_ROWS, t), F32)]


def _gmlp_kernel(x_ref, w_in_ref, b_in_ref, vn_g_ref, vn_b_ref, w_s_ref, b_st_ref, w_out_ref,
                 ln_g_ref, ln_b_ref, wr_hi_ref, wr_lo_ref, br_ref, x1_ref, route_ref, gated_ref, *, alpha):
    x = x_ref[...]
    tm, _ = x.shape
    groups, chunk, _ = w_s_ref.shape
    z = _dot(x.astype(BF16), w_in_ref[...]) + b_in_ref[...]
    z = 0.5 * z * (1.0 + lax.erf(z * (2.0 ** -0.5)))
    half = z.shape[1] // 2
    gd = half // groups
    u = z[:, :half]
    v = _layer_norm(z[:, half:], vn_g_ref[...], vn_b_ref[...]).astype(BF16)
    r = lax.broadcasted_iota(I32, (chunk, chunk), 0)
    c = lax.broadcasted_iota(I32, (chunk, chunk), 1)
    causal = r >= c
    for g in range(groups):
        w_c = jnp.where(causal, w_s_ref[g], 0.0).astype(BF16)
        bias = b_st_ref[:, g:g + 1]
        for ci in range(tm // chunk):
            rows = slice(ci * chunk, (ci + 1) * chunk)
            cols = slice(g * gd, (g + 1) * gd)
            s = _dot(w_c, v[rows, cols]) + bias
            gated_ref[rows, cols] = (u[rows, cols] * s).astype(BF16)
    mix = _dot(gated_ref[...], w_out_ref[...])
    _norm_and_route(alpha * x + mix, ln_g_ref[...], ln_b_ref[...], wr_hi_ref[...], wr_lo_ref[...],
                    br_ref[...], x1_ref, route_ref)


def _gmlp_layer(xt, w_in, b_in, vn_g, vn_b, w_s, b_s, w_out, ln_g, ln_b, router, alpha, tm):
    t, d = xt.shape
    half = w_out.shape[0]
    wr_hi, wr_lo, br = router
    args = (xt, w_in.astype(BF16), b_in[None], vn_g[None], vn_b[None], w_s, b_s.T, w_out.astype(BF16),
            ln_g[None], ln_b[None], wr_hi, wr_lo, br)
    in_specs = [pl.BlockSpec((tm, d), lambda i: (i, 0))] + [_full(a.shape) for a in args[1:]]
    return pl.pallas_call(
        functools.partial(_gmlp_kernel, alpha=alpha),
        grid=(t // tm,),
        in_specs=in_specs,
        out_specs=_mixer_out_specs(tm, d),
        out_shape=_mixer_out_shape(t, d),
        scratch_shapes=[pltpu.VMEM((tm, half), BF16)],
        compiler_params=_cparams(("parallel",)),
        name="gmlp",
    )(*args)


def _dsa_proj_kernel(x_ref, wq_ref, wc_ref, wqi_ref, wkw_ref, kvg_ref, wuk_t_ref,
                     qlat_ref, ckv_ref, ckv_t_ref, qidx_ref, kidx_ref, wi_t_ref, *, scale_q, scale_w):
    xb = x_ref[...].astype(BF16)
    nq = qlat_ref.shape[0]
    kc = ckv_t_ref.shape[2]
    heads, hd, _ = wuk_t_ref.shape
    q = _dot(xb, wq_ref[...])
    for h in range(heads):
        ql = (_dot(q[:, h * hd:(h + 1) * hd].astype(BF16), wuk_t_ref[h]) * scale_q).astype(BF16)
        for b in range(nq):
            qlat_ref[b, h * Q_BLOCK:(h + 1) * Q_BLOCK, :] = ql[b * Q_BLOCK:(b + 1) * Q_BLOCK]
    c = _dot(xb, wc_ref[...])
    ms = jnp.mean(c * c, axis=-1, keepdims=True)
    c = c * lax.rsqrt(ms + RMS_EPS) * kvg_ref[...]
    ckv_ref[...] = c.astype(BF16)
    for b in range(ckv_t_ref.shape[0]):
        ckv_t_ref[b] = c[b * kc:(b + 1) * kc].T.astype(BF16)
    qi = _dot(xb, wqi_ref[...]).astype(BF16)
    for h in range(IDX_HEADS):
        for b in range(nq):
            qidx_ref[b, h * Q_BLOCK:(h + 1) * Q_BLOCK, :] = qi[b * Q_BLOCK:(b + 1) * Q_BLOCK,
                                                               h * IDX_DIM:(h + 1) * IDX_DIM]
    kw = _dot(xb, wkw_ref[...])
    kidx_ref[...] = kw[:, :IDX_DIM].astype(BF16)
    wi_t_ref[...] = kw.T[IDX_DIM:IDX_DIM + IDX_HEADS] * scale_w


def _dsa_attn_kernel(qidx_ref, wi_t_ref, qlat_ref, kidx_ref, ckv_ref, ckv_t_ref, o_ref,
                     key_ref, acc_ref, m_ref, l_ref, a_ref, p_ref, *, k_sel, heads):
    qb = pl.program_id(1)
    kc = ckv_t_ref.shape[2]
    slab = key_ref.shape[1]
    cps = slab // kc
    nq = Q_BLOCK
    pair = 2 * nq
    n_pairs = heads // 2
    n_chunks = (qb * nq + nq + kc - 1) // kc
    n_slabs = (n_chunks + cps - 1) // cps

    def key_chunk(j):
        return key_ref.at[j // cps, pl.ds(pl.multiple_of((j % cps) * kc, kc), kc), :]

    q_pos = qb * nq + lax.broadcasted_iota(I32, (kc, nq), 1)
    k_off = lax.broadcasted_iota(I32, (kc, nq), 0)

    def pad_chunk(j, carry):
        key_chunk(j)[...] = jnp.full((kc, nq), jnp.nan, F32)
        return carry

    lax.fori_loop(n_chunks, n_slabs * cps, pad_chunk, 0)

    def score_chunk(j, carry):
        kch = kidx_ref[pl.ds(pl.multiple_of(j * kc, kc), kc), :]
        tot = None
        for p in range(IDX_HEADS // 2):
            sc = _dot_t(kch, qidx_ref[p * pair:(p + 1) * pair, :])
            for hh in range(2):
                h = 2 * p + hh
                r = jnp.maximum(sc[:, hh * nq:(hh + 1) * nq], 0.0) * wi_t_ref[h:h + 1, :]
                tot = r if tot is None else tot + r
        key_chunk(j)[...] = jnp.where(k_off + j * kc <= q_pos, tot, jnp.nan)
        return carry

    lax.fori_loop(0, n_chunks, score_chunk, 0)

    def count(t, strict=False):
        tb = jnp.broadcast_to(t, (slab, nq))

        def body(s, acc):
            blk = key_ref[s]
            hit = jnp.where(blk > tb if strict else blk >= tb, 1.0, 0.0)
            return acc + jnp.sum(hit.reshape(slab // COUNT_ACC, COUNT_ACC, nq), axis=0)

        acc = lax.fori_loop(0, n_slabs, body, jnp.zeros((COUNT_ACC, nq), F32))
        return jnp.sum(acc, axis=0, keepdims=True)

    def decode(code):
        return lax.bitcast_convert_type(code ^ ((code >> 31) & 0x7FFFFFFF), F32)

    def bit_step(b, state):
        c, n_at = state
        t = c + jnp.left_shift(jnp.int32(1), 31 - b)
        n_t = count(decode(t))
        take = n_t >= float(k_sel)
        return jnp.where(take, t, c), jnp.where(take, n_t, n_at)

    c_sel, n_ge = lax.fori_loop(0, 32, bit_step, (jnp.full((1, nq), INT_MIN, I32), jnp.zeros((1, nq), F32)))
    c_sel = jnp.where(c_sel == INT_MIN, -jnp.inf, decode(c_sel))

    has_tie = jnp.max(n_ge) > float(k_sel)

    @pl.when(has_tie)
    def _():
        need = float(k_sel) - count(c_sel, strict=True)
        tie_col = n_ge > float(k_sel)
        cb = jnp.broadcast_to(c_sel, (kc, nq))
        lower = (lax.broadcasted_iota(I32, (kc, kc), 1) < lax.broadcasted_iota(I32, (kc, kc), 0))
        lower = jnp.where(lower, 1.0, 0.0).astype(BF16)

        def fix(j, seen):
            blk = key_chunk(j)[...]
            eq = blk == cb
            eq_f = jnp.where(eq, 1.0, 0.0)
            rank = seen + _dot(lower, eq_f.astype(BF16))
            drop = eq & tie_col & (rank >= need)
            key_chunk(j)[...] = jnp.where(drop, jnp.nan, blk)
            return seen + jnp.sum(eq_f, axis=0, keepdims=True)

        lax.fori_loop(0, n_chunks, fix, jnp.zeros((1, nq), F32))

    thr = jnp.broadcast_to(c_sel, (kc, nq))
    m_ref[...] = jnp.full(m_ref.shape, NEG_BIG, F32)
    l_ref[...] = jnp.zeros(l_ref.shape, F32)
    acc_ref[...] = jnp.zeros(acc_ref.shape, F32)

    def softmax_chunk(j):
        ck = ckv_ref[pl.ds(pl.multiple_of(j * kc, kc), kc), :]
        bias = jnp.where(key_chunk(j)[...] >= thr, 0.0, NEG_BIG)
        bias = jnp.concatenate([bias, bias], axis=1)
        slot = j % 2
        for p in range(n_pairs):
            lg = _dot_t(ck, qlat_ref[p * pair:(p + 1) * pair, :]) + bias
            m_old = m_ref[p]
            m_new = jnp.maximum(m_old, jnp.max(lg, axis=0, keepdims=True))
            pr = jnp.exp2(lg - m_new)
            alpha = jnp.exp2(m_old - m_new)
            l_ref[p] = alpha * l_ref[p] + jnp.sum(pr, axis=0, keepdims=True)
            m_ref[p] = m_new
            a_ref[slot, p] = alpha
            p_ref[slot, p] = pr.astype(BF16)

    def accumulate_chunk(j):
        ck_t = ckv_t_ref[j]
        slot = j % 2
        for p in range(n_pairs):
            acc_ref[p] = a_ref[slot, p] * acc_ref[p] + _dot(ck_t, p_ref[slot, p])

    def att_step(j, carry):
        accumulate_chunk(j - 1)
        softmax_chunk(j)
        return carry

    softmax_chunk(0)
    lax.fori_loop(1, n_chunks, att_step, 0)
    accumulate_chunk(n_chunks - 1)
    for p in range(n_pairs):
        o_ref[p * pair:(p + 1) * pair, :] = (acc_ref[p] * (1.0 / l_ref[p])).T.astype(o_ref.dtype)


def _dsa_out_kernel(o_ref, x_ref, wuv_ref, wout_ref, ln_g_ref, ln_b_ref, wr_hi_ref, wr_lo_ref, br_ref,
                    x1_ref, route_ref, o2_ref, *, alpha):
    nq = o_ref.shape[0]
    heads, _, hd = wuv_ref.shape
    for b in range(nq):
        for h in range(heads):
            oh = o_ref[b, h * Q_BLOCK:(h + 1) * Q_BLOCK, :]
            o2_ref[b * Q_BLOCK:(b + 1) * Q_BLOCK, h * hd:(h + 1) * hd] = _dot(oh, wuv_ref[h]).astype(BF16)
    mix = _dot(o2_ref[...], wout_ref[...])
    _norm_and_route(alpha * x_ref[...] + mix, ln_g_ref[...], ln_b_ref[...], wr_hi_ref[...], wr_lo_ref[...],
                    br_ref[...], x1_ref, route_ref)


def _dsa_layer(xt, batch, w_in, kv_g, w_uk, w_uv, w_out, ln_g, ln_b, router, alpha, tm):
    t, d = xt.shape
    seq = t // batch
    heads, lat, hd = w_uk.shape
    qw = heads * hd
    nqb = t // Q_BLOCK
    nq = tm // Q_BLOCK
    k_sel = min(TOPK_MAX, seq // 4)
    kc = min(KEY_CHUNK, seq)
    slab = min(COUNT_SLAB, seq)
    w_in = w_in.astype(BF16)
    o1, o2, o3 = qw, qw + lat, qw + lat + IDX_HEADS * IDX_DIM
    wkw = jnp.pad(w_in[:, o3:], ((0, 0), (0, LANES - (w_in.shape[1] - o3))))
    proj_args = (xt, w_in[:, :o1], w_in[:, o1:o2], w_in[:, o2:o3], wkw, kv_g[None],
                 jnp.swapaxes(w_uk, 1, 2).astype(BF16))
    qlat, ckv, ckv_t, qidx, kidx, wi_t = pl.pallas_call(
        functools.partial(_dsa_proj_kernel, scale_q=hd ** -0.5 * LOG2E, scale_w=(IDX_HEADS * IDX_DIM) ** -0.5),
        grid=(t // tm,),
        in_specs=[pl.BlockSpec((tm, d), lambda i: (i, 0))] + [_full(a.shape) for a in proj_args[1:]],
        out_specs=[pl.BlockSpec((nq, heads * Q_BLOCK, lat), lambda i: (i, 0, 0)),
                   pl.BlockSpec((tm, lat), lambda i: (i, 0)),
                   pl.BlockSpec((tm // kc, lat, kc), lambda i: (i, 0, 0)),
                   pl.BlockSpec((nq, IDX_HEADS * Q_BLOCK, IDX_DIM), lambda i: (i, 0, 0)),
                   pl.BlockSpec((tm, IDX_DIM), lambda i: (i, 0)),
                   pl.BlockSpec((IDX_HEADS, tm), lambda i: (0, i))],
        out_shape=[jax.ShapeDtypeStruct((nqb, heads * Q_BLOCK, lat), BF16),
                   jax.ShapeDtypeStruct((t, lat), BF16),
                   jax.ShapeDtypeStruct((t // kc, lat, kc), BF16),
                   jax.ShapeDtypeStruct((nqb, IDX_HEADS * Q_BLOCK, IDX_DIM), BF16),
                   jax.ShapeDtypeStruct((t, IDX_DIM), BF16),
                   jax.ShapeDtypeStruct((IDX_HEADS, t), F32)],
        compiler_params=_cparams(("parallel",)),
        name="dsa_proj",
    )(*proj_args)

    nq_seq = seq // Q_BLOCK
    n_kc = seq // kc
    rows = heads * Q_BLOCK
    o = pl.pallas_call(
        functools.partial(_dsa_attn_kernel, k_sel=k_sel, heads=heads),
        grid=(batch, nq_seq),
        in_specs=[pl.BlockSpec((None, IDX_HEADS * Q_BLOCK, IDX_DIM), lambda b, q: (b * nq_seq + q, 0, 0)),
                  pl.BlockSpec((IDX_HEADS, Q_BLOCK), lambda b, q: (0, b * nq_seq + q)),
                  pl.BlockSpec((None, rows, lat), lambda b, q: (b * nq_seq + q, 0, 0)),
                  pl.BlockSpec((None, seq, IDX_DIM), lambda b, q: (b, 0, 0)),
                  pl.BlockSpec((None, seq, lat), lambda b, q: (b, 0, 0)),
                  pl.BlockSpec((None, n_kc, lat, kc), lambda b, q: (b, 0, 0, 0))],
        out_specs=pl.BlockSpec((None, rows, lat), lambda b, q: (b * nq_seq + q, 0, 0)),
        out_shape=jax.ShapeDtypeStruct((nqb, rows, lat), BF16),
        scratch_shapes=[pltpu.VMEM((seq // slab, slab, Q_BLOCK), F32),
                        pltpu.VMEM((heads // 2, lat, 2 * Q_BLOCK), F32),
                        pltpu.VMEM((heads // 2, 1, 2 * Q_BLOCK), F32),
                        pltpu.VMEM((heads // 2, 1, 2 * Q_BLOCK), F32),
                        pltpu.VMEM((2, heads // 2, 1, 2 * Q_BLOCK), F32),
                        pltpu.VMEM((2, heads // 2, kc, 2 * Q_BLOCK), BF16)],
        compiler_params=_cparams(("parallel", "arbitrary")),
        name="dsa_attn",
    )(qidx, wi_t, qlat, kidx.reshape(batch, seq, IDX_DIM), ckv.reshape(batch, seq, lat),
      ckv_t.reshape(batch, n_kc, lat, kc))

    wr_hi, wr_lo, br = router
    out_args = (o, xt, w_uv.astype(BF16), w_out.astype(BF16), ln_g[None], ln_b[None], wr_hi, wr_lo, br)
    return pl.pallas_call(
        functools.partial(_dsa_out_kernel, alpha=alpha),
        grid=(t // tm,),
        in_specs=[pl.BlockSpec((nq, rows, lat), lambda i: (i, 0, 0)),
                  pl.BlockSpec((tm, d), lambda i: (i, 0))] + [_full(a.shape) for a in out_args[2:]],
        out_specs=_mixer_out_specs(tm, d),
        out_shape=_mixer_out_shape(t, d),
        scratch_shapes=[pltpu.VMEM((tm, qw), BF16)],
        compiler_params=_cparams(("parallel",)),
        name="dsa_out",
    )(*out_args)


DMA_UNROLL = 8


def _start_token_moves(n, copy_of):
    def body(g, carry):
        for u in range(DMA_UNROLL):
            copy_of(g * DMA_UNROLL + u).start(priority=u % 2)
        return carry
    lax.fori_loop(0, n // DMA_UNROLL, body, 0)


def _tok(ref, i, rows):
    return ref.at[pl.ds(i * rows, rows)]


def _rows_of(ref, j, n, rows):
    return ref.at[pl.ds(j, n, stride=rows), :]


class _TokenGather:
    def __init__(self, i, n, idx_hbm, src_hbm, idx_smem, buf, isem, rsem):
        self.i, self.n = i, n
        self.idx_hbm, self.src_hbm, self.idx_smem, self.buf, self.isem, self.rsem = (
            idx_hbm, src_hbm, idx_smem, buf, isem, rsem)
        self.tm = idx_smem.shape[2]
        self.rows = buf.shape[1] // self.tm
        self.slot = i % 2

    def _idx_copy(self, blk, s):
        return pltpu.make_async_copy(self.idx_hbm.at[blk], self.idx_smem.at[s], self.isem.at[s])

    def _token_copy(self, s, r):
        return pltpu.make_async_copy(_tok(self.src_hbm, self.idx_smem[s, 0, r], self.rows),
                                     _tok(self.buf.at[s], r, self.rows), self.rsem.at[s])

    def _wait_tokens(self, s):
        pltpu.make_async_copy(self.src_hbm.at[pl.ds(0, self.tm * self.rows)], self.buf.at[s], self.rsem.at[s]).wait()

    def arrive(self):
        @pl.when(self.i == 0)
        def _():
            first = self._idx_copy(0, 0)
            first.start()
            first.wait()
            _start_token_moves(self.tm, lambda r: self._token_copy(0, r))
            self._idx_copy(min(1, self.n - 1), 1).start()

        self._idx_copy(jnp.minimum(self.i + 1, self.n - 1), 1 - self.slot).wait()
        self._wait_tokens(self.slot)

    def prefetch(self):
        nslot = 1 - self.slot
        for r in range(self.tm):
            self._token_copy(nslot, r).start(priority=r % 2)
        self._idx_copy(jnp.minimum(self.i + 2, self.n - 1), self.slot).start()

    def finish(self):
        @pl.when(self.i == self.n - 1)
        def _():
            self._wait_tokens(1 - self.slot)
            self._idx_copy(self.n - 1, self.slot).wait()


def _plan_kernel(route_ref, x_ref, pos_ref, tinfo_ref, xs_ref,
                 hist_ref, off_ref, carry_ref, stage_ref, ppos_vmem, ppos_smem, fill_vmem, fill_smem, zero_ref,
                 ssem, psem, zsem, *, tile):
    ps = pl.program_id(0)
    b = pl.program_id(1)
    nb = pl.num_programs(1)
    n_cls, pb = hist_ref.shape
    kt = tinfo_ref.shape[1]
    cls = route_ref[0:1, :].astype(I32)
    cid = lax.broadcasted_iota(I32, (n_cls, pb), 0)
    onehot = jnp.where(cid == cls, 1.0, 0.0)

    def zero_fill(start):
        def piece(tok0, n_tok):
            cp = pltpu.make_async_copy(zero_ref.at[pl.ds(0, n_tok * X_ROWS)],
                                       xs_ref.at[pl.ds(tok0 * X_ROWS, n_tok * X_ROWS)], zsem)
            cp.start() if start else cp.wait()

        def per_class(c, carry):
            first, n = fill_smem[0, c], fill_smem[1, c]
            bit = tile // 2
            while bit:
                @pl.when((n & bit) != 0)
                def _(bit=bit):
                    piece(first + (n & -(2 * bit)), bit)
                bit //= 2
            return carry

        lax.fori_loop(0, n_cls, per_class, 0)

        def per_tile(k, carry):
            piece(k * tile, tile)
            return carry

        lax.fori_loop(fill_smem[2, 0], xs_ref.shape[0] // (tile * X_ROWS), per_tile, 0)

    @pl.when(ps == 0)
    def _():
        @pl.when(b == 0)
        def _():
            hist_ref[...] = jnp.zeros(hist_ref.shape, F32)
            zero_ref[...] = jnp.zeros(zero_ref.shape, F32)
        hist_ref[...] += onehot

    @pl.when((ps == 1) & (b == 0))
    def _():
        counts = jnp.sum(hist_ref[...], axis=1, keepdims=True)
        tiles = jnp.floor((counts + float(tile - 1)) * (1.0 / tile))
        lower = lax.broadcasted_iota(I32, (n_cls, n_cls), 1) < lax.broadcasted_iota(I32, (n_cls, n_cls), 0)
        lower = jnp.where(lower, 1.0, 0.0).astype(BF16)
        off = _dot(lower, jnp.broadcast_to(tiles, (n_cls, LANES)).astype(BF16))[:, :1]
        off_ref[...] = off * float(tile)
        carry_ref[...] = jnp.zeros(carry_ref.shape, F32)
        k = lax.broadcasted_iota(I32, (n_cls, kt), 1).astype(F32)
        cid_f = lax.broadcasted_iota(I32, (n_cls, kt), 0).astype(F32)
        mine = (k >= off) & (k < off + tiles)
        tcls = jnp.sum(jnp.where(mine, cid_f, 0.0), axis=0, keepdims=True)
        n_used = jnp.sum(tiles, axis=0, keepdims=True)
        last = jnp.max(jnp.where(tiles > 0.0, cid_f[:, :1], 0.0), axis=0, keepdims=True)
        tcls = jnp.where(k[:1] >= n_used, last, tcls)
        row = lax.broadcasted_iota(I32, (ROUTE_ROWS, kt), 0)
        tinfo_ref[...] = jnp.where(row == 0, tcls, jnp.where(row == 1, n_used, 0.0)).astype(I32)
        as_row = lambda col: jnp.broadcast_to(col, (n_cls, LANES)).T[0:1]
        frow = lax.broadcasted_iota(I32, fill_vmem.shape, 0)
        fill_vmem[...] = jnp.where(frow == 0, as_row(off * float(tile) + counts),
                                   jnp.where(frow == 1, as_row(tiles * float(tile) - counts), n_used)).astype(I32)
        to_fill = pltpu.make_async_copy(fill_vmem, fill_smem, psem)
        to_fill.start()
        to_fill.wait()
        zero_fill(start=True)

    @pl.when(ps == 1)
    def _():
        upper = lax.broadcasted_iota(I32, (pb, pb), 0) < lax.broadcasted_iota(I32, (pb, pb), 1)
        upper = jnp.where(upper, 1.0, 0.0).astype(BF16)
        before = _dot(onehot.astype(BF16), upper)
        ppos = jnp.sum(onehot * (before + carry_ref[...] + off_ref[...]), axis=0, keepdims=True).astype(I32)
        carry_ref[...] += jnp.sum(onehot, axis=1, keepdims=True)
        pos_ref[...] = ppos
        ppos_vmem[...] = ppos
        to_smem = pltpu.make_async_copy(ppos_vmem, ppos_smem, psem)
        to_smem.start()

        slot = b % 2
        stage = stage_ref.at[slot]

        def drain(s):
            pltpu.make_async_copy(stage_ref.at[s], xs_ref.at[pl.ds(0, pb * X_ROWS)], ssem.at[s]).wait()

        @pl.when(b >= 2)
        def _():
            drain(slot)

        for j in range(Y_ROWS):
            _rows_of(stage, j, pb, X_ROWS)[...] = x_ref[:, j * LANES:(j + 1) * LANES]
        record = jnp.concatenate([route_ref[...], jnp.zeros((LANES - ROUTE_ROWS, pb), F32)], axis=0).T
        _rows_of(stage, Y_ROWS, pb, X_ROWS)[...] = record

        to_smem.wait()
        _start_token_moves(pb, lambda r: pltpu.make_async_copy(
            _tok(stage, r, X_ROWS), _tok(xs_ref, ppos_smem[0, r], X_ROWS), ssem.at[slot]))

        @pl.when(b == nb - 1)
        def _():
            drain(slot)

            @pl.when(nb > 1)
            def _():
                drain(1 - slot)

            zero_fill(start=False)


def _moe_kernel(tcls_ref, nused_ref, xs_ref, win_ref, wout_ref, ys_ref):
    i = pl.program_id(0)
    tm = xs_ref.shape[0] // X_ROWS
    ff = wout_ref.shape[1]

    @pl.when(i < nused_ref[0])
    def _():
        cls = tcls_ref[i]
        xb = jnp.concatenate([_rows_of(xs_ref, j, tm, X_ROWS)[...].astype(BF16) for j in range(Y_ROWS)], axis=1)
        record = _rows_of(xs_ref, Y_ROWS, tm, X_ROWS)[...]
        y = None
        for e, gate in (((cls >> 3) & 7, record[:, 1:2]), (cls & 7, record[:, 2:3])):
            h = _dot(xb, win_ref[e])
            a = h[:, :ff]
            act = (a * jax.nn.sigmoid(a) * h[:, ff:] * gate).astype(BF16)
            ye = _dot(act, wout_ref[e])
            y = ye if y is None else y + ye
        for j in range(Y_ROWS):
            _rows_of(ys_ref, j, tm, Y_ROWS)[...] = y[:, j * LANES:(j + 1) * LANES]

    @pl.when(i >= nused_ref[0])
    def _():
        ys_ref[...] = jnp.zeros(ys_ref.shape, F32)


def _post_kernel(pos_ref, ys_ref, x1_ref, p_ref, ln_g_ref, ln_b_ref, pw_ref, gw_ref, gb_ref, out_ref,
                 idx_smem, ybuf, isem, rsem, *, alpha, n_blocks):
    i = pl.program_id(0)
    tm = x1_ref.shape[0]
    gather = _TokenGather(i, n_blocks, pos_ref, ys_ref, idx_smem, ybuf, isem, rsem)
    gather.arrive()
    gather.prefetch()
    yb = ybuf.at[i % 2]
    ffn = jnp.concatenate([_rows_of(yb, j, tm, Y_ROWS)[...] for j in range(Y_ROWS)], axis=1)
    x2 = _layer_norm(alpha * x1_ref[...] + ffn, ln_g_ref[...], ln_b_ref[...])
    gate = jax.nn.sigmoid(_dot(x2.astype(BF16), gw_ref[...]) + gb_ref[...])
    out_ref[...] = x2 + _dot(p_ref[...].astype(BF16), pw_ref[...]) * gate
    gather.finish()


def _moe_and_post(x1, route, layer, p, e_w_in, e_w_out, ln_g, ln_b, ple_w, ple_gw, ple_gb, alpha):
    t, d = x1.shape
    assert d == LANES * Y_ROWS
    depth, n_exp, _, ff2 = e_w_in.shape
    epg = n_exp // N_EGROUPS
    pb = _tile(t, PLAN_BLOCK)
    nb = t // pb
    nt = t // MOE_TILE + N_EGROUPS * (epg * (epg - 1) // 2)
    kt = -(-nt // LANES) * LANES
    n_sorted = nt * MOE_TILE

    pos, tinfo, xs = pl.pallas_call(
        functools.partial(_plan_kernel, tile=MOE_TILE),
        grid=(2, nb),
        in_specs=[pl.BlockSpec((ROUTE_ROWS, pb), lambda ps, b: (0, b)),
                  pl.BlockSpec((pb, d), lambda ps, b: (ps * b, 0))],
        out_specs=[pl.BlockSpec((None, 1, pb), lambda ps, b: (ps * b, 0, 0)),
                   pl.BlockSpec((ROUTE_ROWS, kt), lambda ps, b: (0, 0)),
                   pl.BlockSpec(memory_space=pl.ANY)],
        out_shape=[jax.ShapeDtypeStruct((nb, 1, pb), I32),
                   jax.ShapeDtypeStruct((ROUTE_ROWS, kt), I32),
                   jax.ShapeDtypeStruct((n_sorted * X_ROWS, LANES), F32)],
        scratch_shapes=[pltpu.VMEM((N_CLASS, pb), F32),
                        pltpu.VMEM((N_CLASS, 1), F32),
                        pltpu.VMEM((N_CLASS, 1), F32),
                        pltpu.VMEM((2, pb * X_ROWS, LANES), F32),
                        pltpu.VMEM((1, pb), I32),
                        pltpu.SMEM((1, pb), I32),
                        pltpu.VMEM((ROUTE_ROWS, N_CLASS), I32),
                        pltpu.SMEM((ROUTE_ROWS, N_CLASS), I32),
                        pltpu.VMEM((MOE_TILE * X_ROWS, LANES), F32),
                        pltpu.SemaphoreType.DMA((2,)),
                        pltpu.SemaphoreType.DMA(()),
                        pltpu.SemaphoreType.DMA(())],
        compiler_params=_cparams(("arbitrary", "arbitrary")),
        name="plan",
    )(route, x1)

    w_in = e_w_in.reshape(depth, N_EGROUPS, epg, d, ff2)
    w_out = e_w_out.reshape(depth, N_EGROUPS, epg, ff2 // 2, d)
    ys = pl.pallas_call(
        _moe_kernel,
        grid_spec=pltpu.PrefetchScalarGridSpec(
            num_scalar_prefetch=2,
            grid=(nt,),
            in_specs=[pl.BlockSpec((MOE_TILE * X_ROWS, LANES), lambda i, tc, nu: (i, 0)),
                      pl.BlockSpec((None, None, epg, d, ff2), lambda i, tc, nu: (layer, tc[i] >> 6, 0, 0, 0),
                                   pipeline_mode=pl.Buffered(1)),
                      pl.BlockSpec((None, None, epg, ff2 // 2, d), lambda i, tc, nu: (layer, tc[i] >> 6, 0, 0, 0),
                                   pipeline_mode=pl.Buffered(1))],
            out_specs=pl.BlockSpec((MOE_TILE * Y_ROWS, LANES), lambda i, tc, nu: (i, 0))),
        out_shape=jax.ShapeDtypeStruct((n_sorted * Y_ROWS, LANES), F32),
        compiler_params=_cparams(("arbitrary",)),
        name="moe",
    )(tinfo[0, :nt], tinfo[1, :1], xs, w_in, w_out)

    post_args = (pos, ys, x1, p, ln_g[None], ln_b[None], ple_w.astype(BF16), ple_gw.astype(BF16), ple_gb[None])
    return pl.pallas_call(
        functools.partial(_post_kernel, alpha=alpha, n_blocks=nb),
        grid=(nb,),
        in_specs=[pl.BlockSpec(memory_space=pl.ANY),
                  pl.BlockSpec(memory_space=pl.ANY),
                  pl.BlockSpec((pb, d), lambda i: (i, 0)),
                  pl.BlockSpec((None, pb, p.shape[2]), lambda i: (layer, i, 0))]
                 + [_full(a.shape) for a in post_args[4:]],
        out_specs=pl.BlockSpec((pb, d), lambda i: (i, 0)),
        out_shape=jax.ShapeDtypeStruct((t, d), F32),
        scratch_shapes=[pltpu.SMEM((2, 1, pb), I32), pltpu.VMEM((2, pb * Y_ROWS, LANES), F32),
                        pltpu.SemaphoreType.DMA((2,)), pltpu.SemaphoreType.DMA((2,))],
        compiler_params=_cparams(("arbitrary",)),
        name="post",
    )(*post_args)


def _router_params(wg, bg, we, be):
    d = wg.shape[0]
    w = jnp.zeros((d, ROUTE_COLS), F32).at[:, :N_EGROUPS].set(wg).at[:, 8:8 + we.shape[1]].set(we)
    b = jnp.zeros((1, ROUTE_COLS), F32).at[0, :N_EGROUPS].set(bg).at[0, 8:8 + be.shape[0]].set(be)
    hi = w.astype(BF16)
    return hi, (w - hi.astype(F32)).astype(BF16), b


def _tile(t, want):
    while t % want:
        want //= 2
    return want


def kernel(x, p, a_w_in, a_b_in, a_vn_g, a_vn_b, a_w_s, a_b_s, a_w_out, b_w_in, b_kv_g, b_w_uk, b_w_uv, b_w_out,
           ln1_g, ln1_b, ln2_g, ln2_b, r_wg, r_bg, r_we, r_be, e_w_in, e_w_out, ple_w, ple_gw, ple_gb):
    batch, seq, d = x.shape
    t = batch * seq
    depth = p.shape[0]
    alpha = (2 * depth) ** 0.25
    chunk = a_w_s.shape[-1]
    xt = x.reshape(t, d)
    pt = p.reshape(depth, t, p.shape[-1])
    for i in range(depth):
        j = i // 2
        router = _router_params(r_wg[i], r_bg[i], r_we[i], r_be[i])
        if i % 2 == 0:
            x1, route = _gmlp_layer(xt, a_w_in[j], a_b_in[j], a_vn_g[j], a_vn_b[j], a_w_s[j], a_b_s[j],
                                    a_w_out[j], ln1_g[i], ln1_b[i], router, alpha, max(chunk, _tile(t, 512)))
        else:
            x1, route = _dsa_layer(xt, batch, b_w_in[j], b_kv_g[j], b_w_uk[j], b_w_uv[j], b_w_out[j],
                                   ln1_g[i], ln1_b[i], router, alpha, _tile(t, 512))
        xt = _moe_and_post(x1, route, i, pt, e_w_in, e_w_out, ln2_g[i], ln2_b[i],
                           ple_w[i], ple_gw[i], ple_gb[i], alpha)
    return xt.reshape(batch, seq, d)
```

```python
import functools

import jax
import jax.numpy as jnp
from jax import lax
from jax.experimental import pallas as pl
from jax.experimental.pallas import tpu as pltpu

F32, BF16, I32 = jnp.float32, jnp.bfloat16, jnp.int32

IDX_HEADS = 8
IDX_DIM = 64
TOPK_MAX = 256
N_EGROUPS = 4
LN_EPS = 1e-5
RMS_EPS = 1e-6

LANES = 128
Q_BLOCK = 128
KEY_CHUNK = 512
COUNT_SLAB = 1024
COUNT_ACC = 64
LOG2E = 1.4426950408889634
VMEM_LIMIT = 48 * 1024 * 1024

ROUTE_COLS = LANES
ROUTE_ROWS = 8
N_CLASS = 64 * N_EGROUPS
Y_ROWS = 8
X_ROWS = Y_ROWS + 1
PLAN_BLOCK = 256
MOE_TILE = 128
INT_MIN = -(2 ** 31)
NEG_BIG = -1e30


def _cparams(sem, vmem=VMEM_LIMIT):
    return pltpu.CompilerParams(dimension_semantics=sem, vmem_limit_bytes=vmem)


def _full(shape):
    n = len(shape)
    return pl.BlockSpec(shape, lambda *_: (0,) * n)


def _dot(a, b):
    return jnp.dot(a, b, preferred_element_type=F32)


def _dot_t(a, b):
    return lax.dot_general(a, b, (((1,), (1,)), ((), ())), preferred_element_type=F32)


def _layer_norm(x, g, b):
    mu = jnp.mean(x, axis=-1, keepdims=True)
    xc = x - mu
    var = jnp.mean(xc * xc, axis=-1, keepdims=True)
    return xc * lax.rsqrt(var + LN_EPS) * g + b


def _route_rows(lg_t):
    g = [lg_t[i:i + 1] for i in range(N_EGROUPS)]
    gmax = functools.reduce(jnp.maximum, g)
    gsel = jnp.where(g[0] >= gmax, 0.0, jnp.where(g[1] >= gmax, 1.0, jnp.where(g[2] >= gmax, 2.0, 3.0)))
    den = functools.reduce(lambda a, b: a + b, [jnp.exp(gi - gmax) for gi in g])
    p_g = 1.0 / den
    el = lg_t[8:16]
    for gi in range(1, N_EGROUPS):
        el = jnp.where(gsel == float(gi), lg_t[8 + 8 * gi:16 + 8 * gi], el)
    eidx = lax.broadcasted_iota(I32, el.shape, 0).astype(F32)
    m1 = jnp.max(el, axis=0, keepdims=True)
    i1 = jnp.min(jnp.where(el == m1, eidx, 8.0), axis=0, keepdims=True)
    el2 = jnp.where(eidx == i1, -jnp.inf, el)
    m2 = jnp.max(el2, axis=0, keepdims=True)
    i2 = jnp.min(jnp.where(el2 == m2, eidx, 8.0), axis=0, keepdims=True)
    r = jnp.exp(m2 - m1)
    inv = 1.0 / (1.0 + r)
    gate1 = p_g * inv
    gate2 = p_g * r * inv
    first_lo = i1 < i2
    lo = jnp.minimum(i1, i2)
    hi = jnp.maximum(i1, i2)
    cls = gsel * 64.0 + lo * 8.0 + hi
    return cls, jnp.where(first_lo, gate1, gate2), jnp.where(first_lo, gate2, gate1)


def _norm_and_route(y, ln_g, ln_b, wr_hi, wr_lo, br, x1_ref, route_ref):
    tm, _ = y.shape
    x1 = _layer_norm(y, ln_g, ln_b)
    x1_ref[...] = x1
    x_hi = x1.astype(BF16)
    x_lo = (x1 - x_hi.astype(F32)).astype(BF16)
    lg = _dot(x_hi, wr_hi) + _dot(x_lo, wr_hi) + _dot(x_hi, wr_lo) + br
    cls, g_lo, g_hi = _route_rows(lg.T)
    row = lax.broadcasted_iota(I32, (ROUTE_ROWS, tm), 0)
    route_ref[...] = jnp.where(row == 0, cls, jnp.where(row == 1, g_lo, jnp.where(row == 2, g_hi, 0.0)))


def _mixer_out_specs(tm, d):
    return [pl.BlockSpec((tm, d), lambda i: (i, 0)), pl.BlockSpec((ROUTE_ROWS, tm), lambda i: (0, i))]


def _mixer_out_shape(t, d):
    return [jax.ShapeDtypeStruct((t, d), F32), jax.ShapeDtypeStruct((ROUTE_ROWS, t), F32)]


def _gmlp_kernel(x_ref, w_in_ref, b_in_ref, vn_g_ref, vn_b_ref, w_s_ref, b_st_ref, w_out_ref,
                 ln_g_ref, ln_b_ref, wr_hi_ref, wr_lo_ref, br_ref, x1_ref, route_ref, gated_ref, *, alpha):
    x = x_ref[...]
    tm, _ = x.shape
    groups, chunk, _ = w_s_ref.shape
    z = _dot(x.astype(BF16), w_in_ref[...]) + b_in_ref[...]
    z = 0.5 * z * (1.0 + lax.erf(z * (2.0 ** -0.5)))
    half = z.shape[1] // 2
    gd = half // groups
    u = z[:, :half]
    v = _layer_norm(z[:, half:], vn_g_ref[...], vn_b_ref[...]).astype(BF16)
    r = lax.broadcasted_iota(I32, (chunk, chunk), 0)
    c = lax.broadcasted_iota(I32, (chunk, chunk), 1)
    causal = r >= c
    for g in range(groups):
        w_c = jnp.where(causal, w_s_ref[g], 0.0).astype(BF16)
        bias = b_st_ref[:, g:g + 1]
        for ci in range(tm // chunk):
            rows = slice(ci * chunk, (ci + 1) * chunk)
            cols = slice(g * gd, (g + 1) * gd)
            s = _dot(w_c, v[rows, cols]) + bias
            gated_ref[rows, cols] = (u[rows, cols] * s).astype(BF16)
    mix = _dot(gated_ref[...], w_out_ref[...])
    _norm_and_route(alpha * x + mix, ln_g_ref[...], ln_b_ref[...], wr_hi_ref[...], wr_lo_ref[...],
                    br_ref[...], x1_ref, route_ref)


def _gmlp_layer(xt, w_in, b_in, vn_g, vn_b, w_s, b_s, w_out, ln_g, ln_b, router, alpha, tm):
    t, d = xt.shape
    half = w_out.shape[0]
    wr_hi, wr_lo, br = router
    args = (xt, w_in.astype(BF16), b_in[None], vn_g[None], vn_b[None], w_s, b_s.T, w_out.astype(BF16),
            ln_g[None], ln_b[None], wr_hi, wr_lo, br)
    in_specs = [pl.BlockSpec((tm, d), lambda i: (i, 0))] + [_full(a.shape) for a in args[1:]]
    return pl.pallas_call(
        functools.partial(_gmlp_kernel, alpha=alpha),
        grid=(t // tm,),
        in_specs=in_specs,
        out_specs=_mixer_out_specs(tm, d),
        out_shape=_mixer_out_shape(t, d),
        scratch_shapes=[pltpu.VMEM((tm, half), BF16)],
        compiler_params=_cparams(("parallel",)),
        name="gmlp",
    )(*args)


def _dsa_proj_kernel(x_ref, wq_ref, wc_ref, wqi_ref, wkw_ref, kvg_ref, wuk_t_ref,
                     qlat_ref, ckv_ref, ckv_t_ref, qidx_ref, kidx_ref, wi_t_ref, *, scale_q, scale_w):
    xb = x_ref[...].astype(BF16)
    nq = qlat_ref.shape[0]
    kc = ckv_t_ref.shape[2]
    heads, hd, _ = wuk_t_ref.shape
    q = _dot(xb, wq_ref[...])
    for h in range(heads):
        ql = (_dot(q[:, h * hd:(h + 1) * hd].astype(BF16), wuk_t_ref[h]) * scale_q).astype(BF16)
        for b in range(nq):
            qlat_ref[b, h * Q_BLOCK:(h + 1) * Q_BLOCK, :] = ql[b * Q_BLOCK:(b + 1) * Q_BLOCK]
    c = _dot(xb, wc_ref[...])
    ms = jnp.mean(c * c, axis=-1, keepdims=True)
    c = c * lax.rsqrt(ms + RMS_EPS) * kvg_ref[...]
    ckv_ref[...] = c.astype(BF16)
    for b in range(ckv_t_ref.shape[0]):
        ckv_t_ref[b] = c[b * kc:(b + 1) * kc].T.astype(BF16)
    qi = _dot(xb, wqi_ref[...]).astype(BF16)
    for h in range(IDX_HEADS):
        for b in range(nq):
            qidx_ref[b, h * Q_BLOCK:(h + 1) * Q_BLOCK, :] = qi[b * Q_BLOCK:(b + 1) * Q_BLOCK,
                                                               h * IDX_DIM:(h + 1) * IDX_DIM]
    kw = _dot(xb, wkw_ref[...])
    kidx_ref[...] = kw[:, :IDX_DIM].astype(BF16)
    wi_t_ref[...] = kw.T[IDX_DIM:IDX_DIM + IDX_HEADS] * scale_w


def _dsa_attn_kernel(qidx_ref, wi_t_ref, qlat_ref, kidx_ref, ckv_ref, ckv_t_ref, o_ref,
                     key_ref, key16_ref, acc_ref, m_ref, l_ref, a_ref, p_ref, *, k_sel, heads):
    qb = pl.program_id(1)
    kc = ckv_t_ref.shape[2]
    slab = key_ref.shape[1]
    cps = slab // kc
    nq = Q_BLOCK
    pair = 2 * nq
    n_pairs = heads // 2
    n_chunks = (qb * nq + nq + kc - 1) // kc
    n_slabs = (n_chunks + cps - 1) // cps

    def key_chunk(j):
        return key_ref.at[j // cps, pl.ds(pl.multiple_of((j % cps) * kc, kc), kc), :]

    q_pos = qb * nq + lax.broadcasted_iota(I32, (kc, nq), 1)
    k_off = lax.broadcasted_iota(I32, (kc, nq), 0)

    def key16_chunk(j):
        return key16_ref.at[j // cps, pl.ds(pl.multiple_of((j % cps) * kc, kc), kc), :]

    def pad_chunk(j, carry):
        key_chunk(j)[...] = jnp.full((kc, nq), jnp.nan, F32)
        key16_chunk(j)[...] = jnp.full((kc, nq), jnp.nan, BF16)
        return carry

    lax.fori_loop(n_chunks, n_slabs * cps, pad_chunk, 0)

    def score_chunk(j, carry):
        kch = kidx_ref[pl.ds(pl.multiple_of(j * kc, kc), kc), :]
        tot = None
        for p in range(IDX_HEADS // 2):
            sc = _dot_t(kch, qidx_ref[p * pair:(p + 1) * pair, :])
            for hh in range(2):
                h = 2 * p + hh
                r = jnp.maximum(sc[:, hh * nq:(hh + 1) * nq], 0.0) * wi_t_ref[h:h + 1, :]
                tot = r if tot is None else tot + r
        tot = jnp.where(k_off + j * kc <= q_pos, tot, jnp.nan)
        key_chunk(j)[...] = tot
        key16_chunk(j)[...] = tot.astype(BF16)
        return carry

    lax.fori_loop(0, n_chunks, score_chunk, 0)

    def count(t, strict=False):
        tb = jnp.broadcast_to(t, (COUNT_ACC, nq))

        def body(s, acc):
            for g in range(slab // COUNT_ACC):
                blk = key_ref[s, g * COUNT_ACC:(g + 1) * COUNT_ACC, :]
                acc = jnp.where(blk > tb if strict else blk >= tb, acc + 1.0, acc)
            return acc

        acc = lax.fori_loop(0, n_slabs, body, jnp.zeros((COUNT_ACC, nq), F32))
        return jnp.sum(acc, axis=0, keepdims=True)

    def count16(t):
        rows16 = 2 * COUNT_ACC
        tb = jnp.broadcast_to(t.astype(BF16), (rows16, nq))

        def body(s, acc):
            for g in range(slab // rows16):
                blk = key16_ref[s, g * rows16:(g + 1) * rows16, :]
                acc = jnp.where(blk >= tb, acc + 1.0, acc)
            return acc

        acc = lax.fori_loop(0, n_slabs, body, jnp.zeros((rows16, nq), BF16))
        return jnp.sum(acc.astype(F32), axis=0, keepdims=True)

    def decode(code):
        return lax.bitcast_convert_type(code ^ ((code >> 31) & 0x7FFFFFFF), F32)

    def decode16(c16):
        return decode(jnp.left_shift(c16, 16) | jnp.where(c16 < 0, 0xFFFF, 0))

    def bit16_step(b, c16):
        t = c16 + jnp.left_shift(jnp.int32(1), 15 - b)
        return jnp.where(count16(decode16(t)) >= float(k_sel), t, c16)

    no_code = -(2 ** 15)
    c16 = lax.fori_loop(0, 16, bit16_step, jnp.full((1, nq), no_code, I32))

    def bit_step(b, c):
        t = c + jnp.left_shift(jnp.int32(1), 17 - b)
        return jnp.where((count(decode(t)) >= float(k_sel)) & (t > c), t, c)

    c_sel = lax.fori_loop(0, 18, bit_step, jnp.left_shift(jnp.maximum(c16, no_code + 1) - 1, 16))
    c_sel = jnp.where(c16 == no_code, -jnp.inf, decode(c_sel))
    n_ge = jnp.where(c16 == no_code, 0.0, count(c_sel))

    has_tie = jnp.max(n_ge) > float(k_sel)

    @pl.when(has_tie)
    def _():
        need = float(k_sel) - count(c_sel, strict=True)
        tie_col = n_ge > float(k_sel)
        cb = jnp.broadcast_to(c_sel, (kc, nq))
        lower = (lax.broadcasted_iota(I32, (kc, kc), 1) < lax.broadcasted_iota(I32, (kc, kc), 0))
        lower = jnp.where(lower, 1.0, 0.0).astype(BF16)

        def fix(j, seen):
            blk = key_chunk(j)[...]
            eq = blk == cb
            eq_f = jnp.where(eq, 1.0, 0.0)
            rank = seen + _dot(lower, eq_f.astype(BF16))
            drop = eq & tie_col & (rank >= need)
            key_chunk(j)[...] = jnp.where(drop, jnp.nan, blk)
            return seen + jnp.sum(eq_f, axis=0, keepdims=True)

        lax.fori_loop(0, n_chunks, fix, jnp.zeros((1, nq), F32))

    thr = jnp.broadcast_to(c_sel, (kc, nq))
    m_ref[...] = jnp.full(m_ref.shape, NEG_BIG, F32)
    l_ref[...] = jnp.zeros(l_ref.shape, F32)
    acc_ref[...] = jnp.zeros(acc_ref.shape, F32)

    def softmax_chunk(j):
        ck = ckv_ref[pl.ds(pl.multiple_of(j * kc, kc), kc), :]
        bias = jnp.where(key_chunk(j)[...] >= thr, 0.0, NEG_BIG)
        bias = jnp.concatenate([bias, bias], axis=1)
        slot = j % 2
        for p in range(n_pairs):
            lg = _dot_t(ck, qlat_ref[p * pair:(p + 1) * pair, :]) + bias
            m_old = m_ref[p]
            m_new = jnp.maximum(m_old, jnp.max(lg, axis=0, keepdims=True))
            pr = jnp.exp2(lg - m_new)
            alpha = jnp.exp2(m_old - m_new)
            l_ref[p] = alpha * l_ref[p] + jnp.sum(pr, axis=0, keepdims=True)
            m_ref[p] = m_new
            a_ref[slot, p] = alpha
            p_ref[slot, p] = pr.astype(BF16)

    def accumulate_chunk(j):
        ck_t = ckv_t_ref[j]
        slot = j % 2
        for p in range(n_pairs):
            acc_ref[p] = a_ref[slot, p] * acc_ref[p] + _dot(ck_t, p_ref[slot, p])

    def att_step(j, carry):
        accumulate_chunk(j - 1)
        softmax_chunk(j)
        return carry

    softmax_chunk(0)
    lax.fori_loop(1, n_chunks, att_step, 0)
    accumulate_chunk(n_chunks - 1)
    for p in range(n_pairs):
        o_ref[p * pair:(p + 1) * pair, :] = (acc_ref[p] * (1.0 / l_ref[p])).T.astype(o_ref.dtype)


def _dsa_out_kernel(o_ref, x_ref, wuv_ref, wout_ref, ln_g_ref, ln_b_ref, wr_hi_ref, wr_lo_ref, br_ref,
                    x1_ref, route_ref, o2_ref, *, alpha):
    nq = o_ref.shape[0]
    heads, _, hd = wuv_ref.shape
    for b in range(nq):
        for h in range(heads):
            oh = o_ref[b, h * Q_BLOCK:(h + 1) * Q_BLOCK, :]
            o2_ref[b * Q_BLOCK:(b + 1) * Q_BLOCK, h * hd:(h + 1) * hd] = _dot(oh, wuv_ref[h]).astype(BF16)
    mix = _dot(o2_ref[...], wout_ref[...])
    _norm_and_route(alpha * x_ref[...] + mix, ln_g_ref[...], ln_b_ref[...], wr_hi_ref[...], wr_lo_ref[...],
                    br_ref[...], x1_ref, route_ref)


def _dsa_layer(xt, batch, w_in, kv_g, w_uk, w_uv, w_out, ln_g, ln_b, router, alpha, tm):
    t, d = xt.shape
    seq = t // batch
    heads, lat, hd = w_uk.shape
    qw = heads * hd
    nqb = t // Q_BLOCK
    nq = tm // Q_BLOCK
    k_sel = min(TOPK_MAX, seq // 4)
    kc = min(KEY_CHUNK, seq)
    slab = min(COUNT_SLAB, seq)
    w_in = w_in.astype(BF16)
    o1, o2, o3 = qw, qw + lat, qw + lat + IDX_HEADS * IDX_DIM
    wkw = jnp.pad(w_in[:, o3:], ((0, 0), (0, LANES - (w_in.shape[1] - o3))))
    proj_args = (xt, w_in[:, :o1], w_in[:, o1:o2], w_in[:, o2:o3], wkw, kv_g[None],
                 jnp.swapaxes(w_uk, 1, 2).astype(BF16))
    qlat, ckv, ckv_t, qidx, kidx, wi_t = pl.pallas_call(
        functools.partial(_dsa_proj_kernel, scale_q=hd ** -0.5 * LOG2E, scale_w=(IDX_HEADS * IDX_DIM) ** -0.5),
        grid=(t // tm,),
        in_specs=[pl.BlockSpec((tm, d), lambda i: (i, 0))] + [_full(a.shape) for a in proj_args[1:]],
        out_specs=[pl.BlockSpec((nq, heads * Q_BLOCK, lat), lambda i: (i, 0, 0)),
                   pl.BlockSpec((tm, lat), lambda i: (i, 0)),
                   pl.BlockSpec((tm // kc, lat, kc), lambda i: (i, 0, 0)),
                   pl.BlockSpec((nq, IDX_HEADS * Q_BLOCK, IDX_DIM), lambda i: (i, 0, 0)),
                   pl.BlockSpec((tm, IDX_DIM), lambda i: (i, 0)),
                   pl.BlockSpec((IDX_HEADS, tm), lambda i: (0, i))],
        out_shape=[jax.ShapeDtypeStruct((nqb, heads * Q_BLOCK, lat), BF16),
                   jax.ShapeDtypeStruct((t, lat), BF16),
                   jax.ShapeDtypeStruct((t // kc, lat, kc), BF16),
                   jax.ShapeDtypeStruct((nqb, IDX_HEADS * Q_BLOCK, IDX_DIM), BF16),
                   jax.ShapeDtypeStruct((t, IDX_DIM), BF16),
                   jax.ShapeDtypeStruct((IDX_HEADS, t), F32)],
        compiler_params=_cparams(("parallel",)),
        name="dsa_proj",
    )(*proj_args)

    nq_seq = seq // Q_BLOCK
    n_kc = seq // kc
    rows = heads * Q_BLOCK
    o = pl.pallas_call(
        functools.partial(_dsa_attn_kernel, k_sel=k_sel, heads=heads),
        grid=(batch, nq_seq),
        in_specs=[pl.BlockSpec((None, IDX_HEADS * Q_BLOCK, IDX_DIM), lambda b, q: (b * nq_seq + q, 0, 0)),
                  pl.BlockSpec((IDX_HEADS, Q_BLOCK), lambda b, q: (0, b * nq_seq + q)),
                  pl.BlockSpec((None, rows, lat), lambda b, q: (b * nq_seq + q, 0, 0)),
                  pl.BlockSpec((None, seq, IDX_DIM), lambda b, q: (b, 0, 0)),
                  pl.BlockSpec((None, seq, lat), lambda b, q: (b, 0, 0)),
                  pl.BlockSpec((None, n_kc, lat, kc), lambda b, q: (b, 0, 0, 0))],
        out_specs=pl.BlockSpec((None, rows, lat), lambda b, q: (b * nq_seq + q, 0, 0)),
        out_shape=jax.ShapeDtypeStruct((nqb, rows, lat), BF16),
        scratch_shapes=[pltpu.VMEM((seq // slab, slab, Q_BLOCK), F32),
                        pltpu.VMEM((seq // slab, slab, Q_BLOCK), BF16),
                        pltpu.VMEM((heads // 2, lat, 2 * Q_BLOCK), F32),
                        pltpu.VMEM((heads // 2, 1, 2 * Q_BLOCK), F32),
                        pltpu.VMEM((heads // 2, 1, 2 * Q_BLOCK), F32),
                        pltpu.VMEM((2, heads // 2, 1, 2 * Q_BLOCK), F32),
                        pltpu.VMEM((2, heads // 2, kc, 2 * Q_BLOCK), BF16)],
        compiler_params=_cparams(("parallel", "arbitrary")),
        name="dsa_attn",
    )(qidx, wi_t, qlat, kidx.reshape(batch, seq, IDX_DIM), ckv.reshape(batch, seq, lat),
      ckv_t.reshape(batch, n_kc, lat, kc))

    wr_hi, wr_lo, br = router
    out_args = (o, xt, w_uv.astype(BF16), w_out.astype(BF16), ln_g[None], ln_b[None], wr_hi, wr_lo, br)
    return pl.pallas_call(
        functools.partial(_dsa_out_kernel, alpha=alpha),
        grid=(t // tm,),
        in_specs=[pl.BlockSpec((nq, rows, lat), lambda i: (i, 0, 0)),
                  pl.BlockSpec((tm, d), lambda i: (i, 0))] + [_full(a.shape) for a in out_args[2:]],
        out_specs=_mixer_out_specs(tm, d),
        out_shape=_mixer_out_shape(t, d),
        scratch_shapes=[pltpu.VMEM((tm, qw), BF16)],
        compiler_params=_cparams(("parallel",)),
        name="dsa_out",
    )(*out_args)


DMA_UNROLL = 8


def _start_token_moves(n, copy_of):
    def body(g, carry):
        for u in range(DMA_UNROLL):
            copy_of(g * DMA_UNROLL + u).start(priority=u % 2)
        return carry
    lax.fori_loop(0, n // DMA_UNROLL, body, 0)


def _tok(ref, i, rows):
    return ref.at[pl.ds(i * rows, rows)]


def _rows_of(ref, j, n, rows):
    return ref.at[pl.ds(j, n, stride=rows), :]


class _TokenGather:
    def __init__(self, i, n, idx_hbm, src_hbm, idx_smem, buf, isem, rsem):
        self.i, self.n = i, n
        self.idx_hbm, self.src_hbm, self.idx_smem, self.buf, self.isem, self.rsem = (
            idx_hbm, src_hbm, idx_smem, buf, isem, rsem)
        self.tm = idx_smem.shape[2]
        self.rows = buf.shape[1] // self.tm
        self.slot = i % 2

    def _idx_copy(self, blk, s):
        return pltpu.make_async_copy(self.idx_hbm.at[blk], self.idx_smem.at[s], self.isem.at[s])

    def _token_copy(self, s, r):
        return pltpu.make_async_copy(_tok(self.src_hbm, self.idx_smem[s, 0, r], self.rows),
                                     _tok(self.buf.at[s], r, self.rows), self.rsem.at[s])

    def _wait_tokens(self, s):
        pltpu.make_async_copy(self.src_hbm.at[pl.ds(0, self.tm * self.rows)], self.buf.at[s], self.rsem.at[s]).wait()

    def arrive(self):
        @pl.when(self.i == 0)
        def _():
            first = self._idx_copy(0, 0)
            first.start()
            first.wait()
            _start_token_moves(self.tm, lambda r: self._token_copy(0, r))
            self._idx_copy(min(1, self.n - 1), 1).start()

        self._idx_copy(jnp.minimum(self.i + 1, self.n - 1), 1 - self.slot).wait()
        self._wait_tokens(self.slot)

    def prefetch(self):
        nslot = 1 - self.slot
        for r in range(self.tm):
            self._token_copy(nslot, r).start(priority=r % 2)
        self._idx_copy(jnp.minimum(self.i + 2, self.n - 1), self.slot).start()

    def finish(self):
        @pl.when(self.i == self.n - 1)
        def _():
            self._wait_tokens(1 - self.slot)
            self._idx_copy(self.n - 1, self.slot).wait()


def _plan_kernel(route_ref, x_ref, pos_ref, tinfo_ref, xs_ref,
                 hist_ref, off_ref, carry_ref, stage_ref, ppos_vmem, ppos_smem, fill_vmem, fill_smem, zero_ref,
                 ssem, psem, zsem, *, tile):
    ps = pl.program_id(0)
    b = pl.program_id(1)
    nb = pl.num_programs(1)
    n_cls, pb = hist_ref.shape
    kt = tinfo_ref.shape[1]
    cls = route_ref[0:1, :].astype(I32)
    cid = lax.broadcasted_iota(I32, (n_cls, pb), 0)
    onehot = jnp.where(cid == cls, 1.0, 0.0)

    def zero_fill(start):
        def piece(tok0, n_tok):
            cp = pltpu.make_async_copy(zero_ref.at[pl.ds(0, n_tok * X_ROWS)],
                                       xs_ref.at[pl.ds(tok0 * X_ROWS, n_tok * X_ROWS)], zsem)
            cp.start() if start else cp.wait()

        def per_class(c, carry):
            first, n = fill_smem[0, c], fill_smem[1, c]
            bit = tile // 2
            while bit:
                @pl.when((n & bit) != 0)
                def _(bit=bit):
                    piece(first + (n & -(2 * bit)), bit)
                bit //= 2
            return carry

        lax.fori_loop(0, n_cls, per_class, 0)

        def per_tile(k, carry):
            piece(k * tile, tile)
            return carry

        lax.fori_loop(fill_smem[2, 0], xs_ref.shape[0] // (tile * X_ROWS), per_tile, 0)

    @pl.when(ps == 0)
    def _():
        @pl.when(b == 0)
        def _():
            hist_ref[...] = jnp.zeros(hist_ref.shape, F32)
            zero_ref[...] = jnp.zeros(zero_ref.shape, F32)
        hist_ref[...] += onehot

    @pl.when((ps == 1) & (b == 0))
    def _():
        counts = jnp.sum(hist_ref[...], axis=1, keepdims=True)
        tiles = jnp.floor((counts + float(tile - 1)) * (1.0 / tile))
        lower = lax.broadcasted_iota(I32, (n_cls, n_cls), 1) < lax.broadcasted_iota(I32, (n_cls, n_cls), 0)
        lower = jnp.where(lower, 1.0, 0.0).astype(BF16)
        off = _dot(lower, jnp.broadcast_to(tiles, (n_cls, LANES)).astype(BF16))[:, :1]
        off_ref[...] = off * float(tile)
        carry_ref[...] = jnp.zeros(carry_ref.shape, F32)
        k = lax.broadcasted_iota(I32, (n_cls, kt), 1).astype(F32)
        cid_f = lax.broadcasted_iota(I32, (n_cls, kt), 0).astype(F32)
        mine = (k >= off) & (k < off + tiles)
        tcls = jnp.sum(jnp.where(mine, cid_f, 0.0), axis=0, keepdims=True)
        n_used = jnp.sum(tiles, axis=0, keepdims=True)
        last = jnp.max(jnp.where(tiles > 0.0, cid_f[:, :1], 0.0), axis=0, keepdims=True)
        tcls = jnp.where(k[:1] >= n_used, last, tcls)
        row = lax.broadcasted_iota(I32, (ROUTE_ROWS, kt), 0)
        tinfo_ref[...] = jnp.where(row == 0, tcls, jnp.where(row == 1, n_used, 0.0)).astype(I32)
        as_row = lambda col: jnp.broadcast_to(col, (n_cls, LANES)).T[0:1]
        frow = lax.broadcasted_iota(I32, fill_vmem.shape, 0)
        fill_vmem[...] = jnp.where(frow == 0, as_row(off * float(tile) + counts),
                                   jnp.where(frow == 1, as_row(tiles * float(tile) - counts), n_used)).astype(I32)
        to_fill = pltpu.make_async_copy(fill_vmem, fill_smem, psem)
        to_fill.start()
        to_fill.wait()
        zero_fill(start=True)

    @pl.when(ps == 1)
    def _():
        upper = lax.broadcasted_iota(I32, (pb, pb), 0) < lax.broadcasted_iota(I32, (pb, pb), 1)
        upper = jnp.where(upper, 1.0, 0.0).astype(BF16)
        before = _dot(onehot.astype(BF16), upper)
        ppos = jnp.sum(onehot * (before + carry_ref[...] + off_ref[...]), axis=0, keepdims=True).astype(I32)
        carry_ref[...] += jnp.sum(onehot, axis=1, keepdims=True)
        pos_ref[...] = ppos
        ppos_vmem[...] = ppos
        to_smem = pltpu.make_async_copy(ppos_vmem, ppos_smem, psem)
        to_smem.start()

        slot = b % 2
        stage = stage_ref.at[slot]

        def drain(s):
            pltpu.make_async_copy(stage_ref.at[s], xs_ref.at[pl.ds(0, pb * X_ROWS)], ssem.at[s]).wait()

        @pl.when(b >= 2)
        def _():
            drain(slot)

        for j in range(Y_ROWS):
            _rows_of(stage, j, pb, X_ROWS)[...] = x_ref[:, j * LANES:(j + 1) * LANES]
        record = jnp.concatenate([route_ref[...], jnp.zeros((LANES - ROUTE_ROWS, pb), F32)], axis=0).T
        _rows_of(stage, Y_ROWS, pb, X_ROWS)[...] = record

        to_smem.wait()
        _start_token_moves(pb, lambda r: pltpu.make_async_copy(
            _tok(stage, r, X_ROWS), _tok(xs_ref, ppos_smem[0, r], X_ROWS), ssem.at[slot]))

        @pl.when(b == nb - 1)
        def _():
            drain(slot)

            @pl.when(nb > 1)
            def _():
                drain(1 - slot)

            zero_fill(start=False)


def _moe_kernel(tcls_ref, nused_ref, xs_ref, win_ref, wout_ref, ys_ref):
    i = pl.program_id(0)
    tm = xs_ref.shape[0] // X_ROWS
    ff = wout_ref.shape[1]

    @pl.when(i < nused_ref[0])
    def _():
        cls = tcls_ref[i]
        xb = jnp.concatenate([_rows_of(xs_ref, j, tm, X_ROWS)[...].astype(BF16) for j in range(Y_ROWS)], axis=1)
        record = _rows_of(xs_ref, Y_ROWS, tm, X_ROWS)[...]
        y = None
        for e, gate in (((cls >> 3) & 7, record[:, 1:2]), (cls & 7, record[:, 2:3])):
            h = _dot(xb, win_ref[e])
            a = h[:, :ff]
            act = (a * jax.nn.sigmoid(a) * h[:, ff:] * gate).astype(BF16)
            ye = _dot(act, wout_ref[e])
            y = ye if y is None else y + ye
        for j in range(Y_ROWS):
            _rows_of(ys_ref, j, tm, Y_ROWS)[...] = y[:, j * LANES:(j + 1) * LANES]

    @pl.when(i >= nused_ref[0])
    def _():
        ys_ref[...] = jnp.zeros(ys_ref.shape, F32)


def _post_kernel(pos_ref, ys_ref, x1_ref, p_ref, ln_g_ref, ln_b_ref, pw_ref, gw_ref, gb_ref, out_ref,
                 idx_smem, ybuf, isem, rsem, *, alpha, n_blocks):
    i = pl.program_id(0)
    tm = x1_ref.shape[0]
    gather = _TokenGather(i, n_blocks, pos_ref, ys_ref, idx_smem, ybuf, isem, rsem)
    gather.arrive()
    gather.prefetch()
    yb = ybuf.at[i % 2]
    ffn = jnp.concatenate([_rows_of(yb, j, tm, Y_ROWS)[...] for j in range(Y_ROWS)], axis=1)
    x2 = _layer_norm(alpha * x1_ref[...] + ffn, ln_g_ref[...], ln_b_ref[...])
    gate = jax.nn.sigmoid(_dot(x2.astype(BF16), gw_ref[...]) + gb_ref[...])
    out_ref[...] = x2 + _dot(p_ref[...].astype(BF16), pw_ref[...]) * gate
    gather.finish()


def _moe_and_post(x1, route, layer, p, e_w_in, e_w_out, ln_g, ln_b, ple_w, ple_gw, ple_gb, alpha):
    t, d = x1.shape
    assert d == LANES * Y_ROWS
    depth, n_exp, _, ff2 = e_w_in.shape
    epg = n_exp // N_EGROUPS
    pb = _tile(t, PLAN_BLOCK)
    nb = t // pb
    nt = t // MOE_TILE + N_EGROUPS * (epg * (epg - 1) // 2)
    kt = -(-nt // LANES) * LANES
    n_sorted = nt * MOE_TILE

    pos, tinfo, xs = pl.pallas_call(
        functools.partial(_plan_kernel, tile=MOE_TILE),
        grid=(2, nb),
        in_specs=[pl.BlockSpec((ROUTE_ROWS, pb), lambda ps, b: (0, b)),
                  pl.BlockSpec((pb, d), lambda ps, b: (ps * b, 0))],
        out_specs=[pl.BlockSpec((None, 1, pb), lambda ps, b: (ps * b, 0, 0)),
                   pl.BlockSpec((ROUTE_ROWS, kt), lambda ps, b: (0, 0)),
                   pl.BlockSpec(memory_space=pl.ANY)],
        out_shape=[jax.ShapeDtypeStruct((nb, 1, pb), I32),
                   jax.ShapeDtypeStruct((ROUTE_ROWS, kt), I32),
                   jax.ShapeDtypeStruct((n_sorted * X_ROWS, LANES), F32)],
        scratch_shapes=[pltpu.VMEM((N_CLASS, pb), F32),
                        pltpu.VMEM((N_CLASS, 1), F32),
                        pltpu.VMEM((N_CLASS, 1), F32),
                        pltpu.VMEM((2, pb * X_ROWS, LANES), F32),
                        pltpu.VMEM((1, pb), I32),
                        pltpu.SMEM((1, pb), I32),
                        pltpu.VMEM((ROUTE_ROWS, N_CLASS), I32),
                        pltpu.SMEM((ROUTE_ROWS, N_CLASS), I32),
                        pltpu.VMEM((MOE_TILE * X_ROWS, LANES), F32),
                        pltpu.SemaphoreType.DMA((2,)),
                        pltpu.SemaphoreType.DMA(()),
                        pltpu.SemaphoreType.DMA(())],
        compiler_params=_cparams(("arbitrary", "arbitrary")),
        name="plan",
    )(route, x1)

    w_in = e_w_in.reshape(depth, N_EGROUPS, epg, d, ff2)
    w_out = e_w_out.reshape(depth, N_EGROUPS, epg, ff2 // 2, d)
    ys = pl.pallas_call(
        _moe_kernel,
        grid_spec=pltpu.PrefetchScalarGridSpec(
            num_scalar_prefetch=2,
            grid=(nt,),
            in_specs=[pl.BlockSpec((MOE_TILE * X_ROWS, LANES), lambda i, tc, nu: (i, 0)),
                      pl.BlockSpec((None, None, epg, d, ff2), lambda i, tc, nu: (layer, tc[i] >> 6, 0, 0, 0),
                                   pipeline_mode=pl.Buffered(1)),
                      pl.BlockSpec((None, None, epg, ff2 // 2, d), lambda i, tc, nu: (layer, tc[i] >> 6, 0, 0, 0),
                                   pipeline_mode=pl.Buffered(1))],
            out_specs=pl.BlockSpec((MOE_TILE * Y_ROWS, LANES), lambda i, tc, nu: (i, 0))),
        out_shape=jax.ShapeDtypeStruct((n_sorted * Y_ROWS, LANES), F32),
        compiler_params=_cparams(("arbitrary",)),
        name="moe",
    )(tinfo[0, :nt], tinfo[1, :1], xs, w_in, w_out)

    post_args = (pos, ys, x1, p, ln_g[None], ln_b[None], ple_w.astype(BF16), ple_gw.astype(BF16), ple_gb[None])
    return pl.pallas_call(
        functools.partial(_post_kernel, alpha=alpha, n_blocks=nb),
        grid=(nb,),
        in_specs=[pl.BlockSpec(memory_space=pl.ANY),
                  pl.BlockSpec(memory_space=pl.ANY),
                  pl.BlockSpec((pb, d), lambda i: (i, 0)),
                  pl.BlockSpec((None, pb, p.shape[2]), lambda i: (layer, i, 0))]
                 + [_full(a.shape) for a in post_args[4:]],
        out_specs=pl.BlockSpec((pb, d), lambda i: (i, 0)),
        out_shape=jax.ShapeDtypeStruct((t, d), F32),
        scratch_shapes=[pltpu.SMEM((2, 1, pb), I32), pltpu.VMEM((2, pb * Y_ROWS, LANES), F32),
                        pltpu.SemaphoreType.DMA((2,)), pltpu.SemaphoreType.DMA((2,))],
        compiler_params=_cparams(("arbitrary",)),
        name="post",
    )(*post_args)


def _router_params(wg, bg, we, be):
    d = wg.shape[0]
    w = jnp.zeros((d, ROUTE_COLS), F32).at[:, :N_EGROUPS].set(wg).at[:, 8:8 + we.shape[1]].set(we)
    b = jnp.zeros((1, ROUTE_COLS), F32).at[0, :N_EGROUPS].set(bg).at[0, 8:8 + be.shape[0]].set(be)
    hi = w.astype(BF16)
    return hi, (w - hi.astype(F32)).astype(BF16), b


def _tile(t, want):
    while t % want:
        want //= 2
    return want


def kernel(x, p, a_w_in, a_b_in, a_vn_g, a_vn_b, a_w_s, a_b_s, a_w_out, b_w_in, b_kv_g, b_w_uk, b_w_uv, b_w_out,
           ln1_g, ln1_b, ln2_g, ln2_b, r_wg, r_bg, r_we, r_be, e_w_in, e_w_out, ple_w, ple_gw, ple_gb):
    batch, seq, d = x.shape
    t = batch * seq
    depth = p.shape[0]
    alpha = (2 * depth) ** 0.25
    chunk = a_w_s.shape[-1]
    xt = x.reshape(t, d)
    pt = p.reshape(depth, t, p.shape[-1])
    for i in range(depth):
        j = i // 2
        router = _router_params(r_wg[i], r_bg[i], r_we[i], r_be[i])
        if i % 2 == 0:
            x1, route = _gmlp_layer(xt, a_w_in[j], a_b_in[j], a_vn_g[j], a_vn_b[j], a_w_s[j], a_b_s[j],
                                    a_w_out[j], ln1_g[i], ln1_b[i], router, alpha, max(chunk, _tile(t, 512)))
        else:
            x1, route = _dsa_layer(xt, batch, b_w_in[j], b_kv_g[j], b_w_uk[j], b_w_uv[j], b_w_out[j],
                                   ln1_g[i], ln1_b[i], router, alpha, _tile(t, 512))
        xt = _moe_and_post(x1, route, i, pt, e_w_in, e_w_out, ln2_g[i], ln2_b[i],
                           ple_w[i], ple_gw[i], ple_gb[i], alpha)
    return xt.reshape(batch, seq, d)
```

```python
import functools

import jax
import jax.numpy as jnp
from jax import lax
from jax.experimental import pallas as pl
from jax.experimental.pallas import tpu as pltpu

F32, BF16, I32 = jnp.float32, jnp.bfloat16, jnp.int32

IDX_HEADS = 8
IDX_DIM = 64
TOPK_MAX = 256
N_EGROUPS = 4
LN_EPS = 1e-5
RMS_EPS = 1e-6

LANES = 128
Q_BLOCK = 128
KEY_CHUNK = 512
COUNT_SLAB = 1024
COUNT_ACC = 64
LOG2E = 1.4426950408889634
VMEM_LIMIT = 48 * 1024 * 1024

ROUTE_COLS = LANES
ROUTE_ROWS = 8
N_CLASS = 64 * N_EGROUPS
Y_ROWS = 8
X_ROWS = Y_ROWS + 1
PLAN_BLOCK = 256
MOE_TILE = 256
INT_MIN = -(2 ** 31)
NEG_BIG = -1e30


def _cparams(sem, vmem=VMEM_LIMIT):
    return pltpu.CompilerParams(dimension_semantics=sem, vmem_limit_bytes=vmem)


def _full(shape):
    n = len(shape)
    return pl.BlockSpec(shape, lambda *_: (0,) * n)


def _dot(a, b):
    return jnp.dot(a, b, preferred_element_type=F32)


def _dot_t(a, b):
    return lax.dot_general(a, b, (((1,), (1,)), ((), ())), preferred_element_type=F32)


def _layer_norm(x, g, b):
    mu = jnp.mean(x, axis=-1, keepdims=True)
    xc = x - mu
    var = jnp.mean(xc * xc, axis=-1, keepdims=True)
    return xc * lax.rsqrt(var + LN_EPS) * g + b


def _route_rows(lg_t):
    g = [lg_t[i:i + 1] for i in range(N_EGROUPS)]
    gmax = functools.reduce(jnp.maximum, g)
    gsel = jnp.where(g[0] >= gmax, 0.0, jnp.where(g[1] >= gmax, 1.0, jnp.where(g[2] >= gmax, 2.0, 3.0)))
    den = functools.reduce(lambda a, b: a + b, [jnp.exp(gi - gmax) for gi in g])
    p_g = 1.0 / den
    el = lg_t[8:16]
    for gi in range(1, N_EGROUPS):
        el = jnp.where(gsel == float(gi), lg_t[8 + 8 * gi:16 + 8 * gi], el)
    eidx = lax.broadcasted_iota(I32, el.shape, 0).astype(F32)
    m1 = jnp.max(el, axis=0, keepdims=True)
    i1 = jnp.min(jnp.where(el == m1, eidx, 8.0), axis=0, keepdims=True)
    el2 = jnp.where(eidx == i1, -jnp.inf, el)
    m2 = jnp.max(el2, axis=0, keepdims=True)
    i2 = jnp.min(jnp.where(el2 == m2, eidx, 8.0), axis=0, keepdims=True)
    r = jnp.exp(m2 - m1)
    inv = 1.0 / (1.0 + r)
    gate1 = p_g * inv
    gate2 = p_g * r * inv
    first_lo = i1 < i2
    lo = jnp.minimum(i1, i2)
    hi = jnp.maximum(i1, i2)
    cls = gsel * 64.0 + lo * 8.0 + hi
    return cls, jnp.where(first_lo, gate1, gate2), jnp.where(first_lo, gate2, gate1)


def _norm_and_route(y, ln_g, ln_b, wr_hi, wr_lo, br, x1_ref, route_ref):
    tm, _ = y.shape
    x1 = _layer_norm(y, ln_g, ln_b)
    x1_ref[...] = x1
    x_hi = x1.astype(BF16)
    x_lo = (x1 - x_hi.astype(F32)).astype(BF16)
    lg = _dot(x_hi, wr_hi) + _dot(x_lo, wr_hi) + _dot(x_hi, wr_lo) + br
    cls, g_lo, g_hi = _route_rows(lg.T)
    row = lax.broadcasted_iota(I32, (ROUTE_ROWS, tm), 0)
    route_ref[...] = jnp.where(row == 0, cls, jnp.where(row == 1, g_lo, jnp.where(row == 2, g_hi, 0.0)))


def _mixer_out_specs(tm, d):
    return [pl.BlockSpec((tm, d), lambda i: (i, 0)), pl.BlockSpec((ROUTE_ROWS, tm), lambda i: (0, i))]


def _mixer_out_shape(t, d):
    return [jax.ShapeDtypeStruct((t, d), F32), jax.ShapeDtypeStruct((ROUTE_ROWS, t), F32)]


def _gmlp_kernel(x_ref, w_in_ref, b_in_ref, vn_g_ref, vn_b_ref, w_s_ref, b_st_ref, w_out_ref,
                 ln_g_ref, ln_b_ref, wr_hi_ref, wr_lo_ref, br_ref, x1_ref, route_ref, gated_ref, *, alpha):
    x = x_ref[...]
    tm, _ = x.shape
    groups, chunk, _ = w_s_ref.shape
    z = _dot(x.astype(BF16), w_in_ref[...]) + b_in_ref[...]
    z = 0.5 * z * (1.0 + lax.erf(z * (2.0 ** -0.5)))
    half = z.shape[1] // 2
    gd = half // groups
    u = z[:, :half]
    v = _layer_norm(z[:, half:], vn_g_ref[...], vn_b_ref[...]).astype(BF16)
    r = lax.broadcasted_iota(I32, (chunk, chunk), 0)
    c = lax.broadcasted_iota(I32, (chunk, chunk), 1)
    causal = r >= c
    for g in range(groups):
        w_c = jnp.where(causal, w_s_ref[g], 0.0).astype(BF16)
        bias = b_st_ref[:, g:g + 1]
        for ci in range(tm // chunk):
            rows = slice(ci * chunk, (ci + 1) * chunk)
            cols = slice(g * gd, (g + 1) * gd)
            s = _dot(w_c, v[rows, cols]) + bias
            gated_ref[rows, cols] = (u[rows, cols] * s).astype(BF16)
    mix = _dot(gated_ref[...], w_out_ref[...])
    _norm_and_route(alpha * x + mix, ln_g_ref[...], ln_b_ref[...], wr_hi_ref[...], wr_lo_ref[...],
                    br_ref[...], x1_ref, route_ref)


def _gmlp_layer(xt, w_in, b_in, vn_g, vn_b, w_s, b_s, w_out, ln_g, ln_b, router, alpha, tm):
    t, d = xt.shape
    half = w_out.shape[0]
    wr_hi, wr_lo, br = router
    args = (xt, w_in.astype(BF16), b_in[None], vn_g[None], vn_b[None], w_s, b_s.T, w_out.astype(BF16),
            ln_g[None], ln_b[None], wr_hi, wr_lo, br)
    in_specs = [pl.BlockSpec((tm, d), lambda i: (i, 0))] + [_full(a.shape) for a in args[1:]]
    return pl.pallas_call(
        functools.partial(_gmlp_kernel, alpha=alpha),
        grid=(t // tm,),
        in_specs=in_specs,
        out_specs=_mixer_out_specs(tm, d),
        out_shape=_mixer_out_shape(t, d),
        scratch_shapes=[pltpu.VMEM((tm, half), BF16)],
        compiler_params=_cparams(("parallel",)),
        name="gmlp",
    )(*args)


def _dsa_proj_kernel(x_ref, wq_ref, wc_ref, wqi_ref, wkw_ref, kvg_ref, wuk_t_ref,
                     qlat_ref, ckv_ref, ckv_t_ref, qidx_ref, kidx_ref, wi_t_ref, *, scale_q, scale_w):
    xb = x_ref[...].astype(BF16)
    nq = qlat_ref.shape[0]
    kc = ckv_t_ref.shape[2]
    heads, hd, _ = wuk_t_ref.shape
    q = _dot(xb, wq_ref[...])
    for h in range(heads):
        ql = (_dot(q[:, h * hd:(h + 1) * hd].astype(BF16), wuk_t_ref[h]) * scale_q).astype(BF16)
        for b in range(nq):
            qlat_ref[b, h * Q_BLOCK:(h + 1) * Q_BLOCK, :] = ql[b * Q_BLOCK:(b + 1) * Q_BLOCK]
    c = _dot(xb, wc_ref[...])
    ms = jnp.mean(c * c, axis=-1, keepdims=True)
    c = c * lax.rsqrt(ms + RMS_EPS) * kvg_ref[...]
    ckv_ref[...] = c.astype(BF16)
    for b in range(ckv_t_ref.shape[0]):
        ckv_t_ref[b] = c[b * kc:(b + 1) * kc].T.astype(BF16)
    qi = _dot(xb, wqi_ref[...]).astype(BF16)
    for h in range(IDX_HEADS):
        for b in range(nq):
            qidx_ref[b, h * Q_BLOCK:(h + 1) * Q_BLOCK, :] = qi[b * Q_BLOCK:(b + 1) * Q_BLOCK,
                                                               h * IDX_DIM:(h + 1) * IDX_DIM]
    kw = _dot(xb, wkw_ref[...])
    kidx_ref[...] = kw[:, :IDX_DIM].astype(BF16)
    wi_t_ref[...] = kw.T[IDX_DIM:IDX_DIM + IDX_HEADS] * scale_w


def _dsa_attn_kernel(qidx_ref, wi_t_ref, qlat_ref, kidx_ref, ckv_ref, ckv_t_ref, o_ref,
                     key_ref, key16_ref, acc_ref, m_ref, l_ref, a_ref, p_ref, *, k_sel, heads):
    qb = pl.program_id(1)
    kc = ckv_t_ref.shape[2]
    slab = key_ref.shape[1]
    cps = slab // kc
    nq = Q_BLOCK
    pair = 2 * nq
    n_pairs = heads // 2
    n_chunks = (qb * nq + nq + kc - 1) // kc
    n_slabs = (n_chunks + cps - 1) // cps

    def key_chunk(j):
        return key_ref.at[j // cps, pl.ds(pl.multiple_of((j % cps) * kc, kc), kc), :]

    q_pos = qb * nq + lax.broadcasted_iota(I32, (kc, nq), 1)
    k_off = lax.broadcasted_iota(I32, (kc, nq), 0)

    def key16_chunk(j):
        return key16_ref.at[j // cps, pl.ds(pl.multiple_of((j % cps) * kc, kc), kc), :]

    def pad_chunk(j, carry):
        key_chunk(j)[...] = jnp.full((kc, nq), jnp.nan, F32)
        key16_chunk(j)[...] = jnp.full((kc, nq), jnp.nan, BF16)
        return carry

    lax.fori_loop(n_chunks, n_slabs * cps, pad_chunk, 0)

    def score_chunk(j, carry):
        kch = kidx_ref[pl.ds(pl.multiple_of(j * kc, kc), kc), :]
        tot = None
        for p in range(IDX_HEADS // 2):
            sc = _dot_t(kch, qidx_ref[p * pair:(p + 1) * pair, :])
            for hh in range(2):
                h = 2 * p + hh
                r = jnp.maximum(sc[:, hh * nq:(hh + 1) * nq], 0.0) * wi_t_ref[h:h + 1, :]
                tot = r if tot is None else tot + r
        tot = jnp.where(k_off + j * kc <= q_pos, tot, jnp.nan)
        key_chunk(j)[...] = tot
        key16_chunk(j)[...] = tot.astype(BF16)
        return carry

    lax.fori_loop(0, n_chunks, score_chunk, 0)

    def count(t, strict=False):
        tb = jnp.broadcast_to(t, (COUNT_ACC, nq))

        def body(s, acc):
            for g in range(slab // COUNT_ACC):
                blk = key_ref[s, g * COUNT_ACC:(g + 1) * COUNT_ACC, :]
                acc = jnp.where(blk > tb if strict else blk >= tb, acc + 1.0, acc)
            return acc

        acc = lax.fori_loop(0, n_slabs, body, jnp.zeros((COUNT_ACC, nq), F32))
        return jnp.sum(acc, axis=0, keepdims=True)

    def count16(t):
        rows16 = 2 * COUNT_ACC
        tb = jnp.broadcast_to(t.astype(BF16), (rows16, nq))

        def body(s, acc):
            for g in range(slab // rows16):
                blk = key16_ref[s, g * rows16:(g + 1) * rows16, :]
                acc = jnp.where(blk >= tb, acc + 1.0, acc)
            return acc

        acc = lax.fori_loop(0, n_slabs, body, jnp.zeros((rows16, nq), BF16))
        return jnp.sum(acc.astype(F32), axis=0, keepdims=True)

    def decode(code):
        return lax.bitcast_convert_type(code ^ ((code >> 31) & 0x7FFFFFFF), F32)

    def decode16(c16):
        return decode(jnp.left_shift(c16, 16) | jnp.where(c16 < 0, 0xFFFF, 0))

    def bit16_step(b, c16):
        t = c16 + jnp.left_shift(jnp.int32(1), 15 - b)
        return jnp.where(count16(decode16(t)) >= float(k_sel), t, c16)

    no_code = -(2 ** 15)
    c16 = lax.fori_loop(0, 16, bit16_step, jnp.full((1, nq), no_code, I32))

    def bit_step(b, c):
        t = c + jnp.left_shift(jnp.int32(1), 17 - b)
        return jnp.where((count(decode(t)) >= float(k_sel)) & (t > c), t, c)

    c_sel = lax.fori_loop(0, 18, bit_step, jnp.left_shift(jnp.maximum(c16, no_code + 1) - 1, 16))
    c_sel = jnp.where(c16 == no_code, -jnp.inf, decode(c_sel))
    n_ge = jnp.where(c16 == no_code, 0.0, count(c_sel))

    has_tie = jnp.max(n_ge) > float(k_sel)

    @pl.when(has_tie)
    def _():
        need = float(k_sel) - count(c_sel, strict=True)
        tie_col = n_ge > float(k_sel)
        cb = jnp.broadcast_to(c_sel, (kc, nq))
        lower = (lax.broadcasted_iota(I32, (kc, kc), 1) < lax.broadcasted_iota(I32, (kc, kc), 0))
        lower = jnp.where(lower, 1.0, 0.0).astype(BF16)

        def fix(j, seen):
            blk = key_chunk(j)[...]
            eq = blk == cb
            eq_f = jnp.where(eq, 1.0, 0.0)
            rank = seen + _dot(lower, eq_f.astype(BF16))
            drop = eq & tie_col & (rank >= need)
            key_chunk(j)[...] = jnp.where(drop, jnp.nan, blk)
            return seen + jnp.sum(eq_f, axis=0, keepdims=True)

        lax.fori_loop(0, n_chunks, fix, jnp.zeros((1, nq), F32))

    thr = jnp.broadcast_to(c_sel, (kc, nq))
    m_ref[...] = jnp.full(m_ref.shape, NEG_BIG, F32)
    l_ref[...] = jnp.zeros(l_ref.shape, F32)
    acc_ref[...] = jnp.zeros(acc_ref.shape, F32)

    def softmax_chunk(j):
        ck = ckv_ref[pl.ds(pl.multiple_of(j * kc, kc), kc), :]
        bias = jnp.where(key_chunk(j)[...] >= thr, 0.0, NEG_BIG)
        bias = jnp.concatenate([bias, bias], axis=1)
        slot = j % 2
        for p in range(n_pairs):
            lg = _dot_t(ck, qlat_ref[p * pair:(p + 1) * pair, :]) + bias
            m_old = m_ref[p]
            m_new = jnp.maximum(m_old, jnp.max(lg, axis=0, keepdims=True))
            pr = jnp.exp2(lg - m_new)
            alpha = jnp.exp2(m_old - m_new)
            l_ref[p] = alpha * l_ref[p] + jnp.sum(pr, axis=0, keepdims=True)
            m_ref[p] = m_new
            a_ref[slot, p] = alpha
            p_ref[slot, p] = pr.astype(BF16)

    def accumulate_chunk(j):
        ck_t = ckv_t_ref[j]
        slot = j % 2
        for p in range(n_pairs):
            acc_ref[p] = a_ref[slot, p] * acc_ref[p] + _dot(ck_t, p_ref[slot, p])

    def att_step(j, carry):
        accumulate_chunk(j - 1)
        softmax_chunk(j)
        return carry

    softmax_chunk(0)
    lax.fori_loop(1, n_chunks, att_step, 0)
    accumulate_chunk(n_chunks - 1)
    for p in range(n_pairs):
        o_ref[p * pair:(p + 1) * pair, :] = (acc_ref[p] * (1.0 / l_ref[p])).T.astype(o_ref.dtype)


def _dsa_out_kernel(o_ref, x_ref, wuv_ref, wout_ref, ln_g_ref, ln_b_ref, wr_hi_ref, wr_lo_ref, br_ref,
                    x1_ref, route_ref, o2_ref, *, alpha):
    nq = o_ref.shape[0]
    heads, _, hd = wuv_ref.shape
    for b in range(nq):
        for h in range(heads):
            oh = o_ref[b, h * Q_BLOCK:(h + 1) * Q_BLOCK, :]
            o2_ref[b * Q_BLOCK:(b + 1) * Q_BLOCK, h * hd:(h + 1) * hd] = _dot(oh, wuv_ref[h]).astype(BF16)
    mix = _dot(o2_ref[...], wout_ref[...])
    _norm_and_route(alpha * x_ref[...] + mix, ln_g_ref[...], ln_b_ref[...], wr_hi_ref[...], wr_lo_ref[...],
                    br_ref[...], x1_ref, route_ref)


def _dsa_layer(xt, batch, w_in, kv_g, w_uk, w_uv, w_out, ln_g, ln_b, router, alpha, tm):
    t, d = xt.shape
    seq = t // batch
    heads, lat, hd = w_uk.shape
    qw = heads * hd
    nqb = t // Q_BLOCK
    nq = tm // Q_BLOCK
    k_sel = min(TOPK_MAX, seq // 4)
    kc = min(KEY_CHUNK, seq)
    slab = min(COUNT_SLAB, seq)
    w_in = w_in.astype(BF16)
    o1, o2, o3 = qw, qw + lat, qw + lat + IDX_HEADS * IDX_DIM
    wkw = jnp.pad(w_in[:, o3:], ((0, 0), (0, LANES - (w_in.shape[1] - o3))))
    proj_args = (xt, w_in[:, :o1], w_in[:, o1:o2], w_in[:, o2:o3], wkw, kv_g[None],
                 jnp.swapaxes(w_uk, 1, 2).astype(BF16))
    qlat, ckv, ckv_t, qidx, kidx, wi_t = pl.pallas_call(
        functools.partial(_dsa_proj_kernel, scale_q=hd ** -0.5 * LOG2E, scale_w=(IDX_HEADS * IDX_DIM) ** -0.5),
        grid=(t // tm,),
        in_specs=[pl.BlockSpec((tm, d), lambda i: (i, 0))] + [_full(a.shape) for a in proj_args[1:]],
        out_specs=[pl.BlockSpec((nq, heads * Q_BLOCK, lat), lambda i: (i, 0, 0)),
                   pl.BlockSpec((tm, lat), lambda i: (i, 0)),
                   pl.BlockSpec((tm // kc, lat, kc), lambda i: (i, 0, 0)),
                   pl.BlockSpec((nq, IDX_HEADS * Q_BLOCK, IDX_DIM), lambda i: (i, 0, 0)),
                   pl.BlockSpec((tm, IDX_DIM), lambda i: (i, 0)),
                   pl.BlockSpec((IDX_HEADS, tm), lambda i: (0, i))],
        out_shape=[jax.ShapeDtypeStruct((nqb, heads * Q_BLOCK, lat), BF16),
                   jax.ShapeDtypeStruct((t, lat), BF16),
                   jax.ShapeDtypeStruct((t // kc, lat, kc), BF16),
                   jax.ShapeDtypeStruct((nqb, IDX_HEADS * Q_BLOCK, IDX_DIM), BF16),
                   jax.ShapeDtypeStruct((t, IDX_DIM), BF16),
                   jax.ShapeDtypeStruct((IDX_HEADS, t), F32)],
        compiler_params=_cparams(("parallel",)),
        name="dsa_proj",
    )(*proj_args)

    nq_seq = seq // Q_BLOCK
    n_kc = seq // kc
    rows = heads * Q_BLOCK
    o = pl.pallas_call(
        functools.partial(_dsa_attn_kernel, k_sel=k_sel, heads=heads),
        grid=(batch, nq_seq),
        in_specs=[pl.BlockSpec((None, IDX_HEADS * Q_BLOCK, IDX_DIM), lambda b, q: (b * nq_seq + q, 0, 0)),
                  pl.BlockSpec((IDX_HEADS, Q_BLOCK), lambda b, q: (0, b * nq_seq + q)),
                  pl.BlockSpec((None, rows, lat), lambda b, q: (b * nq_seq + q, 0, 0)),
                  pl.BlockSpec((None, seq, IDX_DIM), lambda b, q: (b, 0, 0)),
                  pl.BlockSpec((None, seq, lat), lambda b, q: (b, 0, 0)),
                  pl.BlockSpec((None, n_kc, lat, kc), lambda b, q: (b, 0, 0, 0))],
        out_specs=pl.BlockSpec((None, rows, lat), lambda b, q: (b * nq_seq + q, 0, 0)),
        out_shape=jax.ShapeDtypeStruct((nqb, rows, lat), BF16),
        scratch_shapes=[pltpu.VMEM((seq // slab, slab, Q_BLOCK), F32),
                        pltpu.VMEM((seq // slab, slab, Q_BLOCK), BF16),
                        pltpu.VMEM((heads // 2, lat, 2 * Q_BLOCK), F32),
                        pltpu.VMEM((heads // 2, 1, 2 * Q_BLOCK), F32),
                        pltpu.VMEM((heads // 2, 1, 2 * Q_BLOCK), F32),
                        pltpu.VMEM((2, heads // 2, 1, 2 * Q_BLOCK), F32),
                        pltpu.VMEM((2, heads // 2, kc, 2 * Q_BLOCK), BF16)],
        compiler_params=_cparams(("parallel", "arbitrary")),
        name="dsa_attn",
    )(qidx, wi_t, qlat, kidx.reshape(batch, seq, IDX_DIM), ckv.reshape(batch, seq, lat),
      ckv_t.reshape(batch, n_kc, lat, kc))

    wr_hi, wr_lo, br = router
    out_args = (o, xt, w_uv.astype(BF16), w_out.astype(BF16), ln_g[None], ln_b[None], wr_hi, wr_lo, br)
    return pl.pallas_call(
        functools.partial(_dsa_out_kernel, alpha=alpha),
        grid=(t // tm,),
        in_specs=[pl.BlockSpec((nq, rows, lat), lambda i: (i, 0, 0)),
                  pl.BlockSpec((tm, d), lambda i: (i, 0))] + [_full(a.shape) for a in out_args[2:]],
        out_specs=_mixer_out_specs(tm, d),
        out_shape=_mixer_out_shape(t, d),
        scratch_shapes=[pltpu.VMEM((tm, qw), BF16)],
        compiler_params=_cparams(("parallel",)),
        name="dsa_out",
    )(*out_args)


DMA_UNROLL = 8


def _start_token_moves(n, copy_of):
    def body(g, carry):
        for u in range(DMA_UNROLL):
            copy_of(g * DMA_UNROLL + u).start(priority=u % 2)
        return carry
    lax.fori_loop(0, n // DMA_UNROLL, body, 0)


def _tok(ref, i, rows):
    return ref.at[pl.ds(i * rows, rows)]


def _rows_of(ref, j, n, rows):
    return ref.at[pl.ds(j, n, stride=rows), :]


class _TokenGather:
    def __init__(self, i, n, idx_hbm, src_hbm, idx_smem, buf, isem, rsem):
        self.i, self.n = i, n
        self.idx_hbm, self.src_hbm, self.idx_smem, self.buf, self.isem, self.rsem = (
            idx_hbm, src_hbm, idx_smem, buf, isem, rsem)
        self.tm = idx_smem.shape[2]
        self.rows = buf.shape[1] // self.tm
        self.slot = i % 2

    def _idx_copy(self, blk, s):
        return pltpu.make_async_copy(self.idx_hbm.at[blk], self.idx_smem.at[s], self.isem.at[s])

    def _token_copy(self, s, r):
        return pltpu.make_async_copy(_tok(self.src_hbm, self.idx_smem[s, 0, r], self.rows),
                                     _tok(self.buf.at[s], r, self.rows), self.rsem.at[s])

    def _wait_tokens(self, s):
        pltpu.make_async_copy(self.src_hbm.at[pl.ds(0, self.tm * self.rows)], self.buf.at[s], self.rsem.at[s]).wait()

    def arrive(self):
        @pl.when(self.i == 0)
        def _():
            first = self._idx_copy(0, 0)
            first.start()
            first.wait()
            _start_token_moves(self.tm, lambda r: self._token_copy(0, r))
            self._idx_copy(min(1, self.n - 1), 1).start()

        self._idx_copy(jnp.minimum(self.i + 1, self.n - 1), 1 - self.slot).wait()
        self._wait_tokens(self.slot)

    def prefetch(self):
        nslot = 1 - self.slot
        for r in range(self.tm):
            self._token_copy(nslot, r).start(priority=r % 2)
        self._idx_copy(jnp.minimum(self.i + 2, self.n - 1), self.slot).start()

    def finish(self):
        @pl.when(self.i == self.n - 1)
        def _():
            self._wait_tokens(1 - self.slot)
            self._idx_copy(self.n - 1, self.slot).wait()


def _plan_kernel(route_ref, x_ref, pos_ref, tinfo_ref, xs_ref,
                 hist_ref, off_ref, carry_ref, stage_ref, ppos_vmem, ppos_smem, fill_vmem, fill_smem, zero_ref,
                 ssem, psem, zsem, *, tile):
    ps = pl.program_id(0)
    b = pl.program_id(1)
    nb = pl.num_programs(1)
    n_cls, pb = hist_ref.shape
    kt = tinfo_ref.shape[1]
    cls = route_ref[0:1, :].astype(I32)
    cid = lax.broadcasted_iota(I32, (n_cls, pb), 0)
    onehot = jnp.where(cid == cls, 1.0, 0.0)

    def zero_fill(start):
        def piece(tok0, n_tok):
            cp = pltpu.make_async_copy(zero_ref.at[pl.ds(0, n_tok * X_ROWS)],
                                       xs_ref.at[pl.ds(tok0 * X_ROWS, n_tok * X_ROWS)], zsem)
            cp.start() if start else cp.wait()

        def per_class(c, carry):
            first, n = fill_smem[0, c], fill_smem[1, c]
            bit = tile // 2
            while bit:
                @pl.when((n & bit) != 0)
                def _(bit=bit):
                    piece(first + (n & -(2 * bit)), bit)
                bit //= 2
            return carry

        lax.fori_loop(0, n_cls, per_class, 0)

        def per_tile(k, carry):
            piece(k * tile, tile)
            return carry

        lax.fori_loop(fill_smem[2, 0], xs_ref.shape[0] // (tile * X_ROWS), per_tile, 0)

    @pl.when(ps == 0)
    def _():
        @pl.when(b == 0)
        def _():
            hist_ref[...] = jnp.zeros(hist_ref.shape, F32)
            zero_ref[...] = jnp.zeros(zero_ref.shape, F32)
        hist_ref[...] += onehot

    @pl.when((ps == 1) & (b == 0))
    def _():
        counts = jnp.sum(hist_ref[...], axis=1, keepdims=True)
        tiles = jnp.floor((counts + float(tile - 1)) * (1.0 / tile))
        lower = lax.broadcasted_iota(I32, (n_cls, n_cls), 1) < lax.broadcasted_iota(I32, (n_cls, n_cls), 0)
        lower = jnp.where(lower, 1.0, 0.0).astype(BF16)
        off = _dot(lower, jnp.broadcast_to(tiles, (n_cls, LANES)).astype(BF16))[:, :1]
        off_ref[...] = off * float(tile)
        carry_ref[...] = jnp.zeros(carry_ref.shape, F32)
        k = lax.broadcasted_iota(I32, (n_cls, kt), 1).astype(F32)
        cid_f = lax.broadcasted_iota(I32, (n_cls, kt), 0).astype(F32)
        mine = (k >= off) & (k < off + tiles)
        tcls = jnp.sum(jnp.where(mine, cid_f, 0.0), axis=0, keepdims=True)
        n_used = jnp.sum(tiles, axis=0, keepdims=True)
        last = jnp.max(jnp.where(tiles > 0.0, cid_f[:, :1], 0.0), axis=0, keepdims=True)
        tcls = jnp.where(k[:1] >= n_used, last, tcls)
        row = lax.broadcasted_iota(I32, (ROUTE_ROWS, kt), 0)
        tinfo_ref[...] = jnp.where(row == 0, tcls, jnp.where(row == 1, n_used, 0.0)).astype(I32)
        as_row = lambda col: jnp.broadcast_to(col, (n_cls, LANES)).T[0:1]
        frow = lax.broadcasted_iota(I32, fill_vmem.shape, 0)
        fill_vmem[...] = jnp.where(frow == 0, as_row(off * float(tile) + counts),
                                   jnp.where(frow == 1, as_row(tiles * float(tile) - counts), n_used)).astype(I32)
        to_fill = pltpu.make_async_copy(fill_vmem, fill_smem, psem)
        to_fill.start()
        to_fill.wait()
        zero_fill(start=True)

    @pl.when(ps == 1)
    def _():
        upper = lax.broadcasted_iota(I32, (pb, pb), 0) < lax.broadcasted_iota(I32, (pb, pb), 1)
        upper = jnp.where(upper, 1.0, 0.0).astype(BF16)
        before = _dot(onehot.astype(BF16), upper)
        ppos = jnp.sum(onehot * (before + carry_ref[...] + off_ref[...]), axis=0, keepdims=True).astype(I32)
        carry_ref[...] += jnp.sum(onehot, axis=1, keepdims=True)
        pos_ref[...] = ppos
        ppos_vmem[...] = ppos
        to_smem = pltpu.make_async_copy(ppos_vmem, ppos_smem, psem)
        to_smem.start()

        slot = b % 2
        stage = stage_ref.at[slot]

        def drain(s):
            pltpu.make_async_copy(stage_ref.at[s], xs_ref.at[pl.ds(0, pb * X_ROWS)], ssem.at[s]).wait()

        @pl.when(b >= 2)
        def _():
            drain(slot)

        for j in range(Y_ROWS):
            _rows_of(stage, j, pb, X_ROWS)[...] = x_ref[:, j * LANES:(j + 1) * LANES]
        record = jnp.concatenate([route_ref[...], jnp.zeros((LANES - ROUTE_ROWS, pb), F32)], axis=0).T
        _rows_of(stage, Y_ROWS, pb, X_ROWS)[...] = record

        to_smem.wait()
        _start_token_moves(pb, lambda r: pltpu.make_async_copy(
            _tok(stage, r, X_ROWS), _tok(xs_ref, ppos_smem[0, r], X_ROWS), ssem.at[slot]))

        @pl.when(b == nb - 1)
        def _():
            drain(slot)

            @pl.when(nb > 1)
            def _():
                drain(1 - slot)

            zero_fill(start=False)


def _moe_kernel(tcls_ref, nused_ref, xs_ref, win_ref, wout_ref, ys_ref):
    i = pl.program_id(0)
    tm = xs_ref.shape[0] // X_ROWS
    ff = wout_ref.shape[1]

    @pl.when(i < nused_ref[0])
    def _():
        cls = tcls_ref[i]
        xb = jnp.concatenate([_rows_of(xs_ref, j, tm, X_ROWS)[...].astype(BF16) for j in range(Y_ROWS)], axis=1)
        record = _rows_of(xs_ref, Y_ROWS, tm, X_ROWS)[...]
        y = None
        for e, gate in (((cls >> 3) & 7, record[:, 1:2]), (cls & 7, record[:, 2:3])):
            h = _dot(xb, win_ref[e])
            a = h[:, :ff]
            act = (a * jax.nn.sigmoid(a) * h[:, ff:] * gate).astype(BF16)
            ye = _dot(act, wout_ref[e])
            y = ye if y is None else y + ye
        for j in range(Y_ROWS):
            _rows_of(ys_ref, j, tm, Y_ROWS)[...] = y[:, j * LANES:(j + 1) * LANES]

    @pl.when(i >= nused_ref[0])
    def _():
        ys_ref[...] = jnp.zeros(ys_ref.shape, F32)


def _post_kernel(pos_ref, ys_ref, x1_ref, p_ref, ln_g_ref, ln_b_ref, pw_ref, gw_ref, gb_ref, out_ref,
                 idx_smem, ybuf, isem, rsem, *, alpha, n_blocks):
    i = pl.program_id(0)
    tm = x1_ref.shape[0]
    gather = _TokenGather(i, n_blocks, pos_ref, ys_ref, idx_smem, ybuf, isem, rsem)
    gather.arrive()
    gather.prefetch()
    yb = ybuf.at[i % 2]
    ffn = jnp.concatenate([_rows_of(yb, j, tm, Y_ROWS)[...] for j in range(Y_ROWS)], axis=1)
    x2 = _layer_norm(alpha * x1_ref[...] + ffn, ln_g_ref[...], ln_b_ref[...])
    gate = jax.nn.sigmoid(_dot(x2.astype(BF16), gw_ref[...]) + gb_ref[...])
    out_ref[...] = x2 + _dot(p_ref[...].astype(BF16), pw_ref[...]) * gate
    gather.finish()


def _moe_and_post(x1, route, layer, p, e_w_in, e_w_out, ln_g, ln_b, ple_w, ple_gw, ple_gb, alpha):
    t, d = x1.shape
    assert d == LANES * Y_ROWS
    depth, n_exp, _, ff2 = e_w_in.shape
    epg = n_exp // N_EGROUPS
    pb = _tile(t, PLAN_BLOCK)
    nb = t // pb
    nt = t // MOE_TILE + N_EGROUPS * (epg * (epg - 1) // 2)
    kt = -(-nt // LANES) * LANES
    n_sorted = nt * MOE_TILE

    pos, tinfo, xs = pl.pallas_call(
        functools.partial(_plan_kernel, tile=MOE_TILE),
        grid=(2, nb),
        in_specs=[pl.BlockSpec((ROUTE_ROWS, pb), lambda ps, b: (0, b)),
                  pl.BlockSpec((pb, d), lambda ps, b: (ps * b, 0))],
        out_specs=[pl.BlockSpec((None, 1, pb), lambda ps, b: (ps * b, 0, 0)),
                   pl.BlockSpec((ROUTE_ROWS, kt), lambda ps, b: (0, 0)),
                   pl.BlockSpec(memory_space=pl.ANY)],
        out_shape=[jax.ShapeDtypeStruct((nb, 1, pb), I32),
                   jax.ShapeDtypeStruct((ROUTE_ROWS, kt), I32),
                   jax.ShapeDtypeStruct((n_sorted * X_ROWS, LANES), F32)],
        scratch_shapes=[pltpu.VMEM((N_CLASS, pb), F32),
                        pltpu.VMEM((N_CLASS, 1), F32),
                        pltpu.VMEM((N_CLASS, 1), F32),
                        pltpu.VMEM((2, pb * X_ROWS, LANES), F32),
                        pltpu.VMEM((1, pb), I32),
                        pltpu.SMEM((1, pb), I32),
                        pltpu.VMEM((ROUTE_ROWS, N_CLASS), I32),
                        pltpu.SMEM((ROUTE_ROWS, N_CLASS), I32),
                        pltpu.VMEM((MOE_TILE * X_ROWS, LANES), F32),
                        pltpu.SemaphoreType.DMA((2,)),
                        pltpu.SemaphoreType.DMA(()),
                        pltpu.SemaphoreType.DMA(())],
        compiler_params=_cparams(("arbitrary", "arbitrary")),
        name="plan",
    )(route, x1)

    w_in = e_w_in.reshape(depth, N_EGROUPS, epg, d, ff2)
    w_out = e_w_out.reshape(depth, N_EGROUPS, epg, ff2 // 2, d)
    ys = pl.pallas_call(
        _moe_kernel,
        grid_spec=pltpu.PrefetchScalarGridSpec(
            num_scalar_prefetch=2,
            grid=(nt,),
            in_specs=[pl.BlockSpec((MOE_TILE * X_ROWS, LANES), lambda i, tc, nu: (i, 0)),
                      pl.BlockSpec((None, None, epg, d, ff2), lambda i, tc, nu: (layer, tc[i] >> 6, 0, 0, 0),
                                   pipeline_mode=pl.Buffered(1)),
                      pl.BlockSpec((None, None, epg, ff2 // 2, d), lambda i, tc, nu: (layer, tc[i] >> 6, 0, 0, 0),
                                   pipeline_mode=pl.Buffered(1))],
            out_specs=pl.BlockSpec((MOE_TILE * Y_ROWS, LANES), lambda i, tc, nu: (i, 0))),
        out_shape=jax.ShapeDtypeStruct((n_sorted * Y_ROWS, LANES), F32),
        compiler_params=_cparams(("arbitrary",)),
        name="moe",
    )(tinfo[0, :nt], tinfo[1, :1], xs, w_in, w_out)

    post_args = (pos, ys, x1, p, ln_g[None], ln_b[None], ple_w.astype(BF16), ple_gw.astype(BF16), ple_gb[None])
    return pl.pallas_call(
        functools.partial(_post_kernel, alpha=alpha, n_blocks=nb),
        grid=(nb,),
        in_specs=[pl.BlockSpec(memory_space=pl.ANY),
                  pl.BlockSpec(memory_space=pl.ANY),
                  pl.BlockSpec((pb, d), lambda i: (i, 0)),
                  pl.BlockSpec((None, pb, p.shape[2]), lambda i: (layer, i, 0))]
                 + [_full(a.shape) for a in post_args[4:]],
        out_specs=pl.BlockSpec((pb, d), lambda i: (i, 0)),
        out_shape=jax.ShapeDtypeStruct((t, d), F32),
        scratch_shapes=[pltpu.SMEM((2, 1, pb), I32), pltpu.VMEM((2, pb * Y_ROWS, LANES), F32),
                        pltpu.SemaphoreType.DMA((2,)), pltpu.SemaphoreType.DMA((2,))],
        compiler_params=_cparams(("arbitrary",)),
        name="post",
    )(*post_args)


def _router_params(wg, bg, we, be):
    d = wg.shape[0]
    w = jnp.zeros((d, ROUTE_COLS), F32).at[:, :N_EGROUPS].set(wg).at[:, 8:8 + we.shape[1]].set(we)
    b = jnp.zeros((1, ROUTE_COLS), F32).at[0, :N_EGROUPS].set(bg).at[0, 8:8 + be.shape[0]].set(be)
    hi = w.astype(BF16)
    return hi, (w - hi.astype(F32)).astype(BF16), b


def _tile(t, want):
    while t % want:
        want //= 2
    return want


def kernel(x, p, a_w_in, a_b_in, a_vn_g, a_vn_b, a_w_s, a_b_s, a_w_out, b_w_in, b_kv_g, b_w_uk, b_w_uv, b_w_out,
           ln1_g, ln1_b, ln2_g, ln2_b, r_wg, r_bg, r_we, r_be, e_w_in, e_w_out, ple_w, ple_gw, ple_gb):
    batch, seq, d = x.shape
    t = batch * seq
    depth = p.shape[0]
    alpha = (2 * depth) ** 0.25
    chunk = a_w_s.shape[-1]
    xt = x.reshape(t, d)
    pt = p.reshape(depth, t, p.shape[-1])
    for i in range(depth):
        j = i // 2
        router = _router_params(r_wg[i], r_bg[i], r_we[i], r_be[i])
        if i % 2 == 0:
            x1, route = _gmlp_layer(xt, a_w_in[j], a_b_in[j], a_vn_g[j], a_vn_b[j], a_w_s[j], a_b_s[j],
                                    a_w_out[j], ln1_g[i], ln1_b[i], router, alpha, max(chunk, _tile(t, 512)))
        else:
            x1, route = _dsa_layer(xt, batch, b_w_in[j], b_kv_g[j], b_w_uk[j], b_w_uv[j], b_w_out[j],
                                   ln1_g[i], ln1_b[i], router, alpha, _tile(t, 512))
        xt = _moe_and_post(x1, route, i, pt, e_w_in, e_w_out, ln2_g[i], ln2_b[i],
                           ple_w[i], ple_gw[i], ple_gb[i], alpha)
    return xt.reshape(batch, seq, d)
```

```python
import functools

import jax
import jax.numpy as jnp
from jax import lax
from jax.experimental import pallas as pl
from jax.experimental.pallas import tpu as pltpu

F32, BF16, I32 = jnp.float32, jnp.bfloat16, jnp.int32

IDX_HEADS = 8
IDX_DIM = 64
TOPK_MAX = 256
N_EGROUPS = 4
LN_EPS = 1e-5
RMS_EPS = 1e-6

LANES = 128
Q_BLOCK = 128
KEY_CHUNK = 512
COUNT_SLAB = 1024
COUNT_ACC = 64
LOG2E = 1.4426950408889634
VMEM_LIMIT = 48 * 1024 * 1024

ROUTE_COLS = LANES
ROUTE_ROWS = 8
N_CLASS = 64 * N_EGROUPS
Y_ROWS = 8
X_ROWS = Y_ROWS + 1
PLAN_BLOCK = 256
MOE_TILE = 128
MOE_STEP_TILES = 2
INT_MIN = -(2 ** 31)
NEG_BIG = -1e30


def _cparams(sem, vmem=VMEM_LIMIT):
    return pltpu.CompilerParams(dimension_semantics=sem, vmem_limit_bytes=vmem)


def _full(shape):
    n = len(shape)
    return pl.BlockSpec(shape, lambda *_: (0,) * n)


def _dot(a, b):
    return jnp.dot(a, b, preferred_element_type=F32)


def _dot_t(a, b):
    return lax.dot_general(a, b, (((1,), (1,)), ((), ())), preferred_element_type=F32)


def _layer_norm(x, g, b):
    mu = jnp.mean(x, axis=-1, keepdims=True)
    xc = x - mu
    var = jnp.mean(xc * xc, axis=-1, keepdims=True)
    return xc * lax.rsqrt(var + LN_EPS) * g + b


def _route_rows(lg_t):
    g = [lg_t[i:i + 1] for i in range(N_EGROUPS)]
    gmax = functools.reduce(jnp.maximum, g)
    gsel = jnp.where(g[0] >= gmax, 0.0, jnp.where(g[1] >= gmax, 1.0, jnp.where(g[2] >= gmax, 2.0, 3.0)))
    den = functools.reduce(lambda a, b: a + b, [jnp.exp(gi - gmax) for gi in g])
    p_g = 1.0 / den
    el = lg_t[8:16]
    for gi in range(1, N_EGROUPS):
        el = jnp.where(gsel == float(gi), lg_t[8 + 8 * gi:16 + 8 * gi], el)
    eidx = lax.broadcasted_iota(I32, el.shape, 0).astype(F32)
    m1 = jnp.max(el, axis=0, keepdims=True)
    i1 = jnp.min(jnp.where(el == m1, eidx, 8.0), axis=0, keepdims=True)
    el2 = jnp.where(eidx == i1, -jnp.inf, el)
    m2 = jnp.max(el2, axis=0, keepdims=True)
    i2 = jnp.min(jnp.where(el2 == m2, eidx, 8.0), axis=0, keepdims=True)
    r = jnp.exp(m2 - m1)
    inv = 1.0 / (1.0 + r)
    gate1 = p_g * inv
    gate2 = p_g * r * inv
    first_lo = i1 < i2
    lo = jnp.minimum(i1, i2)
    hi = jnp.maximum(i1, i2)
    cls = gsel * 64.0 + lo * 8.0 + hi
    return cls, jnp.where(first_lo, gate1, gate2), jnp.where(first_lo, gate2, gate1)


def _norm_and_route(y, ln_g, ln_b, wr_hi, wr_lo, br, x1_ref, route_ref):
    tm, _ = y.shape
    x1 = _layer_norm(y, ln_g, ln_b)
    x1_ref[...] = x1
    x_hi = x1.astype(BF16)
    x_lo = (x1 - x_hi.astype(F32)).astype(BF16)
    lg = _dot(x_hi, wr_hi) + _dot(x_lo, wr_hi) + _dot(x_hi, wr_lo) + br
    cls, g_lo, g_hi = _route_rows(lg.T)
    row = lax.broadcasted_iota(I32, (ROUTE_ROWS, tm), 0)
    route_ref[...] = jnp.where(row == 0, cls, jnp.where(row == 1, g_lo, jnp.where(row == 2, g_hi, 0.0)))


def _mixer_out_specs(tm, d):
    return [pl.BlockSpec((tm, d), lambda i: (i, 0)), pl.BlockSpec((ROUTE_ROWS, tm), lambda i: (0, i))]


def _mixer_out_shape(t, d):
    return [jax.ShapeDtypeStruct((t, d), F32), jax.ShapeDtypeStruct((ROUTE_ROWS, t), F32)]


def _gmlp_kernel(x_ref, w_in_ref, b_in_ref, vn_g_ref, vn_b_ref, w_s_ref, b_st_ref, w_out_ref,
                 ln_g_ref, ln_b_ref, wr_hi_ref, wr_lo_ref, br_ref, x1_ref, route_ref, gated_ref, *, alpha):
    x = x_ref[...]
    tm, _ = x.shape
    groups, chunk, _ = w_s_ref.shape
    z = _dot(x.astype(BF16), w_in_ref[...]) + b_in_ref[...]
    z = 0.5 * z * (1.0 + lax.erf(z * (2.0 ** -0.5)))
    half = z.shape[1] // 2
    gd = half // groups
    u = z[:, :half]
    v = _layer_norm(z[:, half:], vn_g_ref[...], vn_b_ref[...]).astype(BF16)
    r = lax.broadcasted_iota(I32, (chunk, chunk), 0)
    c = lax.broadcasted_iota(I32, (chunk, chunk), 1)
    causal = r >= c
    for g in range(groups):
        w_c = jnp.where(causal, w_s_ref[g], 0.0).astype(BF16)
        bias = b_st_ref[:, g:g + 1]
        for ci in range(tm // chunk):
            rows = slice(ci * chunk, (ci + 1) * chunk)
            cols = slice(g * gd, (g + 1) * gd)
            s = _dot(w_c, v[rows, cols]) + bias
            gated_ref[rows, cols] = (u[rows, cols] * s).astype(BF16)
    mix = _dot(gated_ref[...], w_out_ref[...])
    _norm_and_route(alpha * x + mix, ln_g_ref[...], ln_b_ref[...], wr_hi_ref[...], wr_lo_ref[...],
                    br_ref[...], x1_ref, route_ref)


def _gmlp_layer(xt, w_in, b_in, vn_g, vn_b, w_s, b_s, w_out, ln_g, ln_b, router, alpha, tm):
    t, d = xt.shape
    half = w_out.shape[0]
    wr_hi, wr_lo, br = router
    args = (xt, w_in.astype(BF16), b_in[None], vn_g[None], vn_b[None], w_s, b_s.T, w_out.astype(BF16),
            ln_g[None], ln_b[None], wr_hi, wr_lo, br)
    in_specs = [pl.BlockSpec((tm, d), lambda i: (i, 0))] + [_full(a.shape) for a in args[1:]]
    return pl.pallas_call(
        functools.partial(_gmlp_kernel, alpha=alpha),
        grid=(t // tm,),
        in_specs=in_specs,
        out_specs=_mixer_out_specs(tm, d),
        out_shape=_mixer_out_shape(t, d),
        scratch_shapes=[pltpu.VMEM((tm, half), BF16)],
        compiler_params=_cparams(("parallel",)),
        name="gmlp",
    )(*args)


def _dsa_proj_kernel(x_ref, wq_ref, wc_ref, wqi_ref, wkw_ref, kvg_ref, wuk_t_ref,
                     qlat_ref, ckv_ref, ckv_t_ref, qidx_ref, kidx_ref, wi_t_ref, *, scale_q, scale_w):
    xb = x_ref[...].astype(BF16)
    nq = qlat_ref.shape[0]
    kc = ckv_t_ref.shape[2]
    heads, hd, _ = wuk_t_ref.shape
    q = _dot(xb, wq_ref[...])
    for h in range(heads):
        ql = (_dot(q[:, h * hd:(h + 1) * hd].astype(BF16), wuk_t_ref[h]) * scale_q).astype(BF16)
        for b in range(nq):
            qlat_ref[b, h * Q_BLOCK:(h + 1) * Q_BLOCK, :] = ql[b * Q_BLOCK:(b + 1) * Q_BLOCK]
    c = _dot(xb, wc_ref[...])
    ms = jnp.mean(c * c, axis=-1, keepdims=True)
    c = c * lax.rsqrt(ms + RMS_EPS) * kvg_ref[...]
    ckv_ref[...] = c.astype(BF16)
    for b in range(ckv_t_ref.shape[0]):
        ckv_t_ref[b] = c[b * kc:(b + 1) * kc].T.astype(BF16)
    qi = _dot(xb, wqi_ref[...]).astype(BF16)
    for h in range(IDX_HEADS):
        for b in range(nq):
            qidx_ref[b, h * Q_BLOCK:(h + 1) * Q_BLOCK, :] = qi[b * Q_BLOCK:(b + 1) * Q_BLOCK,
                                                               h * IDX_DIM:(h + 1) * IDX_DIM]
    kw = _dot(xb, wkw_ref[...])
    kidx_ref[...] = kw[:, :IDX_DIM].astype(BF16)
    wi_t_ref[...] = kw.T[IDX_DIM:IDX_DIM + IDX_HEADS] * scale_w


def _dsa_attn_kernel(qidx_ref, wi_t_ref, qlat_ref, kidx_ref, ckv_ref, ckv_t_ref, o_ref,
                     key_ref, key16_ref, acc_ref, m_ref, l_ref, a_ref, p_ref, *, k_sel, heads):
    qb = pl.program_id(1)
    kc = ckv_t_ref.shape[2]
    slab = key_ref.shape[1]
    cps = slab // kc
    nq = Q_BLOCK
    pair = 2 * nq
    n_pairs = heads // 2
    n_chunks = (qb * nq + nq + kc - 1) // kc
    n_slabs = (n_chunks + cps - 1) // cps

    def key_chunk(j):
        return key_ref.at[j // cps, pl.ds(pl.multiple_of((j % cps) * kc, kc), kc), :]

    q_pos = qb * nq + lax.broadcasted_iota(I32, (kc, nq), 1)
    k_off = lax.broadcasted_iota(I32, (kc, nq), 0)

    def key16_chunk(j):
        return key16_ref.at[j // cps, pl.ds(pl.multiple_of((j % cps) * kc, kc), kc), :]

    def pad_chunk(j, carry):
        key_chunk(j)[...] = jnp.full((kc, nq), jnp.nan, F32)
        key16_chunk(j)[...] = jnp.full((kc, nq), jnp.nan, BF16)
        return carry

    lax.fori_loop(n_chunks, n_slabs * cps, pad_chunk, 0)

    def score_chunk(j, carry):
        kch = kidx_ref[pl.ds(pl.multiple_of(j * kc, kc), kc), :]
        tot = None
        for p in range(IDX_HEADS // 2):
            sc = _dot_t(kch, qidx_ref[p * pair:(p + 1) * pair, :])
            for hh in range(2):
                h = 2 * p + hh
                r = jnp.maximum(sc[:, hh * nq:(hh + 1) * nq], 0.0) * wi_t_ref[h:h + 1, :]
                tot = r if tot is None else tot + r
        tot = jnp.where(k_off + j * kc <= q_pos, tot, jnp.nan)
        key_chunk(j)[...] = tot
        key16_chunk(j)[...] = tot.astype(BF16)
        return carry

    lax.fori_loop(0, n_chunks, score_chunk, 0)

    def count(t, strict=False):
        tb = jnp.broadcast_to(t, (COUNT_ACC, nq))

        def body(s, acc):
            for g in range(slab // COUNT_ACC):
                blk = key_ref[s, g * COUNT_ACC:(g + 1) * COUNT_ACC, :]
                acc = jnp.where(blk > tb if strict else blk >= tb, acc + 1.0, acc)
            return acc

        acc = lax.fori_loop(0, n_slabs, body, jnp.zeros((COUNT_ACC, nq), F32))
        return jnp.sum(acc, axis=0, keepdims=True)

    def count16(t):
        rows16 = 2 * COUNT_ACC
        tb = jnp.broadcast_to(t.astype(BF16), (rows16, nq))

        def body(s, acc):
            for g in range(slab // rows16):
                blk = key16_ref[s, g * rows16:(g + 1) * rows16, :]
                acc = jnp.where(blk >= tb, acc + 1.0, acc)
            return acc

        acc = lax.fori_loop(0, n_slabs, body, jnp.zeros((rows16, nq), BF16))
        return jnp.sum(acc.astype(F32), axis=0, keepdims=True)

    def decode(code):
        return lax.bitcast_convert_type(code ^ ((code >> 31) & 0x7FFFFFFF), F32)

    def decode16(c16):
        return decode(jnp.left_shift(c16, 16) | jnp.where(c16 < 0, 0xFFFF, 0))

    def bit16_step(b, c16):
        t = c16 + jnp.left_shift(jnp.int32(1), 15 - b)
        return jnp.where(count16(decode16(t)) >= float(k_sel), t, c16)

    no_code = -(2 ** 15)
    c16 = lax.fori_loop(0, 16, bit16_step, jnp.full((1, nq), no_code, I32))

    def bit_step(b, c):
        t = c + jnp.left_shift(jnp.int32(1), 17 - b)
        return jnp.where((count(decode(t)) >= float(k_sel)) & (t > c), t, c)

    c_sel = lax.fori_loop(0, 18, bit_step, jnp.left_shift(jnp.maximum(c16, no_code + 1) - 1, 16))
    c_sel = jnp.where(c16 == no_code, -jnp.inf, decode(c_sel))
    n_ge = jnp.where(c16 == no_code, 0.0, count(c_sel))

    has_tie = jnp.max(n_ge) > float(k_sel)

    @pl.when(has_tie)
    def _():
        need = float(k_sel) - count(c_sel, strict=True)
        tie_col = n_ge > float(k_sel)
        cb = jnp.broadcast_to(c_sel, (kc, nq))
        lower = (lax.broadcasted_iota(I32, (kc, kc), 1) < lax.broadcasted_iota(I32, (kc, kc), 0))
        lower = jnp.where(lower, 1.0, 0.0).astype(BF16)

        def fix(j, seen):
            blk = key_chunk(j)[...]
            eq = blk == cb
            eq_f = jnp.where(eq, 1.0, 0.0)
            rank = seen + _dot(lower, eq_f.astype(BF16))
            drop = eq & tie_col & (rank >= need)
            key_chunk(j)[...] = jnp.where(drop, jnp.nan, blk)
            return seen + jnp.sum(eq_f, axis=0, keepdims=True)

        lax.fori_loop(0, n_chunks, fix, jnp.zeros((1, nq), F32))

    thr = jnp.broadcast_to(c_sel, (kc, nq))
    m_ref[...] = jnp.full(m_ref.shape, NEG_BIG, F32)
    l_ref[...] = jnp.zeros(l_ref.shape, F32)
    acc_ref[...] = jnp.zeros(acc_ref.shape, F32)

    def softmax_chunk(j):
        ck = ckv_ref[pl.ds(pl.multiple_of(j * kc, kc), kc), :]
        bias = jnp.where(key_chunk(j)[...] >= thr, 0.0, NEG_BIG)
        bias = jnp.concatenate([bias, bias], axis=1)
        slot = j % 2
        for p in range(n_pairs):
            lg = _dot_t(ck, qlat_ref[p * pair:(p + 1) * pair, :]) + bias
            m_old = m_ref[p]
            m_new = jnp.maximum(m_old, jnp.max(lg, axis=0, keepdims=True))
            pr = jnp.exp2(lg - m_new)
            alpha = jnp.exp2(m_old - m_new)
            l_ref[p] = alpha * l_ref[p] + jnp.sum(pr, axis=0, keepdims=True)
            m_ref[p] = m_new
            a_ref[slot, p] = alpha
            p_ref[slot, p] = pr.astype(BF16)

    def accumulate_chunk(j):
        ck_t = ckv_t_ref[j]
        slot = j % 2
        for p in range(n_pairs):
            acc_ref[p] = a_ref[slot, p] * acc_ref[p] + _dot(ck_t, p_ref[slot, p])

    def att_step(j, carry):
        accumulate_chunk(j - 1)
        softmax_chunk(j)
        return carry

    softmax_chunk(0)
    lax.fori_loop(1, n_chunks, att_step, 0)
    accumulate_chunk(n_chunks - 1)
    for p in range(n_pairs):
        o_ref[p * pair:(p + 1) * pair, :] = (acc_ref[p] * (1.0 / l_ref[p])).T.astype(o_ref.dtype)


def _dsa_out_kernel(o_ref, x_ref, wuv_ref, wout_ref, ln_g_ref, ln_b_ref, wr_hi_ref, wr_lo_ref, br_ref,
                    x1_ref, route_ref, o2_ref, *, alpha):
    nq = o_ref.shape[0]
    heads, _, hd = wuv_ref.shape
    for b in range(nq):
        for h in range(heads):
            oh = o_ref[b, h * Q_BLOCK:(h + 1) * Q_BLOCK, :]
            o2_ref[b * Q_BLOCK:(b + 1) * Q_BLOCK, h * hd:(h + 1) * hd] = _dot(oh, wuv_ref[h]).astype(BF16)
    mix = _dot(o2_ref[...], wout_ref[...])
    _norm_and_route(alpha * x_ref[...] + mix, ln_g_ref[...], ln_b_ref[...], wr_hi_ref[...], wr_lo_ref[...],
                    br_ref[...], x1_ref, route_ref)


def _dsa_layer(xt, batch, w_in, kv_g, w_uk, w_uv, w_out, ln_g, ln_b, router, alpha, tm):
    t, d = xt.shape
    seq = t // batch
    heads, lat, hd = w_uk.shape
    qw = heads * hd
    nqb = t // Q_BLOCK
    nq = tm // Q_BLOCK
    k_sel = min(TOPK_MAX, seq // 4)
    kc = min(KEY_CHUNK, seq)
    slab = min(COUNT_SLAB, seq)
    w_in = w_in.astype(BF16)
    o1, o2, o3 = qw, qw + lat, qw + lat + IDX_HEADS * IDX_DIM
    wkw = jnp.pad(w_in[:, o3:], ((0, 0), (0, LANES - (w_in.shape[1] - o3))))
    proj_args = (xt, w_in[:, :o1], w_in[:, o1:o2], w_in[:, o2:o3], wkw, kv_g[None],
                 jnp.swapaxes(w_uk, 1, 2).astype(BF16))
    qlat, ckv, ckv_t, qidx, kidx, wi_t = pl.pallas_call(
        functools.partial(_dsa_proj_kernel, scale_q=hd ** -0.5 * LOG2E, scale_w=(IDX_HEADS * IDX_DIM) ** -0.5),
        grid=(t // tm,),
        in_specs=[pl.BlockSpec((tm, d), lambda i: (i, 0))] + [_full(a.shape) for a in proj_args[1:]],
        out_specs=[pl.BlockSpec((nq, heads * Q_BLOCK, lat), lambda i: (i, 0, 0)),
                   pl.BlockSpec((tm, lat), lambda i: (i, 0)),
                   pl.BlockSpec((tm // kc, lat, kc), lambda i: (i, 0, 0)),
                   pl.BlockSpec((nq, IDX_HEADS * Q_BLOCK, IDX_DIM), lambda i: (i, 0, 0)),
                   pl.BlockSpec((tm, IDX_DIM), lambda i: (i, 0)),
                   pl.BlockSpec((IDX_HEADS, tm), lambda i: (0, i))],
        out_shape=[jax.ShapeDtypeStruct((nqb, heads * Q_BLOCK, lat), BF16),
                   jax.ShapeDtypeStruct((t, lat), BF16),
                   jax.ShapeDtypeStruct((t // kc, lat, kc), BF16),
                   jax.ShapeDtypeStruct((nqb, IDX_HEADS * Q_BLOCK, IDX_DIM), BF16),
                   jax.ShapeDtypeStruct((t, IDX_DIM), BF16),
                   jax.ShapeDtypeStruct((IDX_HEADS, t), F32)],
        compiler_params=_cparams(("parallel",)),
        name="dsa_proj",
    )(*proj_args)

    nq_seq = seq // Q_BLOCK
    n_kc = seq // kc
    rows = heads * Q_BLOCK
    o = pl.pallas_call(
        functools.partial(_dsa_attn_kernel, k_sel=k_sel, heads=heads),
        grid=(batch, nq_seq),
        in_specs=[pl.BlockSpec((None, IDX_HEADS * Q_BLOCK, IDX_DIM), lambda b, q: (b * nq_seq + q, 0, 0)),
                  pl.BlockSpec((IDX_HEADS, Q_BLOCK), lambda b, q: (0, b * nq_seq + q)),
                  pl.BlockSpec((None, rows, lat), lambda b, q: (b * nq_seq + q, 0, 0)),
                  pl.BlockSpec((None, seq, IDX_DIM), lambda b, q: (b, 0, 0)),
                  pl.BlockSpec((None, seq, lat), lambda b, q: (b, 0, 0)),
                  pl.BlockSpec((None, n_kc, lat, kc), lambda b, q: (b, 0, 0, 0))],
        out_specs=pl.BlockSpec((None, rows, lat), lambda b, q: (b * nq_seq + q, 0, 0)),
        out_shape=jax.ShapeDtypeStruct((nqb, rows, lat), BF16),
        scratch_shapes=[pltpu.VMEM((seq // slab, slab, Q_BLOCK), F32),
                        pltpu.VMEM((seq // slab, slab, Q_BLOCK), BF16),
                        pltpu.VMEM((heads // 2, lat, 2 * Q_BLOCK), F32),
                        pltpu.VMEM((heads // 2, 1, 2 * Q_BLOCK), F32),
                        pltpu.VMEM((heads // 2, 1, 2 * Q_BLOCK), F32),
                        pltpu.VMEM((2, heads // 2, 1, 2 * Q_BLOCK), F32),
                        pltpu.VMEM((2, heads // 2, kc, 2 * Q_BLOCK), BF16)],
        compiler_params=_cparams(("parallel", "arbitrary")),
        name="dsa_attn",
    )(qidx, wi_t, qlat, kidx.reshape(batch, seq, IDX_DIM), ckv.reshape(batch, seq, lat),
      ckv_t.reshape(batch, n_kc, lat, kc))

    wr_hi, wr_lo, br = router
    out_args = (o, xt, w_uv.astype(BF16), w_out.astype(BF16), ln_g[None], ln_b[None], wr_hi, wr_lo, br)
    return pl.pallas_call(
        functools.partial(_dsa_out_kernel, alpha=alpha),
        grid=(t // tm,),
        in_specs=[pl.BlockSpec((nq, rows, lat), lambda i: (i, 0, 0)),
                  pl.BlockSpec((tm, d), lambda i: (i, 0))] + [_full(a.shape) for a in out_args[2:]],
        out_specs=_mixer_out_specs(tm, d),
        out_shape=_mixer_out_shape(t, d),
        scratch_shapes=[pltpu.VMEM((tm, qw), BF16)],
        compiler_params=_cparams(("parallel",)),
        name="dsa_out",
    )(*out_args)


DMA_UNROLL = 8


def _start_token_moves(n, copy_of):
    def body(g, carry):
        for u in range(DMA_UNROLL):
            copy_of(g * DMA_UNROLL + u).start(priority=u % 2)
        return carry
    lax.fori_loop(0, n // DMA_UNROLL, body, 0)


def _tok(ref, i, rows):
    return ref.at[pl.ds(i * rows, rows)]


def _rows_of(ref, j, n, rows):
    return ref.at[pl.ds(j, n, stride=rows), :]


class _TokenGather:
    def __init__(self, i, n, idx_hbm, src_hbm, idx_smem, buf, isem, rsem):
        self.i, self.n = i, n
        self.idx_hbm, self.src_hbm, self.idx_smem, self.buf, self.isem, self.rsem = (
            idx_hbm, src_hbm, idx_smem, buf, isem, rsem)
        self.tm = idx_smem.shape[2]
        self.rows = buf.shape[1] // self.tm
        self.slot = i % 2

    def _idx_copy(self, blk, s):
        return pltpu.make_async_copy(self.idx_hbm.at[blk], self.idx_smem.at[s], self.isem.at[s])

    def _token_copy(self, s, r):
        return pltpu.make_async_copy(_tok(self.src_hbm, self.idx_smem[s, 0, r], self.rows),
                                     _tok(self.buf.at[s], r, self.rows), self.rsem.at[s])

    def _wait_tokens(self, s):
        pltpu.make_async_copy(self.src_hbm.at[pl.ds(0, self.tm * self.rows)], self.buf.at[s], self.rsem.at[s]).wait()

    def arrive(self):
        @pl.when(self.i == 0)
        def _():
            first = self._idx_copy(0, 0)
            first.start()
            first.wait()
            _start_token_moves(self.tm, lambda r: self._token_copy(0, r))
            self._idx_copy(min(1, self.n - 1), 1).start()

        self._idx_copy(jnp.minimum(self.i + 1, self.n - 1), 1 - self.slot).wait()
        self._wait_tokens(self.slot)

    def prefetch(self):
        nslot = 1 - self.slot
        for r in range(self.tm):
            self._token_copy(nslot, r).start(priority=r % 2)
        self._idx_copy(jnp.minimum(self.i + 2, self.n - 1), self.slot).start()

    def finish(self):
        @pl.when(self.i == self.n - 1)
        def _():
            self._wait_tokens(1 - self.slot)
            self._idx_copy(self.n - 1, self.slot).wait()


def _plan_kernel(route_ref, x_ref, pos_ref, tinfo_ref, xs_ref,
                 hist_ref, off_ref, carry_ref, stage_ref, ppos_vmem, ppos_smem, fill_vmem, fill_smem, zero_ref,
                 ssem, psem, zsem, *, tile):
    ps = pl.program_id(0)
    b = pl.program_id(1)
    nb = pl.num_programs(1)
    n_cls, pb = hist_ref.shape
    kt = tinfo_ref.shape[1]
    cls = route_ref[0:1, :].astype(I32)
    cid = lax.broadcasted_iota(I32, (n_cls, pb), 0)
    onehot = jnp.where(cid == cls, 1.0, 0.0)

    def zero_fill(start):
        def piece(tok0, n_tok):
            cp = pltpu.make_async_copy(zero_ref.at[pl.ds(0, n_tok * X_ROWS)],
                                       xs_ref.at[pl.ds(tok0 * X_ROWS, n_tok * X_ROWS)], zsem)
            cp.start() if start else cp.wait()

        def per_class(c, carry):
            first, n = fill_smem[0, c], fill_smem[1, c]
            bit = tile
            while bit:
                @pl.when((n & bit) != 0)
                def _(bit=bit):
                    piece(first + (n & -(2 * bit)), bit)
                bit //= 2
            return carry

        lax.fori_loop(0, n_cls, per_class, 0)

        def per_tile(k, carry):
            piece(k * tile, tile)
            return carry

        lax.fori_loop(fill_smem[2, 0], xs_ref.shape[0] // (tile * X_ROWS), per_tile, 0)

    @pl.when(ps == 0)
    def _():
        @pl.when(b == 0)
        def _():
            hist_ref[...] = jnp.zeros(hist_ref.shape, F32)
            zero_ref[...] = jnp.zeros(zero_ref.shape, F32)
        hist_ref[...] += onehot

    @pl.when((ps == 1) & (b == 0))
    def _():
        counts = jnp.sum(hist_ref[...], axis=1, keepdims=True)
        tiles = jnp.floor((counts + float(tile - 1)) * (1.0 / tile))
        cid_col = lax.broadcasted_iota(I32, (n_cls, 1), 0)
        per_group = n_cls // N_EGROUPS
        for g in range(N_EGROUPS):
            g_tiles = jnp.sum(tiles[g * per_group:(g + 1) * per_group], axis=0, keepdims=True)
            extra = jnp.ceil(g_tiles * (1.0 / MOE_STEP_TILES)) * float(MOE_STEP_TILES) - g_tiles
            tiles = jnp.where(cid_col == (g + 1) * per_group - 1, extra, tiles)
        lower = lax.broadcasted_iota(I32, (n_cls, n_cls), 1) < lax.broadcasted_iota(I32, (n_cls, n_cls), 0)
        lower = jnp.where(lower, 1.0, 0.0).astype(BF16)
        off = _dot(lower, jnp.broadcast_to(tiles, (n_cls, LANES)).astype(BF16))[:, :1]
        off_ref[...] = off * float(tile)
        carry_ref[...] = jnp.zeros(carry_ref.shape, F32)
        k = lax.broadcasted_iota(I32, (n_cls, kt), 1).astype(F32)
        cid_f = lax.broadcasted_iota(I32, (n_cls, kt), 0).astype(F32)
        mine = (k >= off) & (k < off + tiles)
        tcls = jnp.sum(jnp.where(mine, cid_f, 0.0), axis=0, keepdims=True)
        n_used = jnp.sum(tiles, axis=0, keepdims=True)
        last = jnp.max(jnp.where(tiles > 0.0, cid_f[:, :1], 0.0), axis=0, keepdims=True)
        tcls = jnp.where(k[:1] >= n_used, last, tcls)
        row = lax.broadcasted_iota(I32, (ROUTE_ROWS, kt), 0)
        tinfo_ref[...] = jnp.where(row == 0, tcls, jnp.where(row == 1, n_used, 0.0)).astype(I32)
        as_row = lambda col: jnp.broadcast_to(col, (n_cls, LANES)).T[0:1]
        frow = lax.broadcasted_iota(I32, fill_vmem.shape, 0)
        fill_vmem[...] = jnp.where(frow == 0, as_row(off * float(tile) + counts),
                                   jnp.where(frow == 1, as_row(tiles * float(tile) - counts), n_used)).astype(I32)
        to_fill = pltpu.make_async_copy(fill_vmem, fill_smem, psem)
        to_fill.start()
        to_fill.wait()
        zero_fill(start=True)

    @pl.when(ps == 1)
    def _():
        upper = lax.broadcasted_iota(I32, (pb, pb), 0) < lax.broadcasted_iota(I32, (pb, pb), 1)
        upper = jnp.where(upper, 1.0, 0.0).astype(BF16)
        before = _dot(onehot.astype(BF16), upper)
        ppos = jnp.sum(onehot * (before + carry_ref[...] + off_ref[...]), axis=0, keepdims=True).astype(I32)
        carry_ref[...] += jnp.sum(onehot, axis=1, keepdims=True)
        pos_ref[...] = ppos
        ppos_vmem[...] = ppos
        to_smem = pltpu.make_async_copy(ppos_vmem, ppos_smem, psem)
        to_smem.start()

        slot = b % 2
        stage = stage_ref.at[slot]

        def drain(s):
            pltpu.make_async_copy(stage_ref.at[s], xs_ref.at[pl.ds(0, pb * X_ROWS)], ssem.at[s]).wait()

        @pl.when(b >= 2)
        def _():
            drain(slot)

        for j in range(Y_ROWS):
            _rows_of(stage, j, pb, X_ROWS)[...] = x_ref[:, j * LANES:(j + 1) * LANES]
        record = jnp.concatenate([route_ref[...], jnp.zeros((LANES - ROUTE_ROWS, pb), F32)], axis=0).T
        _rows_of(stage, Y_ROWS, pb, X_ROWS)[...] = record

        to_smem.wait()
        _start_token_moves(pb, lambda r: pltpu.make_async_copy(
            _tok(stage, r, X_ROWS), _tok(xs_ref, ppos_smem[0, r], X_ROWS), ssem.at[slot]))

        @pl.when(b == nb - 1)
        def _():
            drain(slot)

            @pl.when(nb > 1)
            def _():
                drain(1 - slot)

            zero_fill(start=False)


def _moe_kernel(tcls_ref, nused_ref, xs_ref, win_ref, wout_ref, ys_ref):
    i = pl.program_id(0)
    tm = xs_ref.shape[0] // (X_ROWS * MOE_STEP_TILES)
    ff = wout_ref.shape[1]

    @pl.when(i * MOE_STEP_TILES < nused_ref[0])
    def _():
        for jt in range(MOE_STEP_TILES):
            cls = tcls_ref[i * MOE_STEP_TILES + jt]
            x0, y0 = jt * tm * X_ROWS, jt * tm * Y_ROWS
            xb = jnp.concatenate([_rows_of(xs_ref, x0 + j, tm, X_ROWS)[...].astype(BF16) for j in range(Y_ROWS)],
                                 axis=1)
            record = _rows_of(xs_ref, x0 + Y_ROWS, tm, X_ROWS)[...]
            y = None
            for e, gate in (((cls >> 3) & 7, record[:, 1:2]), (cls & 7, record[:, 2:3])):
                h = _dot(xb, win_ref[e])
                a = h[:, :ff]
                act = (a * jax.nn.sigmoid(a) * h[:, ff:] * gate).astype(BF16)
                ye = _dot(act, wout_ref[e])
                y = ye if y is None else y + ye
            for j in range(Y_ROWS):
                _rows_of(ys_ref, y0 + j, tm, Y_ROWS)[...] = y[:, j * LANES:(j + 1) * LANES]

    @pl.when(i * MOE_STEP_TILES >= nused_ref[0])
    def _():
        ys_ref[...] = jnp.zeros(ys_ref.shape, F32)


def _post_kernel(pos_ref, ys_ref, x1_ref, p_ref, ln_g_ref, ln_b_ref, pw_ref, gw_ref, gb_ref, out_ref,
                 idx_smem, ybuf, isem, rsem, *, alpha, n_blocks):
    i = pl.program_id(0)
    tm = x1_ref.shape[0]
    gather = _TokenGather(i, n_blocks, pos_ref, ys_ref, idx_smem, ybuf, isem, rsem)
    gather.arrive()
    gather.prefetch()
    yb = ybuf.at[i % 2]
    ffn = jnp.concatenate([_rows_of(yb, j, tm, Y_ROWS)[...] for j in range(Y_ROWS)], axis=1)
    x2 = _layer_norm(alpha * x1_ref[...] + ffn, ln_g_ref[...], ln_b_ref[...])
    gate = jax.nn.sigmoid(_dot(x2.astype(BF16), gw_ref[...]) + gb_ref[...])
    out_ref[...] = x2 + _dot(p_ref[...].astype(BF16), pw_ref[...]) * gate
    gather.finish()


def _moe_and_post(x1, route, layer, p, e_w_in, e_w_out, ln_g, ln_b, ple_w, ple_gw, ple_gb, alpha):
    t, d = x1.shape
    assert d == LANES * Y_ROWS
    depth, n_exp, _, ff2 = e_w_in.shape
    epg = n_exp // N_EGROUPS
    pb = _tile(t, PLAN_BLOCK)
    nb = t // pb
    nt = t // MOE_TILE + N_EGROUPS * (epg * (epg - 1) // 2) + N_EGROUPS * (MOE_STEP_TILES - 1)
    nt = -(-nt // MOE_STEP_TILES) * MOE_STEP_TILES
    kt = -(-nt // LANES) * LANES
    n_sorted = nt * MOE_TILE

    pos, tinfo, xs = pl.pallas_call(
        functools.partial(_plan_kernel, tile=MOE_TILE),
        grid=(2, nb),
        in_specs=[pl.BlockSpec((ROUTE_ROWS, pb), lambda ps, b: (0, b)),
                  pl.BlockSpec((pb, d), lambda ps, b: (ps * b, 0))],
        out_specs=[pl.BlockSpec((None, 1, pb), lambda ps, b: (ps * b, 0, 0)),
                   pl.BlockSpec((ROUTE_ROWS, kt), lambda ps, b: (0, 0)),
                   pl.BlockSpec(memory_space=pl.ANY)],
        out_shape=[jax.ShapeDtypeStruct((nb, 1, pb), I32),
                   jax.ShapeDtypeStruct((ROUTE_ROWS, kt), I32),
                   jax.ShapeDtypeStruct((n_sorted * X_ROWS, LANES), F32)],
        scratch_shapes=[pltpu.VMEM((N_CLASS, pb), F32),
                        pltpu.VMEM((N_CLASS, 1), F32),
                        pltpu.VMEM((N_CLASS, 1), F32),
                        pltpu.VMEM((2, pb * X_ROWS, LANES), F32),
                        pltpu.VMEM((1, pb), I32),
                        pltpu.SMEM((1, pb), I32),
                        pltpu.VMEM((ROUTE_ROWS, N_CLASS), I32),
                        pltpu.SMEM((ROUTE_ROWS, N_CLASS), I32),
                        pltpu.VMEM((MOE_TILE * X_ROWS, LANES), F32),
                        pltpu.SemaphoreType.DMA((2,)),
                        pltpu.SemaphoreType.DMA(()),
                        pltpu.SemaphoreType.DMA(())],
        compiler_params=_cparams(("arbitrary", "arbitrary")),
        name="plan",
    )(route, x1)

    w_in = e_w_in.reshape(depth, N_EGROUPS, epg, d, ff2)
    w_out = e_w_out.reshape(depth, N_EGROUPS, epg, ff2 // 2, d)
    step_rows = MOE_TILE * MOE_STEP_TILES

    def group(i, tile_cls):
        return tile_cls[i * MOE_STEP_TILES] >> 6
    ys = pl.pallas_call(
        _moe_kernel,
        grid_spec=pltpu.PrefetchScalarGridSpec(
            num_scalar_prefetch=2,
            grid=(nt // MOE_STEP_TILES,),
            in_specs=[pl.BlockSpec((step_rows * X_ROWS, LANES), lambda i, tc, nu: (i, 0)),
                      pl.BlockSpec((None, None, epg, d, ff2), lambda i, tc, nu: (layer, group(i, tc), 0, 0, 0),
                                   pipeline_mode=pl.Buffered(1)),
                      pl.BlockSpec((None, None, epg, ff2 // 2, d), lambda i, tc, nu: (layer, group(i, tc), 0, 0, 0),
                                   pipeline_mode=pl.Buffered(1))],
            out_specs=pl.BlockSpec((step_rows * Y_ROWS, LANES), lambda i, tc, nu: (i, 0))),
        out_shape=jax.ShapeDtypeStruct((n_sorted * Y_ROWS, LANES), F32),
        compiler_params=_cparams(("arbitrary",)),
        name="moe",
    )(tinfo[0, :nt], tinfo[1, :1], xs, w_in, w_out)

    post_args = (pos, ys, x1, p, ln_g[None], ln_b[None], ple_w.astype(BF16), ple_gw.astype(BF16), ple_gb[None])
    return pl.pallas_call(
        functools.partial(_post_kernel, alpha=alpha, n_blocks=nb),
        grid=(nb,),
        in_specs=[pl.BlockSpec(memory_space=pl.ANY),
                  pl.BlockSpec(memory_space=pl.ANY),
                  pl.BlockSpec((pb, d), lambda i: (i, 0)),
                  pl.BlockSpec((None, pb, p.shape[2]), lambda i: (layer, i, 0))]
                 + [_full(a.shape) for a in post_args[4:]],
        out_specs=pl.BlockSpec((pb, d), lambda i: (i, 0)),
        out_shape=jax.ShapeDtypeStruct((t, d), F32),
        scratch_shapes=[pltpu.SMEM((2, 1, pb), I32), pltpu.VMEM((2, pb * Y_ROWS, LANES), F32),
                        pltpu.SemaphoreType.DMA((2,)), pltpu.SemaphoreType.DMA((2,))],
        compiler_params=_cparams(("arbitrary",)),
        name="post",
    )(*post_args)


def _router_params(wg, bg, we, be):
    d = wg.shape[0]
    w = jnp.zeros((d, ROUTE_COLS), F32).at[:, :N_EGROUPS].set(wg).at[:, 8:8 + we.shape[1]].set(we)
    b = jnp.zeros((1, ROUTE_COLS), F32).at[0, :N_EGROUPS].set(bg).at[0, 8:8 + be.shape[0]].set(be)
    hi = w.astype(BF16)
    return hi, (w - hi.astype(F32)).astype(BF16), b


def _tile(t, want):
    while t % want:
        want //= 2
    return want


def kernel(x, p, a_w_in, a_b_in, a_vn_g, a_vn_b, a_w_s, a_b_s, a_w_out, b_w_in, b_kv_g, b_w_uk, b_w_uv, b_w_out,
           ln1_g, ln1_b, ln2_g, ln2_b, r_wg, r_bg, r_we, r_be, e_w_in, e_w_out, ple_w, ple_gw, ple_gb):
    batch, seq, d = x.shape
    t = batch * seq
    depth = p.shape[0]
    alpha = (2 * depth) ** 0.25
    chunk = a_w_s.shape[-1]
    xt = x.reshape(t, d)
    pt = p.reshape(depth, t, p.shape[-1])
    for i in range(depth):
        j = i // 2
        router = _router_params(r_wg[i], r_bg[i], r_we[i], r_be[i])
        if i % 2 == 0:
            x1, route = _gmlp_layer(xt, a_w_in[j], a_b_in[j], a_vn_g[j], a_vn_b[j], a_w_s[j], a_b_s[j],
                                    a_w_out[j], ln1_g[i], ln1_b[i], router, alpha, max(chunk, _tile(t, 512)))
        else:
            x1, route = _dsa_layer(xt, batch, b_w_in[j], b_kv_g[j], b_w_uk[j], b_w_uv[j], b_w_out[j],
                                   ln1_g[i], ln1_b[i], router, alpha, _tile(t, 512))
        xt = _moe_and_post(x1, route, i, pt, e_w_in, e_w_out, ln2_g[i], ln2_b[i],
                           ple_w[i], ple_gw[i], ple_gb[i], alpha)
    return xt.reshape(batch, seq, d)
```

```python
import functools

import jax
import jax.numpy as jnp
from jax import lax
from jax.experimental import pallas as pl
from jax.experimental.pallas import tpu as pltpu

F32, BF16, I32 = jnp.float32, jnp.bfloat16, jnp.int32

IDX_HEADS = 8
IDX_DIM = 64
TOPK_MAX = 256
N_EGROUPS = 4
LN_EPS = 1e-5
RMS_EPS = 1e-6

LANES = 128
Q_BLOCK = 128
KEY_CHUNK = 512
COUNT_SLAB = 1024
COUNT_ACC = 64
LOG2E = 1.4426950408889634
VMEM_LIMIT = 48 * 1024 * 1024

ROUTE_COLS = LANES
ROUTE_ROWS = 8
N_CLASS = 64 * N_EGROUPS
Y_ROWS = 8
X_ROWS = Y_ROWS + 1
PLAN_BLOCK = 256
MOE_TILE = 128
MOE_STEP_TILES = 4
ZERO_TILES = 1 << ((MOE_STEP_TILES - 1).bit_length() - 1) if MOE_STEP_TILES > 1 else 1
INT_MIN = -(2 ** 31)
NEG_BIG = -1e30


def _cparams(sem, vmem=VMEM_LIMIT):
    return pltpu.CompilerParams(dimension_semantics=sem, vmem_limit_bytes=vmem)


def _full(shape):
    n = len(shape)
    return pl.BlockSpec(shape, lambda *_: (0,) * n)


def _dot(a, b):
    return jnp.dot(a, b, preferred_element_type=F32)


def _dot_t(a, b):
    return lax.dot_general(a, b, (((1,), (1,)), ((), ())), preferred_element_type=F32)


def _layer_norm(x, g, b):
    mu = jnp.mean(x, axis=-1, keepdims=True)
    xc = x - mu
    var = jnp.mean(xc * xc, axis=-1, keepdims=True)
    return xc * lax.rsqrt(var + LN_EPS) * g + b


def _route_rows(lg_t):
    g = [lg_t[i:i + 1] for i in range(N_EGROUPS)]
    gmax = functools.reduce(jnp.maximum, g)
    gsel = jnp.where(g[0] >= gmax, 0.0, jnp.where(g[1] >= gmax, 1.0, jnp.where(g[2] >= gmax, 2.0, 3.0)))
    den = functools.reduce(lambda a, b: a + b, [jnp.exp(gi - gmax) for gi in g])
    p_g = 1.0 / den
    el = lg_t[8:16]
    for gi in range(1, N_EGROUPS):
        el = jnp.where(gsel == float(gi), lg_t[8 + 8 * gi:16 + 8 * gi], el)
    eidx = lax.broadcasted_iota(I32, el.shape, 0).astype(F32)
    m1 = jnp.max(el, axis=0, keepdims=True)
    i1 = jnp.min(jnp.where(el == m1, eidx, 8.0), axis=0, keepdims=True)
    el2 = jnp.where(eidx == i1, -jnp.inf, el)
    m2 = jnp.max(el2, axis=0, keepdims=True)
    i2 = jnp.min(jnp.where(el2 == m2, eidx, 8.0), axis=0, keepdims=True)
    r = jnp.exp(m2 - m1)
    inv = 1.0 / (1.0 + r)
    gate1 = p_g * inv
    gate2 = p_g * r * inv
    first_lo = i1 < i2
    lo = jnp.minimum(i1, i2)
    hi = jnp.maximum(i1, i2)
    cls = gsel * 64.0 + lo * 8.0 + hi
    return cls, jnp.where(first_lo, gate1, gate2), jnp.where(first_lo, gate2, gate1)


def _norm_and_route(y, ln_g, ln_b, wr_hi, wr_lo, br, x1_ref, route_ref):
    tm, _ = y.shape
    x1 = _layer_norm(y, ln_g, ln_b)
    x1_ref[...] = x1
    x_hi = x1.astype(BF16)
    x_lo = (x1 - x_hi.astype(F32)).astype(BF16)
    lg = _dot(x_hi, wr_hi) + _dot(x_lo, wr_hi) + _dot(x_hi, wr_lo) + br
    cls, g_lo, g_hi = _route_rows(lg.T)
    row = lax.broadcasted_iota(I32, (ROUTE_ROWS, tm), 0)
    route_ref[...] = jnp.where(row == 0, cls, jnp.where(row == 1, g_lo, jnp.where(row == 2, g_hi, 0.0)))


def _mixer_out_specs(tm, d):
    return [pl.BlockSpec((tm, d), lambda i: (i, 0)), pl.BlockSpec((ROUTE_ROWS, tm), lambda i: (0, i))]


def _mixer_out_shape(t, d):
    return [jax.ShapeDtypeStruct((t, d), F32), jax.ShapeDtypeStruct((ROUTE_ROWS, t), F32)]


def _gmlp_kernel(x_ref, w_in_ref, b_in_ref, vn_g_ref, vn_b_ref, w_s_ref, b_st_ref, w_out_ref,
                 ln_g_ref, ln_b_ref, wr_hi_ref, wr_lo_ref, br_ref, x1_ref, route_ref, gated_ref, *, alpha):
    x = x_ref[...]
    tm, _ = x.shape
    groups, chunk, _ = w_s_ref.shape
    z = _dot(x.astype(BF16), w_in_ref[...]) + b_in_ref[...]
    z = 0.5 * z * (1.0 + lax.erf(z * (2.0 ** -0.5)))
    half = z.shape[1] // 2
    gd = half // groups
    u = z[:, :half]
    v = _layer_norm(z[:, half:], vn_g_ref[...], vn_b_ref[...]).astype(BF16)
    r = lax.broadcasted_iota(I32, (chunk, chunk), 0)
    c = lax.broadcasted_iota(I32, (chunk, chunk), 1)
    causal = r >= c
    for g in range(groups):
        w_c = jnp.where(causal, w_s_ref[g], 0.0).astype(BF16)
        bias = b_st_ref[:, g:g + 1]
        for ci in range(tm // chunk):
            rows = slice(ci * chunk, (ci + 1) * chunk)
            cols = slice(g * gd, (g + 1) * gd)
            s = _dot(w_c, v[rows, cols]) + bias
            gated_ref[rows, cols] = (u[rows, cols] * s).astype(BF16)
    mix = _dot(gated_ref[...], w_out_ref[...])
    _norm_and_route(alpha * x + mix, ln_g_ref[...], ln_b_ref[...], wr_hi_ref[...], wr_lo_ref[...],
                    br_ref[...], x1_ref, route_ref)


def _gmlp_layer(xt, w_in, b_in, vn_g, vn_b, w_s, b_s, w_out, ln_g, ln_b, router, alpha, tm):
    t, d = xt.shape
    half = w_out.shape[0]
    wr_hi, wr_lo, br = router
    args = (xt, w_in.astype(BF16), b_in[None], vn_g[None], vn_b[None], w_s, b_s.T, w_out.astype(BF16),
            ln_g[None], ln_b[None], wr_hi, wr_lo, br)
    in_specs = [pl.BlockSpec((tm, d), lambda i: (i, 0))] + [_full(a.shape) for a in args[1:]]
    return pl.pallas_call(
        functools.partial(_gmlp_kernel, alpha=alpha),
        grid=(t // tm,),
        in_specs=in_specs,
        out_specs=_mixer_out_specs(tm, d),
        out_shape=_mixer_out_shape(t, d),
        scratch_shapes=[pltpu.VMEM((tm, half), BF16)],
        compiler_params=_cparams(("parallel",)),
        name="gmlp",
    )(*args)


def _dsa_proj_kernel(x_ref, wq_ref, wc_ref, wqi_ref, wkw_ref, kvg_ref, wuk_t_ref,
                     qlat_ref, ckv_ref, ckv_t_ref, qidx_ref, kidx_ref, wi_t_ref, *, scale_q, scale_w):
    xb = x_ref[...].astype(BF16)
    nq = qlat_ref.shape[0]
    kc = ckv_t_ref.shape[2]
    heads, hd, _ = wuk_t_ref.shape
    q = _dot(xb, wq_ref[...])
    for h in range(heads):
        ql = (_dot(q[:, h * hd:(h + 1) * hd].astype(BF16), wuk_t_ref[h]) * scale_q).astype(BF16)
        for b in range(nq):
            qlat_ref[b, h * Q_BLOCK:(h + 1) * Q_BLOCK, :] = ql[b * Q_BLOCK:(b + 1) * Q_BLOCK]
    c = _dot(xb, wc_ref[...])
    ms = jnp.mean(c * c, axis=-1, keepdims=True)
    c = c * lax.rsqrt(ms + RMS_EPS) * kvg_ref[...]
    ckv_ref[...] = c.astype(BF16)
    for b in range(ckv_t_ref.shape[0]):
        ckv_t_ref[b] = c[b * kc:(b + 1) * kc].T.astype(BF16)
    qi = _dot(xb, wqi_ref[...]).astype(BF16)
    for h in range(IDX_HEADS):
        for b in range(nq):
            qidx_ref[b, h * Q_BLOCK:(h + 1) * Q_BLOCK, :] = qi[b * Q_BLOCK:(b + 1) * Q_BLOCK,
                                                               h * IDX_DIM:(h + 1) * IDX_DIM]
    kw = _dot(xb, wkw_ref[...])
    kidx_ref[...] = kw[:, :IDX_DIM].astype(BF16)
    wi_t_ref[...] = kw.T[IDX_DIM:IDX_DIM + IDX_HEADS] * scale_w


def _dsa_attn_kernel(qidx_ref, wi_t_ref, qlat_ref, kidx_ref, ckv_ref, ckv_t_ref, o_ref,
                     key_ref, key16_ref, acc_ref, m_ref, l_ref, a_ref, p_ref, *, k_sel, heads):
    qb = pl.program_id(1)
    kc = ckv_t_ref.shape[2]
    slab = key_ref.shape[1]
    cps = slab // kc
    nq = Q_BLOCK
    pair = 2 * nq
    n_pairs = heads // 2
    n_chunks = (qb * nq + nq + kc - 1) // kc
    n_slabs = (n_chunks + cps - 1) // cps

    def key_chunk(j):
        return key_ref.at[j // cps, pl.ds(pl.multiple_of((j % cps) * kc, kc), kc), :]

    q_pos = qb * nq + lax.broadcasted_iota(I32, (kc, nq), 1)
    k_off = lax.broadcasted_iota(I32, (kc, nq), 0)

    def key16_chunk(j):
        return key16_ref.at[j // cps, pl.ds(pl.multiple_of((j % cps) * kc, kc), kc), :]

    def pad_chunk(j, carry):
        key_chunk(j)[...] = jnp.full((kc, nq), jnp.nan, F32)
        key16_chunk(j)[...] = jnp.full((kc, nq), jnp.nan, BF16)
        return carry

    lax.fori_loop(n_chunks, n_slabs * cps, pad_chunk, 0)

    def score_chunk(j, carry):
        kch = kidx_ref[pl.ds(pl.multiple_of(j * kc, kc), kc), :]
        tot = None
        for p in range(IDX_HEADS // 2):
            sc = _dot_t(kch, qidx_ref[p * pair:(p + 1) * pair, :])
            for hh in range(2):
                h = 2 * p + hh
                r = jnp.maximum(sc[:, hh * nq:(hh + 1) * nq], 0.0) * wi_t_ref[h:h + 1, :]
                tot = r if tot is None else tot + r
        tot = jnp.where(k_off + j * kc <= q_pos, tot, jnp.nan)
        key_chunk(j)[...] = tot
        key16_chunk(j)[...] = tot.astype(BF16)
        return carry

    lax.fori_loop(0, n_chunks, score_chunk, 0)

    def count(t, strict=False):
        tb = jnp.broadcast_to(t, (COUNT_ACC, nq))

        def body(s, acc):
            for g in range(slab // COUNT_ACC):
                blk = key_ref[s, g * COUNT_ACC:(g + 1) * COUNT_ACC, :]
                acc = jnp.where(blk > tb if strict else blk >= tb, acc + 1.0, acc)
            return acc

        acc = lax.fori_loop(0, n_slabs, body, jnp.zeros((COUNT_ACC, nq), F32))
        return jnp.sum(acc, axis=0, keepdims=True)

    def count16(t):
        rows16 = 2 * COUNT_ACC
        tb = jnp.broadcast_to(t.astype(BF16), (rows16, nq))

        def body(s, acc):
            for g in range(slab // rows16):
                blk = key16_ref[s, g * rows16:(g + 1) * rows16, :]
                acc = jnp.where(blk >= tb, acc + 1.0, acc)
            return acc

        acc = lax.fori_loop(0, n_slabs, body, jnp.zeros((rows16, nq), BF16))
        return jnp.sum(acc.astype(F32), axis=0, keepdims=True)

    def decode(code):
        return lax.bitcast_convert_type(code ^ ((code >> 31) & 0x7FFFFFFF), F32)

    def decode16(c16):
        return decode(jnp.left_shift(c16, 16) | jnp.where(c16 < 0, 0xFFFF, 0))

    def bit16_step(b, c16):
        t = c16 + jnp.left_shift(jnp.int32(1), 15 - b)
        return jnp.where(count16(decode16(t)) >= float(k_sel), t, c16)

    no_code = -(2 ** 15)
    c16 = lax.fori_loop(0, 16, bit16_step, jnp.full((1, nq), no_code, I32))

    def bit_step(b, c):
        t = c + jnp.left_shift(jnp.int32(1), 17 - b)
        return jnp.where((count(decode(t)) >= float(k_sel)) & (t > c), t, c)

    c_sel = lax.fori_loop(0, 18, bit_step, jnp.left_shift(jnp.maximum(c16, no_code + 1) - 1, 16))
    c_sel = jnp.where(c16 == no_code, -jnp.inf, decode(c_sel))
    n_ge = jnp.where(c16 == no_code, 0.0, count(c_sel))

    has_tie = jnp.max(n_ge) > float(k_sel)

    @pl.when(has_tie)
    def _():
        need = float(k_sel) - count(c_sel, strict=True)
        tie_col = n_ge > float(k_sel)
        cb = jnp.broadcast_to(c_sel, (kc, nq))
        lower = (lax.broadcasted_iota(I32, (kc, kc), 1) < lax.broadcasted_iota(I32, (kc, kc), 0))
        lower = jnp.where(lower, 1.0, 0.0).astype(BF16)

        def fix(j, seen):
            blk = key_chunk(j)[...]
            eq = blk == cb
            eq_f = jnp.where(eq, 1.0, 0.0)
            rank = seen + _dot(lower, eq_f.astype(BF16))
            drop = eq & tie_col & (rank >= need)
            key_chunk(j)[...] = jnp.where(drop, jnp.nan, blk)
            return seen + jnp.sum(eq_f, axis=0, keepdims=True)

        lax.fori_loop(0, n_chunks, fix, jnp.zeros((1, nq), F32))

    thr = jnp.broadcast_to(c_sel, (kc, nq))
    m_ref[...] = jnp.full(m_ref.shape, NEG_BIG, F32)
    l_ref[...] = jnp.zeros(l_ref.shape, F32)
    acc_ref[...] = jnp.zeros(acc_ref.shape, F32)

    def softmax_chunk(j):
        ck = ckv_ref[pl.ds(pl.multiple_of(j * kc, kc), kc), :]
        bias = jnp.where(key_chunk(j)[...] >= thr, 0.0, NEG_BIG)
        bias = jnp.concatenate([bias, bias], axis=1)
        slot = j % 2
        for p in range(n_pairs):
            lg = _dot_t(ck, qlat_ref[p * pair:(p + 1) * pair, :]) + bias
            m_old = m_ref[p]
            m_new = jnp.maximum(m_old, jnp.max(lg, axis=0, keepdims=True))
            pr = jnp.exp2(lg - m_new)
            alpha = jnp.exp2(m_old - m_new)
            l_ref[p] = alpha * l_ref[p] + jnp.sum(pr, axis=0, keepdims=True)
            m_ref[p] = m_new
            a_ref[slot, p] = alpha
            p_ref[slot, p] = pr.astype(BF16)

    def accumulate_chunk(j):
        ck_t = ckv_t_ref[j]
        slot = j % 2
        for p in range(n_pairs):
            acc_ref[p] = a_ref[slot, p] * acc_ref[p] + _dot(ck_t, p_ref[slot, p])

    def att_step(j, carry):
        accumulate_chunk(j - 1)
        softmax_chunk(j)
        return carry

    softmax_chunk(0)
    lax.fori_loop(1, n_chunks, att_step, 0)
    accumulate_chunk(n_chunks - 1)
    for p in range(n_pairs):
        o_ref[p * pair:(p + 1) * pair, :] = (acc_ref[p] * (1.0 / l_ref[p])).T.astype(o_ref.dtype)


def _dsa_out_kernel(o_ref, x_ref, wuv_ref, wout_ref, ln_g_ref, ln_b_ref, wr_hi_ref, wr_lo_ref, br_ref,
                    x1_ref, route_ref, o2_ref, *, alpha):
    nq = o_ref.shape[0]
    heads, _, hd = wuv_ref.shape
    for b in range(nq):
        for h in range(heads):
            oh = o_ref[b, h * Q_BLOCK:(h + 1) * Q_BLOCK, :]
            o2_ref[b * Q_BLOCK:(b + 1) * Q_BLOCK, h * hd:(h + 1) * hd] = _dot(oh, wuv_ref[h]).astype(BF16)
    mix = _dot(o2_ref[...], wout_ref[...])
    _norm_and_route(alpha * x_ref[...] + mix, ln_g_ref[...], ln_b_ref[...], wr_hi_ref[...], wr_lo_ref[...],
                    br_ref[...], x1_ref, route_ref)


def _dsa_layer(xt, batch, w_in, kv_g, w_uk, w_uv, w_out, ln_g, ln_b, router, alpha, tm):
    t, d = xt.shape
    seq = t // batch
    heads, lat, hd = w_uk.shape
    qw = heads * hd
    nqb = t // Q_BLOCK
    nq = tm // Q_BLOCK
    k_sel = min(TOPK_MAX, seq // 4)
    kc = min(KEY_CHUNK, seq)
    slab = min(COUNT_SLAB, seq)
    w_in = w_in.astype(BF16)
    o1, o2, o3 = qw, qw + lat, qw + lat + IDX_HEADS * IDX_DIM
    wkw = jnp.pad(w_in[:, o3:], ((0, 0), (0, LANES - (w_in.shape[1] - o3))))
    proj_args = (xt, w_in[:, :o1], w_in[:, o1:o2], w_in[:, o2:o3], wkw, kv_g[None],
                 jnp.swapaxes(w_uk, 1, 2).astype(BF16))
    qlat, ckv, ckv_t, qidx, kidx, wi_t = pl.pallas_call(
        functools.partial(_dsa_proj_kernel, scale_q=hd ** -0.5 * LOG2E, scale_w=(IDX_HEADS * IDX_DIM) ** -0.5),
        grid=(t // tm,),
        in_specs=[pl.BlockSpec((tm, d), lambda i: (i, 0))] + [_full(a.shape) for a in proj_args[1:]],
        out_specs=[pl.BlockSpec((nq, heads * Q_BLOCK, lat), lambda i: (i, 0, 0)),
                   pl.BlockSpec((tm, lat), lambda i: (i, 0)),
                   pl.BlockSpec((tm // kc, lat, kc), lambda i: (i, 0, 0)),
                   pl.BlockSpec((nq, IDX_HEADS * Q_BLOCK, IDX_DIM), lambda i: (i, 0, 0)),
                   pl.BlockSpec((tm, IDX_DIM), lambda i: (i, 0)),
                   pl.BlockSpec((IDX_HEADS, tm), lambda i: (0, i))],
        out_shape=[jax.ShapeDtypeStruct((nqb, heads * Q_BLOCK, lat), BF16),
                   jax.ShapeDtypeStruct((t, lat), BF16),
                   jax.ShapeDtypeStruct((t // kc, lat, kc), BF16),
                   jax.ShapeDtypeStruct((nqb, IDX_HEADS * Q_BLOCK, IDX_DIM), BF16),
                   jax.ShapeDtypeStruct((t, IDX_DIM), BF16),
                   jax.ShapeDtypeStruct((IDX_HEADS, t), F32)],
        compiler_params=_cparams(("parallel",)),
        name="dsa_proj",
    )(*proj_args)

    nq_seq = seq // Q_BLOCK
    n_kc = seq // kc
    rows = heads * Q_BLOCK
    o = pl.pallas_call(
        functools.partial(_dsa_attn_kernel, k_sel=k_sel, heads=heads),
        grid=(batch, nq_seq),
        in_specs=[pl.BlockSpec((None, IDX_HEADS * Q_BLOCK, IDX_DIM), lambda b, q: (b * nq_seq + q, 0, 0)),
                  pl.BlockSpec((IDX_HEADS, Q_BLOCK), lambda b, q: (0, b * nq_seq + q)),
                  pl.BlockSpec((None, rows, lat), lambda b, q: (b * nq_seq + q, 0, 0)),
                  pl.BlockSpec((None, seq, IDX_DIM), lambda b, q: (b, 0, 0)),
                  pl.BlockSpec((None, seq, lat), lambda b, q: (b, 0, 0)),
                  pl.BlockSpec((None, n_kc, lat, kc), lambda b, q: (b, 0, 0, 0))],
        out_specs=pl.BlockSpec((None, rows, lat), lambda b, q: (b * nq_seq + q, 0, 0)),
        out_shape=jax.ShapeDtypeStruct((nqb, rows, lat), BF16),
        scratch_shapes=[pltpu.VMEM((seq // slab, slab, Q_BLOCK), F32),
                        pltpu.VMEM((seq // slab, slab, Q_BLOCK), BF16),
                        pltpu.VMEM((heads // 2, lat, 2 * Q_BLOCK), F32),
                        pltpu.VMEM((heads // 2, 1, 2 * Q_BLOCK), F32),
                        pltpu.VMEM((heads // 2, 1, 2 * Q_BLOCK), F32),
                        pltpu.VMEM((2, heads // 2, 1, 2 * Q_BLOCK), F32),
                        pltpu.VMEM((2, heads // 2, kc, 2 * Q_BLOCK), BF16)],
        compiler_params=_cparams(("parallel", "arbitrary")),
        name="dsa_attn",
    )(qidx, wi_t, qlat, kidx.reshape(batch, seq, IDX_DIM), ckv.reshape(batch, seq, lat),
      ckv_t.reshape(batch, n_kc, lat, kc))

    wr_hi, wr_lo, br = router
    out_args = (o, xt, w_uv.astype(BF16), w_out.astype(BF16), ln_g[None], ln_b[None], wr_hi, wr_lo, br)
    return pl.pallas_call(
        functools.partial(_dsa_out_kernel, alpha=alpha),
        grid=(t // tm,),
        in_specs=[pl.BlockSpec((nq, rows, lat), lambda i: (i, 0, 0)),
                  pl.BlockSpec((tm, d), lambda i: (i, 0))] + [_full(a.shape) for a in out_args[2:]],
        out_specs=_mixer_out_specs(tm, d),
        out_shape=_mixer_out_shape(t, d),
        scratch_shapes=[pltpu.VMEM((tm, qw), BF16)],
        compiler_params=_cparams(("parallel",)),
        name="dsa_out",
    )(*out_args)


DMA_UNROLL = 8


def _start_token_moves(n, copy_of):
    def body(g, carry):
        for u in range(DMA_UNROLL):
            copy_of(g * DMA_UNROLL + u).start(priority=u % 2)
        return carry
    lax.fori_loop(0, n // DMA_UNROLL, body, 0)


def _tok(ref, i, rows):
    return ref.at[pl.ds(i * rows, rows)]


def _rows_of(ref, j, n, rows):
    return ref.at[pl.ds(j, n, stride=rows), :]


class _TokenGather:
    def __init__(self, i, n, idx_hbm, src_hbm, idx_smem, buf, isem, rsem):
        self.i, self.n = i, n
        self.idx_hbm, self.src_hbm, self.idx_smem, self.buf, self.isem, self.rsem = (
            idx_hbm, src_hbm, idx_smem, buf, isem, rsem)
        self.tm = idx_smem.shape[2]
        self.rows = buf.shape[1] // self.tm
        self.slot = i % 2

    def _idx_copy(self, blk, s):
        return pltpu.make_async_copy(self.idx_hbm.at[blk], self.idx_smem.at[s], self.isem.at[s])

    def _token_copy(self, s, r):
        return pltpu.make_async_copy(_tok(self.src_hbm, self.idx_smem[s, 0, r], self.rows),
                                     _tok(self.buf.at[s], r, self.rows), self.rsem.at[s])

    def _wait_tokens(self, s):
        pltpu.make_async_copy(self.src_hbm.at[pl.ds(0, self.tm * self.rows)], self.buf.at[s], self.rsem.at[s]).wait()

    def arrive(self):
        @pl.when(self.i == 0)
        def _():
            first = self._idx_copy(0, 0)
            first.start()
            first.wait()
            _start_token_moves(self.tm, lambda r: self._token_copy(0, r))
            self._idx_copy(min(1, self.n - 1), 1).start()

        self._idx_copy(jnp.minimum(self.i + 1, self.n - 1), 1 - self.slot).wait()
        self._wait_tokens(self.slot)

    def prefetch(self):
        nslot = 1 - self.slot
        for r in range(self.tm):
            self._token_copy(nslot, r).start(priority=r % 2)
        self._idx_copy(jnp.minimum(self.i + 2, self.n - 1), self.slot).start()

    def finish(self):
        @pl.when(self.i == self.n - 1)
        def _():
            self._wait_tokens(1 - self.slot)
            self._idx_copy(self.n - 1, self.slot).wait()


def _plan_kernel(route_ref, x_ref, pos_ref, tinfo_ref, xs_ref,
                 hist_ref, off_ref, carry_ref, stage_ref, ppos_vmem, ppos_smem, fill_vmem, fill_smem, zero_ref,
                 ssem, psem, zsem, *, tile):
    ps = pl.program_id(0)
    b = pl.program_id(1)
    nb = pl.num_programs(1)
    n_cls, pb = hist_ref.shape
    kt = tinfo_ref.shape[1]
    cls = route_ref[0:1, :].astype(I32)
    cid = lax.broadcasted_iota(I32, (n_cls, pb), 0)
    onehot = jnp.where(cid == cls, 1.0, 0.0)

    def zero_fill(start):
        def piece(tok0, n_tok):
            cp = pltpu.make_async_copy(zero_ref.at[pl.ds(0, n_tok * X_ROWS)],
                                       xs_ref.at[pl.ds(tok0 * X_ROWS, n_tok * X_ROWS)], zsem)
            cp.start() if start else cp.wait()

        def per_class(c, carry):
            first, n = fill_smem[0, c], fill_smem[1, c]
            bit = zero_ref.shape[0] // X_ROWS
            while bit:
                @pl.when((n & bit) != 0)
                def _(bit=bit):
                    piece(first + (n & -(2 * bit)), bit)
                bit //= 2
            return carry

        lax.fori_loop(0, n_cls, per_class, 0)

        def per_tile(k, carry):
            piece(k * tile, tile)
            return carry

        lax.fori_loop(fill_smem[2, 0], xs_ref.shape[0] // (tile * X_ROWS), per_tile, 0)

    @pl.when(ps == 0)
    def _():
        @pl.when(b == 0)
        def _():
            hist_ref[...] = jnp.zeros(hist_ref.shape, F32)
            zero_ref[...] = jnp.zeros(zero_ref.shape, F32)
        hist_ref[...] += onehot

    @pl.when((ps == 1) & (b == 0))
    def _():
        counts = jnp.sum(hist_ref[...], axis=1, keepdims=True)
        tiles = jnp.floor((counts + float(tile - 1)) * (1.0 / tile))
        cid_col = lax.broadcasted_iota(I32, (n_cls, 1), 0)
        per_group = n_cls // N_EGROUPS
        for g in range(N_EGROUPS):
            g_tiles = jnp.sum(tiles[g * per_group:(g + 1) * per_group], axis=0, keepdims=True)
            extra = jnp.ceil(g_tiles * (1.0 / MOE_STEP_TILES)) * float(MOE_STEP_TILES) - g_tiles
            tiles = jnp.where(cid_col == (g + 1) * per_group - 1, extra, tiles)
        lower = lax.broadcasted_iota(I32, (n_cls, n_cls), 1) < lax.broadcasted_iota(I32, (n_cls, n_cls), 0)
        lower = jnp.where(lower, 1.0, 0.0).astype(BF16)
        off = _dot(lower, jnp.broadcast_to(tiles, (n_cls, LANES)).astype(BF16))[:, :1]
        off_ref[...] = off * float(tile)
        carry_ref[...] = jnp.zeros(carry_ref.shape, F32)
        k = lax.broadcasted_iota(I32, (n_cls, kt), 1).astype(F32)
        cid_f = lax.broadcasted_iota(I32, (n_cls, kt), 0).astype(F32)
        mine = (k >= off) & (k < off + tiles)
        tcls = jnp.sum(jnp.where(mine, cid_f, 0.0), axis=0, keepdims=True)
        n_used = jnp.sum(tiles, axis=0, keepdims=True)
        last = jnp.max(jnp.where(tiles > 0.0, cid_f[:, :1], 0.0), axis=0, keepdims=True)
        tcls = jnp.where(k[:1] >= n_used, last, tcls)
        row = lax.broadcasted_iota(I32, (ROUTE_ROWS, kt), 0)
        tinfo_ref[...] = jnp.where(row == 0, tcls, jnp.where(row == 1, n_used, 0.0)).astype(I32)
        as_row = lambda col: jnp.broadcast_to(col, (n_cls, LANES)).T[0:1]
        frow = lax.broadcasted_iota(I32, fill_vmem.shape, 0)
        fill_vmem[...] = jnp.where(frow == 0, as_row(off * float(tile) + counts),
                                   jnp.where(frow == 1, as_row(tiles * float(tile) - counts), n_used)).astype(I32)
        to_fill = pltpu.make_async_copy(fill_vmem, fill_smem, psem)
        to_fill.start()
        to_fill.wait()
        zero_fill(start=True)

    @pl.when(ps == 1)
    def _():
        upper = lax.broadcasted_iota(I32, (pb, pb), 0) < lax.broadcasted_iota(I32, (pb, pb), 1)
        upper = jnp.where(upper, 1.0, 0.0).astype(BF16)
        before = _dot(onehot.astype(BF16), upper)
        ppos = jnp.sum(onehot * (before + carry_ref[...] + off_ref[...]), axis=0, keepdims=True).astype(I32)
        carry_ref[...] += jnp.sum(onehot, axis=1, keepdims=True)
        pos_ref[...] = ppos
        ppos_vmem[...] = ppos
        to_smem = pltpu.make_async_copy(ppos_vmem, ppos_smem, psem)
        to_smem.start()

        slot = b % 2
        stage = stage_ref.at[slot]

        def drain(s):
            pltpu.make_async_copy(stage_ref.at[s], xs_ref.at[pl.ds(0, pb * X_ROWS)], ssem.at[s]).wait()

        @pl.when(b >= 2)
        def _():
            drain(slot)

        for j in range(Y_ROWS):
            _rows_of(stage, j, pb, X_ROWS)[...] = x_ref[:, j * LANES:(j + 1) * LANES]
        record = jnp.concatenate([route_ref[...], jnp.zeros((LANES - ROUTE_ROWS, pb), F32)], axis=0).T
        _rows_of(stage, Y_ROWS, pb, X_ROWS)[...] = record

        to_smem.wait()
        _start_token_moves(pb, lambda r: pltpu.make_async_copy(
            _tok(stage, r, X_ROWS), _tok(xs_ref, ppos_smem[0, r], X_ROWS), ssem.at[slot]))

        @pl.when(b == nb - 1)
        def _():
            drain(slot)

            @pl.when(nb > 1)
            def _():
                drain(1 - slot)

            zero_fill(start=False)


def _moe_kernel(tcls_ref, nused_ref, xs_ref, win_ref, wout_ref, ys_ref):
    i = pl.program_id(0)
    tm = xs_ref.shape[0] // (X_ROWS * MOE_STEP_TILES)
    ff = wout_ref.shape[1]

    @pl.when(i * MOE_STEP_TILES < nused_ref[0])
    def _():
        for jt in range(MOE_STEP_TILES):
            cls = tcls_ref[i * MOE_STEP_TILES + jt]
            x0, y0 = jt * tm * X_ROWS, jt * tm * Y_ROWS
            xb = jnp.concatenate([_rows_of(xs_ref, x0 + j, tm, X_ROWS)[...].astype(BF16) for j in range(Y_ROWS)],
                                 axis=1)
            record = _rows_of(xs_ref, x0 + Y_ROWS, tm, X_ROWS)[...]
            y = None
            for e, gate in (((cls >> 3) & 7, record[:, 1:2]), (cls & 7, record[:, 2:3])):
                h = _dot(xb, win_ref[e])
                a = h[:, :ff]
                act = (a * jax.nn.sigmoid(a) * h[:, ff:] * gate).astype(BF16)
                ye = _dot(act, wout_ref[e])
                y = ye if y is None else y + ye
            for j in range(Y_ROWS):
                _rows_of(ys_ref, y0 + j, tm, Y_ROWS)[...] = y[:, j * LANES:(j + 1) * LANES]

    @pl.when(i * MOE_STEP_TILES >= nused_ref[0])
    def _():
        ys_ref[...] = jnp.zeros(ys_ref.shape, F32)


def _post_kernel(pos_ref, ys_ref, x1_ref, p_ref, ln_g_ref, ln_b_ref, pw_ref, gw_ref, gb_ref, out_ref,
                 idx_smem, ybuf, isem, rsem, *, alpha, n_blocks):
    i = pl.program_id(0)
    tm = x1_ref.shape[0]
    gather = _TokenGather(i, n_blocks, pos_ref, ys_ref, idx_smem, ybuf, isem, rsem)
    gather.arrive()
    gather.prefetch()
    yb = ybuf.at[i % 2]
    ffn = jnp.concatenate([_rows_of(yb, j, tm, Y_ROWS)[...] for j in range(Y_ROWS)], axis=1)
    x2 = _layer_norm(alpha * x1_ref[...] + ffn, ln_g_ref[...], ln_b_ref[...])
    gate = jax.nn.sigmoid(_dot(x2.astype(BF16), gw_ref[...]) + gb_ref[...])
    out_ref[...] = x2 + _dot(p_ref[...].astype(BF16), pw_ref[...]) * gate
    gather.finish()


def _moe_and_post(x1, route, layer, p, e_w_in, e_w_out, ln_g, ln_b, ple_w, ple_gw, ple_gb, alpha):
    t, d = x1.shape
    assert d == LANES * Y_ROWS
    depth, n_exp, _, ff2 = e_w_in.shape
    epg = n_exp // N_EGROUPS
    pb = _tile(t, PLAN_BLOCK)
    nb = t // pb
    nt = t // MOE_TILE + N_EGROUPS * (epg * (epg - 1) // 2) + N_EGROUPS * (MOE_STEP_TILES - 1)
    nt = -(-nt // MOE_STEP_TILES) * MOE_STEP_TILES
    kt = -(-nt // LANES) * LANES
    n_sorted = nt * MOE_TILE

    pos, tinfo, xs = pl.pallas_call(
        functools.partial(_plan_kernel, tile=MOE_TILE),
        grid=(2, nb),
        in_specs=[pl.BlockSpec((ROUTE_ROWS, pb), lambda ps, b: (0, b)),
                  pl.BlockSpec((pb, d), lambda ps, b: (ps * b, 0))],
        out_specs=[pl.BlockSpec((None, 1, pb), lambda ps, b: (ps * b, 0, 0)),
                   pl.BlockSpec((ROUTE_ROWS, kt), lambda ps, b: (0, 0)),
                   pl.BlockSpec(memory_space=pl.ANY)],
        out_shape=[jax.ShapeDtypeStruct((nb, 1, pb), I32),
                   jax.ShapeDtypeStruct((ROUTE_ROWS, kt), I32),
                   jax.ShapeDtypeStruct((n_sorted * X_ROWS, LANES), F32)],
        scratch_shapes=[pltpu.VMEM((N_CLASS, pb), F32),
                        pltpu.VMEM((N_CLASS, 1), F32),
                        pltpu.VMEM((N_CLASS, 1), F32),
                        pltpu.VMEM((2, pb * X_ROWS, LANES), F32),
                        pltpu.VMEM((1, pb), I32),
                        pltpu.SMEM((1, pb), I32),
                        pltpu.VMEM((ROUTE_ROWS, N_CLASS), I32),
                        pltpu.SMEM((ROUTE_ROWS, N_CLASS), I32),
                        pltpu.VMEM((MOE_TILE * ZERO_TILES * X_ROWS, LANES), F32),
                        pltpu.SemaphoreType.DMA((2,)),
                        pltpu.SemaphoreType.DMA(()),
                        pltpu.SemaphoreType.DMA(())],
        compiler_params=_cparams(("arbitrary", "arbitrary")),
        name="plan",
    )(route, x1)

    w_in = e_w_in.reshape(depth, N_EGROUPS, epg, d, ff2)
    w_out = e_w_out.reshape(depth, N_EGROUPS, epg, ff2 // 2, d)
    step_rows = MOE_TILE * MOE_STEP_TILES

    def group(i, tile_cls):
        return tile_cls[i * MOE_STEP_TILES] >> 6
    ys = pl.pallas_call(
        _moe_kernel,
        grid_spec=pltpu.PrefetchScalarGridSpec(
            num_scalar_prefetch=2,
            grid=(nt // MOE_STEP_TILES,),
            in_specs=[pl.BlockSpec((step_rows * X_ROWS, LANES), lambda i, tc, nu: (i, 0)),
                      pl.BlockSpec((None, None, epg, d, ff2), lambda i, tc, nu: (layer, group(i, tc), 0, 0, 0),
                                   pipeline_mode=pl.Buffered(1)),
                      pl.BlockSpec((None, None, epg, ff2 // 2, d), lambda i, tc, nu: (layer, group(i, tc), 0, 0, 0),
                                   pipeline_mode=pl.Buffered(1))],
            out_specs=pl.BlockSpec((step_rows * Y_ROWS, LANES), lambda i, tc, nu: (i, 0))),
        out_shape=jax.ShapeDtypeStruct((n_sorted * Y_ROWS, LANES), F32),
        compiler_params=_cparams(("arbitrary",)),
        name="moe",
    )(tinfo[0, :nt], tinfo[1, :1], xs, w_in, w_out)

    post_args = (pos, ys, x1, p, ln_g[None], ln_b[None], ple_w.astype(BF16), ple_gw.astype(BF16), ple_gb[None])
    return pl.pallas_call(
        functools.partial(_post_kernel, alpha=alpha, n_blocks=nb),
        grid=(nb,),
        in_specs=[pl.BlockSpec(memory_space=pl.ANY),
                  pl.BlockSpec(memory_space=pl.ANY),
                  pl.BlockSpec((pb, d), lambda i: (i, 0)),
                  pl.BlockSpec((None, pb, p.shape[2]), lambda i: (layer, i, 0))]
                 + [_full(a.shape) for a in post_args[4:]],
        out_specs=pl.BlockSpec((pb, d), lambda i: (i, 0)),
        out_shape=jax.ShapeDtypeStruct((t, d), F32),
        scratch_shapes=[pltpu.SMEM((2, 1, pb), I32), pltpu.VMEM((2, pb * Y_ROWS, LANES), F32),
                        pltpu.SemaphoreType.DMA((2,)), pltpu.SemaphoreType.DMA((2,))],
        compiler_params=_cparams(("arbitrary",)),
        name="post",
    )(*post_args)


def _router_params(wg, bg, we, be):
    d = wg.shape[0]
    w = jnp.zeros((d, ROUTE_COLS), F32).at[:, :N_EGROUPS].set(wg).at[:, 8:8 + we.shape[1]].set(we)
    b = jnp.zeros((1, ROUTE_COLS), F32).at[0, :N_EGROUPS].set(bg).at[0, 8:8 + be.shape[0]].set(be)
    hi = w.astype(BF16)
    return hi, (w - hi.astype(F32)).astype(BF16), b


def _tile(t, want):
    while t % want:
        want //= 2
    return want


def kernel(x, p, a_w_in, a_b_in, a_vn_g, a_vn_b, a_w_s, a_b_s, a_w_out, b_w_in, b_kv_g, b_w_uk, b_w_uv, b_w_out,
           ln1_g, ln1_b, ln2_g, ln2_b, r_wg, r_bg, r_we, r_be, e_w_in, e_w_out, ple_w, ple_gw, ple_gb):
    batch, seq, d = x.shape
    t = batch * seq
    depth = p.shape[0]
    alpha = (2 * depth) ** 0.25
    chunk = a_w_s.shape[-1]
    xt = x.reshape(t, d)
    pt = p.reshape(depth, t, p.shape[-1])
    for i in range(depth):
        j = i // 2
        router = _router_params(r_wg[i], r_bg[i], r_we[i], r_be[i])
        if i % 2 == 0:
            x1, route = _gmlp_layer(xt, a_w_in[j], a_b_in[j], a_vn_g[j], a_vn_b[j], a_w_s[j], a_b_s[j],
                                    a_w_out[j], ln1_g[i], ln1_b[i], router, alpha, max(chunk, _tile(t, 512)))
        else:
            x1, route = _dsa_layer(xt, batch, b_w_in[j], b_kv_g[j], b_w_uk[j], b_w_uv[j], b_w_out[j],
                                   ln1_g[i], ln1_b[i], router, alpha, _tile(t, 512))
        xt = _moe_and_post(x1, route, i, pt, e_w_in, e_w_out, ln2_g[i], ln2_b[i],
                           ple_w[i], ple_gw[i], ple_gb[i], alpha)
    return xt.reshape(batch, seq, d)
```

```python
import functools

import jax
import jax.numpy as jnp
from jax import lax
from jax.experimental import pallas as pl
from jax.experimental.pallas import tpu as pltpu

F32, BF16, I32 = jnp.float32, jnp.bfloat16, jnp.int32

IDX_HEADS = 8
IDX_DIM = 64
TOPK_MAX = 256
N_EGROUPS = 4
LN_EPS = 1e-5
RMS_EPS = 1e-6

LANES = 128
Q_BLOCK = 128
KEY_CHUNK = 512
COUNT_SLAB = 1024
COUNT_ACC = 64
LOG2E = 1.4426950408889634
VMEM_LIMIT = 48 * 1024 * 1024

ROUTE_COLS = LANES
ROUTE_ROWS = 8
N_CLASS = 64 * N_EGROUPS
Y_ROWS = 8
X_ROWS = Y_ROWS + 1
PLAN_BLOCK = 512
MOE_TILE = 128
MOE_STEP_TILES = 4
ZERO_TILES = 1 << ((MOE_STEP_TILES - 1).bit_length() - 1) if MOE_STEP_TILES > 1 else 1
INT_MIN = -(2 ** 31)
NEG_BIG = -1e30


def _cparams(sem, vmem=VMEM_LIMIT):
    return pltpu.CompilerParams(dimension_semantics=sem, vmem_limit_bytes=vmem)


def _full(shape):
    n = len(shape)
    return pl.BlockSpec(shape, lambda *_: (0,) * n)


def _dot(a, b):
    return jnp.dot(a, b, preferred_element_type=F32)


def _dot_t(a, b):
    return lax.dot_general(a, b, (((1,), (1,)), ((), ())), preferred_element_type=F32)


def _layer_norm(x, g, b):
    mu = jnp.mean(x, axis=-1, keepdims=True)
    xc = x - mu
    var = jnp.mean(xc * xc, axis=-1, keepdims=True)
    return xc * lax.rsqrt(var + LN_EPS) * g + b


def _route_rows(lg_t):
    g = [lg_t[i:i + 1] for i in range(N_EGROUPS)]
    gmax = functools.reduce(jnp.maximum, g)
    gsel = jnp.where(g[0] >= gmax, 0.0, jnp.where(g[1] >= gmax, 1.0, jnp.where(g[2] >= gmax, 2.0, 3.0)))
    den = functools.reduce(lambda a, b: a + b, [jnp.exp(gi - gmax) for gi in g])
    p_g = 1.0 / den
    el = lg_t[8:16]
    for gi in range(1, N_EGROUPS):
        el = jnp.where(gsel == float(gi), lg_t[8 + 8 * gi:16 + 8 * gi], el)
    eidx = lax.broadcasted_iota(I32, el.shape, 0).astype(F32)
    m1 = jnp.max(el, axis=0, keepdims=True)
    i1 = jnp.min(jnp.where(el == m1, eidx, 8.0), axis=0, keepdims=True)
    el2 = jnp.where(eidx == i1, -jnp.inf, el)
    m2 = jnp.max(el2, axis=0, keepdims=True)
    i2 = jnp.min(jnp.where(el2 == m2, eidx, 8.0), axis=0, keepdims=True)
    r = jnp.exp(m2 - m1)
    inv = 1.0 / (1.0 + r)
    gate1 = p_g * inv
    gate2 = p_g * r * inv
    first_lo = i1 < i2
    lo = jnp.minimum(i1, i2)
    hi = jnp.maximum(i1, i2)
    cls = gsel * 64.0 + lo * 8.0 + hi
    return cls, jnp.where(first_lo, gate1, gate2), jnp.where(first_lo, gate2, gate1)


def _norm_and_route(y, ln_g, ln_b, wr_hi, wr_lo, br, x1_ref, route_ref):
    tm, _ = y.shape
    x1 = _layer_norm(y, ln_g, ln_b)
    x1_ref[...] = x1
    x_hi = x1.astype(BF16)
    x_lo = (x1 - x_hi.astype(F32)).astype(BF16)
    lg = _dot(x_hi, wr_hi) + _dot(x_lo, wr_hi) + _dot(x_hi, wr_lo) + br
    cls, g_lo, g_hi = _route_rows(lg.T)
    row = lax.broadcasted_iota(I32, (ROUTE_ROWS, tm), 0)
    route_ref[...] = jnp.where(row == 0, cls, jnp.where(row == 1, g_lo, jnp.where(row == 2, g_hi, 0.0)))


def _mixer_out_specs(tm, d):
    return [pl.BlockSpec((tm, d), lambda i: (i, 0)), pl.BlockSpec((ROUTE_ROWS, tm), lambda i: (0, i))]


def _mixer_out_shape(t, d):
    return [jax.ShapeDtypeStruct((t, d), F32), jax.ShapeDtypeStruct((ROUTE_ROWS, t), F32)]


def _gmlp_kernel(x_ref, w_in_ref, b_in_ref, vn_g_ref, vn_b_ref, w_s_ref, b_st_ref, w_out_ref,
                 ln_g_ref, ln_b_ref, wr_hi_ref, wr_lo_ref, br_ref, x1_ref, route_ref, gated_ref, *, alpha):
    x = x_ref[...]
    tm, _ = x.shape
    groups, chunk, _ = w_s_ref.shape
    z = _dot(x.astype(BF16), w_in_ref[...]) + b_in_ref[...]
    z = 0.5 * z * (1.0 + lax.erf(z * (2.0 ** -0.5)))
    half = z.shape[1] // 2
    gd = half // groups
    u = z[:, :half]
    v = _layer_norm(z[:, half:], vn_g_ref[...], vn_b_ref[...]).astype(BF16)
    r = lax.broadcasted_iota(I32, (chunk, chunk), 0)
    c = lax.broadcasted_iota(I32, (chunk, chunk), 1)
    causal = r >= c
    for g in range(groups):
        w_c = jnp.where(causal, w_s_ref[g], 0.0).astype(BF16)
        bias = b_st_ref[:, g:g + 1]
        for ci in range(tm // chunk):
            rows = slice(ci * chunk, (ci + 1) * chunk)
            cols = slice(g * gd, (g + 1) * gd)
            s = _dot(w_c, v[rows, cols]) + bias
            gated_ref[rows, cols] = (u[rows, cols] * s).astype(BF16)
    mix = _dot(gated_ref[...], w_out_ref[...])
    _norm_and_route(alpha * x + mix, ln_g_ref[...], ln_b_ref[...], wr_hi_ref[...], wr_lo_ref[...],
                    br_ref[...], x1_ref, route_ref)


def _gmlp_layer(xt, w_in, b_in, vn_g, vn_b, w_s, b_s, w_out, ln_g, ln_b, router, alpha, tm):
    t, d = xt.shape
    half = w_out.shape[0]
    wr_hi, wr_lo, br = router
    args = (xt, w_in.astype(BF16), b_in[None], vn_g[None], vn_b[None], w_s, b_s.T, w_out.astype(BF16),
            ln_g[None], ln_b[None], wr_hi, wr_lo, br)
    in_specs = [pl.BlockSpec((tm, d), lambda i: (i, 0))] + [_full(a.shape) for a in args[1:]]
    return pl.pallas_call(
        functools.partial(_gmlp_kernel, alpha=alpha),
        grid=(t // tm,),
        in_specs=in_specs,
        out_specs=_mixer_out_specs(tm, d),
        out_shape=_mixer_out_shape(t, d),
        scratch_shapes=[pltpu.VMEM((tm, half), BF16)],
        compiler_params=_cparams(("parallel",)),
        name="gmlp",
    )(*args)


def _dsa_proj_kernel(x_ref, wq_ref, wc_ref, wqi_ref, wkw_ref, kvg_ref, wuk_t_ref,
                     qlat_ref, ckv_ref, ckv_t_ref, qidx_ref, kidx_ref, wi_t_ref, *, scale_q, scale_w):
    xb = x_ref[...].astype(BF16)
    nq = qlat_ref.shape[0]
    kc = ckv_t_ref.shape[2]
    heads, hd, _ = wuk_t_ref.shape
    q = _dot(xb, wq_ref[...])
    for h in range(heads):
        ql = (_dot(q[:, h * hd:(h + 1) * hd].astype(BF16), wuk_t_ref[h]) * scale_q).astype(BF16)
        for b in range(nq):
            qlat_ref[b, h * Q_BLOCK:(h + 1) * Q_BLOCK, :] = ql[b * Q_BLOCK:(b + 1) * Q_BLOCK]
    c = _dot(xb, wc_ref[...])
    ms = jnp.mean(c * c, axis=-1, keepdims=True)
    c = c * lax.rsqrt(ms + RMS_EPS) * kvg_ref[...]
    ckv_ref[...] = c.astype(BF16)
    for b in range(ckv_t_ref.shape[0]):
        ckv_t_ref[b] = c[b * kc:(b + 1) * kc].T.astype(BF16)
    qi = _dot(xb, wqi_ref[...]).astype(BF16)
    for h in range(IDX_HEADS):
        for b in range(nq):
            qidx_ref[b, h * Q_BLOCK:(h + 1) * Q_BLOCK, :] = qi[b * Q_BLOCK:(b + 1) * Q_BLOCK,
                                                               h * IDX_DIM:(h + 1) * IDX_DIM]
    kw = _dot(xb, wkw_ref[...])
    kidx_ref[...] = kw[:, :IDX_DIM].astype(BF16)
    wi_t_ref[...] = kw.T[IDX_DIM:IDX_DIM + IDX_HEADS] * scale_w


def _dsa_attn_kernel(qidx_ref, wi_t_ref, qlat_ref, kidx_ref, ckv_ref, ckv_t_ref, o_ref,
                     key_ref, key16_ref, acc_ref, m_ref, l_ref, a_ref, p_ref, *, k_sel, heads):
    qb = pl.program_id(1)
    kc = ckv_t_ref.shape[2]
    slab = key_ref.shape[1]
    cps = slab // kc
    nq = Q_BLOCK
    pair = 2 * nq
    n_pairs = heads // 2
    n_chunks = (qb * nq + nq + kc - 1) // kc
    n_slabs = (n_chunks + cps - 1) // cps

    def key_chunk(j):
        return key_ref.at[j // cps, pl.ds(pl.multiple_of((j % cps) * kc, kc), kc), :]

    q_pos = qb * nq + lax.broadcasted_iota(I32, (kc, nq), 1)
    k_off = lax.broadcasted_iota(I32, (kc, nq), 0)

    def key16_chunk(j):
        return key16_ref.at[j // cps, pl.ds(pl.multiple_of((j % cps) * kc, kc), kc), :]

    def pad_chunk(j, carry):
        key_chunk(j)[...] = jnp.full((kc, nq), jnp.nan, F32)
        key16_chunk(j)[...] = jnp.full((kc, nq), jnp.nan, BF16)
        return carry

    lax.fori_loop(n_chunks, n_slabs * cps, pad_chunk, 0)

    def score_chunk(j, carry):
        kch = kidx_ref[pl.ds(pl.multiple_of(j * kc, kc), kc), :]
        tot = None
        for p in range(IDX_HEADS // 2):
            sc = _dot_t(kch, qidx_ref[p * pair:(p + 1) * pair, :])
            for hh in range(2):
                h = 2 * p + hh
                r = jnp.maximum(sc[:, hh * nq:(hh + 1) * nq], 0.0) * wi_t_ref[h:h + 1, :]
                tot = r if tot is None else tot + r
        tot = jnp.where(k_off + j * kc <= q_pos, tot, jnp.nan)
        key_chunk(j)[...] = tot
        key16_chunk(j)[...] = tot.astype(BF16)
        return carry

    lax.fori_loop(0, n_chunks, score_chunk, 0)

    def count(t, strict=False):
        tb = jnp.broadcast_to(t, (COUNT_ACC, nq))

        def body(s, acc):
            for g in range(slab // COUNT_ACC):
                blk = key_ref[s, g * COUNT_ACC:(g + 1) * COUNT_ACC, :]
                acc = jnp.where(blk > tb if strict else blk >= tb, acc + 1.0, acc)
            return acc

        acc = lax.fori_loop(0, n_slabs, body, jnp.zeros((COUNT_ACC, nq), F32))
        return jnp.sum(acc, axis=0, keepdims=True)

    def count16(t):
        rows16 = 2 * COUNT_ACC
        tb = jnp.broadcast_to(t.astype(BF16), (rows16, nq))

        def body(s, acc):
            for g in range(slab // rows16):
                blk = key16_ref[s, g * rows16:(g + 1) * rows16, :]
                acc = jnp.where(blk >= tb, acc + 1.0, acc)
            return acc

        acc = lax.fori_loop(0, n_slabs, body, jnp.zeros((rows16, nq), BF16))
        return jnp.sum(acc.astype(F32), axis=0, keepdims=True)

    def decode(code):
        return lax.bitcast_convert_type(code ^ ((code >> 31) & 0x7FFFFFFF), F32)

    def decode16(c16):
        return decode(jnp.left_shift(c16, 16) | jnp.where(c16 < 0, 0xFFFF, 0))

    def bit16_step(b, c16):
        t = c16 + jnp.left_shift(jnp.int32(1), 15 - b)
        return jnp.where(count16(decode16(t)) >= float(k_sel), t, c16)

    no_code = -(2 ** 15)
    c16 = lax.fori_loop(0, 16, bit16_step, jnp.full((1, nq), no_code, I32))

    def bit_step(b, c):
        t = c + jnp.left_shift(jnp.int32(1), 17 - b)
        return jnp.where((count(decode(t)) >= float(k_sel)) & (t > c), t, c)

    c_sel = lax.fori_loop(0, 18, bit_step, jnp.left_shift(jnp.maximum(c16, no_code + 1) - 1, 16))
    c_sel = jnp.where(c16 == no_code, -jnp.inf, decode(c_sel))
    n_ge = jnp.where(c16 == no_code, 0.0, count(c_sel))

    has_tie = jnp.max(n_ge) > float(k_sel)

    @pl.when(has_tie)
    def _():
        need = float(k_sel) - count(c_sel, strict=True)
        tie_col = n_ge > float(k_sel)
        cb = jnp.broadcast_to(c_sel, (kc, nq))
        lower = (lax.broadcasted_iota(I32, (kc, kc), 1) < lax.broadcasted_iota(I32, (kc, kc), 0))
        lower = jnp.where(lower, 1.0, 0.0).astype(BF16)

        def fix(j, seen):
            blk = key_chunk(j)[...]
            eq = blk == cb
            eq_f = jnp.where(eq, 1.0, 0.0)
            rank = seen + _dot(lower, eq_f.astype(BF16))
            drop = eq & tie_col & (rank >= need)
            key_chunk(j)[...] = jnp.where(drop, jnp.nan, blk)
            return seen + jnp.sum(eq_f, axis=0, keepdims=True)

        lax.fori_loop(0, n_chunks, fix, jnp.zeros((1, nq), F32))

    thr = jnp.broadcast_to(c_sel, (kc, nq))
    m_ref[...] = jnp.full(m_ref.shape, NEG_BIG, F32)
    l_ref[...] = jnp.zeros(l_ref.shape, F32)
    acc_ref[...] = jnp.zeros(acc_ref.shape, F32)

    def softmax_chunk(j):
        ck = ckv_ref[pl.ds(pl.multiple_of(j * kc, kc), kc), :]
        bias = jnp.where(key_chunk(j)[...] >= thr, 0.0, NEG_BIG)
        bias = jnp.concatenate([bias, bias], axis=1)
        slot = j % 2
        for p in range(n_pairs):
            lg = _dot_t(ck, qlat_ref[p * pair:(p + 1) * pair, :]) + bias
            m_old = m_ref[p]
            m_new = jnp.maximum(m_old, jnp.max(lg, axis=0, keepdims=True))
            pr = jnp.exp2(lg - m_new)
            alpha = jnp.exp2(m_old - m_new)
            l_ref[p] = alpha * l_ref[p] + jnp.sum(pr, axis=0, keepdims=True)
            m_ref[p] = m_new
            a_ref[slot, p] = alpha
            p_ref[slot, p] = pr.astype(BF16)

    def accumulate_chunk(j):
        ck_t = ckv_t_ref[j]
        slot = j % 2
        for p in range(n_pairs):
            acc_ref[p] = a_ref[slot, p] * acc_ref[p] + _dot(ck_t, p_ref[slot, p])

    def att_step(j, carry):
        accumulate_chunk(j - 1)
        softmax_chunk(j)
        return carry

    softmax_chunk(0)
    lax.fori_loop(1, n_chunks, att_step, 0)
    accumulate_chunk(n_chunks - 1)
    for p in range(n_pairs):
        o_ref[p * pair:(p + 1) * pair, :] = (acc_ref[p] * (1.0 / l_ref[p])).T.astype(o_ref.dtype)


def _dsa_out_kernel(o_ref, x_ref, wuv_ref, wout_ref, ln_g_ref, ln_b_ref, wr_hi_ref, wr_lo_ref, br_ref,
                    x1_ref, route_ref, o2_ref, *, alpha):
    nq = o_ref.shape[0]
    heads, _, hd = wuv_ref.shape
    for b in range(nq):
        for h in range(heads):
            oh = o_ref[b, h * Q_BLOCK:(h + 1) * Q_BLOCK, :]
            o2_ref[b * Q_BLOCK:(b + 1) * Q_BLOCK, h * hd:(h + 1) * hd] = _dot(oh, wuv_ref[h]).astype(BF16)
    mix = _dot(o2_ref[...], wout_ref[...])
    _norm_and_route(alpha * x_ref[...] + mix, ln_g_ref[...], ln_b_ref[...], wr_hi_ref[...], wr_lo_ref[...],
                    br_ref[...], x1_ref, route_ref)


def _dsa_layer(xt, batch, w_in, kv_g, w_uk, w_uv, w_out, ln_g, ln_b, router, alpha, tm):
    t, d = xt.shape
    seq = t // batch
    heads, lat, hd = w_uk.shape
    qw = heads * hd
    nqb = t // Q_BLOCK
    nq = tm // Q_BLOCK
    k_sel = min(TOPK_MAX, seq // 4)
    kc = min(KEY_CHUNK, seq)
    slab = min(COUNT_SLAB, seq)
    w_in = w_in.astype(BF16)
    o1, o2, o3 = qw, qw + lat, qw + lat + IDX_HEADS * IDX_DIM
    wkw = jnp.pad(w_in[:, o3:], ((0, 0), (0, LANES - (w_in.shape[1] - o3))))
    proj_args = (xt, w_in[:, :o1], w_in[:, o1:o2], w_in[:, o2:o3], wkw, kv_g[None],
                 jnp.swapaxes(w_uk, 1, 2).astype(BF16))
    qlat, ckv, ckv_t, qidx, kidx, wi_t = pl.pallas_call(
        functools.partial(_dsa_proj_kernel, scale_q=hd ** -0.5 * LOG2E, scale_w=(IDX_HEADS * IDX_DIM) ** -0.5),
        grid=(t // tm,),
        in_specs=[pl.BlockSpec((tm, d), lambda i: (i, 0))] + [_full(a.shape) for a in proj_args[1:]],
        out_specs=[pl.BlockSpec((nq, heads * Q_BLOCK, lat), lambda i: (i, 0, 0)),
                   pl.BlockSpec((tm, lat), lambda i: (i, 0)),
                   pl.BlockSpec((tm // kc, lat, kc), lambda i: (i, 0, 0)),
                   pl.BlockSpec((nq, IDX_HEADS * Q_BLOCK, IDX_DIM), lambda i: (i, 0, 0)),
                   pl.BlockSpec((tm, IDX_DIM), lambda i: (i, 0)),
                   pl.BlockSpec((IDX_HEADS, tm), lambda i: (0, i))],
        out_shape=[jax.ShapeDtypeStruct((nqb, heads * Q_BLOCK, lat), BF16),
                   jax.ShapeDtypeStruct((t, lat), BF16),
                   jax.ShapeDtypeStruct((t // kc, lat, kc), BF16),
                   jax.ShapeDtypeStruct((nqb, IDX_HEADS * Q_BLOCK, IDX_DIM), BF16),
                   jax.ShapeDtypeStruct((t, IDX_DIM), BF16),
                   jax.ShapeDtypeStruct((IDX_HEADS, t), F32)],
        compiler_params=_cparams(("parallel",)),
        name="dsa_proj",
    )(*proj_args)

    nq_seq = seq // Q_BLOCK
    n_kc = seq // kc
    rows = heads * Q_BLOCK
    o = pl.pallas_call(
        functools.partial(_dsa_attn_kernel, k_sel=k_sel, heads=heads),
        grid=(batch, nq_seq),
        in_specs=[pl.BlockSpec((None, IDX_HEADS * Q_BLOCK, IDX_DIM), lambda b, q: (b * nq_seq + q, 0, 0)),
                  pl.BlockSpec((IDX_HEADS, Q_BLOCK), lambda b, q: (0, b * nq_seq + q)),
                  pl.BlockSpec((None, rows, lat), lambda b, q: (b * nq_seq + q, 0, 0)),
                  pl.BlockSpec((None, seq, IDX_DIM), lambda b, q: (b, 0, 0)),
                  pl.BlockSpec((None, seq, lat), lambda b, q: (b, 0, 0)),
                  pl.BlockSpec((None, n_kc, lat, kc), lambda b, q: (b, 0, 0, 0))],
        out_specs=pl.BlockSpec((None, rows, lat), lambda b, q: (b * nq_seq + q, 0, 0)),
        out_shape=jax.ShapeDtypeStruct((nqb, rows, lat), BF16),
        scratch_shapes=[pltpu.VMEM((seq // slab, slab, Q_BLOCK), F32),
                        pltpu.VMEM((seq // slab, slab, Q_BLOCK), BF16),
                        pltpu.VMEM((heads // 2, lat, 2 * Q_BLOCK), F32),
                        pltpu.VMEM((heads // 2, 1, 2 * Q_BLOCK), F32),
                        pltpu.VMEM((heads // 2, 1, 2 * Q_BLOCK), F32),
                        pltpu.VMEM((2, heads // 2, 1, 2 * Q_BLOCK), F32),
                        pltpu.VMEM((2, heads // 2, kc, 2 * Q_BLOCK), BF16)],
        compiler_params=_cparams(("parallel", "arbitrary")),
        name="dsa_attn",
    )(qidx, wi_t, qlat, kidx.reshape(batch, seq, IDX_DIM), ckv.reshape(batch, seq, lat),
      ckv_t.reshape(batch, n_kc, lat, kc))

    wr_hi, wr_lo, br = router
    out_args = (o, xt, w_uv.astype(BF16), w_out.astype(BF16), ln_g[None], ln_b[None], wr_hi, wr_lo, br)
    return pl.pallas_call(
        functools.partial(_dsa_out_kernel, alpha=alpha),
        grid=(t // tm,),
        in_specs=[pl.BlockSpec((nq, rows, lat), lambda i: (i, 0, 0)),
                  pl.BlockSpec((tm, d), lambda i: (i, 0))] + [_full(a.shape) for a in out_args[2:]],
        out_specs=_mixer_out_specs(tm, d),
        out_shape=_mixer_out_shape(t, d),
        scratch_shapes=[pltpu.VMEM((tm, qw), BF16)],
        compiler_params=_cparams(("parallel",)),
        name="dsa_out",
    )(*out_args)


DMA_UNROLL = 8


def _start_token_moves(n, copy_of):
    def body(g, carry):
        for u in range(DMA_UNROLL):
            copy_of(g * DMA_UNROLL + u).start(priority=u % 2)
        return carry
    lax.fori_loop(0, n // DMA_UNROLL, body, 0)


def _tok(ref, i, rows):
    return ref.at[pl.ds(i * rows, rows)]


def _rows_of(ref, j, n, rows):
    return ref.at[pl.ds(j, n, stride=rows), :]


class _TokenGather:
    def __init__(self, i, n, idx_hbm, src_hbm, idx_smem, buf, isem, rsem):
        self.i, self.n = i, n
        self.idx_hbm, self.src_hbm, self.idx_smem, self.buf, self.isem, self.rsem = (
            idx_hbm, src_hbm, idx_smem, buf, isem, rsem)
        self.tm = idx_smem.shape[2]
        self.rows = buf.shape[1] // self.tm
        self.slot = i % 2

    def _idx_copy(self, blk, s):
        return pltpu.make_async_copy(self.idx_hbm.at[blk], self.idx_smem.at[s], self.isem.at[s])

    def _token_copy(self, s, r):
        return pltpu.make_async_copy(_tok(self.src_hbm, self.idx_smem[s, 0, r], self.rows),
                                     _tok(self.buf.at[s], r, self.rows), self.rsem.at[s])

    def _wait_tokens(self, s):
        pltpu.make_async_copy(self.src_hbm.at[pl.ds(0, self.tm * self.rows)], self.buf.at[s], self.rsem.at[s]).wait()

    def arrive(self):
        @pl.when(self.i == 0)
        def _():
            first = self._idx_copy(0, 0)
            first.start()
            first.wait()
            _start_token_moves(self.tm, lambda r: self._token_copy(0, r))
            self._idx_copy(min(1, self.n - 1), 1).start()

        self._idx_copy(jnp.minimum(self.i + 1, self.n - 1), 1 - self.slot).wait()
        self._wait_tokens(self.slot)

    def prefetch(self):
        nslot = 1 - self.slot
        for r in range(self.tm):
            self._token_copy(nslot, r).start(priority=r % 2)
        self._idx_copy(jnp.minimum(self.i + 2, self.n - 1), self.slot).start()

    def finish(self):
        @pl.when(self.i == self.n - 1)
        def _():
            self._wait_tokens(1 - self.slot)
            self._idx_copy(self.n - 1, self.slot).wait()


def _plan_kernel(route_ref, x_ref, pos_ref, tinfo_ref, xs_ref,
                 hist_ref, off_ref, carry_ref, stage_ref, ppos_vmem, ppos_smem, fill_vmem, fill_smem, zero_ref,
                 ssem, psem, zsem, *, tile):
    ps = pl.program_id(0)
    b = pl.program_id(1)
    nb = pl.num_programs(1)
    n_cls, pb = hist_ref.shape
    kt = tinfo_ref.shape[1]
    cls = route_ref[0:1, :].astype(I32)
    cid = lax.broadcasted_iota(I32, (n_cls, pb), 0)
    onehot = jnp.where(cid == cls, 1.0, 0.0)

    def zero_fill(start):
        def piece(tok0, n_tok):
            cp = pltpu.make_async_copy(zero_ref.at[pl.ds(0, n_tok * X_ROWS)],
                                       xs_ref.at[pl.ds(tok0 * X_ROWS, n_tok * X_ROWS)], zsem)
            cp.start() if start else cp.wait()

        def per_class(c, carry):
            first, n = fill_smem[0, c], fill_smem[1, c]
            bit = zero_ref.shape[0] // X_ROWS
            while bit:
                @pl.when((n & bit) != 0)
                def _(bit=bit):
                    piece(first + (n & -(2 * bit)), bit)
                bit //= 2
            return carry

        lax.fori_loop(0, n_cls, per_class, 0)

        def per_tile(k, carry):
            piece(k * tile, tile)
            return carry

        lax.fori_loop(fill_smem[2, 0], xs_ref.shape[0] // (tile * X_ROWS), per_tile, 0)

    @pl.when(ps == 0)
    def _():
        @pl.when(b == 0)
        def _():
            hist_ref[...] = jnp.zeros(hist_ref.shape, F32)
            zero_ref[...] = jnp.zeros(zero_ref.shape, F32)
        hist_ref[...] += onehot

    @pl.when((ps == 1) & (b == 0))
    def _():
        counts = jnp.sum(hist_ref[...], axis=1, keepdims=True)
        tiles = jnp.floor((counts + float(tile - 1)) * (1.0 / tile))
        cid_col = lax.broadcasted_iota(I32, (n_cls, 1), 0)
        per_group = n_cls // N_EGROUPS
        for g in range(N_EGROUPS):
            g_tiles = jnp.sum(tiles[g * per_group:(g + 1) * per_group], axis=0, keepdims=True)
            extra = jnp.ceil(g_tiles * (1.0 / MOE_STEP_TILES)) * float(MOE_STEP_TILES) - g_tiles
            tiles = jnp.where(cid_col == (g + 1) * per_group - 1, extra, tiles)
        lower = lax.broadcasted_iota(I32, (n_cls, n_cls), 1) < lax.broadcasted_iota(I32, (n_cls, n_cls), 0)
        lower = jnp.where(lower, 1.0, 0.0).astype(BF16)
        off = _dot(lower, jnp.broadcast_to(tiles, (n_cls, LANES)).astype(BF16))[:, :1]
        off_ref[...] = off * float(tile)
        carry_ref[...] = jnp.zeros(carry_ref.shape, F32)
        k = lax.broadcasted_iota(I32, (n_cls, kt), 1).astype(F32)
        cid_f = lax.broadcasted_iota(I32, (n_cls, kt), 0).astype(F32)
        mine = (k >= off) & (k < off + tiles)
        tcls = jnp.sum(jnp.where(mine, cid_f, 0.0), axis=0, keepdims=True)
        n_used = jnp.sum(tiles, axis=0, keepdims=True)
        last = jnp.max(jnp.where(tiles > 0.0, cid_f[:, :1], 0.0), axis=0, keepdims=True)
        tcls = jnp.where(k[:1] >= n_used, last, tcls)
        row = lax.broadcasted_iota(I32, (ROUTE_ROWS, kt), 0)
        tinfo_ref[...] = jnp.where(row == 0, tcls, jnp.where(row == 1, n_used, 0.0)).astype(I32)
        as_row = lambda col: jnp.broadcast_to(col, (n_cls, LANES)).T[0:1]
        frow = lax.broadcasted_iota(I32, fill_vmem.shape, 0)
        fill_vmem[...] = jnp.where(frow == 0, as_row(off * float(tile) + counts),
                                   jnp.where(frow == 1, as_row(tiles * float(tile) - counts), n_used)).astype(I32)
        to_fill = pltpu.make_async_copy(fill_vmem, fill_smem, psem)
        to_fill.start()
        to_fill.wait()
        zero_fill(start=True)

    @pl.when(ps == 1)
    def _():
        upper = lax.broadcasted_iota(I32, (pb, pb), 0) < lax.broadcasted_iota(I32, (pb, pb), 1)
        upper = jnp.where(upper, 1.0, 0.0).astype(BF16)
        before = _dot(onehot.astype(BF16), upper)
        ppos = jnp.sum(onehot * (before + carry_ref[...] + off_ref[...]), axis=0, keepdims=True).astype(I32)
        carry_ref[...] += jnp.sum(onehot, axis=1, keepdims=True)
        pos_ref[...] = ppos
        ppos_vmem[...] = ppos
        to_smem = pltpu.make_async_copy(ppos_vmem, ppos_smem, psem)
        to_smem.start()

        slot = b % 2
        stage = stage_ref.at[slot]

        def drain(s):
            pltpu.make_async_copy(stage_ref.at[s], xs_ref.at[pl.ds(0, pb * X_ROWS)], ssem.at[s]).wait()

        @pl.when(b >= 2)
        def _():
            drain(slot)

        for j in range(Y_ROWS):
            _rows_of(stage, j, pb, X_ROWS)[...] = x_ref[:, j * LANES:(j + 1) * LANES]
        record = jnp.concatenate([route_ref[...], jnp.zeros((LANES - ROUTE_ROWS, pb), F32)], axis=0).T
        _rows_of(stage, Y_ROWS, pb, X_ROWS)[...] = record

        to_smem.wait()
        _start_token_moves(pb, lambda r: pltpu.make_async_copy(
            _tok(stage, r, X_ROWS), _tok(xs_ref, ppos_smem[0, r], X_ROWS), ssem.at[slot]))

        @pl.when(b == nb - 1)
        def _():
            drain(slot)

            @pl.when(nb > 1)
            def _():
                drain(1 - slot)

            zero_fill(start=False)


def _moe_kernel(tcls_ref, nused_ref, xs_ref, win_ref, wout_ref, ys_ref):
    i = pl.program_id(0)
    tm = xs_ref.shape[0] // (X_ROWS * MOE_STEP_TILES)
    ff = wout_ref.shape[1]

    @pl.when(i * MOE_STEP_TILES < nused_ref[0])
    def _():
        for jt in range(MOE_STEP_TILES):
            cls = tcls_ref[i * MOE_STEP_TILES + jt]
            x0, y0 = jt * tm * X_ROWS, jt * tm * Y_ROWS
            xb = jnp.concatenate([_rows_of(xs_ref, x0 + j, tm, X_ROWS)[...].astype(BF16) for j in range(Y_ROWS)],
                                 axis=1)
            record = _rows_of(xs_ref, x0 + Y_ROWS, tm, X_ROWS)[...]
            y = None
            for e, gate in (((cls >> 3) & 7, record[:, 1:2]), (cls & 7, record[:, 2:3])):
                h = _dot(xb, win_ref[e])
                a = h[:, :ff]
                act = (a * jax.nn.sigmoid(a) * h[:, ff:] * gate).astype(BF16)
                ye = _dot(act, wout_ref[e])
                y = ye if y is None else y + ye
            for j in range(Y_ROWS):
                _rows_of(ys_ref, y0 + j, tm, Y_ROWS)[...] = y[:, j * LANES:(j + 1) * LANES]

    @pl.when(i * MOE_STEP_TILES >= nused_ref[0])
    def _():
        ys_ref[...] = jnp.zeros(ys_ref.shape, F32)


def _post_kernel(pos_ref, ys_ref, x1_ref, p_ref, ln_g_ref, ln_b_ref, pw_ref, gw_ref, gb_ref, out_ref,
                 idx_smem, ybuf, isem, rsem, *, alpha, n_blocks):
    i = pl.program_id(0)
    tm = x1_ref.shape[0]
    gather = _TokenGather(i, n_blocks, pos_ref, ys_ref, idx_smem, ybuf, isem, rsem)
    gather.arrive()
    gather.prefetch()
    yb = ybuf.at[i % 2]
    ffn = jnp.concatenate([_rows_of(yb, j, tm, Y_ROWS)[...] for j in range(Y_ROWS)], axis=1)
    x2 = _layer_norm(alpha * x1_ref[...] + ffn, ln_g_ref[...], ln_b_ref[...])
    gate = jax.nn.sigmoid(_dot(x2.astype(BF16), gw_ref[...]) + gb_ref[...])
    out_ref[...] = x2 + _dot(p_ref[...].astype(BF16), pw_ref[...]) * gate
    gather.finish()


def _moe_and_post(x1, route, layer, p, e_w_in, e_w_out, ln_g, ln_b, ple_w, ple_gw, ple_gb, alpha):
    t, d = x1.shape
    assert d == LANES * Y_ROWS
    depth, n_exp, _, ff2 = e_w_in.shape
    epg = n_exp // N_EGROUPS
    pb = _tile(t, PLAN_BLOCK)
    nb = t // pb
    nt = t // MOE_TILE + N_EGROUPS * (epg * (epg - 1) // 2) + N_EGROUPS * (MOE_STEP_TILES - 1)
    nt = -(-nt // MOE_STEP_TILES) * MOE_STEP_TILES
    kt = -(-nt // LANES) * LANES
    n_sorted = nt * MOE_TILE

    pos, tinfo, xs = pl.pallas_call(
        functools.partial(_plan_kernel, tile=MOE_TILE),
        grid=(2, nb),
        in_specs=[pl.BlockSpec((ROUTE_ROWS, pb), lambda ps, b: (0, b)),
                  pl.BlockSpec((pb, d), lambda ps, b: (ps * b, 0))],
        out_specs=[pl.BlockSpec((None, 1, pb), lambda ps, b: (ps * b, 0, 0)),
                   pl.BlockSpec((ROUTE_ROWS, kt), lambda ps, b: (0, 0)),
                   pl.BlockSpec(memory_space=pl.ANY)],
        out_shape=[jax.ShapeDtypeStruct((nb, 1, pb), I32),
                   jax.ShapeDtypeStruct((ROUTE_ROWS, kt), I32),
                   jax.ShapeDtypeStruct((n_sorted * X_ROWS, LANES), F32)],
        scratch_shapes=[pltpu.VMEM((N_CLASS, pb), F32),
                        pltpu.VMEM((N_CLASS, 1), F32),
                        pltpu.VMEM((N_CLASS, 1), F32),
                        pltpu.VMEM((2, pb * X_ROWS, LANES), F32),
                        pltpu.VMEM((1, pb), I32),
                        pltpu.SMEM((1, pb), I32),
                        pltpu.VMEM((ROUTE_ROWS, N_CLASS), I32),
                        pltpu.SMEM((ROUTE_ROWS, N_CLASS), I32),
                        pltpu.VMEM((MOE_TILE * ZERO_TILES * X_ROWS, LANES), F32),
                        pltpu.SemaphoreType.DMA((2,)),
                        pltpu.SemaphoreType.DMA(()),
                        pltpu.SemaphoreType.DMA(())],
        compiler_params=_cparams(("arbitrary", "arbitrary")),
        name="plan",
    )(route, x1)

    w_in = e_w_in.reshape(depth, N_EGROUPS, epg, d, ff2)
    w_out = e_w_out.reshape(depth, N_EGROUPS, epg, ff2 // 2, d)
    step_rows = MOE_TILE * MOE_STEP_TILES

    def group(i, tile_cls):
        return tile_cls[i * MOE_STEP_TILES] >> 6
    ys = pl.pallas_call(
        _moe_kernel,
        grid_spec=pltpu.PrefetchScalarGridSpec(
            num_scalar_prefetch=2,
            grid=(nt // MOE_STEP_TILES,),
            in_specs=[pl.BlockSpec((step_rows * X_ROWS, LANES), lambda i, tc, nu: (i, 0)),
                      pl.BlockSpec((None, None, epg, d, ff2), lambda i, tc, nu: (layer, group(i, tc), 0, 0, 0),
                                   pipeline_mode=pl.Buffered(1)),
                      pl.BlockSpec((None, None, epg, ff2 // 2, d), lambda i, tc, nu: (layer, group(i, tc), 0, 0, 0),
                                   pipeline_mode=pl.Buffered(1))],
            out_specs=pl.BlockSpec((step_rows * Y_ROWS, LANES), lambda i, tc, nu: (i, 0))),
        out_shape=jax.ShapeDtypeStruct((n_sorted * Y_ROWS, LANES), F32),
        compiler_params=_cparams(("arbitrary",)),
        name="moe",
    )(tinfo[0, :nt], tinfo[1, :1], xs, w_in, w_out)

    post_args = (pos, ys, x1, p, ln_g[None], ln_b[None], ple_w.astype(BF16), ple_gw.astype(BF16), ple_gb[None])
    return pl.pallas_call(
        functools.partial(_post_kernel, alpha=alpha, n_blocks=nb),
        grid=(nb,),
        in_specs=[pl.BlockSpec(memory_space=pl.ANY),
                  pl.BlockSpec(memory_space=pl.ANY),
                  pl.BlockSpec((pb, d), lambda i: (i, 0)),
                  pl.BlockSpec((None, pb, p.shape[2]), lambda i: (layer, i, 0))]
                 + [_full(a.shape) for a in post_args[4:]],
        out_specs=pl.BlockSpec((pb, d), lambda i: (i, 0)),
        out_shape=jax.ShapeDtypeStruct((t, d), F32),
        scratch_shapes=[pltpu.SMEM((2, 1, pb), I32), pltpu.VMEM((2, pb * Y_ROWS, LANES), F32),
                        pltpu.SemaphoreType.DMA((2,)), pltpu.SemaphoreType.DMA((2,))],
        compiler_params=_cparams(("arbitrary",)),
        name="post",
    )(*post_args)


def _router_params(wg, bg, we, be):
    d = wg.shape[0]
    w = jnp.zeros((d, ROUTE_COLS), F32).at[:, :N_EGROUPS].set(wg).at[:, 8:8 + we.shape[1]].set(we)
    b = jnp.zeros((1, ROUTE_COLS), F32).at[0, :N_EGROUPS].set(bg).at[0, 8:8 + be.shape[0]].set(be)
    hi = w.astype(BF16)
    return hi, (w - hi.astype(F32)).astype(BF16), b


def _tile(t, want):
    while t % want:
        want //= 2
    return want


def kernel(x, p, a_w_in, a_b_in, a_vn_g, a_vn_b, a_w_s, a_b_s, a_w_out, b_w_in, b_kv_g, b_w_uk, b_w_uv, b_w_out,
           ln1_g, ln1_b, ln2_g, ln2_b, r_wg, r_bg, r_we, r_be, e_w_in, e_w_out, ple_w, ple_gw, ple_gb):
    batch, seq, d = x.shape
    t = batch * seq
    depth = p.shape[0]
    alpha = (2 * depth) ** 0.25
    chunk = a_w_s.shape[-1]
    xt = x.reshape(t, d)
    pt = p.reshape(depth, t, p.shape[-1])
    for i in range(depth):
        j = i // 2
        router = _router_params(r_wg[i], r_bg[i], r_we[i], r_be[i])
        if i % 2 == 0:
            x1, route = _gmlp_layer(xt, a_w_in[j], a_b_in[j], a_vn_g[j], a_vn_b[j], a_w_s[j], a_b_s[j],
                                    a_w_out[j], ln1_g[i], ln1_b[i], router, alpha, max(chunk, _tile(t, 512)))
        else:
            x1, route = _dsa_layer(xt, batch, b_w_in[j], b_kv_g[j], b_w_uk[j], b_w_uv[j], b_w_out[j],
                                   ln1_g[i], ln1_b[i], router, alpha, _tile(t, 512))
        xt = _moe_and_post(x1, route, i, pt, e_w_in, e_w_out, ln2_g[i], ln2_b[i],
                           ple_w[i], ple_gw[i], ple_gb[i], alpha)
    return xt.reshape(batch, seq, d)
```

```python
import functools

import jax
import jax.numpy as jnp
from jax import lax
from jax.experimental import pallas as pl
from jax.experimental.pallas import tpu as pltpu

F32, BF16, I32 = jnp.float32, jnp.bfloat16, jnp.int32

IDX_HEADS = 8
IDX_DIM = 64
TOPK_MAX = 256
N_EGROUPS = 4
LN_EPS = 1e-5
RMS_EPS = 1e-6

LANES = 128
Q_BLOCK = 128
KEY_CHUNK = 512
COUNT_SLAB = 1024
COUNT_ACC = 64
LOG2E = 1.4426950408889634
VMEM_LIMIT = 48 * 1024 * 1024

ROUTE_COLS = LANES
ROUTE_ROWS = 8
N_CLASS = 64 * N_EGROUPS
Y_ROWS = 8
X_ROWS = Y_ROWS + 1
PLAN_BLOCK = 1024
MOE_TILE = 128
MOE_STEP_TILES = 4
ZERO_TILES = 1 << ((MOE_STEP_TILES - 1).bit_length() - 1) if MOE_STEP_TILES > 1 else 1
INT_MIN = -(2 ** 31)
NEG_BIG = -1e30


def _cparams(sem, vmem=VMEM_LIMIT):
    return pltpu.CompilerParams(dimension_semantics=sem, vmem_limit_bytes=vmem)


def _full(shape):
    n = len(shape)
    return pl.BlockSpec(shape, lambda *_: (0,) * n)


def _dot(a, b):
    return jnp.dot(a, b, preferred_element_type=F32)


def _dot_t(a, b):
    return lax.dot_general(a, b, (((1,), (1,)), ((), ())), preferred_element_type=F32)


def _layer_norm(x, g, b):
    mu = jnp.mean(x, axis=-1, keepdims=True)
    xc = x - mu
    var = jnp.mean(xc * xc, axis=-1, keepdims=True)
    return xc * lax.rsqrt(var + LN_EPS) * g + b


def _route_rows(lg_t):
    g = [lg_t[i:i + 1] for i in range(N_EGROUPS)]
    gmax = functools.reduce(jnp.maximum, g)
    gsel = jnp.where(g[0] >= gmax, 0.0, jnp.where(g[1] >= gmax, 1.0, jnp.where(g[2] >= gmax, 2.0, 3.0)))
    den = functools.reduce(lambda a, b: a + b, [jnp.exp(gi - gmax) for gi in g])
    p_g = 1.0 / den
    el = lg_t[8:16]
    for gi in range(1, N_EGROUPS):
        el = jnp.where(gsel == float(gi), lg_t[8 + 8 * gi:16 + 8 * gi], el)
    eidx = lax.broadcasted_iota(I32, el.shape, 0).astype(F32)
    m1 = jnp.max(el, axis=0, keepdims=True)
    i1 = jnp.min(jnp.where(el == m1, eidx, 8.0), axis=0, keepdims=True)
    el2 = jnp.where(eidx == i1, -jnp.inf, el)
    m2 = jnp.max(el2, axis=0, keepdims=True)
    i2 = jnp.min(jnp.where(el2 == m2, eidx, 8.0), axis=0, keepdims=True)
    r = jnp.exp(m2 - m1)
    inv = 1.0 / (1.0 + r)
    gate1 = p_g * inv
    gate2 = p_g * r * inv
    first_lo = i1 < i2
    lo = jnp.minimum(i1, i2)
    hi = jnp.maximum(i1, i2)
    cls = gsel * 64.0 + lo * 8.0 + hi
    return cls, jnp.where(first_lo, gate1, gate2), jnp.where(first_lo, gate2, gate1)


def _norm_and_route(y, ln_g, ln_b, wr_hi, wr_lo, br, x1_ref, route_ref):
    tm, _ = y.shape
    x1 = _layer_norm(y, ln_g, ln_b)
    x1_ref[...] = x1
    x_hi = x1.astype(BF16)
    x_lo = (x1 - x_hi.astype(F32)).astype(BF16)
    lg = _dot(x_hi, wr_hi) + _dot(x_lo, wr_hi) + _dot(x_hi, wr_lo) + br
    cls, g_lo, g_hi = _route_rows(lg.T)
    row = lax.broadcasted_iota(I32, (ROUTE_ROWS, tm), 0)
    route_ref[...] = jnp.where(row == 0, cls, jnp.where(row == 1, g_lo, jnp.where(row == 2, g_hi, 0.0)))


def _mixer_out_specs(tm, d):
    return [pl.BlockSpec((tm, d), lambda i: (i, 0)), pl.BlockSpec((ROUTE_ROWS, tm), lambda i: (0, i))]


def _mixer_out_shape(t, d):
    return [jax.ShapeDtypeStruct((t, d), F32), jax.ShapeDtypeStruct((ROUTE_ROWS, t), F32)]


def _gmlp_kernel(x_ref, w_in_ref, b_in_ref, vn_g_ref, vn_b_ref, w_s_ref, b_st_ref, w_out_ref,
                 ln_g_ref, ln_b_ref, wr_hi_ref, wr_lo_ref, br_ref, x1_ref, route_ref, gated_ref, *, alpha):
    x = x_ref[...]
    tm, _ = x.shape
    groups, chunk, _ = w_s_ref.shape
    z = _dot(x.astype(BF16), w_in_ref[...]) + b_in_ref[...]
    z = 0.5 * z * (1.0 + lax.erf(z * (2.0 ** -0.5)))
    half = z.shape[1] // 2
    gd = half // groups
    u = z[:, :half]
    v = _layer_norm(z[:, half:], vn_g_ref[...], vn_b_ref[...]).astype(BF16)
    r = lax.broadcasted_iota(I32, (chunk, chunk), 0)
    c = lax.broadcasted_iota(I32, (chunk, chunk), 1)
    causal = r >= c
    for g in range(groups):
        w_c = jnp.where(causal, w_s_ref[g], 0.0).astype(BF16)
        bias = b_st_ref[:, g:g + 1]
        for ci in range(tm // chunk):
            rows = slice(ci * chunk, (ci + 1) * chunk)
            cols = slice(g * gd, (g + 1) * gd)
            s = _dot(w_c, v[rows, cols]) + bias
            gated_ref[rows, cols] = (u[rows, cols] * s).astype(BF16)
    mix = _dot(gated_ref[...], w_out_ref[...])
    _norm_and_route(alpha * x + mix, ln_g_ref[...], ln_b_ref[...], wr_hi_ref[...], wr_lo_ref[...],
                    br_ref[...], x1_ref, route_ref)


def _gmlp_layer(xt, w_in, b_in, vn_g, vn_b, w_s, b_s, w_out, ln_g, ln_b, router, alpha, tm):
    t, d = xt.shape
    half = w_out.shape[0]
    wr_hi, wr_lo, br = router
    args = (xt, w_in.astype(BF16), b_in[None], vn_g[None], vn_b[None], w_s, b_s.T, w_out.astype(BF16),
            ln_g[None], ln_b[None], wr_hi, wr_lo, br)
    in_specs = [pl.BlockSpec((tm, d), lambda i: (i, 0))] + [_full(a.shape) for a in args[1:]]
    return pl.pallas_call(
        functools.partial(_gmlp_kernel, alpha=alpha),
        grid=(t // tm,),
        in_specs=in_specs,
        out_specs=_mixer_out_specs(tm, d),
        out_shape=_mixer_out_shape(t, d),
        scratch_shapes=[pltpu.VMEM((tm, half), BF16)],
        compiler_params=_cparams(("parallel",)),
        name="gmlp",
    )(*args)


def _dsa_proj_kernel(x_ref, wq_ref, wc_ref, wqi_ref, wkw_ref, kvg_ref, wuk_t_ref,
                     qlat_ref, ckv_ref, ckv_t_ref, qidx_ref, kidx_ref, wi_t_ref, *, scale_q, scale_w):
    xb = x_ref[...].astype(BF16)
    nq = qlat_ref.shape[0]
    kc = ckv_t_ref.shape[2]
    heads, hd, _ = wuk_t_ref.shape
    q = _dot(xb, wq_ref[...])
    for h in range(heads):
        ql = (_dot(q[:, h * hd:(h + 1) * hd].astype(BF16), wuk_t_ref[h]) * scale_q).astype(BF16)
        for b in range(nq):
            qlat_ref[b, h * Q_BLOCK:(h + 1) * Q_BLOCK, :] = ql[b * Q_BLOCK:(b + 1) * Q_BLOCK]
    c = _dot(xb, wc_ref[...])
    ms = jnp.mean(c * c, axis=-1, keepdims=True)
    c = c * lax.rsqrt(ms + RMS_EPS) * kvg_ref[...]
    ckv_ref[...] = c.astype(BF16)
    for b in range(ckv_t_ref.shape[0]):
        ckv_t_ref[b] = c[b * kc:(b + 1) * kc].T.astype(BF16)
    qi = _dot(xb, wqi_ref[...]).astype(BF16)
    for h in range(IDX_HEADS):
        for b in range(nq):
            qidx_ref[b, h * Q_BLOCK:(h + 1) * Q_BLOCK, :] = qi[b * Q_BLOCK:(b + 1) * Q_BLOCK,
                                                               h * IDX_DIM:(h + 1) * IDX_DIM]
    kw = _dot(xb, wkw_ref[...])
    kidx_ref[...] = kw[:, :IDX_DIM].astype(BF16)
    wi_t_ref[...] = kw.T[IDX_DIM:IDX_DIM + IDX_HEADS] * scale_w


def _dsa_attn_kernel(qidx_ref, wi_t_ref, qlat_ref, kidx_ref, ckv_ref, ckv_t_ref, o_ref,
                     key_ref, key16_ref, acc_ref, m_ref, l_ref, a_ref, p_ref, *, k_sel, heads):
    qb = pl.program_id(1)
    kc = ckv_t_ref.shape[2]
    slab = key_ref.shape[1]
    cps = slab // kc
    nq = Q_BLOCK
    pair = 2 * nq
    n_pairs = heads // 2
    n_chunks = (qb * nq + nq + kc - 1) // kc
    n_slabs = (n_chunks + cps - 1) // cps

    def key_chunk(j):
        return key_ref.at[j // cps, pl.ds(pl.multiple_of((j % cps) * kc, kc), kc), :]

    q_pos = qb * nq + lax.broadcasted_iota(I32, (kc, nq), 1)
    k_off = lax.broadcasted_iota(I32, (kc, nq), 0)

    def key16_chunk(j):
        return key16_ref.at[j // cps, pl.ds(pl.multiple_of((j % cps) * kc, kc), kc), :]

    def pad_chunk(j, carry):
        key_chunk(j)[...] = jnp.full((kc, nq), jnp.nan, F32)
        key16_chunk(j)[...] = jnp.full((kc, nq), jnp.nan, BF16)
        return carry

    lax.fori_loop(n_chunks, n_slabs * cps, pad_chunk, 0)

    def score_chunk(j, carry):
        kch = kidx_ref[pl.ds(pl.multiple_of(j * kc, kc), kc), :]
        tot = None
        for p in range(IDX_HEADS // 2):
            sc = _dot_t(kch, qidx_ref[p * pair:(p + 1) * pair, :])
            for hh in range(2):
                h = 2 * p + hh
                r = jnp.maximum(sc[:, hh * nq:(hh + 1) * nq], 0.0) * wi_t_ref[h:h + 1, :]
                tot = r if tot is None else tot + r
        tot = jnp.where(k_off + j * kc <= q_pos, tot, jnp.nan)
        key_chunk(j)[...] = tot
        key16_chunk(j)[...] = tot.astype(BF16)
        return carry

    lax.fori_loop(0, n_chunks, score_chunk, 0)

    def count(t, strict=False):
        tb = jnp.broadcast_to(t, (COUNT_ACC, nq))

        def body(s, acc):
            for g in range(slab // COUNT_ACC):
                blk = key_ref[s, g * COUNT_ACC:(g + 1) * COUNT_ACC, :]
                acc = jnp.where(blk > tb if strict else blk >= tb, acc + 1.0, acc)
            return acc

        acc = lax.fori_loop(0, n_slabs, body, jnp.zeros((COUNT_ACC, nq), F32))
        return jnp.sum(acc, axis=0, keepdims=True)

    def count16(t):
        rows16 = 2 * COUNT_ACC
        tb = jnp.broadcast_to(t.astype(BF16), (rows16, nq))

        def body(s, acc):
            for g in range(slab // rows16):
                blk = key16_ref[s, g * rows16:(g + 1) * rows16, :]
                acc = jnp.where(blk >= tb, acc + 1.0, acc)
            return acc

        acc = lax.fori_loop(0, n_slabs, body, jnp.zeros((rows16, nq), BF16))
        return jnp.sum(acc.astype(F32), axis=0, keepdims=True)

    def decode(code):
        return lax.bitcast_convert_type(code ^ ((code >> 31) & 0x7FFFFFFF), F32)

    def decode16(c16):
        return decode(jnp.left_shift(c16, 16) | jnp.where(c16 < 0, 0xFFFF, 0))

    def bit16_step(b, c16):
        t = c16 + jnp.left_shift(jnp.int32(1), 15 - b)
        return jnp.where(count16(decode16(t)) >= float(k_sel), t, c16)

    no_code = -(2 ** 15)
    c16 = lax.fori_loop(0, 16, bit16_step, jnp.full((1, nq), no_code, I32))

    def bit_step(b, c):
        t = c + jnp.left_shift(jnp.int32(1), 17 - b)
        return jnp.where((count(decode(t)) >= float(k_sel)) & (t > c), t, c)

    c_sel = lax.fori_loop(0, 18, bit_step, jnp.left_shift(jnp.maximum(c16, no_code + 1) - 1, 16))
    c_sel = jnp.where(c16 == no_code, -jnp.inf, decode(c_sel))
    n_ge = jnp.where(c16 == no_code, 0.0, count(c_sel))

    has_tie = jnp.max(n_ge) > float(k_sel)

    @pl.when(has_tie)
    def _():
        need = float(k_sel) - count(c_sel, strict=True)
        tie_col = n_ge > float(k_sel)
        cb = jnp.broadcast_to(c_sel, (kc, nq))
        lower = (lax.broadcasted_iota(I32, (kc, kc), 1) < lax.broadcasted_iota(I32, (kc, kc), 0))
        lower = jnp.where(lower, 1.0, 0.0).astype(BF16)

        def fix(j, seen):
            blk = key_chunk(j)[...]
            eq = blk == cb
            eq_f = jnp.where(eq, 1.0, 0.0)
            rank = seen + _dot(lower, eq_f.astype(BF16))
            drop = eq & tie_col & (rank >= need)
            key_chunk(j)[...] = jnp.where(drop, jnp.nan, blk)
            return seen + jnp.sum(eq_f, axis=0, keepdims=True)

        lax.fori_loop(0, n_chunks, fix, jnp.zeros((1, nq), F32))

    thr = jnp.broadcast_to(c_sel, (kc, nq))
    m_ref[...] = jnp.full(m_ref.shape, NEG_BIG, F32)
    l_ref[...] = jnp.zeros(l_ref.shape, F32)
    acc_ref[...] = jnp.zeros(acc_ref.shape, F32)

    def softmax_chunk(j):
        ck = ckv_ref[pl.ds(pl.multiple_of(j * kc, kc), kc), :]
        bias = jnp.where(key_chunk(j)[...] >= thr, 0.0, NEG_BIG)
        bias = jnp.concatenate([bias, bias], axis=1)
        slot = j % 2
        for p in range(n_pairs):
            lg = _dot_t(ck, qlat_ref[p * pair:(p + 1) * pair, :]) + bias
            m_old = m_ref[p]
            m_new = jnp.maximum(m_old, jnp.max(lg, axis=0, keepdims=True))
            pr = jnp.exp2(lg - m_new)
            alpha = jnp.exp2(m_old - m_new)
            l_ref[p] = alpha * l_ref[p] + jnp.sum(pr, axis=0, keepdims=True)
            m_ref[p] = m_new
            a_ref[slot, p] = alpha
            p_ref[slot, p] = pr.astype(BF16)

    def accumulate_chunk(j):
        ck_t = ckv_t_ref[j]
        slot = j % 2
        for p in range(n_pairs):
            acc_ref[p] = a_ref[slot, p] * acc_ref[p] + _dot(ck_t, p_ref[slot, p])

    def att_step(j, carry):
        accumulate_chunk(j - 1)
        softmax_chunk(j)
        return carry

    softmax_chunk(0)
    lax.fori_loop(1, n_chunks, att_step, 0)
    accumulate_chunk(n_chunks - 1)
    for p in range(n_pairs):
        o_ref[p * pair:(p + 1) * pair, :] = (acc_ref[p] * (1.0 / l_ref[p])).T.astype(o_ref.dtype)


def _dsa_out_kernel(o_ref, x_ref, wuv_ref, wout_ref, ln_g_ref, ln_b_ref, wr_hi_ref, wr_lo_ref, br_ref,
                    x1_ref, route_ref, o2_ref, *, alpha):
    nq = o_ref.shape[0]
    heads, _, hd = wuv_ref.shape
    for b in range(nq):
        for h in range(heads):
            oh = o_ref[b, h * Q_BLOCK:(h + 1) * Q_BLOCK, :]
            o2_ref[b * Q_BLOCK:(b + 1) * Q_BLOCK, h * hd:(h + 1) * hd] = _dot(oh, wuv_ref[h]).astype(BF16)
    mix = _dot(o2_ref[...], wout_ref[...])
    _norm_and_route(alpha * x_ref[...] + mix, ln_g_ref[...], ln_b_ref[...], wr_hi_ref[...], wr_lo_ref[...],
                    br_ref[...], x1_ref, route_ref)


def _dsa_layer(xt, batch, w_in, kv_g, w_uk, w_uv, w_out, ln_g, ln_b, router, alpha, tm):
    t, d = xt.shape
    seq = t // batch
    heads, lat, hd = w_uk.shape
    qw = heads * hd
    nqb = t // Q_BLOCK
    nq = tm // Q_BLOCK
    k_sel = min(TOPK_MAX, seq // 4)
    kc = min(KEY_CHUNK, seq)
    slab = min(COUNT_SLAB, seq)
    w_in = w_in.astype(BF16)
    o1, o2, o3 = qw, qw + lat, qw + lat + IDX_HEADS * IDX_DIM
    wkw = jnp.pad(w_in[:, o3:], ((0, 0), (0, LANES - (w_in.shape[1] - o3))))
    proj_args = (xt, w_in[:, :o1], w_in[:, o1:o2], w_in[:, o2:o3], wkw, kv_g[None],
                 jnp.swapaxes(w_uk, 1, 2).astype(BF16))
    qlat, ckv, ckv_t, qidx, kidx, wi_t = pl.pallas_call(
        functools.partial(_dsa_proj_kernel, scale_q=hd ** -0.5 * LOG2E, scale_w=(IDX_HEADS * IDX_DIM) ** -0.5),
        grid=(t // tm,),
        in_specs=[pl.BlockSpec((tm, d), lambda i: (i, 0))] + [_full(a.shape) for a in proj_args[1:]],
        out_specs=[pl.BlockSpec((nq, heads * Q_BLOCK, lat), lambda i: (i, 0, 0)),
                   pl.BlockSpec((tm, lat), lambda i: (i, 0)),
                   pl.BlockSpec((tm // kc, lat, kc), lambda i: (i, 0, 0)),
                   pl.BlockSpec((nq, IDX_HEADS * Q_BLOCK, IDX_DIM), lambda i: (i, 0, 0)),
                   pl.BlockSpec((tm, IDX_DIM), lambda i: (i, 0)),
                   pl.BlockSpec((IDX_HEADS, tm), lambda i: (0, i))],
        out_shape=[jax.ShapeDtypeStruct((nqb, heads * Q_BLOCK, lat), BF16),
                   jax.ShapeDtypeStruct((t, lat), BF16),
                   jax.ShapeDtypeStruct((t // kc, lat, kc), BF16),
                   jax.ShapeDtypeStruct((nqb, IDX_HEADS * Q_BLOCK, IDX_DIM), BF16),
                   jax.ShapeDtypeStruct((t, IDX_DIM), BF16),
                   jax.ShapeDtypeStruct((IDX_HEADS, t), F32)],
        compiler_params=_cparams(("parallel",)),
        name="dsa_proj",
    )(*proj_args)

    nq_seq = seq // Q_BLOCK
    n_kc = seq // kc
    rows = heads * Q_BLOCK
    o = pl.pallas_call(
        functools.partial(_dsa_attn_kernel, k_sel=k_sel, heads=heads),
        grid=(batch, nq_seq),
        in_specs=[pl.BlockSpec((None, IDX_HEADS * Q_BLOCK, IDX_DIM), lambda b, q: (b * nq_seq + q, 0, 0)),
                  pl.BlockSpec((IDX_HEADS, Q_BLOCK), lambda b, q: (0, b * nq_seq + q)),
                  pl.BlockSpec((None, rows, lat), lambda b, q: (b * nq_seq + q, 0, 0)),
                  pl.BlockSpec((None, seq, IDX_DIM), lambda b, q: (b, 0, 0)),
                  pl.BlockSpec((None, seq, lat), lambda b, q: (b, 0, 0)),
                  pl.BlockSpec((None, n_kc, lat, kc), lambda b, q: (b, 0, 0, 0))],
        out_specs=pl.BlockSpec((None, rows, lat), lambda b, q: (b * nq_seq + q, 0, 0)),
        out_shape=jax.ShapeDtypeStruct((nqb, rows, lat), BF16),
        scratch_shapes=[pltpu.VMEM((seq // slab, slab, Q_BLOCK), F32),
                        pltpu.VMEM((seq // slab, slab, Q_BLOCK), BF16),
                        pltpu.VMEM((heads // 2, lat, 2 * Q_BLOCK), F32),
                        pltpu.VMEM((heads // 2, 1, 2 * Q_BLOCK), F32),
                        pltpu.VMEM((heads // 2, 1, 2 * Q_BLOCK), F32),
                        pltpu.VMEM((2, heads // 2, 1, 2 * Q_BLOCK), F32),
                        pltpu.VMEM((2, heads // 2, kc, 2 * Q_BLOCK), BF16)],
        compiler_params=_cparams(("parallel", "arbitrary")),
        name="dsa_attn",
    )(qidx, wi_t, qlat, kidx.reshape(batch, seq, IDX_DIM), ckv.reshape(batch, seq, lat),
      ckv_t.reshape(batch, n_kc, lat, kc))

    wr_hi, wr_lo, br = router
    out_args = (o, xt, w_uv.astype(BF16), w_out.astype(BF16), ln_g[None], ln_b[None], wr_hi, wr_lo, br)
    return pl.pallas_call(
        functools.partial(_dsa_out_kernel, alpha=alpha),
        grid=(t // tm,),
        in_specs=[pl.BlockSpec((nq, rows, lat), lambda i: (i, 0, 0)),
                  pl.BlockSpec((tm, d), lambda i: (i, 0))] + [_full(a.shape) for a in out_args[2:]],
        out_specs=_mixer_out_specs(tm, d),
        out_shape=_mixer_out_shape(t, d),
        scratch_shapes=[pltpu.VMEM((tm, qw), BF16)],
        compiler_params=_cparams(("parallel",)),
        name="dsa_out",
    )(*out_args)


DMA_UNROLL = 8


def _start_token_moves(n, copy_of):
    def body(g, carry):
        for u in range(DMA_UNROLL):
            copy_of(g * DMA_UNROLL + u).start(priority=u % 2)
        return carry
    lax.fori_loop(0, n // DMA_UNROLL, body, 0)


def _tok(ref, i, rows):
    return ref.at[pl.ds(i * rows, rows)]


def _rows_of(ref, j, n, rows):
    return ref.at[pl.ds(j, n, stride=rows), :]


class _TokenGather:
    def __init__(self, i, n, idx_hbm, src_hbm, idx_smem, buf, isem, rsem):
        self.i, self.n = i, n
        self.idx_hbm, self.src_hbm, self.idx_smem, self.buf, self.isem, self.rsem = (
            idx_hbm, src_hbm, idx_smem, buf, isem, rsem)
        self.tm = idx_smem.shape[2]
        self.rows = buf.shape[1] // self.tm
        self.slot = i % 2

    def _idx_copy(self, blk, s):
        return pltpu.make_async_copy(self.idx_hbm.at[blk], self.idx_smem.at[s], self.isem.at[s])

    def _token_copy(self, s, r):
        return pltpu.make_async_copy(_tok(self.src_hbm, self.idx_smem[s, 0, r], self.rows),
                                     _tok(self.buf.at[s], r, self.rows), self.rsem.at[s])

    def _wait_tokens(self, s):
        pltpu.make_async_copy(self.src_hbm.at[pl.ds(0, self.tm * self.rows)], self.buf.at[s], self.rsem.at[s]).wait()

    def arrive(self):
        @pl.when(self.i == 0)
        def _():
            first = self._idx_copy(0, 0)
            first.start()
            first.wait()
            _start_token_moves(self.tm, lambda r: self._token_copy(0, r))
            self._idx_copy(min(1, self.n - 1), 1).start()

        self._idx_copy(jnp.minimum(self.i + 1, self.n - 1), 1 - self.slot).wait()
        self._wait_tokens(self.slot)

    def prefetch(self):
        nslot = 1 - self.slot
        for r in range(self.tm):
            self._token_copy(nslot, r).start(priority=r % 2)
        self._idx_copy(jnp.minimum(self.i + 2, self.n - 1), self.slot).start()

    def finish(self):
        @pl.when(self.i == self.n - 1)
        def _():
            self._wait_tokens(1 - self.slot)
            self._idx_copy(self.n - 1, self.slot).wait()


def _plan_kernel(route_ref, x_ref, pos_ref, tinfo_ref, xs_ref,
                 hist_ref, off_ref, carry_ref, stage_ref, ppos_vmem, ppos_smem, fill_vmem, fill_smem, zero_ref,
                 ssem, psem, zsem, *, tile):
    ps = pl.program_id(0)
    b = pl.program_id(1)
    nb = pl.num_programs(1)
    n_cls, pb = hist_ref.shape
    kt = tinfo_ref.shape[1]
    cls = route_ref[0:1, :].astype(I32)
    cid = lax.broadcasted_iota(I32, (n_cls, pb), 0)
    onehot = jnp.where(cid == cls, 1.0, 0.0)

    def zero_fill(start):
        def piece(tok0, n_tok):
            cp = pltpu.make_async_copy(zero_ref.at[pl.ds(0, n_tok * X_ROWS)],
                                       xs_ref.at[pl.ds(tok0 * X_ROWS, n_tok * X_ROWS)], zsem)
            cp.start() if start else cp.wait()

        def per_class(c, carry):
            first, n = fill_smem[0, c], fill_smem[1, c]
            bit = zero_ref.shape[0] // X_ROWS
            while bit:
                @pl.when((n & bit) != 0)
                def _(bit=bit):
                    piece(first + (n & -(2 * bit)), bit)
                bit //= 2
            return carry

        lax.fori_loop(0, n_cls, per_class, 0)

        def per_tile(k, carry):
            piece(k * tile, tile)
            return carry

        lax.fori_loop(fill_smem[2, 0], xs_ref.shape[0] // (tile * X_ROWS), per_tile, 0)

    @pl.when(ps == 0)
    def _():
        @pl.when(b == 0)
        def _():
            hist_ref[...] = jnp.zeros(hist_ref.shape, F32)
            zero_ref[...] = jnp.zeros(zero_ref.shape, F32)
        hist_ref[...] += onehot

    @pl.when((ps == 1) & (b == 0))
    def _():
        counts = jnp.sum(hist_ref[...], axis=1, keepdims=True)
        tiles = jnp.floor((counts + float(tile - 1)) * (1.0 / tile))
        cid_col = lax.broadcasted_iota(I32, (n_cls, 1), 0)
        per_group = n_cls // N_EGROUPS
        for g in range(N_EGROUPS):
            g_tiles = jnp.sum(tiles[g * per_group:(g + 1) * per_group], axis=0, keepdims=True)
            extra = jnp.ceil(g_tiles * (1.0 / MOE_STEP_TILES)) * float(MOE_STEP_TILES) - g_tiles
            tiles = jnp.where(cid_col == (g + 1) * per_group - 1, extra, tiles)
        lower = lax.broadcasted_iota(I32, (n_cls, n_cls), 1) < lax.broadcasted_iota(I32, (n_cls, n_cls), 0)
        lower = jnp.where(lower, 1.0, 0.0).astype(BF16)
        off = _dot(lower, jnp.broadcast_to(tiles, (n_cls, LANES)).astype(BF16))[:, :1]
        off_ref[...] = off * float(tile)
        carry_ref[...] = jnp.zeros(carry_ref.shape, F32)
        k = lax.broadcasted_iota(I32, (n_cls, kt), 1).astype(F32)
        cid_f = lax.broadcasted_iota(I32, (n_cls, kt), 0).astype(F32)
        mine = (k >= off) & (k < off + tiles)
        tcls = jnp.sum(jnp.where(mine, cid_f, 0.0), axis=0, keepdims=True)
        n_used = jnp.sum(tiles, axis=0, keepdims=True)
        last = jnp.max(jnp.where(tiles > 0.0, cid_f[:, :1], 0.0), axis=0, keepdims=True)
        tcls = jnp.where(k[:1] >= n_used, last, tcls)
        row = lax.broadcasted_iota(I32, (ROUTE_ROWS, kt), 0)
        tinfo_ref[...] = jnp.where(row == 0, tcls, jnp.where(row == 1, n_used, 0.0)).astype(I32)
        as_row = lambda col: jnp.broadcast_to(col, (n_cls, LANES)).T[0:1]
        frow = lax.broadcasted_iota(I32, fill_vmem.shape, 0)
        fill_vmem[...] = jnp.where(frow == 0, as_row(off * float(tile) + counts),
                                   jnp.where(frow == 1, as_row(tiles * float(tile) - counts), n_used)).astype(I32)
        to_fill = pltpu.make_async_copy(fill_vmem, fill_smem, psem)
        to_fill.start()
        to_fill.wait()
        zero_fill(start=True)

    @pl.when(ps == 1)
    def _():
        upper = lax.broadcasted_iota(I32, (pb, pb), 0) < lax.broadcasted_iota(I32, (pb, pb), 1)
        upper = jnp.where(upper, 1.0, 0.0).astype(BF16)
        before = _dot(onehot.astype(BF16), upper)
        ppos = jnp.sum(onehot * (before + carry_ref[...] + off_ref[...]), axis=0, keepdims=True).astype(I32)
        carry_ref[...] += jnp.sum(onehot, axis=1, keepdims=True)
        pos_ref[...] = ppos
        ppos_vmem[...] = ppos
        to_smem = pltpu.make_async_copy(ppos_vmem, ppos_smem, psem)
        to_smem.start()

        slot = b % 2
        stage = stage_ref.at[slot]

        def drain(s):
            pltpu.make_async_copy(stage_ref.at[s], xs_ref.at[pl.ds(0, pb * X_ROWS)], ssem.at[s]).wait()

        @pl.when(b >= 2)
        def _():
            drain(slot)

        for j in range(Y_ROWS):
            _rows_of(stage, j, pb, X_ROWS)[...] = x_ref[:, j * LANES:(j + 1) * LANES]
        record = jnp.concatenate([route_ref[...], jnp.zeros((LANES - ROUTE_ROWS, pb), F32)], axis=0).T
        _rows_of(stage, Y_ROWS, pb, X_ROWS)[...] = record

        to_smem.wait()
        _start_token_moves(pb, lambda r: pltpu.make_async_copy(
            _tok(stage, r, X_ROWS), _tok(xs_ref, ppos_smem[0, r], X_ROWS), ssem.at[slot]))

        @pl.when(b == nb - 1)
        def _():
            drain(slot)

            @pl.when(nb > 1)
            def _():
                drain(1 - slot)

            zero_fill(start=False)


def _moe_kernel(tcls_ref, nused_ref, xs_ref, win_ref, wout_ref, ys_ref):
    i = pl.program_id(0)
    tm = xs_ref.shape[0] // (X_ROWS * MOE_STEP_TILES)
    ff = wout_ref.shape[1]

    @pl.when(i * MOE_STEP_TILES < nused_ref[0])
    def _():
        for jt in range(MOE_STEP_TILES):
            cls = tcls_ref[i * MOE_STEP_TILES + jt]
            x0, y0 = jt * tm * X_ROWS, jt * tm * Y_ROWS
            xb = jnp.concatenate([_rows_of(xs_ref, x0 + j, tm, X_ROWS)[...].astype(BF16) for j in range(Y_ROWS)],
                                 axis=1)
            record = _rows_of(xs_ref, x0 + Y_ROWS, tm, X_ROWS)[...]
            y = None
            for e, gate in (((cls >> 3) & 7, record[:, 1:2]), (cls & 7, record[:, 2:3])):
                h = _dot(xb, win_ref[e])
                a = h[:, :ff]
                act = (a * jax.nn.sigmoid(a) * h[:, ff:] * gate).astype(BF16)
                ye = _dot(act, wout_ref[e])
                y = ye if y is None else y + ye
            for j in range(Y_ROWS):
                _rows_of(ys_ref, y0 + j, tm, Y_ROWS)[...] = y[:, j * LANES:(j + 1) * LANES]

    @pl.when(i * MOE_STEP_TILES >= nused_ref[0])
    def _():
        ys_ref[...] = jnp.zeros(ys_ref.shape, F32)


def _post_kernel(pos_ref, ys_ref, x1_ref, p_ref, ln_g_ref, ln_b_ref, pw_ref, gw_ref, gb_ref, out_ref,
                 idx_smem, ybuf, isem, rsem, *, alpha, n_blocks):
    i = pl.program_id(0)
    tm = x1_ref.shape[0]
    gather = _TokenGather(i, n_blocks, pos_ref, ys_ref, idx_smem, ybuf, isem, rsem)
    gather.arrive()
    gather.prefetch()
    yb = ybuf.at[i % 2]
    ffn = jnp.concatenate([_rows_of(yb, j, tm, Y_ROWS)[...] for j in range(Y_ROWS)], axis=1)
    x2 = _layer_norm(alpha * x1_ref[...] + ffn, ln_g_ref[...], ln_b_ref[...])
    gate = jax.nn.sigmoid(_dot(x2.astype(BF16), gw_ref[...]) + gb_ref[...])
    out_ref[...] = x2 + _dot(p_ref[...].astype(BF16), pw_ref[...]) * gate
    gather.finish()


def _moe_and_post(x1, route, layer, p, e_w_in, e_w_out, ln_g, ln_b, ple_w, ple_gw, ple_gb, alpha):
    t, d = x1.shape
    assert d == LANES * Y_ROWS
    depth, n_exp, _, ff2 = e_w_in.shape
    epg = n_exp // N_EGROUPS
    pb = _tile(t, PLAN_BLOCK)
    nb = t // pb
    nt = t // MOE_TILE + N_EGROUPS * (epg * (epg - 1) // 2) + N_EGROUPS * (MOE_STEP_TILES - 1)
    nt = -(-nt // MOE_STEP_TILES) * MOE_STEP_TILES
    kt = -(-nt // LANES) * LANES
    n_sorted = nt * MOE_TILE

    pos, tinfo, xs = pl.pallas_call(
        functools.partial(_plan_kernel, tile=MOE_TILE),
        grid=(2, nb),
        in_specs=[pl.BlockSpec((ROUTE_ROWS, pb), lambda ps, b: (0, b)),
                  pl.BlockSpec((pb, d), lambda ps, b: (ps * b, 0))],
        out_specs=[pl.BlockSpec((None, 1, pb), lambda ps, b: (ps * b, 0, 0)),
                   pl.BlockSpec((ROUTE_ROWS, kt), lambda ps, b: (0, 0)),
                   pl.BlockSpec(memory_space=pl.ANY)],
        out_shape=[jax.ShapeDtypeStruct((nb, 1, pb), I32),
                   jax.ShapeDtypeStruct((ROUTE_ROWS, kt), I32),
                   jax.ShapeDtypeStruct((n_sorted * X_ROWS, LANES), F32)],
        scratch_shapes=[pltpu.VMEM((N_CLASS, pb), F32),
                        pltpu.VMEM((N_CLASS, 1), F32),
                        pltpu.VMEM((N_CLASS, 1), F32),
                        pltpu.VMEM((2, pb * X_ROWS, LANES), F32),
                        pltpu.VMEM((1, pb), I32),
                        pltpu.SMEM((1, pb), I32),
                        pltpu.VMEM((ROUTE_ROWS, N_CLASS), I32),
                        pltpu.SMEM((ROUTE_ROWS, N_CLASS), I32),
                        pltpu.VMEM((MOE_TILE * ZERO_TILES * X_ROWS, LANES), F32),
                        pltpu.SemaphoreType.DMA((2,)),
                        pltpu.SemaphoreType.DMA(()),
                        pltpu.SemaphoreType.DMA(())],
        compiler_params=_cparams(("arbitrary", "arbitrary")),
        name="plan",
    )(route, x1)

    w_in = e_w_in.reshape(depth, N_EGROUPS, epg, d, ff2)
    w_out = e_w_out.reshape(depth, N_EGROUPS, epg, ff2 // 2, d)
    step_rows = MOE_TILE * MOE_STEP_TILES

    def group(i, tile_cls):
        return tile_cls[i * MOE_STEP_TILES] >> 6
    ys = pl.pallas_call(
        _moe_kernel,
        grid_spec=pltpu.PrefetchScalarGridSpec(
            num_scalar_prefetch=2,
            grid=(nt // MOE_STEP_TILES,),
            in_specs=[pl.BlockSpec((step_rows * X_ROWS, LANES), lambda i, tc, nu: (i, 0)),
                      pl.BlockSpec((None, None, epg, d, ff2), lambda i, tc, nu: (layer, group(i, tc), 0, 0, 0),
                                   pipeline_mode=pl.Buffered(1)),
                      pl.BlockSpec((None, None, epg, ff2 // 2, d), lambda i, tc, nu: (layer, group(i, tc), 0, 0, 0),
                                   pipeline_mode=pl.Buffered(1))],
            out_specs=pl.BlockSpec((step_rows * Y_ROWS, LANES), lambda i, tc, nu: (i, 0))),
        out_shape=jax.ShapeDtypeStruct((n_sorted * Y_ROWS, LANES), F32),
        compiler_params=_cparams(("arbitrary",)),
        name="moe",
    )(tinfo[0, :nt], tinfo[1, :1], xs, w_in, w_out)

    post_args = (pos, ys, x1, p, ln_g[None], ln_b[None], ple_w.astype(BF16), ple_gw.astype(BF16), ple_gb[None])
    return pl.pallas_call(
        functools.partial(_post_kernel, alpha=alpha, n_blocks=nb),
        grid=(nb,),
        in_specs=[pl.BlockSpec(memory_space=pl.ANY),
                  pl.BlockSpec(memory_space=pl.ANY),
                  pl.BlockSpec((pb, d), lambda i: (i, 0)),
                  pl.BlockSpec((None, pb, p.shape[2]), lambda i: (layer, i, 0))]
                 + [_full(a.shape) for a in post_args[4:]],
        out_specs=pl.BlockSpec((pb, d), lambda i: (i, 0)),
        out_shape=jax.ShapeDtypeStruct((t, d), F32),
        scratch_shapes=[pltpu.SMEM((2, 1, pb), I32), pltpu.VMEM((2, pb * Y_ROWS, LANES), F32),
                        pltpu.SemaphoreType.DMA((2,)), pltpu.SemaphoreType.DMA((2,))],
        compiler_params=_cparams(("arbitrary",)),
        name="post",
    )(*post_args)


def _router_params(wg, bg, we, be):
    d = wg.shape[0]
    w = jnp.zeros((d, ROUTE_COLS), F32).at[:, :N_EGROUPS].set(wg).at[:, 8:8 + we.shape[1]].set(we)
    b = jnp.zeros((1, ROUTE_COLS), F32).at[0, :N_EGROUPS].set(bg).at[0, 8:8 + be.shape[0]].set(be)
    hi = w.astype(BF16)
    return hi, (w - hi.astype(F32)).astype(BF16), b


def _tile(t, want):
    while t % want:
        want //= 2
    return want


def kernel(x, p, a_w_in, a_b_in, a_vn_g, a_vn_b, a_w_s, a_b_s, a_w_out, b_w_in, b_kv_g, b_w_uk, b_w_uv, b_w_out,
           ln1_g, ln1_b, ln2_g, ln2_b, r_wg, r_bg, r_we, r_be, e_w_in, e_w_out, ple_w, ple_gw, ple_gb):
    batch, seq, d = x.shape
    t = batch * seq
    depth = p.shape[0]
    alpha = (2 * depth) ** 0.25
    chunk = a_w_s.shape[-1]
    xt = x.reshape(t, d)
    pt = p.reshape(depth, t, p.shape[-1])
    for i in range(depth):
        j = i // 2
        router = _router_params(r_wg[i], r_bg[i], r_we[i], r_be[i])
        if i % 2 == 0:
            x1, route = _gmlp_layer(xt, a_w_in[j], a_b_in[j], a_vn_g[j], a_vn_b[j], a_w_s[j], a_b_s[j],
                                    a_w_out[j], ln1_g[i], ln1_b[i], router, alpha, max(chunk, _tile(t, 512)))
        else:
            x1, route = _dsa_layer(xt, batch, b_w_in[j], b_kv_g[j], b_w_uk[j], b_w_uv[j], b_w_out[j],
                                   ln1_g[i], ln1_b[i], router, alpha, _tile(t, 512))
        xt = _moe_and_post(x1, route, i, pt, e_w_in, e_w_out, ln2_g[i], ln2_b[i],
                           ple_w[i], ple_gw[i], ple_gb[i], alpha)
    return xt.reshape(batch, seq, d)
```

```python
import functools

import jax
import jax.numpy as jnp
from jax import lax
from jax.experimental import pallas as pl
from jax.experimental.pallas import tpu as pltpu

F32, BF16, I32 = jnp.float32, jnp.bfloat16, jnp.int32

IDX_HEADS = 8
IDX_DIM = 64
TOPK_MAX = 256
N_EGROUPS = 4
LN_EPS = 1e-5
RMS_EPS = 1e-6

LANES = 128
Q_BLOCK = 128
KEY_CHUNK = 512
COUNT_SLAB = 1024
COUNT_ACC = 64
LOG2E = 1.4426950408889634
VMEM_LIMIT = 48 * 1024 * 1024

ROUTE_COLS = LANES
ROUTE_ROWS = 8
N_CLASS = 64 * N_EGROUPS
Y_ROWS = 8
X_ROWS = Y_ROWS + 1
PLAN_BLOCK = 1024
GATHER_BLOCK = 512
MOE_TILE = 128
MOE_STEP_TILES = 4
ZERO_TILES = 1 << ((MOE_STEP_TILES - 1).bit_length() - 1) if MOE_STEP_TILES > 1 else 1
INT_MIN = -(2 ** 31)
NEG_BIG = -1e30


def _cparams(sem, vmem=VMEM_LIMIT):
    return pltpu.CompilerParams(dimension_semantics=sem, vmem_limit_bytes=vmem)


def _full(shape):
    n = len(shape)
    return pl.BlockSpec(shape, lambda *_: (0,) * n)


def _dot(a, b):
    return jnp.dot(a, b, preferred_element_type=F32)


def _dot_t(a, b):
    return lax.dot_general(a, b, (((1,), (1,)), ((), ())), preferred_element_type=F32)


def _layer_norm(x, g, b):
    mu = jnp.mean(x, axis=-1, keepdims=True)
    xc = x - mu
    var = jnp.mean(xc * xc, axis=-1, keepdims=True)
    return xc * lax.rsqrt(var + LN_EPS) * g + b


def _route_rows(lg_t):
    g = [lg_t[i:i + 1] for i in range(N_EGROUPS)]
    gmax = functools.reduce(jnp.maximum, g)
    gsel = jnp.where(g[0] >= gmax, 0.0, jnp.where(g[1] >= gmax, 1.0, jnp.where(g[2] >= gmax, 2.0, 3.0)))
    den = functools.reduce(lambda a, b: a + b, [jnp.exp(gi - gmax) for gi in g])
    p_g = 1.0 / den
    el = lg_t[8:16]
    for gi in range(1, N_EGROUPS):
        el = jnp.where(gsel == float(gi), lg_t[8 + 8 * gi:16 + 8 * gi], el)
    eidx = lax.broadcasted_iota(I32, el.shape, 0).astype(F32)
    m1 = jnp.max(el, axis=0, keepdims=True)
    i1 = jnp.min(jnp.where(el == m1, eidx, 8.0), axis=0, keepdims=True)
    el2 = jnp.where(eidx == i1, -jnp.inf, el)
    m2 = jnp.max(el2, axis=0, keepdims=True)
    i2 = jnp.min(jnp.where(el2 == m2, eidx, 8.0), axis=0, keepdims=True)
    r = jnp.exp(m2 - m1)
    inv = 1.0 / (1.0 + r)
    gate1 = p_g * inv
    gate2 = p_g * r * inv
    first_lo = i1 < i2
    lo = jnp.minimum(i1, i2)
    hi = jnp.maximum(i1, i2)
    cls = gsel * 64.0 + lo * 8.0 + hi
    return cls, jnp.where(first_lo, gate1, gate2), jnp.where(first_lo, gate2, gate1)


def _norm_and_route(y, ln_g, ln_b, wr_hi, wr_lo, br, x1_ref, route_ref):
    tm, _ = y.shape
    x1 = _layer_norm(y, ln_g, ln_b)
    x1_ref[...] = x1
    x_hi = x1.astype(BF16)
    x_lo = (x1 - x_hi.astype(F32)).astype(BF16)
    lg = _dot(x_hi, wr_hi) + _dot(x_lo, wr_hi) + _dot(x_hi, wr_lo) + br
    cls, g_lo, g_hi = _route_rows(lg.T)
    row = lax.broadcasted_iota(I32, (ROUTE_ROWS, tm), 0)
    route_ref[...] = jnp.where(row == 0, cls, jnp.where(row == 1, g_lo, jnp.where(row == 2, g_hi, 0.0)))


def _mixer_out_specs(tm, d):
    return [pl.BlockSpec((tm, d), lambda i: (i, 0)), pl.BlockSpec((ROUTE_ROWS, tm), lambda i: (0, i))]


def _mixer_out_shape(t, d):
    return [jax.ShapeDtypeStruct((t, d), F32), jax.ShapeDtypeStruct((ROUTE_ROWS, t), F32)]


def _gmlp_kernel(x_ref, w_in_ref, b_in_ref, vn_g_ref, vn_b_ref, w_s_ref, b_st_ref, w_out_ref,
                 ln_g_ref, ln_b_ref, wr_hi_ref, wr_lo_ref, br_ref, x1_ref, route_ref, gated_ref, *, alpha):
    x = x_ref[...]
    tm, _ = x.shape
    groups, chunk, _ = w_s_ref.shape
    z = _dot(x.astype(BF16), w_in_ref[...]) + b_in_ref[...]
    z = 0.5 * z * (1.0 + lax.erf(z * (2.0 ** -0.5)))
    half = z.shape[1] // 2
    gd = half // groups
    u = z[:, :half]
    v = _layer_norm(z[:, half:], vn_g_ref[...], vn_b_ref[...]).astype(BF16)
    r = lax.broadcasted_iota(I32, (chunk, chunk), 0)
    c = lax.broadcasted_iota(I32, (chunk, chunk), 1)
    causal = r >= c
    for g in range(groups):
        w_c = jnp.where(causal, w_s_ref[g], 0.0).astype(BF16)
        bias = b_st_ref[:, g:g + 1]
        for ci in range(tm // chunk):
            rows = slice(ci * chunk, (ci + 1) * chunk)
            cols = slice(g * gd, (g + 1) * gd)
            s = _dot(w_c, v[rows, cols]) + bias
            gated_ref[rows, cols] = (u[rows, cols] * s).astype(BF16)
    mix = _dot(gated_ref[...], w_out_ref[...])
    _norm_and_route(alpha * x + mix, ln_g_ref[...], ln_b_ref[...], wr_hi_ref[...], wr_lo_ref[...],
                    br_ref[...], x1_ref, route_ref)


def _gmlp_layer(xt, w_in, b_in, vn_g, vn_b, w_s, b_s, w_out, ln_g, ln_b, router, alpha, tm):
    t, d = xt.shape
    half = w_out.shape[0]
    wr_hi, wr_lo, br = router
    args = (xt, w_in.astype(BF16), b_in[None], vn_g[None], vn_b[None], w_s, b_s.T, w_out.astype(BF16),
            ln_g[None], ln_b[None], wr_hi, wr_lo, br)
    in_specs = [pl.BlockSpec((tm, d), lambda i: (i, 0))] + [_full(a.shape) for a in args[1:]]
    return pl.pallas_call(
        functools.partial(_gmlp_kernel, alpha=alpha),
        grid=(t // tm,),
        in_specs=in_specs,
        out_specs=_mixer_out_specs(tm, d),
        out_shape=_mixer_out_shape(t, d),
        scratch_shapes=[pltpu.VMEM((tm, half), BF16)],
        compiler_params=_cparams(("parallel",)),
        name="gmlp",
    )(*args)


def _dsa_proj_kernel(x_ref, wq_ref, wc_ref, wqi_ref, wkw_ref, kvg_ref, wuk_t_ref,
                     qlat_ref, ckv_ref, ckv_t_ref, qidx_ref, kidx_ref, wi_t_ref, *, scale_q, scale_w):
    xb = x_ref[...].astype(BF16)
    nq = qlat_ref.shape[0]
    kc = ckv_t_ref.shape[2]
    heads, hd, _ = wuk_t_ref.shape
    q = _dot(xb, wq_ref[...])
    for h in range(heads):
        ql = (_dot(q[:, h * hd:(h + 1) * hd].astype(BF16), wuk_t_ref[h]) * scale_q).astype(BF16)
        for b in range(nq):
            qlat_ref[b, h * Q_BLOCK:(h + 1) * Q_BLOCK, :] = ql[b * Q_BLOCK:(b + 1) * Q_BLOCK]
    c = _dot(xb, wc_ref[...])
    ms = jnp.mean(c * c, axis=-1, keepdims=True)
    c = c * lax.rsqrt(ms + RMS_EPS) * kvg_ref[...]
    ckv_ref[...] = c.astype(BF16)
    for b in range(ckv_t_ref.shape[0]):
        ckv_t_ref[b] = c[b * kc:(b + 1) * kc].T.astype(BF16)
    qi = _dot(xb, wqi_ref[...]).astype(BF16)
    for h in range(IDX_HEADS):
        for b in range(nq):
            qidx_ref[b, h * Q_BLOCK:(h + 1) * Q_BLOCK, :] = qi[b * Q_BLOCK:(b + 1) * Q_BLOCK,
                                                               h * IDX_DIM:(h + 1) * IDX_DIM]
    kw = _dot(xb, wkw_ref[...])
    kidx_ref[...] = kw[:, :IDX_DIM].astype(BF16)
    wi_t_ref[...] = kw.T[IDX_DIM:IDX_DIM + IDX_HEADS] * scale_w


def _dsa_attn_kernel(qidx_ref, wi_t_ref, qlat_ref, kidx_ref, ckv_ref, ckv_t_ref, o_ref,
                     key_ref, key16_ref, acc_ref, m_ref, l_ref, a_ref, p_ref, *, k_sel, heads):
    qb = pl.program_id(1)
    kc = ckv_t_ref.shape[2]
    slab = key_ref.shape[1]
    cps = slab // kc
    nq = Q_BLOCK
    pair = 2 * nq
    n_pairs = heads // 2
    n_chunks = (qb * nq + nq + kc - 1) // kc
    n_slabs = (n_chunks + cps - 1) // cps

    def key_chunk(j):
        return key_ref.at[j // cps, pl.ds(pl.multiple_of((j % cps) * kc, kc), kc), :]

    q_pos = qb * nq + lax.broadcasted_iota(I32, (kc, nq), 1)
    k_off = lax.broadcasted_iota(I32, (kc, nq), 0)

    def key16_chunk(j):
        return key16_ref.at[j // cps, pl.ds(pl.multiple_of((j % cps) * kc, kc), kc), :]

    def pad_chunk(j, carry):
        key_chunk(j)[...] = jnp.full((kc, nq), jnp.nan, F32)
        key16_chunk(j)[...] = jnp.full((kc, nq), jnp.nan, BF16)
        return carry

    lax.fori_loop(n_chunks, n_slabs * cps, pad_chunk, 0)

    def score_chunk(j, carry):
        kch = kidx_ref[pl.ds(pl.multiple_of(j * kc, kc), kc), :]
        tot = None
        for p in range(IDX_HEADS // 2):
            sc = _dot_t(kch, qidx_ref[p * pair:(p + 1) * pair, :])
            for hh in range(2):
                h = 2 * p + hh
                r = jnp.maximum(sc[:, hh * nq:(hh + 1) * nq], 0.0) * wi_t_ref[h:h + 1, :]
                tot = r if tot is None else tot + r
        tot = jnp.where(k_off + j * kc <= q_pos, tot, jnp.nan)
        key_chunk(j)[...] = tot
        key16_chunk(j)[...] = tot.astype(BF16)
        return carry

    lax.fori_loop(0, n_chunks, score_chunk, 0)

    def count(t, strict=False):
        tb = jnp.broadcast_to(t, (COUNT_ACC, nq))

        def body(s, acc):
            for g in range(slab // COUNT_ACC):
                blk = key_ref[s, g * COUNT_ACC:(g + 1) * COUNT_ACC, :]
                acc = jnp.where(blk > tb if strict else blk >= tb, acc + 1.0, acc)
            return acc

        acc = lax.fori_loop(0, n_slabs, body, jnp.zeros((COUNT_ACC, nq), F32))
        return jnp.sum(acc, axis=0, keepdims=True)

    def count16(t):
        rows16 = 2 * COUNT_ACC
        tb = jnp.broadcast_to(t.astype(BF16), (rows16, nq))

        def body(s, acc):
            for g in range(slab // rows16):
                blk = key16_ref[s, g * rows16:(g + 1) * rows16, :]
                acc = jnp.where(blk >= tb, acc + 1.0, acc)
            return acc

        acc = lax.fori_loop(0, n_slabs, body, jnp.zeros((rows16, nq), BF16))
        return jnp.sum(acc.astype(F32), axis=0, keepdims=True)

    def decode(code):
        return lax.bitcast_convert_type(code ^ ((code >> 31) & 0x7FFFFFFF), F32)

    def decode16(c16):
        return decode(jnp.left_shift(c16, 16) | jnp.where(c16 < 0, 0xFFFF, 0))

    def bit16_step(b, c16):
        t = c16 + jnp.left_shift(jnp.int32(1), 15 - b)
        return jnp.where(count16(decode16(t)) >= float(k_sel), t, c16)

    no_code = -(2 ** 15)
    c16 = lax.fori_loop(0, 16, bit16_step, jnp.full((1, nq), no_code, I32))

    def bit_step(b, c):
        t = c + jnp.left_shift(jnp.int32(1), 17 - b)
        return jnp.where((count(decode(t)) >= float(k_sel)) & (t > c), t, c)

    c_sel = lax.fori_loop(0, 18, bit_step, jnp.left_shift(jnp.maximum(c16, no_code + 1) - 1, 16))
    c_sel = jnp.where(c16 == no_code, -jnp.inf, decode(c_sel))
    n_ge = jnp.where(c16 == no_code, 0.0, count(c_sel))

    has_tie = jnp.max(n_ge) > float(k_sel)

    @pl.when(has_tie)
    def _():
        need = float(k_sel) - count(c_sel, strict=True)
        tie_col = n_ge > float(k_sel)
        cb = jnp.broadcast_to(c_sel, (kc, nq))
        lower = (lax.broadcasted_iota(I32, (kc, kc), 1) < lax.broadcasted_iota(I32, (kc, kc), 0))
        lower = jnp.where(lower, 1.0, 0.0).astype(BF16)

        def fix(j, seen):
            blk = key_chunk(j)[...]
            eq = blk == cb
            eq_f = jnp.where(eq, 1.0, 0.0)
            rank = seen + _dot(lower, eq_f.astype(BF16))
            drop = eq & tie_col & (rank >= need)
            key_chunk(j)[...] = jnp.where(drop, jnp.nan, blk)
            return seen + jnp.sum(eq_f, axis=0, keepdims=True)

        lax.fori_loop(0, n_chunks, fix, jnp.zeros((1, nq), F32))

    thr = jnp.broadcast_to(c_sel, (kc, nq))
    m_ref[...] = jnp.full(m_ref.shape, NEG_BIG, F32)
    l_ref[...] = jnp.zeros(l_ref.shape, F32)
    acc_ref[...] = jnp.zeros(acc_ref.shape, F32)

    def softmax_chunk(j):
        ck = ckv_ref[pl.ds(pl.multiple_of(j * kc, kc), kc), :]
        bias = jnp.where(key_chunk(j)[...] >= thr, 0.0, NEG_BIG)
        bias = jnp.concatenate([bias, bias], axis=1)
        slot = j % 2
        for p in range(n_pairs):
            lg = _dot_t(ck, qlat_ref[p * pair:(p + 1) * pair, :]) + bias
            m_old = m_ref[p]
            m_new = jnp.maximum(m_old, jnp.max(lg, axis=0, keepdims=True))
            pr = jnp.exp2(lg - m_new)
            alpha = jnp.exp2(m_old - m_new)
            l_ref[p] = alpha * l_ref[p] + jnp.sum(pr, axis=0, keepdims=True)
            m_ref[p] = m_new
            a_ref[slot, p] = alpha
            p_ref[slot, p] = pr.astype(BF16)

    def accumulate_chunk(j):
        ck_t = ckv_t_ref[j]
        slot = j % 2
        for p in range(n_pairs):
            acc_ref[p] = a_ref[slot, p] * acc_ref[p] + _dot(ck_t, p_ref[slot, p])

    def att_step(j, carry):
        accumulate_chunk(j - 1)
        softmax_chunk(j)
        return carry

    softmax_chunk(0)
    lax.fori_loop(1, n_chunks, att_step, 0)
    accumulate_chunk(n_chunks - 1)
    for p in range(n_pairs):
        o_ref[p * pair:(p + 1) * pair, :] = (acc_ref[p] * (1.0 / l_ref[p])).T.astype(o_ref.dtype)


def _dsa_out_kernel(o_ref, x_ref, wuv_ref, wout_ref, ln_g_ref, ln_b_ref, wr_hi_ref, wr_lo_ref, br_ref,
                    x1_ref, route_ref, o2_ref, *, alpha):
    nq = o_ref.shape[0]
    heads, _, hd = wuv_ref.shape
    for b in range(nq):
        for h in range(heads):
            oh = o_ref[b, h * Q_BLOCK:(h + 1) * Q_BLOCK, :]
            o2_ref[b * Q_BLOCK:(b + 1) * Q_BLOCK, h * hd:(h + 1) * hd] = _dot(oh, wuv_ref[h]).astype(BF16)
    mix = _dot(o2_ref[...], wout_ref[...])
    _norm_and_route(alpha * x_ref[...] + mix, ln_g_ref[...], ln_b_ref[...], wr_hi_ref[...], wr_lo_ref[...],
                    br_ref[...], x1_ref, route_ref)


def _dsa_layer(xt, batch, w_in, kv_g, w_uk, w_uv, w_out, ln_g, ln_b, router, alpha, tm):
    t, d = xt.shape
    seq = t // batch
    heads, lat, hd = w_uk.shape
    qw = heads * hd
    nqb = t // Q_BLOCK
    nq = tm // Q_BLOCK
    k_sel = min(TOPK_MAX, seq // 4)
    kc = min(KEY_CHUNK, seq)
    slab = min(COUNT_SLAB, seq)
    w_in = w_in.astype(BF16)
    o1, o2, o3 = qw, qw + lat, qw + lat + IDX_HEADS * IDX_DIM
    wkw = jnp.pad(w_in[:, o3:], ((0, 0), (0, LANES - (w_in.shape[1] - o3))))
    proj_args = (xt, w_in[:, :o1], w_in[:, o1:o2], w_in[:, o2:o3], wkw, kv_g[None],
                 jnp.swapaxes(w_uk, 1, 2).astype(BF16))
    qlat, ckv, ckv_t, qidx, kidx, wi_t = pl.pallas_call(
        functools.partial(_dsa_proj_kernel, scale_q=hd ** -0.5 * LOG2E, scale_w=(IDX_HEADS * IDX_DIM) ** -0.5),
        grid=(t // tm,),
        in_specs=[pl.BlockSpec((tm, d), lambda i: (i, 0))] + [_full(a.shape) for a in proj_args[1:]],
        out_specs=[pl.BlockSpec((nq, heads * Q_BLOCK, lat), lambda i: (i, 0, 0)),
                   pl.BlockSpec((tm, lat), lambda i: (i, 0)),
                   pl.BlockSpec((tm // kc, lat, kc), lambda i: (i, 0, 0)),
                   pl.BlockSpec((nq, IDX_HEADS * Q_BLOCK, IDX_DIM), lambda i: (i, 0, 0)),
                   pl.BlockSpec((tm, IDX_DIM), lambda i: (i, 0)),
                   pl.BlockSpec((IDX_HEADS, tm), lambda i: (0, i))],
        out_shape=[jax.ShapeDtypeStruct((nqb, heads * Q_BLOCK, lat), BF16),
                   jax.ShapeDtypeStruct((t, lat), BF16),
                   jax.ShapeDtypeStruct((t // kc, lat, kc), BF16),
                   jax.ShapeDtypeStruct((nqb, IDX_HEADS * Q_BLOCK, IDX_DIM), BF16),
                   jax.ShapeDtypeStruct((t, IDX_DIM), BF16),
                   jax.ShapeDtypeStruct((IDX_HEADS, t), F32)],
        compiler_params=_cparams(("parallel",)),
        name="dsa_proj",
    )(*proj_args)

    nq_seq = seq // Q_BLOCK
    n_kc = seq // kc
    rows = heads * Q_BLOCK
    o = pl.pallas_call(
        functools.partial(_dsa_attn_kernel, k_sel=k_sel, heads=heads),
        grid=(batch, nq_seq),
        in_specs=[pl.BlockSpec((None, IDX_HEADS * Q_BLOCK, IDX_DIM), lambda b, q: (b * nq_seq + q, 0, 0)),
                  pl.BlockSpec((IDX_HEADS, Q_BLOCK), lambda b, q: (0, b * nq_seq + q)),
                  pl.BlockSpec((None, rows, lat), lambda b, q: (b * nq_seq + q, 0, 0)),
                  pl.BlockSpec((None, seq, IDX_DIM), lambda b, q: (b, 0, 0)),
                  pl.BlockSpec((None, seq, lat), lambda b, q: (b, 0, 0)),
                  pl.BlockSpec((None, n_kc, lat, kc), lambda b, q: (b, 0, 0, 0))],
        out_specs=pl.BlockSpec((None, rows, lat), lambda b, q: (b * nq_seq + q, 0, 0)),
        out_shape=jax.ShapeDtypeStruct((nqb, rows, lat), BF16),
        scratch_shapes=[pltpu.VMEM((seq // slab, slab, Q_BLOCK), F32),
                        pltpu.VMEM((seq // slab, slab, Q_BLOCK), BF16),
                        pltpu.VMEM((heads // 2, lat, 2 * Q_BLOCK), F32),
                        pltpu.VMEM((heads // 2, 1, 2 * Q_BLOCK), F32),
                        pltpu.VMEM((heads // 2, 1, 2 * Q_BLOCK), F32),
                        pltpu.VMEM((2, heads // 2, 1, 2 * Q_BLOCK), F32),
                        pltpu.VMEM((2, heads // 2, kc, 2 * Q_BLOCK), BF16)],
        compiler_params=_cparams(("parallel", "arbitrary")),
        name="dsa_attn",
    )(qidx, wi_t, qlat, kidx.reshape(batch, seq, IDX_DIM), ckv.reshape(batch, seq, lat),
      ckv_t.reshape(batch, n_kc, lat, kc))

    wr_hi, wr_lo, br = router
    out_args = (o, xt, w_uv.astype(BF16), w_out.astype(BF16), ln_g[None], ln_b[None], wr_hi, wr_lo, br)
    return pl.pallas_call(
        functools.partial(_dsa_out_kernel, alpha=alpha),
        grid=(t // tm,),
        in_specs=[pl.BlockSpec((nq, rows, lat), lambda i: (i, 0, 0)),
                  pl.BlockSpec((tm, d), lambda i: (i, 0))] + [_full(a.shape) for a in out_args[2:]],
        out_specs=_mixer_out_specs(tm, d),
        out_shape=_mixer_out_shape(t, d),
        scratch_shapes=[pltpu.VMEM((tm, qw), BF16)],
        compiler_params=_cparams(("parallel",)),
        name="dsa_out",
    )(*out_args)


DMA_UNROLL = 8


def _start_token_moves(n, copy_of):
    def body(g, carry):
        for u in range(DMA_UNROLL):
            copy_of(g * DMA_UNROLL + u).start(priority=u % 2)
        return carry
    lax.fori_loop(0, n // DMA_UNROLL, body, 0)


def _tok(ref, i, rows):
    return ref.at[pl.ds(i * rows, rows)]


def _rows_of(ref, j, n, rows):
    return ref.at[pl.ds(j, n, stride=rows), :]


class _TokenGather:
    def __init__(self, i, n, idx_hbm, src_hbm, idx_smem, buf, isem, rsem):
        self.i, self.n = i, n
        self.idx_hbm, self.src_hbm, self.idx_smem, self.buf, self.isem, self.rsem = (
            idx_hbm, src_hbm, idx_smem, buf, isem, rsem)
        self.tm = idx_smem.shape[2]
        self.rows = buf.shape[1] // self.tm
        self.slot = i % 2

    def _idx_copy(self, blk, s):
        return pltpu.make_async_copy(self.idx_hbm.at[blk], self.idx_smem.at[s], self.isem.at[s])

    def _token_copy(self, s, r):
        return pltpu.make_async_copy(_tok(self.src_hbm, self.idx_smem[s, 0, r], self.rows),
                                     _tok(self.buf.at[s], r, self.rows), self.rsem.at[s])

    def _wait_tokens(self, s):
        pltpu.make_async_copy(self.src_hbm.at[pl.ds(0, self.tm * self.rows)], self.buf.at[s], self.rsem.at[s]).wait()

    def arrive(self):
        @pl.when(self.i == 0)
        def _():
            first = self._idx_copy(0, 0)
            first.start()
            first.wait()
            _start_token_moves(self.tm, lambda r: self._token_copy(0, r))
            self._idx_copy(min(1, self.n - 1), 1).start()

        self._idx_copy(jnp.minimum(self.i + 1, self.n - 1), 1 - self.slot).wait()
        self._wait_tokens(self.slot)

    def prefetch(self):
        nslot = 1 - self.slot
        for r in range(self.tm):
            self._token_copy(nslot, r).start(priority=r % 2)
        self._idx_copy(jnp.minimum(self.i + 2, self.n - 1), self.slot).start()

    def finish(self):
        @pl.when(self.i == self.n - 1)
        def _():
            self._wait_tokens(1 - self.slot)
            self._idx_copy(self.n - 1, self.slot).wait()


def _plan_kernel(route_ref, x_ref, pos_ref, tinfo_ref, xs_ref,
                 hist_ref, off_ref, carry_ref, stage_ref, ppos_vmem, ppos_smem, fill_vmem, fill_smem, zero_ref,
                 ssem, psem, zsem, *, tile):
    ps = pl.program_id(0)
    b = pl.program_id(1)
    nb = pl.num_programs(1)
    n_cls, pb = hist_ref.shape
    kt = tinfo_ref.shape[1]
    cls = route_ref[0:1, :].astype(I32)
    cid = lax.broadcasted_iota(I32, (n_cls, pb), 0)
    onehot = jnp.where(cid == cls, 1.0, 0.0)

    def zero_fill(start):
        def piece(tok0, n_tok):
            cp = pltpu.make_async_copy(zero_ref.at[pl.ds(0, n_tok * X_ROWS)],
                                       xs_ref.at[pl.ds(tok0 * X_ROWS, n_tok * X_ROWS)], zsem)
            cp.start() if start else cp.wait()

        def per_class(c, carry):
            first, n = fill_smem[0, c], fill_smem[1, c]
            bit = zero_ref.shape[0] // X_ROWS
            while bit:
                @pl.when((n & bit) != 0)
                def _(bit=bit):
                    piece(first + (n & -(2 * bit)), bit)
                bit //= 2
            return carry

        lax.fori_loop(0, n_cls, per_class, 0)

        def per_tile(k, carry):
            piece(k * tile, tile)
            return carry

        lax.fori_loop(fill_smem[2, 0], xs_ref.shape[0] // (tile * X_ROWS), per_tile, 0)

    @pl.when(ps == 0)
    def _():
        @pl.when(b == 0)
        def _():
            hist_ref[...] = jnp.zeros(hist_ref.shape, F32)
            zero_ref[...] = jnp.zeros(zero_ref.shape, F32)
        hist_ref[...] += onehot

    @pl.when((ps == 1) & (b == 0))
    def _():
        counts = jnp.sum(hist_ref[...], axis=1, keepdims=True)
        tiles = jnp.floor((counts + float(tile - 1)) * (1.0 / tile))
        cid_col = lax.broadcasted_iota(I32, (n_cls, 1), 0)
        per_group = n_cls // N_EGROUPS
        for g in range(N_EGROUPS):
            g_tiles = jnp.sum(tiles[g * per_group:(g + 1) * per_group], axis=0, keepdims=True)
            extra = jnp.ceil(g_tiles * (1.0 / MOE_STEP_TILES)) * float(MOE_STEP_TILES) - g_tiles
            tiles = jnp.where(cid_col == (g + 1) * per_group - 1, extra, tiles)
        lower = lax.broadcasted_iota(I32, (n_cls, n_cls), 1) < lax.broadcasted_iota(I32, (n_cls, n_cls), 0)
        lower = jnp.where(lower, 1.0, 0.0).astype(BF16)
        off = _dot(lower, jnp.broadcast_to(tiles, (n_cls, LANES)).astype(BF16))[:, :1]
        off_ref[...] = off * float(tile)
        carry_ref[...] = jnp.zeros(carry_ref.shape, F32)
        k = lax.broadcasted_iota(I32, (n_cls, kt), 1).astype(F32)
        cid_f = lax.broadcasted_iota(I32, (n_cls, kt), 0).astype(F32)
        mine = (k >= off) & (k < off + tiles)
        tcls = jnp.sum(jnp.where(mine, cid_f, 0.0), axis=0, keepdims=True)
        n_used = jnp.sum(tiles, axis=0, keepdims=True)
        last = jnp.max(jnp.where(tiles > 0.0, cid_f[:, :1], 0.0), axis=0, keepdims=True)
        tcls = jnp.where(k[:1] >= n_used, last, tcls)
        row = lax.broadcasted_iota(I32, (ROUTE_ROWS, kt), 0)
        tinfo_ref[...] = jnp.where(row == 0, tcls, jnp.where(row == 1, n_used, 0.0)).astype(I32)
        as_row = lambda col: jnp.broadcast_to(col, (n_cls, LANES)).T[0:1]
        frow = lax.broadcasted_iota(I32, fill_vmem.shape, 0)
        fill_vmem[...] = jnp.where(frow == 0, as_row(off * float(tile) + counts),
                                   jnp.where(frow == 1, as_row(tiles * float(tile) - counts), n_used)).astype(I32)
        to_fill = pltpu.make_async_copy(fill_vmem, fill_smem, psem)
        to_fill.start()
        to_fill.wait()
        zero_fill(start=True)

    @pl.when(ps == 1)
    def _():
        upper = lax.broadcasted_iota(I32, (pb, pb), 0) < lax.broadcasted_iota(I32, (pb, pb), 1)
        upper = jnp.where(upper, 1.0, 0.0).astype(BF16)
        before = _dot(onehot.astype(BF16), upper)
        ppos = jnp.sum(onehot * (before + carry_ref[...] + off_ref[...]), axis=0, keepdims=True).astype(I32)
        carry_ref[...] += jnp.sum(onehot, axis=1, keepdims=True)
        gb = pos_ref.shape[2]
        for h in range(pos_ref.shape[0]):
            pos_ref[h] = ppos[:, h * gb:(h + 1) * gb]
        ppos_vmem[...] = ppos
        to_smem = pltpu.make_async_copy(ppos_vmem, ppos_smem, psem)
        to_smem.start()

        slot = b % 2
        stage = stage_ref.at[slot]

        def drain(s):
            pltpu.make_async_copy(stage_ref.at[s], xs_ref.at[pl.ds(0, pb * X_ROWS)], ssem.at[s]).wait()

        @pl.when(b >= 2)
        def _():
            drain(slot)

        for j in range(Y_ROWS):
            _rows_of(stage, j, pb, X_ROWS)[...] = x_ref[:, j * LANES:(j + 1) * LANES]
        record = jnp.concatenate([route_ref[...], jnp.zeros((LANES - ROUTE_ROWS, pb), F32)], axis=0).T
        _rows_of(stage, Y_ROWS, pb, X_ROWS)[...] = record

        to_smem.wait()
        _start_token_moves(pb, lambda r: pltpu.make_async_copy(
            _tok(stage, r, X_ROWS), _tok(xs_ref, ppos_smem[0, r], X_ROWS), ssem.at[slot]))

        @pl.when(b == nb - 1)
        def _():
            drain(slot)

            @pl.when(nb > 1)
            def _():
                drain(1 - slot)

            zero_fill(start=False)


def _moe_kernel(tcls_ref, nused_ref, xs_ref, win_ref, wout_ref, ys_ref):
    i = pl.program_id(0)
    tm = xs_ref.shape[0] // (X_ROWS * MOE_STEP_TILES)
    ff = wout_ref.shape[1]

    @pl.when(i * MOE_STEP_TILES < nused_ref[0])
    def _():
        for jt in range(MOE_STEP_TILES):
            cls = tcls_ref[i * MOE_STEP_TILES + jt]
            x0, y0 = jt * tm * X_ROWS, jt * tm * Y_ROWS
            xb = jnp.concatenate([_rows_of(xs_ref, x0 + j, tm, X_ROWS)[...].astype(BF16) for j in range(Y_ROWS)],
                                 axis=1)
            record = _rows_of(xs_ref, x0 + Y_ROWS, tm, X_ROWS)[...]
            y = None
            for e, gate in (((cls >> 3) & 7, record[:, 1:2]), (cls & 7, record[:, 2:3])):
                h = _dot(xb, win_ref[e])
                a = h[:, :ff]
                act = (a * jax.nn.sigmoid(a) * h[:, ff:] * gate).astype(BF16)
                ye = _dot(act, wout_ref[e])
                y = ye if y is None else y + ye
            for j in range(Y_ROWS):
                _rows_of(ys_ref, y0 + j, tm, Y_ROWS)[...] = y[:, j * LANES:(j + 1) * LANES]

    @pl.when(i * MOE_STEP_TILES >= nused_ref[0])
    def _():
        ys_ref[...] = jnp.zeros(ys_ref.shape, F32)


def _post_kernel(pos_ref, ys_ref, x1_ref, p_ref, ln_g_ref, ln_b_ref, pw_ref, gw_ref, gb_ref, out_ref,
                 idx_smem, ybuf, isem, rsem, *, alpha, n_blocks):
    i = pl.program_id(0)
    tm = x1_ref.shape[0]
    gather = _TokenGather(i, n_blocks, pos_ref, ys_ref, idx_smem, ybuf, isem, rsem)
    gather.arrive()
    gather.prefetch()
    yb = ybuf.at[i % 2]
    ffn = jnp.concatenate([_rows_of(yb, j, tm, Y_ROWS)[...] for j in range(Y_ROWS)], axis=1)
    x2 = _layer_norm(alpha * x1_ref[...] + ffn, ln_g_ref[...], ln_b_ref[...])
    gate = jax.nn.sigmoid(_dot(x2.astype(BF16), gw_ref[...]) + gb_ref[...])
    out_ref[...] = x2 + _dot(p_ref[...].astype(BF16), pw_ref[...]) * gate
    gather.finish()


def _moe_and_post(x1, route, layer, p, e_w_in, e_w_out, ln_g, ln_b, ple_w, ple_gw, ple_gb, alpha):
    t, d = x1.shape
    assert d == LANES * Y_ROWS
    depth, n_exp, _, ff2 = e_w_in.shape
    epg = n_exp // N_EGROUPS
    pb = _tile(t, PLAN_BLOCK)
    nb = t // pb
    gb = _tile(pb, GATHER_BLOCK)
    nt = t // MOE_TILE + N_EGROUPS * (epg * (epg - 1) // 2) + N_EGROUPS * (MOE_STEP_TILES - 1)
    nt = -(-nt // MOE_STEP_TILES) * MOE_STEP_TILES
    kt = -(-nt // LANES) * LANES
    n_sorted = nt * MOE_TILE

    pos, tinfo, xs = pl.pallas_call(
        functools.partial(_plan_kernel, tile=MOE_TILE),
        grid=(2, nb),
        in_specs=[pl.BlockSpec((ROUTE_ROWS, pb), lambda ps, b: (0, b)),
                  pl.BlockSpec((pb, d), lambda ps, b: (ps * b, 0))],
        out_specs=[pl.BlockSpec((pb // gb, 1, gb), lambda ps, b: (ps * b, 0, 0)),
                   pl.BlockSpec((ROUTE_ROWS, kt), lambda ps, b: (0, 0)),
                   pl.BlockSpec(memory_space=pl.ANY)],
        out_shape=[jax.ShapeDtypeStruct((t // gb, 1, gb), I32),
                   jax.ShapeDtypeStruct((ROUTE_ROWS, kt), I32),
                   jax.ShapeDtypeStruct((n_sorted * X_ROWS, LANES), F32)],
        scratch_shapes=[pltpu.VMEM((N_CLASS, pb), F32),
                        pltpu.VMEM((N_CLASS, 1), F32),
                        pltpu.VMEM((N_CLASS, 1), F32),
                        pltpu.VMEM((2, pb * X_ROWS, LANES), F32),
                        pltpu.VMEM((1, pb), I32),
                        pltpu.SMEM((1, pb), I32),
                        pltpu.VMEM((ROUTE_ROWS, N_CLASS), I32),
                        pltpu.SMEM((ROUTE_ROWS, N_CLASS), I32),
                        pltpu.VMEM((MOE_TILE * ZERO_TILES * X_ROWS, LANES), F32),
                        pltpu.SemaphoreType.DMA((2,)),
                        pltpu.SemaphoreType.DMA(()),
                        pltpu.SemaphoreType.DMA(())],
        compiler_params=_cparams(("arbitrary", "arbitrary")),
        name="plan",
    )(route, x1)

    w_in = e_w_in.reshape(depth, N_EGROUPS, epg, d, ff2)
    w_out = e_w_out.reshape(depth, N_EGROUPS, epg, ff2 // 2, d)
    step_rows = MOE_TILE * MOE_STEP_TILES

    def group(i, tile_cls):
        return tile_cls[i * MOE_STEP_TILES] >> 6
    ys = pl.pallas_call(
        _moe_kernel,
        grid_spec=pltpu.PrefetchScalarGridSpec(
            num_scalar_prefetch=2,
            grid=(nt // MOE_STEP_TILES,),
            in_specs=[pl.BlockSpec((step_rows * X_ROWS, LANES), lambda i, tc, nu: (i, 0)),
                      pl.BlockSpec((None, None, epg, d, ff2), lambda i, tc, nu: (layer, group(i, tc), 0, 0, 0),
                                   pipeline_mode=pl.Buffered(1)),
                      pl.BlockSpec((None, None, epg, ff2 // 2, d), lambda i, tc, nu: (layer, group(i, tc), 0, 0, 0),
                                   pipeline_mode=pl.Buffered(1))],
            out_specs=pl.BlockSpec((step_rows * Y_ROWS, LANES), lambda i, tc, nu: (i, 0))),
        out_shape=jax.ShapeDtypeStruct((n_sorted * Y_ROWS, LANES), F32),
        compiler_params=_cparams(("arbitrary",)),
        name="moe",
    )(tinfo[0, :nt], tinfo[1, :1], xs, w_in, w_out)

    post_args = (pos, ys, x1, p, ln_g[None], ln_b[None], ple_w.astype(BF16), ple_gw.astype(BF16), ple_gb[None])
    return pl.pallas_call(
        functools.partial(_post_kernel, alpha=alpha, n_blocks=t // gb),
        grid=(t // gb,),
        in_specs=[pl.BlockSpec(memory_space=pl.ANY),
                  pl.BlockSpec(memory_space=pl.ANY),
                  pl.BlockSpec((gb, d), lambda i: (i, 0)),
                  pl.BlockSpec((None, gb, p.shape[2]), lambda i: (layer, i, 0))]
                 + [_full(a.shape) for a in post_args[4:]],
        out_specs=pl.BlockSpec((gb, d), lambda i: (i, 0)),
        out_shape=jax.ShapeDtypeStruct((t, d), F32),
        scratch_shapes=[pltpu.SMEM((2, 1, gb), I32), pltpu.VMEM((2, gb * Y_ROWS, LANES), F32),
                        pltpu.SemaphoreType.DMA((2,)), pltpu.SemaphoreType.DMA((2,))],
        compiler_params=_cparams(("arbitrary",)),
        name="post",
    )(*post_args)


def _router_params(wg, bg, we, be):
    d = wg.shape[0]
    w = jnp.zeros((d, ROUTE_COLS), F32).at[:, :N_EGROUPS].set(wg).at[:, 8:8 + we.shape[1]].set(we)
    b = jnp.zeros((1, ROUTE_COLS), F32).at[0, :N_EGROUPS].set(bg).at[0, 8:8 + be.shape[0]].set(be)
    hi = w.astype(BF16)
    return hi, (w - hi.astype(F32)).astype(BF16), b


def _tile(t, want):
    while t % want:
        want //= 2
    return want


def kernel(x, p, a_w_in, a_b_in, a_vn_g, a_vn_b, a_w_s, a_b_s, a_w_out, b_w_in, b_kv_g, b_w_uk, b_w_uv, b_w_out,
           ln1_g, ln1_b, ln2_g, ln2_b, r_wg, r_bg, r_we, r_be, e_w_in, e_w_out, ple_w, ple_gw, ple_gb):
    batch, seq, d = x.shape
    t = batch * seq
    depth = p.shape[0]
    alpha = (2 * depth) ** 0.25
    chunk = a_w_s.shape[-1]
    xt = x.reshape(t, d)
    pt = p.reshape(depth, t, p.shape[-1])
    for i in range(depth):
        j = i // 2
        router = _router_params(r_wg[i], r_bg[i], r_we[i], r_be[i])
        if i % 2 == 0:
            x1, route = _gmlp_layer(xt, a_w_in[j], a_b_in[j], a_vn_g[j], a_vn_b[j], a_w_s[j], a_b_s[j],
                                    a_w_out[j], ln1_g[i], ln1_b[i], router, alpha, max(chunk, _tile(t, 512)))
        else:
            x1, route = _dsa_layer(xt, batch, b_w_in[j], b_kv_g[j], b_w_uk[j], b_w_uv[j], b_w_out[j],
                                   ln1_g[i], ln1_b[i], router, alpha, _tile(t, 512))
        xt = _moe_and_post(x1, route, i, pt, e_w_in, e_w_out, ln2_g[i], ln2_b[i],
                           ple_w[i], ple_gw[i], ple_gb[i], alpha)
    return xt.reshape(batch, seq, d)
```

```python
import functools

import jax
import jax.numpy as jnp
from jax import lax
from jax.experimental import pallas as pl
from jax.experimental.pallas import tpu as pltpu

F32, BF16, I32 = jnp.float32, jnp.bfloat16, jnp.int32

IDX_HEADS = 8
IDX_DIM = 64
TOPK_MAX = 256
N_EGROUPS = 4
LN_EPS = 1e-5
RMS_EPS = 1e-6

LANES = 128
MIXER_TILE = 1024
Q_BLOCK = 128
KEY_CHUNK = 512
COUNT_SLAB = 1024
COUNT_ACC = 64
LOG2E = 1.4426950408889634
VMEM_LIMIT = 48 * 1024 * 1024

ROUTE_COLS = LANES
ROUTE_ROWS = 8
N_CLASS = 64 * N_EGROUPS
Y_ROWS = 8
X_ROWS = Y_ROWS + 1
PLAN_BLOCK = 1024
GATHER_BLOCK = 512
MOE_TILE = 128
MOE_STEP_TILES = 4
ZERO_TILES = 1 << ((MOE_STEP_TILES - 1).bit_length() - 1) if MOE_STEP_TILES > 1 else 1
INT_MIN = -(2 ** 31)
NEG_BIG = -1e30


def _cparams(sem, vmem=VMEM_LIMIT):
    return pltpu.CompilerParams(dimension_semantics=sem, vmem_limit_bytes=vmem)


def _full(shape):
    n = len(shape)
    return pl.BlockSpec(shape, lambda *_: (0,) * n)


def _dot(a, b):
    return jnp.dot(a, b, preferred_element_type=F32)


def _dot_t(a, b):
    return lax.dot_general(a, b, (((1,), (1,)), ((), ())), preferred_element_type=F32)


def _layer_norm(x, g, b):
    mu = jnp.mean(x, axis=-1, keepdims=True)
    xc = x - mu
    var = jnp.mean(xc * xc, axis=-1, keepdims=True)
    return xc * lax.rsqrt(var + LN_EPS) * g + b


def _route_rows(lg_t):
    g = [lg_t[i:i + 1] for i in range(N_EGROUPS)]
    gmax = functools.reduce(jnp.maximum, g)
    gsel = jnp.where(g[0] >= gmax, 0.0, jnp.where(g[1] >= gmax, 1.0, jnp.where(g[2] >= gmax, 2.0, 3.0)))
    den = functools.reduce(lambda a, b: a + b, [jnp.exp(gi - gmax) for gi in g])
    p_g = 1.0 / den
    el = lg_t[8:16]
    for gi in range(1, N_EGROUPS):
        el = jnp.where(gsel == float(gi), lg_t[8 + 8 * gi:16 + 8 * gi], el)
    eidx = lax.broadcasted_iota(I32, el.shape, 0).astype(F32)
    m1 = jnp.max(el, axis=0, keepdims=True)
    i1 = jnp.min(jnp.where(el == m1, eidx, 8.0), axis=0, keepdims=True)
    el2 = jnp.where(eidx == i1, -jnp.inf, el)
    m2 = jnp.max(el2, axis=0, keepdims=True)
    i2 = jnp.min(jnp.where(el2 == m2, eidx, 8.0), axis=0, keepdims=True)
    r = jnp.exp(m2 - m1)
    inv = 1.0 / (1.0 + r)
    gate1 = p_g * inv
    gate2 = p_g * r * inv
    first_lo = i1 < i2
    lo = jnp.minimum(i1, i2)
    hi = jnp.maximum(i1, i2)
    cls = gsel * 64.0 + lo * 8.0 + hi
    return cls, jnp.where(first_lo, gate1, gate2), jnp.where(first_lo, gate2, gate1)


def _norm_and_route(y, ln_g, ln_b, wr_hi, wr_lo, br, x1_ref, route_ref):
    tm, _ = y.shape
    x1 = _layer_norm(y, ln_g, ln_b)
    x1_ref[...] = x1
    x_hi = x1.astype(BF16)
    x_lo = (x1 - x_hi.astype(F32)).astype(BF16)
    lg = _dot(x_hi, wr_hi) + _dot(x_lo, wr_hi) + _dot(x_hi, wr_lo) + br
    cls, g_lo, g_hi = _route_rows(lg.T)
    row = lax.broadcasted_iota(I32, (ROUTE_ROWS, tm), 0)
    route_ref[...] = jnp.where(row == 0, cls, jnp.where(row == 1, g_lo, jnp.where(row == 2, g_hi, 0.0)))


def _mixer_out_specs(tm, d):
    return [pl.BlockSpec((tm, d), lambda i: (i, 0)), pl.BlockSpec((ROUTE_ROWS, tm), lambda i: (0, i))]


def _mixer_out_shape(t, d):
    return [jax.ShapeDtypeStruct((t, d), F32), jax.ShapeDtypeStruct((ROUTE_ROWS, t), F32)]


def _gmlp_kernel(x_ref, w_in_ref, b_in_ref, vn_g_ref, vn_b_ref, w_s_ref, b_st_ref, w_out_ref,
                 ln_g_ref, ln_b_ref, wr_hi_ref, wr_lo_ref, br_ref, x1_ref, route_ref, gated_ref, *, alpha):
    x = x_ref[...]
    tm, _ = x.shape
    groups, chunk, _ = w_s_ref.shape
    z = _dot(x.astype(BF16), w_in_ref[...]) + b_in_ref[...]
    z = 0.5 * z * (1.0 + lax.erf(z * (2.0 ** -0.5)))
    half = z.shape[1] // 2
    gd = half // groups
    u = z[:, :half]
    v = _layer_norm(z[:, half:], vn_g_ref[...], vn_b_ref[...]).astype(BF16)
    r = lax.broadcasted_iota(I32, (chunk, chunk), 0)
    c = lax.broadcasted_iota(I32, (chunk, chunk), 1)
    causal = r >= c
    for g in range(groups):
        w_c = jnp.where(causal, w_s_ref[g], 0.0).astype(BF16)
        bias = b_st_ref[:, g:g + 1]
        for ci in range(tm // chunk):
            rows = slice(ci * chunk, (ci + 1) * chunk)
            cols = slice(g * gd, (g + 1) * gd)
            s = _dot(w_c, v[rows, cols]) + bias
            gated_ref[rows, cols] = (u[rows, cols] * s).astype(BF16)
    mix = _dot(gated_ref[...], w_out_ref[...])
    _norm_and_route(alpha * x + mix, ln_g_ref[...], ln_b_ref[...], wr_hi_ref[...], wr_lo_ref[...],
                    br_ref[...], x1_ref, route_ref)


def _gmlp_layer(xt, w_in, b_in, vn_g, vn_b, w_s, b_s, w_out, ln_g, ln_b, router, alpha, tm):
    t, d = xt.shape
    half = w_out.shape[0]
    wr_hi, wr_lo, br = router
    args = (xt, w_in.astype(BF16), b_in[None], vn_g[None], vn_b[None], w_s, b_s.T, w_out.astype(BF16),
            ln_g[None], ln_b[None], wr_hi, wr_lo, br)
    in_specs = [pl.BlockSpec((tm, d), lambda i: (i, 0))] + [_full(a.shape) for a in args[1:]]
    return pl.pallas_call(
        functools.partial(_gmlp_kernel, alpha=alpha),
        grid=(t // tm,),
        in_specs=in_specs,
        out_specs=_mixer_out_specs(tm, d),
        out_shape=_mixer_out_shape(t, d),
        scratch_shapes=[pltpu.VMEM((tm, half), BF16)],
        compiler_params=_cparams(("parallel",)),
        name="gmlp",
    )(*args)


def _dsa_proj_kernel(x_ref, wq_ref, wc_ref, wqi_ref, wkw_ref, kvg_ref, wuk_t_ref,
                     qlat_ref, ckv_ref, ckv_t_ref, qidx_ref, kidx_ref, wi_t_ref, *, scale_q, scale_w):
    xb = x_ref[...].astype(BF16)
    nq = qlat_ref.shape[0]
    kc = ckv_t_ref.shape[2]
    heads, hd, _ = wuk_t_ref.shape
    q = _dot(xb, wq_ref[...])
    for h in range(heads):
        ql = (_dot(q[:, h * hd:(h + 1) * hd].astype(BF16), wuk_t_ref[h]) * scale_q).astype(BF16)
        for b in range(nq):
            qlat_ref[b, h * Q_BLOCK:(h + 1) * Q_BLOCK, :] = ql[b * Q_BLOCK:(b + 1) * Q_BLOCK]
    c = _dot(xb, wc_ref[...])
    ms = jnp.mean(c * c, axis=-1, keepdims=True)
    c = c * lax.rsqrt(ms + RMS_EPS) * kvg_ref[...]
    ckv_ref[...] = c.astype(BF16)
    for b in range(ckv_t_ref.shape[0]):
        ckv_t_ref[b] = c[b * kc:(b + 1) * kc].T.astype(BF16)
    qi = _dot(xb, wqi_ref[...]).astype(BF16)
    for h in range(IDX_HEADS):
        for b in range(nq):
            qidx_ref[b, h * Q_BLOCK:(h + 1) * Q_BLOCK, :] = qi[b * Q_BLOCK:(b + 1) * Q_BLOCK,
                                                               h * IDX_DIM:(h + 1) * IDX_DIM]
    kw = _dot(xb, wkw_ref[...])
    kidx_ref[...] = kw[:, :IDX_DIM].astype(BF16)
    wi_t_ref[...] = kw.T[IDX_DIM:IDX_DIM + IDX_HEADS] * scale_w


def _dsa_attn_kernel(qidx_ref, wi_t_ref, qlat_ref, kidx_ref, ckv_ref, ckv_t_ref, o_ref,
                     key_ref, key16_ref, acc_ref, m_ref, l_ref, a_ref, p_ref, *, k_sel, heads):
    qb = pl.program_id(1)
    kc = ckv_t_ref.shape[2]
    slab = key_ref.shape[1]
    cps = slab // kc
    nq = Q_BLOCK
    pair = 2 * nq
    n_pairs = heads // 2
    n_chunks = (qb * nq + nq + kc - 1) // kc
    n_slabs = (n_chunks + cps - 1) // cps

    def key_chunk(j):
        return key_ref.at[j // cps, pl.ds(pl.multiple_of((j % cps) * kc, kc), kc), :]

    q_pos = qb * nq + lax.broadcasted_iota(I32, (kc, nq), 1)
    k_off = lax.broadcasted_iota(I32, (kc, nq), 0)

    def key16_chunk(j):
        return key16_ref.at[j // cps, pl.ds(pl.multiple_of((j % cps) * kc, kc), kc), :]

    def pad_chunk(j, carry):
        key_chunk(j)[...] = jnp.full((kc, nq), jnp.nan, F32)
        key16_chunk(j)[...] = jnp.full((kc, nq), jnp.nan, BF16)
        return carry

    lax.fori_loop(n_chunks, n_slabs * cps, pad_chunk, 0)

    def score_chunk(j, carry):
        kch = kidx_ref[pl.ds(pl.multiple_of(j * kc, kc), kc), :]
        tot = None
        for p in range(IDX_HEADS // 2):
            sc = _dot_t(kch, qidx_ref[p * pair:(p + 1) * pair, :])
            for hh in range(2):
                h = 2 * p + hh
                r = jnp.maximum(sc[:, hh * nq:(hh + 1) * nq], 0.0) * wi_t_ref[h:h + 1, :]
                tot = r if tot is None else tot + r
        tot = jnp.where(k_off + j * kc <= q_pos, tot, jnp.nan)
        key_chunk(j)[...] = tot
        key16_chunk(j)[...] = tot.astype(BF16)
        return carry

    lax.fori_loop(0, n_chunks, score_chunk, 0)

    def count(t, strict=False):
        tb = jnp.broadcast_to(t, (COUNT_ACC, nq))

        def body(s, acc):
            for g in range(slab // COUNT_ACC):
                blk = key_ref[s, g * COUNT_ACC:(g + 1) * COUNT_ACC, :]
                acc = jnp.where(blk > tb if strict else blk >= tb, acc + 1.0, acc)
            return acc

        acc = lax.fori_loop(0, n_slabs, body, jnp.zeros((COUNT_ACC, nq), F32))
        return jnp.sum(acc, axis=0, keepdims=True)

    def count16(t):
        rows16 = 2 * COUNT_ACC
        tb = jnp.broadcast_to(t.astype(BF16), (rows16, nq))

        def body(s, acc):
            for g in range(slab // rows16):
                blk = key16_ref[s, g * rows16:(g + 1) * rows16, :]
                acc = jnp.where(blk >= tb, acc + 1.0, acc)
            return acc

        acc = lax.fori_loop(0, n_slabs, body, jnp.zeros((rows16, nq), BF16))
        return jnp.sum(acc.astype(F32), axis=0, keepdims=True)

    def decode(code):
        return lax.bitcast_convert_type(code ^ ((code >> 31) & 0x7FFFFFFF), F32)

    def decode16(c16):
        return decode(jnp.left_shift(c16, 16) | jnp.where(c16 < 0, 0xFFFF, 0))

    def bit16_step(b, c16):
        t = c16 + jnp.left_shift(jnp.int32(1), 15 - b)
        return jnp.where(count16(decode16(t)) >= float(k_sel), t, c16)

    no_code = -(2 ** 15)
    c16 = lax.fori_loop(0, 16, bit16_step, jnp.full((1, nq), no_code, I32))

    def bit_step(b, c):
        t = c + jnp.left_shift(jnp.int32(1), 17 - b)
        return jnp.where((count(decode(t)) >= float(k_sel)) & (t > c), t, c)

    c_sel = lax.fori_loop(0, 18, bit_step, jnp.left_shift(jnp.maximum(c16, no_code + 1) - 1, 16))
    c_sel = jnp.where(c16 == no_code, -jnp.inf, decode(c_sel))
    n_ge = jnp.where(c16 == no_code, 0.0, count(c_sel))

    has_tie = jnp.max(n_ge) > float(k_sel)

    @pl.when(has_tie)
    def _():
        need = float(k_sel) - count(c_sel, strict=True)
        tie_col = n_ge > float(k_sel)
        cb = jnp.broadcast_to(c_sel, (kc, nq))
        lower = (lax.broadcasted_iota(I32, (kc, kc), 1) < lax.broadcasted_iota(I32, (kc, kc), 0))
        lower = jnp.where(lower, 1.0, 0.0).astype(BF16)

        def fix(j, seen):
            blk = key_chunk(j)[...]
            eq = blk == cb
            eq_f = jnp.where(eq, 1.0, 0.0)
            rank = seen + _dot(lower, eq_f.astype(BF16))
            drop = eq & tie_col & (rank >= need)
            key_chunk(j)[...] = jnp.where(drop, jnp.nan, blk)
            return seen + jnp.sum(eq_f, axis=0, keepdims=True)

        lax.fori_loop(0, n_chunks, fix, jnp.zeros((1, nq), F32))

    thr = jnp.broadcast_to(c_sel, (kc, nq))
    m_ref[...] = jnp.full(m_ref.shape, NEG_BIG, F32)
    l_ref[...] = jnp.zeros(l_ref.shape, F32)
    acc_ref[...] = jnp.zeros(acc_ref.shape, F32)

    def softmax_chunk(j):
        ck = ckv_ref[pl.ds(pl.multiple_of(j * kc, kc), kc), :]
        bias = jnp.where(key_chunk(j)[...] >= thr, 0.0, NEG_BIG)
        bias = jnp.concatenate([bias, bias], axis=1)
        slot = j % 2
        for p in range(n_pairs):
            lg = _dot_t(ck, qlat_ref[p * pair:(p + 1) * pair, :]) + bias
            m_old = m_ref[p]
            m_new = jnp.maximum(m_old, jnp.max(lg, axis=0, keepdims=True))
            pr = jnp.exp2(lg - m_new)
            alpha = jnp.exp2(m_old - m_new)
            l_ref[p] = alpha * l_ref[p] + jnp.sum(pr, axis=0, keepdims=True)
            m_ref[p] = m_new
            a_ref[slot, p] = alpha
            p_ref[slot, p] = pr.astype(BF16)

    def accumulate_chunk(j):
        ck_t = ckv_t_ref[j]
        slot = j % 2
        for p in range(n_pairs):
            acc_ref[p] = a_ref[slot, p] * acc_ref[p] + _dot(ck_t, p_ref[slot, p])

    def att_step(j, carry):
        accumulate_chunk(j - 1)
        softmax_chunk(j)
        return carry

    softmax_chunk(0)
    lax.fori_loop(1, n_chunks, att_step, 0)
    accumulate_chunk(n_chunks - 1)
    for p in range(n_pairs):
        o_ref[p * pair:(p + 1) * pair, :] = (acc_ref[p] * (1.0 / l_ref[p])).T.astype(o_ref.dtype)


def _dsa_out_kernel(o_ref, x_ref, wuv_ref, wout_ref, ln_g_ref, ln_b_ref, wr_hi_ref, wr_lo_ref, br_ref,
                    x1_ref, route_ref, o2_ref, *, alpha):
    nq = o_ref.shape[0]
    heads, _, hd = wuv_ref.shape
    for b in range(nq):
        for h in range(heads):
            oh = o_ref[b, h * Q_BLOCK:(h + 1) * Q_BLOCK, :]
            o2_ref[b * Q_BLOCK:(b + 1) * Q_BLOCK, h * hd:(h + 1) * hd] = _dot(oh, wuv_ref[h]).astype(BF16)
    mix = _dot(o2_ref[...], wout_ref[...])
    _norm_and_route(alpha * x_ref[...] + mix, ln_g_ref[...], ln_b_ref[...], wr_hi_ref[...], wr_lo_ref[...],
                    br_ref[...], x1_ref, route_ref)


def _dsa_layer(xt, batch, w_in, kv_g, w_uk, w_uv, w_out, ln_g, ln_b, router, alpha, tm):
    t, d = xt.shape
    seq = t // batch
    heads, lat, hd = w_uk.shape
    qw = heads * hd
    nqb = t // Q_BLOCK
    nq = tm // Q_BLOCK
    k_sel = min(TOPK_MAX, seq // 4)
    kc = min(KEY_CHUNK, seq)
    slab = min(COUNT_SLAB, seq)
    w_in = w_in.astype(BF16)
    o1, o2, o3 = qw, qw + lat, qw + lat + IDX_HEADS * IDX_DIM
    wkw = jnp.pad(w_in[:, o3:], ((0, 0), (0, LANES - (w_in.shape[1] - o3))))
    proj_args = (xt, w_in[:, :o1], w_in[:, o1:o2], w_in[:, o2:o3], wkw, kv_g[None],
                 jnp.swapaxes(w_uk, 1, 2).astype(BF16))
    qlat, ckv, ckv_t, qidx, kidx, wi_t = pl.pallas_call(
        functools.partial(_dsa_proj_kernel, scale_q=hd ** -0.5 * LOG2E, scale_w=(IDX_HEADS * IDX_DIM) ** -0.5),
        grid=(t // tm,),
        in_specs=[pl.BlockSpec((tm, d), lambda i: (i, 0))] + [_full(a.shape) for a in proj_args[1:]],
        out_specs=[pl.BlockSpec((nq, heads * Q_BLOCK, lat), lambda i: (i, 0, 0)),
                   pl.BlockSpec((tm, lat), lambda i: (i, 0)),
                   pl.BlockSpec((tm // kc, lat, kc), lambda i: (i, 0, 0)),
                   pl.BlockSpec((nq, IDX_HEADS * Q_BLOCK, IDX_DIM), lambda i: (i, 0, 0)),
                   pl.BlockSpec((tm, IDX_DIM), lambda i: (i, 0)),
                   pl.BlockSpec((IDX_HEADS, tm), lambda i: (0, i))],
        out_shape=[jax.ShapeDtypeStruct((nqb, heads * Q_BLOCK, lat), BF16),
                   jax.ShapeDtypeStruct((t, lat), BF16),
                   jax.ShapeDtypeStruct((t // kc, lat, kc), BF16),
                   jax.ShapeDtypeStruct((nqb, IDX_HEADS * Q_BLOCK, IDX_DIM), BF16),
                   jax.ShapeDtypeStruct((t, IDX_DIM), BF16),
                   jax.ShapeDtypeStruct((IDX_HEADS, t), F32)],
        compiler_params=_cparams(("parallel",)),
        name="dsa_proj",
    )(*proj_args)

    nq_seq = seq // Q_BLOCK
    n_kc = seq // kc
    rows = heads * Q_BLOCK
    o = pl.pallas_call(
        functools.partial(_dsa_attn_kernel, k_sel=k_sel, heads=heads),
        grid=(batch, nq_seq),
        in_specs=[pl.BlockSpec((None, IDX_HEADS * Q_BLOCK, IDX_DIM), lambda b, q: (b * nq_seq + q, 0, 0)),
                  pl.BlockSpec((IDX_HEADS, Q_BLOCK), lambda b, q: (0, b * nq_seq + q)),
                  pl.BlockSpec((None, rows, lat), lambda b, q: (b * nq_seq + q, 0, 0)),
                  pl.BlockSpec((None, seq, IDX_DIM), lambda b, q: (b, 0, 0)),
                  pl.BlockSpec((None, seq, lat), lambda b, q: (b, 0, 0)),
                  pl.BlockSpec((None, n_kc, lat, kc), lambda b, q: (b, 0, 0, 0))],
        out_specs=pl.BlockSpec((None, rows, lat), lambda b, q: (b * nq_seq + q, 0, 0)),
        out_shape=jax.ShapeDtypeStruct((nqb, rows, lat), BF16),
        scratch_shapes=[pltpu.VMEM((seq // slab, slab, Q_BLOCK), F32),
                        pltpu.VMEM((seq // slab, slab, Q_BLOCK), BF16),
                        pltpu.VMEM((heads // 2, lat, 2 * Q_BLOCK), F32),
                        pltpu.VMEM((heads // 2, 1, 2 * Q_BLOCK), F32),
                        pltpu.VMEM((heads // 2, 1, 2 * Q_BLOCK), F32),
                        pltpu.VMEM((2, heads // 2, 1, 2 * Q_BLOCK), F32),
                        pltpu.VMEM((2, heads // 2, kc, 2 * Q_BLOCK), BF16)],
        compiler_params=_cparams(("parallel", "arbitrary")),
        name="dsa_attn",
    )(qidx, wi_t, qlat, kidx.reshape(batch, seq, IDX_DIM), ckv.reshape(batch, seq, lat),
      ckv_t.reshape(batch, n_kc, lat, kc))

    wr_hi, wr_lo, br = router
    out_args = (o, xt, w_uv.astype(BF16), w_out.astype(BF16), ln_g[None], ln_b[None], wr_hi, wr_lo, br)
    return pl.pallas_call(
        functools.partial(_dsa_out_kernel, alpha=alpha),
        grid=(t // tm,),
        in_specs=[pl.BlockSpec((nq, rows, lat), lambda i: (i, 0, 0)),
                  pl.BlockSpec((tm, d), lambda i: (i, 0))] + [_full(a.shape) for a in out_args[2:]],
        out_specs=_mixer_out_specs(tm, d),
        out_shape=_mixer_out_shape(t, d),
        scratch_shapes=[pltpu.VMEM((tm, qw), BF16)],
        compiler_params=_cparams(("parallel",)),
        name="dsa_out",
    )(*out_args)


DMA_UNROLL = 8


def _start_token_moves(n, copy_of):
    def body(g, carry):
        for u in range(DMA_UNROLL):
            copy_of(g * DMA_UNROLL + u).start(priority=u % 2)
        return carry
    lax.fori_loop(0, n // DMA_UNROLL, body, 0)


def _tok(ref, i, rows):
    return ref.at[pl.ds(i * rows, rows)]


def _rows_of(ref, j, n, rows):
    return ref.at[pl.ds(j, n, stride=rows), :]


class _TokenGather:
    def __init__(self, i, n, idx_hbm, src_hbm, idx_smem, buf, isem, rsem):
        self.i, self.n = i, n
        self.idx_hbm, self.src_hbm, self.idx_smem, self.buf, self.isem, self.rsem = (
            idx_hbm, src_hbm, idx_smem, buf, isem, rsem)
        self.tm = idx_smem.shape[2]
        self.rows = buf.shape[1] // self.tm
        self.slot = i % 2

    def _idx_copy(self, blk, s):
        return pltpu.make_async_copy(self.idx_hbm.at[blk], self.idx_smem.at[s], self.isem.at[s])

    def _token_copy(self, s, r):
        return pltpu.make_async_copy(_tok(self.src_hbm, self.idx_smem[s, 0, r], self.rows),
                                     _tok(self.buf.at[s], r, self.rows), self.rsem.at[s])

    def _wait_tokens(self, s):
        pltpu.make_async_copy(self.src_hbm.at[pl.ds(0, self.tm * self.rows)], self.buf.at[s], self.rsem.at[s]).wait()

    def arrive(self):
        @pl.when(self.i == 0)
        def _():
            first = self._idx_copy(0, 0)
            first.start()
            first.wait()
            _start_token_moves(self.tm, lambda r: self._token_copy(0, r))
            self._idx_copy(min(1, self.n - 1), 1).start()

        self._idx_copy(jnp.minimum(self.i + 1, self.n - 1), 1 - self.slot).wait()
        self._wait_tokens(self.slot)

    def prefetch(self):
        nslot = 1 - self.slot
        for r in range(self.tm):
            self._token_copy(nslot, r).start(priority=r % 2)
        self._idx_copy(jnp.minimum(self.i + 2, self.n - 1), self.slot).start()

    def finish(self):
        @pl.when(self.i == self.n - 1)
        def _():
            self._wait_tokens(1 - self.slot)
            self._idx_copy(self.n - 1, self.slot).wait()


def _plan_kernel(route_ref, x_ref, pos_ref, tinfo_ref, xs_ref,
                 hist_ref, off_ref, carry_ref, stage_ref, ppos_vmem, ppos_smem, fill_vmem, fill_smem, zero_ref,
                 ssem, psem, zsem, *, tile):
    ps = pl.program_id(0)
    b = pl.program_id(1)
    nb = pl.num_programs(1)
    n_cls, pb = hist_ref.shape
    kt = tinfo_ref.shape[1]
    cls = route_ref[0:1, :].astype(I32)
    cid = lax.broadcasted_iota(I32, (n_cls, pb), 0)
    onehot = jnp.where(cid == cls, 1.0, 0.0)

    def zero_fill(start):
        def piece(tok0, n_tok):
            cp = pltpu.make_async_copy(zero_ref.at[pl.ds(0, n_tok * X_ROWS)],
                                       xs_ref.at[pl.ds(tok0 * X_ROWS, n_tok * X_ROWS)], zsem)
            cp.start() if start else cp.wait()

        def per_class(c, carry):
            first, n = fill_smem[0, c], fill_smem[1, c]
            bit = zero_ref.shape[0] // X_ROWS
            while bit:
                @pl.when((n & bit) != 0)
                def _(bit=bit):
                    piece(first + (n & -(2 * bit)), bit)
                bit //= 2
            return carry

        lax.fori_loop(0, n_cls, per_class, 0)

        def per_tile(k, carry):
            piece(k * tile, tile)
            return carry

        lax.fori_loop(fill_smem[2, 0], xs_ref.shape[0] // (tile * X_ROWS), per_tile, 0)

    @pl.when(ps == 0)
    def _():
        @pl.when(b == 0)
        def _():
            hist_ref[...] = jnp.zeros(hist_ref.shape, F32)
            zero_ref[...] = jnp.zeros(zero_ref.shape, F32)
        hist_ref[...] += onehot

    @pl.when((ps == 1) & (b == 0))
    def _():
        counts = jnp.sum(hist_ref[...], axis=1, keepdims=True)
        tiles = jnp.floor((counts + float(tile - 1)) * (1.0 / tile))
        cid_col = lax.broadcasted_iota(I32, (n_cls, 1), 0)
        per_group = n_cls // N_EGROUPS
        for g in range(N_EGROUPS):
            g_tiles = jnp.sum(tiles[g * per_group:(g + 1) * per_group], axis=0, keepdims=True)
            extra = jnp.ceil(g_tiles * (1.0 / MOE_STEP_TILES)) * float(MOE_STEP_TILES) - g_tiles
            tiles = jnp.where(cid_col == (g + 1) * per_group - 1, extra, tiles)
        lower = lax.broadcasted_iota(I32, (n_cls, n_cls), 1) < lax.broadcasted_iota(I32, (n_cls, n_cls), 0)
        lower = jnp.where(lower, 1.0, 0.0).astype(BF16)
        off = _dot(lower, jnp.broadcast_to(tiles, (n_cls, LANES)).astype(BF16))[:, :1]
        off_ref[...] = off * float(tile)
        carry_ref[...] = jnp.zeros(carry_ref.shape, F32)
        k = lax.broadcasted_iota(I32, (n_cls, kt), 1).astype(F32)
        cid_f = lax.broadcasted_iota(I32, (n_cls, kt), 0).astype(F32)
        mine = (k >= off) & (k < off + tiles)
        tcls = jnp.sum(jnp.where(mine, cid_f, 0.0), axis=0, keepdims=True)
        n_used = jnp.sum(tiles, axis=0, keepdims=True)
        last = jnp.max(jnp.where(tiles > 0.0, cid_f[:, :1], 0.0), axis=0, keepdims=True)
        tcls = jnp.where(k[:1] >= n_used, last, tcls)
        row = lax.broadcasted_iota(I32, (ROUTE_ROWS, kt), 0)
        tinfo_ref[...] = jnp.where(row == 0, tcls, jnp.where(row == 1, n_used, 0.0)).astype(I32)
        as_row = lambda col: jnp.broadcast_to(col, (n_cls, LANES)).T[0:1]
        frow = lax.broadcasted_iota(I32, fill_vmem.shape, 0)
        fill_vmem[...] = jnp.where(frow == 0, as_row(off * float(tile) + counts),
                                   jnp.where(frow == 1, as_row(tiles * float(tile) - counts), n_used)).astype(I32)
        to_fill = pltpu.make_async_copy(fill_vmem, fill_smem, psem)
        to_fill.start()
        to_fill.wait()
        zero_fill(start=True)

    @pl.when(ps == 1)
    def _():
        upper = lax.broadcasted_iota(I32, (pb, pb), 0) < lax.broadcasted_iota(I32, (pb, pb), 1)
        upper = jnp.where(upper, 1.0, 0.0).astype(BF16)
        before = _dot(onehot.astype(BF16), upper)
        ppos = jnp.sum(onehot * (before + carry_ref[...] + off_ref[...]), axis=0, keepdims=True).astype(I32)
        carry_ref[...] += jnp.sum(onehot, axis=1, keepdims=True)
        gb = pos_ref.shape[2]
        for h in range(pos_ref.shape[0]):
            pos_ref[h] = ppos[:, h * gb:(h + 1) * gb]
        ppos_vmem[...] = ppos
        to_smem = pltpu.make_async_copy(ppos_vmem, ppos_smem, psem)
        to_smem.start()

        slot = b % 2
        stage = stage_ref.at[slot]

        def drain(s):
            pltpu.make_async_copy(stage_ref.at[s], xs_ref.at[pl.ds(0, pb * X_ROWS)], ssem.at[s]).wait()

        @pl.when(b >= 2)
        def _():
            drain(slot)

        for j in range(Y_ROWS):
            _rows_of(stage, j, pb, X_ROWS)[...] = x_ref[:, j * LANES:(j + 1) * LANES]
        record = jnp.concatenate([route_ref[...], jnp.zeros((LANES - ROUTE_ROWS, pb), F32)], axis=0).T
        _rows_of(stage, Y_ROWS, pb, X_ROWS)[...] = record

        to_smem.wait()
        _start_token_moves(pb, lambda r: pltpu.make_async_copy(
            _tok(stage, r, X_ROWS), _tok(xs_ref, ppos_smem[0, r], X_ROWS), ssem.at[slot]))

        @pl.when(b == nb - 1)
        def _():
            drain(slot)

            @pl.when(nb > 1)
            def _():
                drain(1 - slot)

            zero_fill(start=False)


def _moe_kernel(tcls_ref, nused_ref, xs_ref, win_ref, wout_ref, ys_ref):
    i = pl.program_id(0)
    tm = xs_ref.shape[0] // (X_ROWS * MOE_STEP_TILES)
    ff = wout_ref.shape[1]

    @pl.when(i * MOE_STEP_TILES < nused_ref[0])
    def _():
        for jt in range(MOE_STEP_TILES):
            cls = tcls_ref[i * MOE_STEP_TILES + jt]
            x0, y0 = jt * tm * X_ROWS, jt * tm * Y_ROWS
            xb = jnp.concatenate([_rows_of(xs_ref, x0 + j, tm, X_ROWS)[...].astype(BF16) for j in range(Y_ROWS)],
                                 axis=1)
            record = _rows_of(xs_ref, x0 + Y_ROWS, tm, X_ROWS)[...]
            y = None
            for e, gate in (((cls >> 3) & 7, record[:, 1:2]), (cls & 7, record[:, 2:3])):
                h = _dot(xb, win_ref[e])
                a = h[:, :ff]
                act = (a * jax.nn.sigmoid(a) * h[:, ff:] * gate).astype(BF16)
                ye = _dot(act, wout_ref[e])
                y = ye if y is None else y + ye
            for j in range(Y_ROWS):
                _rows_of(ys_ref, y0 + j, tm, Y_ROWS)[...] = y[:, j * LANES:(j + 1) * LANES]

    @pl.when(i * MOE_STEP_TILES >= nused_ref[0])
    def _():
        ys_ref[...] = jnp.zeros(ys_ref.shape, F32)


def _post_kernel(pos_ref, ys_ref, x1_ref, p_ref, ln_g_ref, ln_b_ref, pw_ref, gw_ref, gb_ref, out_ref,
                 idx_smem, ybuf, isem, rsem, *, alpha, n_blocks):
    i = pl.program_id(0)
    tm = x1_ref.shape[0]
    gather = _TokenGather(i, n_blocks, pos_ref, ys_ref, idx_smem, ybuf, isem, rsem)
    gather.arrive()
    gather.prefetch()
    yb = ybuf.at[i % 2]
    ffn = jnp.concatenate([_rows_of(yb, j, tm, Y_ROWS)[...] for j in range(Y_ROWS)], axis=1)
    x2 = _layer_norm(alpha * x1_ref[...] + ffn, ln_g_ref[...], ln_b_ref[...])
    gate = jax.nn.sigmoid(_dot(x2.astype(BF16), gw_ref[...]) + gb_ref[...])
    out_ref[...] = x2 + _dot(p_ref[...].astype(BF16), pw_ref[...]) * gate
    gather.finish()


def _moe_and_post(x1, route, layer, p, e_w_in, e_w_out, ln_g, ln_b, ple_w, ple_gw, ple_gb, alpha):
    t, d = x1.shape
    assert d == LANES * Y_ROWS
    depth, n_exp, _, ff2 = e_w_in.shape
    epg = n_exp // N_EGROUPS
    pb = _tile(t, PLAN_BLOCK)
    nb = t // pb
    gb = _tile(pb, GATHER_BLOCK)
    nt = t // MOE_TILE + N_EGROUPS * (epg * (epg - 1) // 2) + N_EGROUPS * (MOE_STEP_TILES - 1)
    nt = -(-nt // MOE_STEP_TILES) * MOE_STEP_TILES
    kt = -(-nt // LANES) * LANES
    n_sorted = nt * MOE_TILE

    pos, tinfo, xs = pl.pallas_call(
        functools.partial(_plan_kernel, tile=MOE_TILE),
        grid=(2, nb),
        in_specs=[pl.BlockSpec((ROUTE_ROWS, pb), lambda ps, b: (0, b)),
                  pl.BlockSpec((pb, d), lambda ps, b: (ps * b, 0))],
        out_specs=[pl.BlockSpec((pb // gb, 1, gb), lambda ps, b: (ps * b, 0, 0)),
                   pl.BlockSpec((ROUTE_ROWS, kt), lambda ps, b: (0, 0)),
                   pl.BlockSpec(memory_space=pl.ANY)],
        out_shape=[jax.ShapeDtypeStruct((t // gb, 1, gb), I32),
                   jax.ShapeDtypeStruct((ROUTE_ROWS, kt), I32),
                   jax.ShapeDtypeStruct((n_sorted * X_ROWS, LANES), F32)],
        scratch_shapes=[pltpu.VMEM((N_CLASS, pb), F32),
                        pltpu.VMEM((N_CLASS, 1), F32),
                        pltpu.VMEM((N_CLASS, 1), F32),
                        pltpu.VMEM((2, pb * X_ROWS, LANES), F32),
                        pltpu.VMEM((1, pb), I32),
                        pltpu.SMEM((1, pb), I32),
                        pltpu.VMEM((ROUTE_ROWS, N_CLASS), I32),
                        pltpu.SMEM((ROUTE_ROWS, N_CLASS), I32),
                        pltpu.VMEM((MOE_TILE * ZERO_TILES * X_ROWS, LANES), F32),
                        pltpu.SemaphoreType.DMA((2,)),
                        pltpu.SemaphoreType.DMA(()),
                        pltpu.SemaphoreType.DMA(())],
        compiler_params=_cparams(("arbitrary", "arbitrary")),
        name="plan",
    )(route, x1)

    w_in = e_w_in.reshape(depth, N_EGROUPS, epg, d, ff2)
    w_out = e_w_out.reshape(depth, N_EGROUPS, epg, ff2 // 2, d)
    step_rows = MOE_TILE * MOE_STEP_TILES

    def group(i, tile_cls):
        return tile_cls[i * MOE_STEP_TILES] >> 6
    ys = pl.pallas_call(
        _moe_kernel,
        grid_spec=pltpu.PrefetchScalarGridSpec(
            num_scalar_prefetch=2,
            grid=(nt // MOE_STEP_TILES,),
            in_specs=[pl.BlockSpec((step_rows * X_ROWS, LANES), lambda i, tc, nu: (i, 0)),
                      pl.BlockSpec((None, None, epg, d, ff2), lambda i, tc, nu: (layer, group(i, tc), 0, 0, 0),
                                   pipeline_mode=pl.Buffered(1)),
                      pl.BlockSpec((None, None, epg, ff2 // 2, d), lambda i, tc, nu: (layer, group(i, tc), 0, 0, 0),
                                   pipeline_mode=pl.Buffered(1))],
            out_specs=pl.BlockSpec((step_rows * Y_ROWS, LANES), lambda i, tc, nu: (i, 0))),
        out_shape=jax.ShapeDtypeStruct((n_sorted * Y_ROWS, LANES), F32),
        compiler_params=_cparams(("arbitrary",)),
        name="moe",
    )(tinfo[0, :nt], tinfo[1, :1], xs, w_in, w_out)

    post_args = (pos, ys, x1, p, ln_g[None], ln_b[None], ple_w.astype(BF16), ple_gw.astype(BF16), ple_gb[None])
    return pl.pallas_call(
        functools.partial(_post_kernel, alpha=alpha, n_blocks=t // gb),
        grid=(t // gb,),
        in_specs=[pl.BlockSpec(memory_space=pl.ANY),
                  pl.BlockSpec(memory_space=pl.ANY),
                  pl.BlockSpec((gb, d), lambda i: (i, 0)),
                  pl.BlockSpec((None, gb, p.shape[2]), lambda i: (layer, i, 0))]
                 + [_full(a.shape) for a in post_args[4:]],
        out_specs=pl.BlockSpec((gb, d), lambda i: (i, 0)),
        out_shape=jax.ShapeDtypeStruct((t, d), F32),
        scratch_shapes=[pltpu.SMEM((2, 1, gb), I32), pltpu.VMEM((2, gb * Y_ROWS, LANES), F32),
                        pltpu.SemaphoreType.DMA((2,)), pltpu.SemaphoreType.DMA((2,))],
        compiler_params=_cparams(("arbitrary",)),
        name="post",
    )(*post_args)


def _router_params(wg, bg, we, be):
    d = wg.shape[0]
    w = jnp.zeros((d, ROUTE_COLS), F32).at[:, :N_EGROUPS].set(wg).at[:, 8:8 + we.shape[1]].set(we)
    b = jnp.zeros((1, ROUTE_COLS), F32).at[0, :N_EGROUPS].set(bg).at[0, 8:8 + be.shape[0]].set(be)
    hi = w.astype(BF16)
    return hi, (w - hi.astype(F32)).astype(BF16), b


def _tile(t, want):
    while t % want:
        want //= 2
    return want


def kernel(x, p, a_w_in, a_b_in, a_vn_g, a_vn_b, a_w_s, a_b_s, a_w_out, b_w_in, b_kv_g, b_w_uk, b_w_uv, b_w_out,
           ln1_g, ln1_b, ln2_g, ln2_b, r_wg, r_bg, r_we, r_be, e_w_in, e_w_out, ple_w, ple_gw, ple_gb):
    batch, seq, d = x.shape
    t = batch * seq
    depth = p.shape[0]
    alpha = (2 * depth) ** 0.25
    chunk = a_w_s.shape[-1]
    xt = x.reshape(t, d)
    pt = p.reshape(depth, t, p.shape[-1])
    for i in range(depth):
        j = i // 2
        router = _router_params(r_wg[i], r_bg[i], r_we[i], r_be[i])
        if i % 2 == 0:
            x1, route = _gmlp_layer(xt, a_w_in[j], a_b_in[j], a_vn_g[j], a_vn_b[j], a_w_s[j], a_b_s[j],
                                    a_w_out[j], ln1_g[i], ln1_b[i], router, alpha, max(chunk, _tile(t, MIXER_TILE)))
        else:
            x1, route = _dsa_layer(xt, batch, b_w_in[j], b_kv_g[j], b_w_uk[j], b_w_uv[j], b_w_out[j],
                                   ln1_g[i], ln1_b[i], router, alpha, max(KEY_CHUNK, _tile(t, MIXER_TILE)))
        xt = _moe_and_post(x1, route, i, pt, e_w_in, e_w_out, ln2_g[i], ln2_b[i],
                           ple_w[i], ple_gw[i], ple_gb[i], alpha)
    return xt.reshape(batch, seq, d)
```

```python
import functools

import jax
import jax.numpy as jnp
from jax import lax
from jax.experimental import pallas as pl
from jax.experimental.pallas import tpu as pltpu

F32, BF16, I32 = jnp.float32, jnp.bfloat16, jnp.int32

IDX_HEADS = 8
IDX_DIM = 64
TOPK_MAX = 256
N_EGROUPS = 4
LN_EPS = 1e-5
RMS_EPS = 1e-6

LANES = 128
MIXER_TILE = 1024
Q_BLOCK = 128
Q_STEP_BLOCKS = 2
KEY_CHUNK = 512
COUNT_SLAB = 1024
COUNT_ACC = 64
LOG2E = 1.4426950408889634
VMEM_LIMIT = 48 * 1024 * 1024

ROUTE_COLS = LANES
ROUTE_ROWS = 8
N_CLASS = 64 * N_EGROUPS
Y_ROWS = 8
X_ROWS = Y_ROWS + 1
PLAN_BLOCK = 1024
GATHER_BLOCK = 512
MOE_TILE = 128
MOE_STEP_TILES = 4
ZERO_TILES = 1 << ((MOE_STEP_TILES - 1).bit_length() - 1) if MOE_STEP_TILES > 1 else 1
INT_MIN = -(2 ** 31)
NEG_BIG = -1e30


def _cparams(sem, vmem=VMEM_LIMIT):
    return pltpu.CompilerParams(dimension_semantics=sem, vmem_limit_bytes=vmem)


def _full(shape):
    n = len(shape)
    return pl.BlockSpec(shape, lambda *_: (0,) * n)


def _dot(a, b):
    return jnp.dot(a, b, preferred_element_type=F32)


def _dot_t(a, b):
    return lax.dot_general(a, b, (((1,), (1,)), ((), ())), preferred_element_type=F32)


def _layer_norm(x, g, b):
    mu = jnp.mean(x, axis=-1, keepdims=True)
    xc = x - mu
    var = jnp.mean(xc * xc, axis=-1, keepdims=True)
    return xc * lax.rsqrt(var + LN_EPS) * g + b


def _route_rows(lg_t):
    g = [lg_t[i:i + 1] for i in range(N_EGROUPS)]
    gmax = functools.reduce(jnp.maximum, g)
    gsel = jnp.where(g[0] >= gmax, 0.0, jnp.where(g[1] >= gmax, 1.0, jnp.where(g[2] >= gmax, 2.0, 3.0)))
    den = functools.reduce(lambda a, b: a + b, [jnp.exp(gi - gmax) for gi in g])
    p_g = 1.0 / den
    el = lg_t[8:16]
    for gi in range(1, N_EGROUPS):
        el = jnp.where(gsel == float(gi), lg_t[8 + 8 * gi:16 + 8 * gi], el)
    eidx = lax.broadcasted_iota(I32, el.shape, 0).astype(F32)
    m1 = jnp.max(el, axis=0, keepdims=True)
    i1 = jnp.min(jnp.where(el == m1, eidx, 8.0), axis=0, keepdims=True)
    el2 = jnp.where(eidx == i1, -jnp.inf, el)
    m2 = jnp.max(el2, axis=0, keepdims=True)
    i2 = jnp.min(jnp.where(el2 == m2, eidx, 8.0), axis=0, keepdims=True)
    r = jnp.exp(m2 - m1)
    inv = 1.0 / (1.0 + r)
    gate1 = p_g * inv
    gate2 = p_g * r * inv
    first_lo = i1 < i2
    lo = jnp.minimum(i1, i2)
    hi = jnp.maximum(i1, i2)
    cls = gsel * 64.0 + lo * 8.0 + hi
    return cls, jnp.where(first_lo, gate1, gate2), jnp.where(first_lo, gate2, gate1)


def _norm_and_route(y, ln_g, ln_b, wr_hi, wr_lo, br, x1_ref, route_ref):
    tm, _ = y.shape
    x1 = _layer_norm(y, ln_g, ln_b)
    x1_ref[...] = x1
    x_hi = x1.astype(BF16)
    x_lo = (x1 - x_hi.astype(F32)).astype(BF16)
    lg = _dot(x_hi, wr_hi) + _dot(x_lo, wr_hi) + _dot(x_hi, wr_lo) + br
    cls, g_lo, g_hi = _route_rows(lg.T)
    row = lax.broadcasted_iota(I32, (ROUTE_ROWS, tm), 0)
    route_ref[...] = jnp.where(row == 0, cls, jnp.where(row == 1, g_lo, jnp.where(row == 2, g_hi, 0.0)))


def _mixer_out_specs(tm, d):
    return [pl.BlockSpec((tm, d), lambda i: (i, 0)), pl.BlockSpec((ROUTE_ROWS, tm), lambda i: (0, i))]


def _mixer_out_shape(t, d):
    return [jax.ShapeDtypeStruct((t, d), F32), jax.ShapeDtypeStruct((ROUTE_ROWS, t), F32)]


def _gmlp_kernel(x_ref, w_in_ref, b_in_ref, vn_g_ref, vn_b_ref, w_s_ref, b_st_ref, w_out_ref,
                 ln_g_ref, ln_b_ref, wr_hi_ref, wr_lo_ref, br_ref, x1_ref, route_ref, gated_ref, *, alpha):
    x = x_ref[...]
    tm, _ = x.shape
    groups, chunk, _ = w_s_ref.shape
    z = _dot(x.astype(BF16), w_in_ref[...]) + b_in_ref[...]
    z = 0.5 * z * (1.0 + lax.erf(z * (2.0 ** -0.5)))
    half = z.shape[1] // 2
    gd = half // groups
    u = z[:, :half]
    v = _layer_norm(z[:, half:], vn_g_ref[...], vn_b_ref[...]).astype(BF16)
    r = lax.broadcasted_iota(I32, (chunk, chunk), 0)
    c = lax.broadcasted_iota(I32, (chunk, chunk), 1)
    causal = r >= c
    for g in range(groups):
        w_c = jnp.where(causal, w_s_ref[g], 0.0).astype(BF16)
        bias = b_st_ref[:, g:g + 1]
        for ci in range(tm // chunk):
            rows = slice(ci * chunk, (ci + 1) * chunk)
            cols = slice(g * gd, (g + 1) * gd)
            s = _dot(w_c, v[rows, cols]) + bias
            gated_ref[rows, cols] = (u[rows, cols] * s).astype(BF16)
    mix = _dot(gated_ref[...], w_out_ref[...])
    _norm_and_route(alpha * x + mix, ln_g_ref[...], ln_b_ref[...], wr_hi_ref[...], wr_lo_ref[...],
                    br_ref[...], x1_ref, route_ref)


def _gmlp_layer(xt, w_in, b_in, vn_g, vn_b, w_s, b_s, w_out, ln_g, ln_b, router, alpha, tm):
    t, d = xt.shape
    half = w_out.shape[0]
    wr_hi, wr_lo, br = router
    args = (xt, w_in.astype(BF16), b_in[None], vn_g[None], vn_b[None], w_s, b_s.T, w_out.astype(BF16),
            ln_g[None], ln_b[None], wr_hi, wr_lo, br)
    in_specs = [pl.BlockSpec((tm, d), lambda i: (i, 0))] + [_full(a.shape) for a in args[1:]]
    return pl.pallas_call(
        functools.partial(_gmlp_kernel, alpha=alpha),
        grid=(t // tm,),
        in_specs=in_specs,
        out_specs=_mixer_out_specs(tm, d),
        out_shape=_mixer_out_shape(t, d),
        scratch_shapes=[pltpu.VMEM((tm, half), BF16)],
        compiler_params=_cparams(("parallel",)),
        name="gmlp",
    )(*args)


def _dsa_proj_kernel(x_ref, wq_ref, wc_ref, wqi_ref, wkw_ref, kvg_ref, wuk_t_ref,
                     qlat_ref, ckv_ref, ckv_t_ref, qidx_ref, kidx_ref, wi_t_ref, *, scale_q, scale_w):
    xb = x_ref[...].astype(BF16)
    nq = qlat_ref.shape[0]
    kc = ckv_t_ref.shape[2]
    heads, hd, _ = wuk_t_ref.shape
    q = _dot(xb, wq_ref[...])
    for h in range(heads):
        ql = (_dot(q[:, h * hd:(h + 1) * hd].astype(BF16), wuk_t_ref[h]) * scale_q).astype(BF16)
        for b in range(nq):
            qlat_ref[b, h * Q_BLOCK:(h + 1) * Q_BLOCK, :] = ql[b * Q_BLOCK:(b + 1) * Q_BLOCK]
    c = _dot(xb, wc_ref[...])
    ms = jnp.mean(c * c, axis=-1, keepdims=True)
    c = c * lax.rsqrt(ms + RMS_EPS) * kvg_ref[...]
    ckv_ref[...] = c.astype(BF16)
    for b in range(ckv_t_ref.shape[0]):
        ckv_t_ref[b] = c[b * kc:(b + 1) * kc].T.astype(BF16)
    qi = _dot(xb, wqi_ref[...]).astype(BF16)
    for h in range(IDX_HEADS):
        for b in range(nq):
            qidx_ref[b, h * Q_BLOCK:(h + 1) * Q_BLOCK, :] = qi[b * Q_BLOCK:(b + 1) * Q_BLOCK,
                                                               h * IDX_DIM:(h + 1) * IDX_DIM]
    kw = _dot(xb, wkw_ref[...])
    kidx_ref[...] = kw[:, :IDX_DIM].astype(BF16)
    wi_t = kw.T[IDX_DIM:IDX_DIM + IDX_HEADS] * scale_w
    for b in range(nq):
        wi_t_ref[b] = wi_t[:, b * Q_BLOCK:(b + 1) * Q_BLOCK]


def _dsa_attn_kernel(qidx_ref, wi_t_ref, qlat_ref, kidx_ref, ckv_ref, ckv_t_ref, o_ref, *scratch, k_sel, heads):
    n_sub = qidx_ref.shape[0]

    def sub_block(s, carry):
        _dsa_attn_block(pl.program_id(1) * n_sub + s, qidx_ref.at[s], wi_t_ref.at[s], qlat_ref.at[s],
                        kidx_ref, ckv_ref, ckv_t_ref, o_ref.at[s], *scratch, k_sel=k_sel, heads=heads)
        return carry

    lax.fori_loop(0, n_sub, sub_block, 0)


def _dsa_attn_block(qb, qidx_ref, wi_t_ref, qlat_ref, kidx_ref, ckv_ref, ckv_t_ref, o_ref,
                    key_ref, key16_ref, acc_ref, m_ref, l_ref, a_ref, p_ref, *, k_sel, heads):
    kc = ckv_t_ref.shape[2]
    slab = key_ref.shape[1]
    cps = slab // kc
    nq = Q_BLOCK
    pair = 2 * nq
    n_pairs = heads // 2
    n_chunks = (qb * nq + nq + kc - 1) // kc
    n_slabs = (n_chunks + cps - 1) // cps

    def key_chunk(j):
        return key_ref.at[j // cps, pl.ds(pl.multiple_of((j % cps) * kc, kc), kc), :]

    q_pos = qb * nq + lax.broadcasted_iota(I32, (kc, nq), 1)
    k_off = lax.broadcasted_iota(I32, (kc, nq), 0)

    def key16_chunk(j):
        return key16_ref.at[j // cps, pl.ds(pl.multiple_of((j % cps) * kc, kc), kc), :]

    def pad_chunk(j, carry):
        key_chunk(j)[...] = jnp.full((kc, nq), jnp.nan, F32)
        key16_chunk(j)[...] = jnp.full((kc, nq), jnp.nan, BF16)
        return carry

    lax.fori_loop(n_chunks, n_slabs * cps, pad_chunk, 0)

    def score_chunk(j, carry):
        kch = kidx_ref[pl.ds(pl.multiple_of(j * kc, kc), kc), :]
        tot = None
        for p in range(IDX_HEADS // 2):
            sc = _dot_t(kch, qidx_ref[p * pair:(p + 1) * pair, :])
            for hh in range(2):
                h = 2 * p + hh
                r = jnp.maximum(sc[:, hh * nq:(hh + 1) * nq], 0.0) * wi_t_ref[h:h + 1, :]
                tot = r if tot is None else tot + r
        tot = jnp.where(k_off + j * kc <= q_pos, tot, jnp.nan)
        key_chunk(j)[...] = tot
        key16_chunk(j)[...] = tot.astype(BF16)
        return carry

    lax.fori_loop(0, n_chunks, score_chunk, 0)

    def count(t, strict=False):
        tb = jnp.broadcast_to(t, (COUNT_ACC, nq))

        def body(s, acc):
            for g in range(slab // COUNT_ACC):
                blk = key_ref[s, g * COUNT_ACC:(g + 1) * COUNT_ACC, :]
                acc = jnp.where(blk > tb if strict else blk >= tb, acc + 1.0, acc)
            return acc

        acc = lax.fori_loop(0, n_slabs, body, jnp.zeros((COUNT_ACC, nq), F32))
        return jnp.sum(acc, axis=0, keepdims=True)

    def count16(t):
        rows16 = 2 * COUNT_ACC
        tb = jnp.broadcast_to(t.astype(BF16), (rows16, nq))

        def body(s, acc):
            for g in range(slab // rows16):
                blk = key16_ref[s, g * rows16:(g + 1) * rows16, :]
                acc = jnp.where(blk >= tb, acc + 1.0, acc)
            return acc

        acc = lax.fori_loop(0, n_slabs, body, jnp.zeros((rows16, nq), BF16))
        return jnp.sum(acc.astype(F32), axis=0, keepdims=True)

    def decode(code):
        return lax.bitcast_convert_type(code ^ ((code >> 31) & 0x7FFFFFFF), F32)

    def decode16(c16):
        return decode(jnp.left_shift(c16, 16) | jnp.where(c16 < 0, 0xFFFF, 0))

    def bit16_step(b, c16):
        t = c16 + jnp.left_shift(jnp.int32(1), 15 - b)
        return jnp.where(count16(decode16(t)) >= float(k_sel), t, c16)

    no_code = -(2 ** 15)
    c16 = lax.fori_loop(0, 16, bit16_step, jnp.full((1, nq), no_code, I32))

    def bit_step(b, c):
        t = c + jnp.left_shift(jnp.int32(1), 17 - b)
        return jnp.where((count(decode(t)) >= float(k_sel)) & (t > c), t, c)

    c_sel = lax.fori_loop(0, 18, bit_step, jnp.left_shift(jnp.maximum(c16, no_code + 1) - 1, 16))
    c_sel = jnp.where(c16 == no_code, -jnp.inf, decode(c_sel))
    n_ge = jnp.where(c16 == no_code, 0.0, count(c_sel))

    has_tie = jnp.max(n_ge) > float(k_sel)

    @pl.when(has_tie)
    def _():
        need = float(k_sel) - count(c_sel, strict=True)
        tie_col = n_ge > float(k_sel)
        cb = jnp.broadcast_to(c_sel, (kc, nq))
        lower = (lax.broadcasted_iota(I32, (kc, kc), 1) < lax.broadcasted_iota(I32, (kc, kc), 0))
        lower = jnp.where(lower, 1.0, 0.0).astype(BF16)

        def fix(j, seen):
            blk = key_chunk(j)[...]
            eq = blk == cb
            eq_f = jnp.where(eq, 1.0, 0.0)
            rank = seen + _dot(lower, eq_f.astype(BF16))
            drop = eq & tie_col & (rank >= need)
            key_chunk(j)[...] = jnp.where(drop, jnp.nan, blk)
            return seen + jnp.sum(eq_f, axis=0, keepdims=True)

        lax.fori_loop(0, n_chunks, fix, jnp.zeros((1, nq), F32))

    thr = jnp.broadcast_to(c_sel, (kc, nq))
    m_ref[...] = jnp.full(m_ref.shape, NEG_BIG, F32)
    l_ref[...] = jnp.zeros(l_ref.shape, F32)
    acc_ref[...] = jnp.zeros(acc_ref.shape, F32)

    def softmax_chunk(j):
        ck = ckv_ref[pl.ds(pl.multiple_of(j * kc, kc), kc), :]
        bias = jnp.where(key_chunk(j)[...] >= thr, 0.0, NEG_BIG)
        bias = jnp.concatenate([bias, bias], axis=1)
        slot = j % 2
        for p in range(n_pairs):
            lg = _dot_t(ck, qlat_ref[p * pair:(p + 1) * pair, :]) + bias
            m_old = m_ref[p]
            m_new = jnp.maximum(m_old, jnp.max(lg, axis=0, keepdims=True))
            pr = jnp.exp2(lg - m_new)
            alpha = jnp.exp2(m_old - m_new)
            l_ref[p] = alpha * l_ref[p] + jnp.sum(pr, axis=0, keepdims=True)
            m_ref[p] = m_new
            a_ref[slot, p] = alpha
            p_ref[slot, p] = pr.astype(BF16)

    def accumulate_chunk(j):
        ck_t = ckv_t_ref[j]
        slot = j % 2
        for p in range(n_pairs):
            acc_ref[p] = a_ref[slot, p] * acc_ref[p] + _dot(ck_t, p_ref[slot, p])

    def att_step(j, carry):
        accumulate_chunk(j - 1)
        softmax_chunk(j)
        return carry

    softmax_chunk(0)
    lax.fori_loop(1, n_chunks, att_step, 0)
    accumulate_chunk(n_chunks - 1)
    for p in range(n_pairs):
        o_ref[p * pair:(p + 1) * pair, :] = (acc_ref[p] * (1.0 / l_ref[p])).T.astype(o_ref.dtype)


def _dsa_out_kernel(o_ref, x_ref, wuv_ref, wout_ref, ln_g_ref, ln_b_ref, wr_hi_ref, wr_lo_ref, br_ref,
                    x1_ref, route_ref, o2_ref, *, alpha):
    nq = o_ref.shape[0]
    heads, _, hd = wuv_ref.shape
    for b in range(nq):
        for h in range(heads):
            oh = o_ref[b, h * Q_BLOCK:(h + 1) * Q_BLOCK, :]
            o2_ref[b * Q_BLOCK:(b + 1) * Q_BLOCK, h * hd:(h + 1) * hd] = _dot(oh, wuv_ref[h]).astype(BF16)
    mix = _dot(o2_ref[...], wout_ref[...])
    _norm_and_route(alpha * x_ref[...] + mix, ln_g_ref[...], ln_b_ref[...], wr_hi_ref[...], wr_lo_ref[...],
                    br_ref[...], x1_ref, route_ref)


def _dsa_layer(xt, batch, w_in, kv_g, w_uk, w_uv, w_out, ln_g, ln_b, router, alpha, tm):
    t, d = xt.shape
    seq = t // batch
    heads, lat, hd = w_uk.shape
    qw = heads * hd
    nqb = t // Q_BLOCK
    nq = tm // Q_BLOCK
    k_sel = min(TOPK_MAX, seq // 4)
    kc = min(KEY_CHUNK, seq)
    slab = min(COUNT_SLAB, seq)
    w_in = w_in.astype(BF16)
    o1, o2, o3 = qw, qw + lat, qw + lat + IDX_HEADS * IDX_DIM
    wkw = jnp.pad(w_in[:, o3:], ((0, 0), (0, LANES - (w_in.shape[1] - o3))))
    proj_args = (xt, w_in[:, :o1], w_in[:, o1:o2], w_in[:, o2:o3], wkw, kv_g[None],
                 jnp.swapaxes(w_uk, 1, 2).astype(BF16))
    qlat, ckv, ckv_t, qidx, kidx, wi_t = pl.pallas_call(
        functools.partial(_dsa_proj_kernel, scale_q=hd ** -0.5 * LOG2E, scale_w=(IDX_HEADS * IDX_DIM) ** -0.5),
        grid=(t // tm,),
        in_specs=[pl.BlockSpec((tm, d), lambda i: (i, 0))] + [_full(a.shape) for a in proj_args[1:]],
        out_specs=[pl.BlockSpec((nq, heads * Q_BLOCK, lat), lambda i: (i, 0, 0)),
                   pl.BlockSpec((tm, lat), lambda i: (i, 0)),
                   pl.BlockSpec((tm // kc, lat, kc), lambda i: (i, 0, 0)),
                   pl.BlockSpec((nq, IDX_HEADS * Q_BLOCK, IDX_DIM), lambda i: (i, 0, 0)),
                   pl.BlockSpec((tm, IDX_DIM), lambda i: (i, 0)),
                   pl.BlockSpec((nq, IDX_HEADS, Q_BLOCK), lambda i: (i, 0, 0))],
        out_shape=[jax.ShapeDtypeStruct((nqb, heads * Q_BLOCK, lat), BF16),
                   jax.ShapeDtypeStruct((t, lat), BF16),
                   jax.ShapeDtypeStruct((t // kc, lat, kc), BF16),
                   jax.ShapeDtypeStruct((nqb, IDX_HEADS * Q_BLOCK, IDX_DIM), BF16),
                   jax.ShapeDtypeStruct((t, IDX_DIM), BF16),
                   jax.ShapeDtypeStruct((nqb, IDX_HEADS, Q_BLOCK), F32)],
        compiler_params=_cparams(("parallel",)),
        name="dsa_proj",
    )(*proj_args)

    nq_seq = seq // Q_BLOCK
    n_kc = seq // kc
    rows = heads * Q_BLOCK
    qsb = _tile(nq_seq, Q_STEP_BLOCKS)
    n_steps = nq_seq // qsb
    q_blocks = lambda b, q: (b * n_steps + q, 0, 0)
    o = pl.pallas_call(
        functools.partial(_dsa_attn_kernel, k_sel=k_sel, heads=heads),
        grid=(batch, n_steps),
        in_specs=[pl.BlockSpec((qsb, IDX_HEADS * Q_BLOCK, IDX_DIM), q_blocks),
                  pl.BlockSpec((qsb, IDX_HEADS, Q_BLOCK), q_blocks),
                  pl.BlockSpec((qsb, rows, lat), q_blocks),
                  pl.BlockSpec((None, seq, IDX_DIM), lambda b, q: (b, 0, 0)),
                  pl.BlockSpec((None, seq, lat), lambda b, q: (b, 0, 0)),
                  pl.BlockSpec((None, n_kc, lat, kc), lambda b, q: (b, 0, 0, 0))],
        out_specs=pl.BlockSpec((qsb, rows, lat), q_blocks),
        out_shape=jax.ShapeDtypeStruct((nqb, rows, lat), BF16),
        scratch_shapes=[pltpu.VMEM((seq // slab, slab, Q_BLOCK), F32),
                        pltpu.VMEM((seq // slab, slab, Q_BLOCK), BF16),
                        pltpu.VMEM((heads // 2, lat, 2 * Q_BLOCK), F32),
                        pltpu.VMEM((heads // 2, 1, 2 * Q_BLOCK), F32),
                        pltpu.VMEM((heads // 2, 1, 2 * Q_BLOCK), F32),
                        pltpu.VMEM((2, heads // 2, 1, 2 * Q_BLOCK), F32),
                        pltpu.VMEM((2, heads // 2, kc, 2 * Q_BLOCK), BF16)],
        compiler_params=_cparams(("parallel", "arbitrary")),
        name="dsa_attn",
    )(qidx, wi_t, qlat, kidx.reshape(batch, seq, IDX_DIM), ckv.reshape(batch, seq, lat),
      ckv_t.reshape(batch, n_kc, lat, kc))

    wr_hi, wr_lo, br = router
    out_args = (o, xt, w_uv.astype(BF16), w_out.astype(BF16), ln_g[None], ln_b[None], wr_hi, wr_lo, br)
    return pl.pallas_call(
        functools.partial(_dsa_out_kernel, alpha=alpha),
        grid=(t // tm,),
        in_specs=[pl.BlockSpec((nq, rows, lat), lambda i: (i, 0, 0)),
                  pl.BlockSpec((tm, d), lambda i: (i, 0))] + [_full(a.shape) for a in out_args[2:]],
        out_specs=_mixer_out_specs(tm, d),
        out_shape=_mixer_out_shape(t, d),
        scratch_shapes=[pltpu.VMEM((tm, qw), BF16)],
        compiler_params=_cparams(("parallel",)),
        name="dsa_out",
    )(*out_args)


DMA_UNROLL = 8


def _start_token_moves(n, copy_of):
    def body(g, carry):
        for u in range(DMA_UNROLL):
            copy_of(g * DMA_UNROLL + u).start(priority=u % 2)
        return carry
    lax.fori_loop(0, n // DMA_UNROLL, body, 0)


def _tok(ref, i, rows):
    return ref.at[pl.ds(i * rows, rows)]


def _rows_of(ref, j, n, rows):
    return ref.at[pl.ds(j, n, stride=rows), :]


class _TokenGather:
    def __init__(self, i, n, idx_hbm, src_hbm, idx_smem, buf, isem, rsem):
        self.i, self.n = i, n
        self.idx_hbm, self.src_hbm, self.idx_smem, self.buf, self.isem, self.rsem = (
            idx_hbm, src_hbm, idx_smem, buf, isem, rsem)
        self.tm = idx_smem.shape[2]
        self.rows = buf.shape[1] // self.tm
        self.slot = i % 2

    def _idx_copy(self, blk, s):
        return pltpu.make_async_copy(self.idx_hbm.at[blk], self.idx_smem.at[s], self.isem.at[s])

    def _token_copy(self, s, r):
        return pltpu.make_async_copy(_tok(self.src_hbm, self.idx_smem[s, 0, r], self.rows),
                                     _tok(self.buf.at[s], r, self.rows), self.rsem.at[s])

    def _wait_tokens(self, s):
        pltpu.make_async_copy(self.src_hbm.at[pl.ds(0, self.tm * self.rows)], self.buf.at[s], self.rsem.at[s]).wait()

    def arrive(self):
        @pl.when(self.i == 0)
        def _():
            first = self._idx_copy(0, 0)
            first.start()
            first.wait()
            _start_token_moves(self.tm, lambda r: self._token_copy(0, r))
            self._idx_copy(min(1, self.n - 1), 1).start()

        self._idx_copy(jnp.minimum(self.i + 1, self.n - 1), 1 - self.slot).wait()
        self._wait_tokens(self.slot)

    def prefetch(self):
        nslot = 1 - self.slot
        for r in range(self.tm):
            self._token_copy(nslot, r).start(priority=r % 2)
        self._idx_copy(jnp.minimum(self.i + 2, self.n - 1), self.slot).start()

    def finish(self):
        @pl.when(self.i == self.n - 1)
        def _():
            self._wait_tokens(1 - self.slot)
            self._idx_copy(self.n - 1, self.slot).wait()


def _plan_kernel(route_ref, x_ref, pos_ref, tinfo_ref, xs_ref,
                 hist_ref, off_ref, carry_ref, stage_ref, ppos_vmem, ppos_smem, fill_vmem, fill_smem, zero_ref,
                 ssem, psem, zsem, *, tile):
    ps = pl.program_id(0)
    b = pl.program_id(1)
    nb = pl.num_programs(1)
    n_cls, pb = hist_ref.shape
    kt = tinfo_ref.shape[1]
    cls = route_ref[0:1, :].astype(I32)
    cid = lax.broadcasted_iota(I32, (n_cls, pb), 0)
    onehot = jnp.where(cid == cls, 1.0, 0.0)

    def zero_fill(start):
        def piece(tok0, n_tok):
            cp = pltpu.make_async_copy(zero_ref.at[pl.ds(0, n_tok * X_ROWS)],
                                       xs_ref.at[pl.ds(tok0 * X_ROWS, n_tok * X_ROWS)], zsem)
            cp.start() if start else cp.wait()

        def per_class(c, carry):
            first, n = fill_smem[0, c], fill_smem[1, c]
            bit = zero_ref.shape[0] // X_ROWS
            while bit:
                @pl.when((n & bit) != 0)
                def _(bit=bit):
                    piece(first + (n & -(2 * bit)), bit)
                bit //= 2
            return carry

        lax.fori_loop(0, n_cls, per_class, 0)

        def per_tile(k, carry):
            piece(k * tile, tile)
            return carry

        lax.fori_loop(fill_smem[2, 0], xs_ref.shape[0] // (tile * X_ROWS), per_tile, 0)

    @pl.when(ps == 0)
    def _():
        @pl.when(b == 0)
        def _():
            hist_ref[...] = jnp.zeros(hist_ref.shape, F32)
            zero_ref[...] = jnp.zeros(zero_ref.shape, F32)
        hist_ref[...] += onehot

    @pl.when((ps == 1) & (b == 0))
    def _():
        counts = jnp.sum(hist_ref[...], axis=1, keepdims=True)
        tiles = jnp.floor((counts + float(tile - 1)) * (1.0 / tile))
        cid_col = lax.broadcasted_iota(I32, (n_cls, 1), 0)
        per_group = n_cls // N_EGROUPS
        for g in range(N_EGROUPS):
            g_tiles = jnp.sum(tiles[g * per_group:(g + 1) * per_group], axis=0, keepdims=True)
            extra = jnp.ceil(g_tiles * (1.0 / MOE_STEP_TILES)) * float(MOE_STEP_TILES) - g_tiles
            tiles = jnp.where(cid_col == (g + 1) * per_group - 1, extra, tiles)
        lower = lax.broadcasted_iota(I32, (n_cls, n_cls), 1) < lax.broadcasted_iota(I32, (n_cls, n_cls), 0)
        lower = jnp.where(lower, 1.0, 0.0).astype(BF16)
        off = _dot(lower, jnp.broadcast_to(tiles, (n_cls, LANES)).astype(BF16))[:, :1]
        off_ref[...] = off * float(tile)
        carry_ref[...] = jnp.zeros(carry_ref.shape, F32)
        k = lax.broadcasted_iota(I32, (n_cls, kt), 1).astype(F32)
        cid_f = lax.broadcasted_iota(I32, (n_cls, kt), 0).astype(F32)
        mine = (k >= off) & (k < off + tiles)
        tcls = jnp.sum(jnp.where(mine, cid_f, 0.0), axis=0, keepdims=True)
        n_used = jnp.sum(tiles, axis=0, keepdims=True)
        last = jnp.max(jnp.where(tiles > 0.0, cid_f[:, :1], 0.0), axis=0, keepdims=True)
        tcls = jnp.where(k[:1] >= n_used, last, tcls)
        row = lax.broadcasted_iota(I32, (ROUTE_ROWS, kt), 0)
        tinfo_ref[...] = jnp.where(row == 0, tcls, jnp.where(row == 1, n_used, 0.0)).astype(I32)
        as_row = lambda col: jnp.broadcast_to(col, (n_cls, LANES)).T[0:1]
        frow = lax.broadcasted_iota(I32, fill_vmem.shape, 0)
        fill_vmem[...] = jnp.where(frow == 0, as_row(off * float(tile) + counts),
                                   jnp.where(frow == 1, as_row(tiles * float(tile) - counts), n_used)).astype(I32)
        to_fill = pltpu.make_async_copy(fill_vmem, fill_smem, psem)
        to_fill.start()
        to_fill.wait()
        zero_fill(start=True)

    @pl.when(ps == 1)
    def _():
        upper = lax.broadcasted_iota(I32, (pb, pb), 0) < lax.broadcasted_iota(I32, (pb, pb), 1)
        upper = jnp.where(upper, 1.0, 0.0).astype(BF16)
        before = _dot(onehot.astype(BF16), upper)
        ppos = jnp.sum(onehot * (before + carry_ref[...] + off_ref[...]), axis=0, keepdims=True).astype(I32)
        carry_ref[...] += jnp.sum(onehot, axis=1, keepdims=True)
        gb = pos_ref.shape[2]
        for h in range(pos_ref.shape[0]):
            pos_ref[h] = ppos[:, h * gb:(h + 1) * gb]
        ppos_vmem[...] = ppos
        to_smem = pltpu.make_async_copy(ppos_vmem, ppos_smem, psem)
        to_smem.start()

        slot = b % 2
        stage = stage_ref.at[slot]

        def drain(s):
            pltpu.make_async_copy(stage_ref.at[s], xs_ref.at[pl.ds(0, pb * X_ROWS)], ssem.at[s]).wait()

        @pl.when(b >= 2)
        def _():
            drain(slot)

        for j in range(Y_ROWS):
            _rows_of(stage, j, pb, X_ROWS)[...] = x_ref[:, j * LANES:(j + 1) * LANES]
        record = jnp.concatenate([route_ref[...], jnp.zeros((LANES - ROUTE_ROWS, pb), F32)], axis=0).T
        _rows_of(stage, Y_ROWS, pb, X_ROWS)[...] = record

        to_smem.wait()
        _start_token_moves(pb, lambda r: pltpu.make_async_copy(
            _tok(stage, r, X_ROWS), _tok(xs_ref, ppos_smem[0, r], X_ROWS), ssem.at[slot]))

        @pl.when(b == nb - 1)
        def _():
            drain(slot)

            @pl.when(nb > 1)
            def _():
                drain(1 - slot)

            zero_fill(start=False)


def _moe_kernel(tcls_ref, nused_ref, xs_ref, win_ref, wout_ref, ys_ref):
    i = pl.program_id(0)
    tm = xs_ref.shape[0] // (X_ROWS * MOE_STEP_TILES)
    ff = wout_ref.shape[1]

    @pl.when(i * MOE_STEP_TILES < nused_ref[0])
    def _():
        for jt in range(MOE_STEP_TILES):
            cls = tcls_ref[i * MOE_STEP_TILES + jt]
            x0, y0 = jt * tm * X_ROWS, jt * tm * Y_ROWS
            xb = jnp.concatenate([_rows_of(xs_ref, x0 + j, tm, X_ROWS)[...].astype(BF16) for j in range(Y_ROWS)],
                                 axis=1)
            record = _rows_of(xs_ref, x0 + Y_ROWS, tm, X_ROWS)[...]
            y = None
            for e, gate in (((cls >> 3) & 7, record[:, 1:2]), (cls & 7, record[:, 2:3])):
                h = _dot(xb, win_ref[e])
                a = h[:, :ff]
                act = (a * jax.nn.sigmoid(a) * h[:, ff:] * gate).astype(BF16)
                ye = _dot(act, wout_ref[e])
                y = ye if y is None else y + ye
            for j in range(Y_ROWS):
                _rows_of(ys_ref, y0 + j, tm, Y_ROWS)[...] = y[:, j * LANES:(j + 1) * LANES]

    @pl.when(i * MOE_STEP_TILES >= nused_ref[0])
    def _():
        ys_ref[...] = jnp.zeros(ys_ref.shape, F32)


def _post_kernel(pos_ref, ys_ref, x1_ref, p_ref, ln_g_ref, ln_b_ref, pw_ref, gw_ref, gb_ref, out_ref,
                 idx_smem, ybuf, isem, rsem, *, alpha, n_blocks):
    i = pl.program_id(0)
    tm = x1_ref.shape[0]
    gather = _TokenGather(i, n_blocks, pos_ref, ys_ref, idx_smem, ybuf, isem, rsem)
    gather.arrive()
    gather.prefetch()
    yb = ybuf.at[i % 2]
    ffn = jnp.concatenate([_rows_of(yb, j, tm, Y_ROWS)[...] for j in range(Y_ROWS)], axis=1)
    x2 = _layer_norm(alpha * x1_ref[...] + ffn, ln_g_ref[...], ln_b_ref[...])
    gate = jax.nn.sigmoid(_dot(x2.astype(BF16), gw_ref[...]) + gb_ref[...])
    out_ref[...] = x2 + _dot(p_ref[...].astype(BF16), pw_ref[...]) * gate
    gather.finish()


def _moe_and_post(x1, route, layer, p, e_w_in, e_w_out, ln_g, ln_b, ple_w, ple_gw, ple_gb, alpha):
    t, d = x1.shape
    assert d == LANES * Y_ROWS
    depth, n_exp, _, ff2 = e_w_in.shape
    epg = n_exp // N_EGROUPS
    pb = _tile(t, PLAN_BLOCK)
    nb = t // pb
    gb = _tile(pb, GATHER_BLOCK)
    nt = t // MOE_TILE + N_EGROUPS * (epg * (epg - 1) // 2) + N_EGROUPS * (MOE_STEP_TILES - 1)
    nt = -(-nt // MOE_STEP_TILES) * MOE_STEP_TILES
    kt = -(-nt // LANES) * LANES
    n_sorted = nt * MOE_TILE

    pos, tinfo, xs = pl.pallas_call(
        functools.partial(_plan_kernel, tile=MOE_TILE),
        grid=(2, nb),
        in_specs=[pl.BlockSpec((ROUTE_ROWS, pb), lambda ps, b: (0, b)),
                  pl.BlockSpec((pb, d), lambda ps, b: (ps * b, 0))],
        out_specs=[pl.BlockSpec((pb // gb, 1, gb), lambda ps, b: (ps * b, 0, 0)),
                   pl.BlockSpec((ROUTE_ROWS, kt), lambda ps, b: (0, 0)),
                   pl.BlockSpec(memory_space=pl.ANY)],
        out_shape=[jax.ShapeDtypeStruct((t // gb, 1, gb), I32),
                   jax.ShapeDtypeStruct((ROUTE_ROWS, kt), I32),
                   jax.ShapeDtypeStruct((n_sorted * X_ROWS, LANES), F32)],
        scratch_shapes=[pltpu.VMEM((N_CLASS, pb), F32),
                        pltpu.VMEM((N_CLASS, 1), F32),
                        pltpu.VMEM((N_CLASS, 1), F32),
                        pltpu.VMEM((2, pb * X_ROWS, LANES), F32),
                        pltpu.VMEM((1, pb), I32),
                        pltpu.SMEM((1, pb), I32),
                        pltpu.VMEM((ROUTE_ROWS, N_CLASS), I32),
                        pltpu.SMEM((ROUTE_ROWS, N_CLASS), I32),
                        pltpu.VMEM((MOE_TILE * ZERO_TILES * X_ROWS, LANES), F32),
                        pltpu.SemaphoreType.DMA((2,)),
                        pltpu.SemaphoreType.DMA(()),
                        pltpu.SemaphoreType.DMA(())],
        compiler_params=_cparams(("arbitrary", "arbitrary")),
        name="plan",
    )(route, x1)

    w_in = e_w_in.reshape(depth, N_EGROUPS, epg, d, ff2)
    w_out = e_w_out.reshape(depth, N_EGROUPS, epg, ff2 // 2, d)
    step_rows = MOE_TILE * MOE_STEP_TILES

    def group(i, tile_cls):
        return tile_cls[i * MOE_STEP_TILES] >> 6
    ys = pl.pallas_call(
        _moe_kernel,
        grid_spec=pltpu.PrefetchScalarGridSpec(
            num_scalar_prefetch=2,
            grid=(nt // MOE_STEP_TILES,),
            in_specs=[pl.BlockSpec((step_rows * X_ROWS, LANES), lambda i, tc, nu: (i, 0)),
                      pl.BlockSpec((None, None, epg, d, ff2), lambda i, tc, nu: (layer, group(i, tc), 0, 0, 0),
                                   pipeline_mode=pl.Buffered(1)),
                      pl.BlockSpec((None, None, epg, ff2 // 2, d), lambda i, tc, nu: (layer, group(i, tc), 0, 0, 0),
                                   pipeline_mode=pl.Buffered(1))],
            out_specs=pl.BlockSpec((step_rows * Y_ROWS, LANES), lambda i, tc, nu: (i, 0))),
        out_shape=jax.ShapeDtypeStruct((n_sorted * Y_ROWS, LANES), F32),
        compiler_params=_cparams(("arbitrary",)),
        name="moe",
    )(tinfo[0, :nt], tinfo[1, :1], xs, w_in, w_out)

    post_args = (pos, ys, x1, p, ln_g[None], ln_b[None], ple_w.astype(BF16), ple_gw.astype(BF16), ple_gb[None])
    return pl.pallas_call(
        functools.partial(_post_kernel, alpha=alpha, n_blocks=t // gb),
        grid=(t // gb,),
        in_specs=[pl.BlockSpec(memory_space=pl.ANY),
                  pl.BlockSpec(memory_space=pl.ANY),
                  pl.BlockSpec((gb, d), lambda i: (i, 0)),
                  pl.BlockSpec((None, gb, p.shape[2]), lambda i: (layer, i, 0))]
                 + [_full(a.shape) for a in post_args[4:]],
        out_specs=pl.BlockSpec((gb, d), lambda i: (i, 0)),
        out_shape=jax.ShapeDtypeStruct((t, d), F32),
        scratch_shapes=[pltpu.SMEM((2, 1, gb), I32), pltpu.VMEM((2, gb * Y_ROWS, LANES), F32),
                        pltpu.SemaphoreType.DMA((2,)), pltpu.SemaphoreType.DMA((2,))],
        compiler_params=_cparams(("arbitrary",)),
        name="post",
    )(*post_args)


def _router_params(wg, bg, we, be):
    d = wg.shape[0]
    w = jnp.zeros((d, ROUTE_COLS), F32).at[:, :N_EGROUPS].set(wg).at[:, 8:8 + we.shape[1]].set(we)
    b = jnp.zeros((1, ROUTE_COLS), F32).at[0, :N_EGROUPS].set(bg).at[0, 8:8 + be.shape[0]].set(be)
    hi = w.astype(BF16)
    return hi, (w - hi.astype(F32)).astype(BF16), b


def _tile(t, want):
    while t % want:
        want //= 2
    return want


def kernel(x, p, a_w_in, a_b_in, a_vn_g, a_vn_b, a_w_s, a_b_s, a_w_out, b_w_in, b_kv_g, b_w_uk, b_w_uv, b_w_out,
           ln1_g, ln1_b, ln2_g, ln2_b, r_wg, r_bg, r_we, r_be, e_w_in, e_w_out, ple_w, ple_gw, ple_gb):
    batch, seq, d = x.shape
    t = batch * seq
    depth = p.shape[0]
    alpha = (2 * depth) ** 0.25
    chunk = a_w_s.shape[-1]
    xt = x.reshape(t, d)
    pt = p.reshape(depth, t, p.shape[-1])
    for i in range(depth):
        j = i // 2
        router = _router_params(r_wg[i], r_bg[i], r_we[i], r_be[i])
        if i % 2 == 0:
            x1, route = _gmlp_layer(xt, a_w_in[j], a_b_in[j], a_vn_g[j], a_vn_b[j], a_w_s[j], a_b_s[j],
                                    a_w_out[j], ln1_g[i], ln1_b[i], router, alpha, max(chunk, _tile(t, MIXER_TILE)))
        else:
            x1, route = _dsa_layer(xt, batch, b_w_in[j], b_kv_g[j], b_w_uk[j], b_w_uv[j], b_w_out[j],
                                   ln1_g[i], ln1_b[i], router, alpha, max(KEY_CHUNK, _tile(t, MIXER_TILE)))
        xt = _moe_and_post(x1, route, i, pt, e_w_in, e_w_out, ln2_g[i], ln2_b[i],
                           ple_w[i], ple_gw[i], ple_gb[i], alpha)
    return xt.reshape(batch, seq, d)
```

```python
import functools

import jax
import jax.numpy as jnp
from jax import lax
from jax.experimental import pallas as pl
from jax.experimental.pallas import tpu as pltpu

F32, BF16, I32 = jnp.float32, jnp.bfloat16, jnp.int32

IDX_HEADS = 8
IDX_DIM = 64
TOPK_MAX = 256
N_EGROUPS = 4
LN_EPS = 1e-5
RMS_EPS = 1e-6

LANES = 128
MIXER_TILE = 1024
Q_BLOCK = 128
KEY_CHUNK = 512
COUNT_SLAB = 1024
COUNT_ACC = 64
LOG2E = 1.4426950408889634
VMEM_LIMIT = 48 * 1024 * 1024

ROUTE_COLS = LANES
ROUTE_ROWS = 8
N_CLASS = 64 * N_EGROUPS
Y_ROWS = 8
X_ROWS = Y_ROWS + 1
PLAN_BLOCK = 1024
GATHER_BLOCK = 512
MOE_TILE = 128
MOE_STEP_TILES = 4
ZERO_TILES = 1 << ((MOE_STEP_TILES - 1).bit_length() - 1) if MOE_STEP_TILES > 1 else 1
INT_MIN = -(2 ** 31)
NEG_BIG = -1e30


def _cparams(sem, vmem=VMEM_LIMIT):
    return pltpu.CompilerParams(dimension_semantics=sem, vmem_limit_bytes=vmem)


def _full(shape):
    n = len(shape)
    return pl.BlockSpec(shape, lambda *_: (0,) * n)


def _dot(a, b):
    return jnp.dot(a, b, preferred_element_type=F32)


def _dot_t(a, b):
    return lax.dot_general(a, b, (((1,), (1,)), ((), ())), preferred_element_type=F32)


def _layer_norm(x, g, b):
    mu = jnp.mean(x, axis=-1, keepdims=True)
    xc = x - mu
    var = jnp.mean(xc * xc, axis=-1, keepdims=True)
    return xc * lax.rsqrt(var + LN_EPS) * g + b


def _route_rows(lg_t):
    g = [lg_t[i:i + 1] for i in range(N_EGROUPS)]
    gmax = functools.reduce(jnp.maximum, g)
    gsel = jnp.where(g[0] >= gmax, 0.0, jnp.where(g[1] >= gmax, 1.0, jnp.where(g[2] >= gmax, 2.0, 3.0)))
    den = functools.reduce(lambda a, b: a + b, [jnp.exp(gi - gmax) for gi in g])
    p_g = 1.0 / den
    el = lg_t[8:16]
    for gi in range(1, N_EGROUPS):
        el = jnp.where(gsel == float(gi), lg_t[8 + 8 * gi:16 + 8 * gi], el)
    eidx = lax.broadcasted_iota(I32, el.shape, 0).astype(F32)
    m1 = jnp.max(el, axis=0, keepdims=True)
    i1 = jnp.min(jnp.where(el == m1, eidx, 8.0), axis=0, keepdims=True)
    el2 = jnp.where(eidx == i1, -jnp.inf, el)
    m2 = jnp.max(el2, axis=0, keepdims=True)
    i2 = jnp.min(jnp.where(el2 == m2, eidx, 8.0), axis=0, keepdims=True)
    r = jnp.exp(m2 - m1)
    inv = 1.0 / (1.0 + r)
    gate1 = p_g * inv
    gate2 = p_g * r * inv
    first_lo = i1 < i2
    lo = jnp.minimum(i1, i2)
    hi = jnp.maximum(i1, i2)
    cls = gsel * 64.0 + lo * 8.0 + hi
    return cls, jnp.where(first_lo, gate1, gate2), jnp.where(first_lo, gate2, gate1)


def _norm_and_route(y, ln_g, ln_b, wr_hi, wr_lo, br, x1_ref, route_ref):
    tm, _ = y.shape
    x1 = _layer_norm(y, ln_g, ln_b)
    x1_ref[...] = x1
    x_hi = x1.astype(BF16)
    x_lo = (x1 - x_hi.astype(F32)).astype(BF16)
    lg = _dot(x_hi, wr_hi) + _dot(x_lo, wr_hi) + _dot(x_hi, wr_lo) + br
    cls, g_lo, g_hi = _route_rows(lg.T)
    row = lax.broadcasted_iota(I32, (ROUTE_ROWS, tm), 0)
    route_ref[...] = jnp.where(row == 0, cls, jnp.where(row == 1, g_lo, jnp.where(row == 2, g_hi, 0.0)))


def _mixer_out_specs(tm, d):
    return [pl.BlockSpec((tm, d), lambda i: (i, 0)), pl.BlockSpec((ROUTE_ROWS, tm), lambda i: (0, i))]


def _mixer_out_shape(t, d):
    return [jax.ShapeDtypeStruct((t, d), F32), jax.ShapeDtypeStruct((ROUTE_ROWS, t), F32)]


def _gmlp_kernel(x_ref, w_in_ref, b_in_ref, vn_g_ref, vn_b_ref, w_s_ref, b_st_ref, w_out_ref,
                 ln_g_ref, ln_b_ref, wr_hi_ref, wr_lo_ref, br_ref, x1_ref, route_ref, gated_ref, *, alpha):
    x = x_ref[...]
    tm, _ = x.shape
    groups, chunk, _ = w_s_ref.shape
    z = _dot(x.astype(BF16), w_in_ref[...]) + b_in_ref[...]
    z = 0.5 * z * (1.0 + lax.erf(z * (2.0 ** -0.5)))
    half = z.shape[1] // 2
    gd = half // groups
    u = z[:, :half]
    v = _layer_norm(z[:, half:], vn_g_ref[...], vn_b_ref[...]).astype(BF16)
    r = lax.broadcasted_iota(I32, (chunk, chunk), 0)
    c = lax.broadcasted_iota(I32, (chunk, chunk), 1)
    causal = r >= c
    for g in range(groups):
        w_c = jnp.where(causal, w_s_ref[g], 0.0).astype(BF16)
        bias = b_st_ref[:, g:g + 1]
        for ci in range(tm // chunk):
            rows = slice(ci * chunk, (ci + 1) * chunk)
            cols = slice(g * gd, (g + 1) * gd)
            s = _dot(w_c, v[rows, cols]) + bias
            gated_ref[rows, cols] = (u[rows, cols] * s).astype(BF16)
    mix = _dot(gated_ref[...], w_out_ref[...])
    _norm_and_route(alpha * x + mix, ln_g_ref[...], ln_b_ref[...], wr_hi_ref[...], wr_lo_ref[...],
                    br_ref[...], x1_ref, route_ref)


def _gmlp_layer(xt, w_in, b_in, vn_g, vn_b, w_s, b_s, w_out, ln_g, ln_b, router, alpha, tm):
    t, d = xt.shape
    half = w_out.shape[0]
    wr_hi, wr_lo, br = router
    args = (xt, w_in.astype(BF16), b_in[None], vn_g[None], vn_b[None], w_s, b_s.T, w_out.astype(BF16),
            ln_g[None], ln_b[None], wr_hi, wr_lo, br)
    in_specs = [pl.BlockSpec((tm, d), lambda i: (i, 0))] + [_full(a.shape) for a in args[1:]]
    return pl.pallas_call(
        functools.partial(_gmlp_kernel, alpha=alpha),
        grid=(t // tm,),
        in_specs=in_specs,
        out_specs=_mixer_out_specs(tm, d),
        out_shape=_mixer_out_shape(t, d),
        scratch_shapes=[pltpu.VMEM((tm, half), BF16)],
        compiler_params=_cparams(("parallel",)),
        name="gmlp",
    )(*args)


def _dsa_proj_kernel(x_ref, wq_ref, wc_ref, wqi_ref, wkw_ref, kvg_ref, wuk_t_ref,
                     qlat_ref, ckv_ref, ckv_t_ref, qidx_ref, kidx_ref, wi_t_ref, *, scale_q, scale_w):
    xb = x_ref[...].astype(BF16)
    nq = qlat_ref.shape[0]
    kc = ckv_t_ref.shape[2]
    heads, hd, _ = wuk_t_ref.shape
    q = _dot(xb, wq_ref[...])
    for h in range(heads):
        ql = (_dot(q[:, h * hd:(h + 1) * hd].astype(BF16), wuk_t_ref[h]) * scale_q).astype(BF16)
        for b in range(nq):
            qlat_ref[b, h * Q_BLOCK:(h + 1) * Q_BLOCK, :] = ql[b * Q_BLOCK:(b + 1) * Q_BLOCK]
    c = _dot(xb, wc_ref[...])
    ms = jnp.mean(c * c, axis=-1, keepdims=True)
    c = c * lax.rsqrt(ms + RMS_EPS) * kvg_ref[...]
    ckv_ref[...] = c.astype(BF16)
    for b in range(ckv_t_ref.shape[0]):
        ckv_t_ref[b] = c[b * kc:(b + 1) * kc].T.astype(BF16)
    qi = _dot(xb, wqi_ref[...]).astype(BF16)
    for h in range(IDX_HEADS):
        for b in range(nq):
            qidx_ref[b, h * Q_BLOCK:(h + 1) * Q_BLOCK, :] = qi[b * Q_BLOCK:(b + 1) * Q_BLOCK,
                                                               h * IDX_DIM:(h + 1) * IDX_DIM]
    kw = _dot(xb, wkw_ref[...])
    kidx_ref[...] = kw[:, :IDX_DIM].astype(BF16)
    wi_t_ref[...] = kw.T[IDX_DIM:IDX_DIM + IDX_HEADS] * scale_w


def _dsa_attn_kernel(qidx_ref, wi_t_ref, qlat_ref, kidx_ref, ckv_ref, ckv_t_ref, o_ref,
                     key_ref, key16_ref, acc_ref, m_ref, l_ref, a_ref, p_ref, *, k_sel, heads):
    qb = pl.program_id(1)
    kc = ckv_t_ref.shape[2]
    slab = key_ref.shape[1]
    cps = slab // kc
    nq = Q_BLOCK
    pair = 2 * nq
    n_pairs = heads // 2
    n_chunks = (qb * nq + nq + kc - 1) // kc
    n_slabs = (n_chunks + cps - 1) // cps

    def key_chunk(j):
        return key_ref.at[j // cps, pl.ds(pl.multiple_of((j % cps) * kc, kc), kc), :]

    q_pos = qb * nq + lax.broadcasted_iota(I32, (kc, nq), 1)
    k_off = lax.broadcasted_iota(I32, (kc, nq), 0)

    def key16_chunk(j):
        return key16_ref.at[j // cps, pl.ds(pl.multiple_of((j % cps) * kc, kc), kc), :]

    def pad_chunk(j, carry):
        key_chunk(j)[...] = jnp.full((kc, nq), jnp.nan, F32)
        key16_chunk(j)[...] = jnp.full((kc, nq), jnp.nan, BF16)
        return carry

    lax.fori_loop(n_chunks, n_slabs * cps, pad_chunk, 0)

    def score_chunk(j, carry):
        kch = kidx_ref[pl.ds(pl.multiple_of(j * kc, kc), kc), :]
        tot = None
        for p in range(IDX_HEADS // 2):
            sc = _dot_t(kch, qidx_ref[p * pair:(p + 1) * pair, :])
            for hh in range(2):
                h = 2 * p + hh
                r = jnp.maximum(sc[:, hh * nq:(hh + 1) * nq], 0.0) * wi_t_ref[h:h + 1, :]
                tot = r if tot is None else tot + r
        tot = jnp.where(k_off + j * kc <= q_pos, tot, jnp.nan)
        key_chunk(j)[...] = tot
        key16_chunk(j)[...] = tot.astype(BF16)
        return carry

    lax.fori_loop(0, n_chunks, score_chunk, 0)

    def count(t, strict=False):
        tb = jnp.broadcast_to(t, (COUNT_ACC, nq))

        def body(s, acc):
            for g in range(slab // COUNT_ACC):
                blk = key_ref[s, g * COUNT_ACC:(g + 1) * COUNT_ACC, :]
                acc = jnp.where(blk > tb if strict else blk >= tb, acc + 1.0, acc)
            return acc

        acc = lax.fori_loop(0, n_slabs, body, jnp.zeros((COUNT_ACC, nq), F32))
        return jnp.sum(acc, axis=0, keepdims=True)

    def count16(t):
        rows16 = 2 * COUNT_ACC
        tb = jnp.broadcast_to(t.astype(BF16), (rows16, nq))

        def body(s, acc):
            for g in range(slab // rows16):
                blk = key16_ref[s, g * rows16:(g + 1) * rows16, :]
                acc = jnp.where(blk >= tb, acc + 1.0, acc)
            return acc

        acc = lax.fori_loop(0, n_slabs, body, jnp.zeros((rows16, nq), BF16))
        return jnp.sum(acc.astype(F32), axis=0, keepdims=True)

    def decode(code):
        return lax.bitcast_convert_type(code ^ ((code >> 31) & 0x7FFFFFFF), F32)

    def decode16(c16):
        return decode(jnp.left_shift(c16, 16) | jnp.where(c16 < 0, 0xFFFF, 0))

    def bit16_step(b, c16):
        t = c16 + jnp.left_shift(jnp.int32(1), 15 - b)
        return jnp.where(count16(decode16(t)) >= float(k_sel), t, c16)

    no_code = -(2 ** 15)
    c16 = lax.fori_loop(0, 16, bit16_step, jnp.full((1, nq), no_code, I32))

    def bit_step(b, c):
        t = c + jnp.left_shift(jnp.int32(1), 17 - b)
        return jnp.where((count(decode(t)) >= float(k_sel)) & (t > c), t, c)

    c_sel = lax.fori_loop(0, 18, bit_step, jnp.left_shift(jnp.maximum(c16, no_code + 1) - 1, 16))
    c_sel = jnp.where(c16 == no_code, -jnp.inf, decode(c_sel))
    n_ge = jnp.where(c16 == no_code, 0.0, count(c_sel))

    has_tie = jnp.max(n_ge) > float(k_sel)

    @pl.when(has_tie)
    def _():
        need = float(k_sel) - count(c_sel, strict=True)
        tie_col = n_ge > float(k_sel)
        cb = jnp.broadcast_to(c_sel, (kc, nq))
        lower = (lax.broadcasted_iota(I32, (kc, kc), 1) < lax.broadcasted_iota(I32, (kc, kc), 0))
        lower = jnp.where(lower, 1.0, 0.0).astype(BF16)

        def fix(j, seen):
            blk = key_chunk(j)[...]
            eq = blk == cb
            eq_f = jnp.where(eq, 1.0, 0.0)
            rank = seen + _dot(lower, eq_f.astype(BF16))
            drop = eq & tie_col & (rank >= need)
            key_chunk(j)[...] = jnp.where(drop, jnp.nan, blk)
            return seen + jnp.sum(eq_f, axis=0, keepdims=True)

        lax.fori_loop(0, n_chunks, fix, jnp.zeros((1, nq), F32))

    thr = jnp.broadcast_to(c_sel, (kc, nq))
    m_ref[...] = jnp.full(m_ref.shape, NEG_BIG, F32)
    l_ref[...] = jnp.zeros(l_ref.shape, F32)
    acc_ref[...] = jnp.zeros(acc_ref.shape, F32)

    def softmax_chunk(j):
        ck = ckv_ref[pl.ds(pl.multiple_of(j * kc, kc), kc), :]
        bias = jnp.where(key_chunk(j)[...] >= thr, 0.0, NEG_BIG)
        bias = jnp.concatenate([bias, bias], axis=1)
        slot = j % 2
        for p in range(n_pairs):
            lg = _dot_t(ck, qlat_ref[p * pair:(p + 1) * pair, :]) + bias
            m_old = m_ref[p]
            m_new = jnp.maximum(m_old, jnp.max(lg, axis=0, keepdims=True))
            pr = jnp.exp2(lg - m_new)
            alpha = jnp.exp2(m_old - m_new)
            l_ref[p] = alpha * l_ref[p] + jnp.sum(pr, axis=0, keepdims=True)
            m_ref[p] = m_new
            a_ref[slot, p] = alpha
            p_ref[slot, p] = pr.astype(BF16)

    def accumulate_chunk(j):
        ck_t = ckv_t_ref[j]
        slot = j % 2
        for p in range(n_pairs):
            acc_ref[p] = a_ref[slot, p] * acc_ref[p] + _dot(ck_t, p_ref[slot, p])

    def att_step(j, carry):
        accumulate_chunk(j - 1)
        softmax_chunk(j)
        return carry

    softmax_chunk(0)
    lax.fori_loop(1, n_chunks, att_step, 0)
    accumulate_chunk(n_chunks - 1)
    for p in range(n_pairs):
        o_ref[p * pair:(p + 1) * pair, :] = (acc_ref[p] * (1.0 / l_ref[p])).T.astype(o_ref.dtype)


def _dsa_out_kernel(o_ref, x_ref, wuv_ref, wout_ref, ln_g_ref, ln_b_ref, wr_hi_ref, wr_lo_ref, br_ref,
                    x1_ref, route_ref, o2_ref, *, alpha):
    nq = o_ref.shape[0]
    heads, _, hd = wuv_ref.shape
    for b in range(nq):
        for h in range(heads):
            oh = o_ref[b, h * Q_BLOCK:(h + 1) * Q_BLOCK, :]
            o2_ref[b * Q_BLOCK:(b + 1) * Q_BLOCK, h * hd:(h + 1) * hd] = _dot(oh, wuv_ref[h]).astype(BF16)
    mix = _dot(o2_ref[...], wout_ref[...])
    _norm_and_route(alpha * x_ref[...] + mix, ln_g_ref[...], ln_b_ref[...], wr_hi_ref[...], wr_lo_ref[...],
                    br_ref[...], x1_ref, route_ref)


def _dsa_layer(xt, batch, w_in, kv_g, w_uk, w_uv, w_out, ln_g, ln_b, router, alpha, tm):
    t, d = xt.shape
    seq = t // batch
    heads, lat, hd = w_uk.shape
    qw = heads * hd
    nqb = t // Q_BLOCK
    nq = tm // Q_BLOCK
    k_sel = min(TOPK_MAX, seq // 4)
    kc = min(KEY_CHUNK, seq)
    slab = min(COUNT_SLAB, seq)
    assert seq // (2 * COUNT_ACC) <= 256, "bf16 partial counts of the threshold search must stay exact"
    w_in = w_in.astype(BF16)
    o1, o2, o3 = qw, qw + lat, qw + lat + IDX_HEADS * IDX_DIM
    wkw = jnp.pad(w_in[:, o3:], ((0, 0), (0, LANES - (w_in.shape[1] - o3))))
    proj_args = (xt, w_in[:, :o1], w_in[:, o1:o2], w_in[:, o2:o3], wkw, kv_g[None],
                 jnp.swapaxes(w_uk, 1, 2).astype(BF16))
    qlat, ckv, ckv_t, qidx, kidx, wi_t = pl.pallas_call(
        functools.partial(_dsa_proj_kernel, scale_q=hd ** -0.5 * LOG2E, scale_w=(IDX_HEADS * IDX_DIM) ** -0.5),
        grid=(t // tm,),
        in_specs=[pl.BlockSpec((tm, d), lambda i: (i, 0))] + [_full(a.shape) for a in proj_args[1:]],
        out_specs=[pl.BlockSpec((nq, heads * Q_BLOCK, lat), lambda i: (i, 0, 0)),
                   pl.BlockSpec((tm, lat), lambda i: (i, 0)),
                   pl.BlockSpec((tm // kc, lat, kc), lambda i: (i, 0, 0)),
                   pl.BlockSpec((nq, IDX_HEADS * Q_BLOCK, IDX_DIM), lambda i: (i, 0, 0)),
                   pl.BlockSpec((tm, IDX_DIM), lambda i: (i, 0)),
                   pl.BlockSpec((IDX_HEADS, tm), lambda i: (0, i))],
        out_shape=[jax.ShapeDtypeStruct((nqb, heads * Q_BLOCK, lat), BF16),
                   jax.ShapeDtypeStruct((t, lat), BF16),
                   jax.ShapeDtypeStruct((t // kc, lat, kc), BF16),
                   jax.ShapeDtypeStruct((nqb, IDX_HEADS * Q_BLOCK, IDX_DIM), BF16),
                   jax.ShapeDtypeStruct((t, IDX_DIM), BF16),
                   jax.ShapeDtypeStruct((IDX_HEADS, t), F32)],
        compiler_params=_cparams(("parallel",)),
        name="dsa_proj",
    )(*proj_args)

    nq_seq = seq // Q_BLOCK
    n_kc = seq // kc
    rows = heads * Q_BLOCK
    o = pl.pallas_call(
        functools.partial(_dsa_attn_kernel, k_sel=k_sel, heads=heads),
        grid=(batch, nq_seq),
        in_specs=[pl.BlockSpec((None, IDX_HEADS * Q_BLOCK, IDX_DIM), lambda b, q: (b * nq_seq + q, 0, 0)),
                  pl.BlockSpec((IDX_HEADS, Q_BLOCK), lambda b, q: (0, b * nq_seq + q)),
                  pl.BlockSpec((None, rows, lat), lambda b, q: (b * nq_seq + q, 0, 0)),
                  pl.BlockSpec((None, seq, IDX_DIM), lambda b, q: (b, 0, 0)),
                  pl.BlockSpec((None, seq, lat), lambda b, q: (b, 0, 0)),
                  pl.BlockSpec((None, n_kc, lat, kc), lambda b, q: (b, 0, 0, 0))],
        out_specs=pl.BlockSpec((None, rows, lat), lambda b, q: (b * nq_seq + q, 0, 0)),
        out_shape=jax.ShapeDtypeStruct((nqb, rows, lat), BF16),
        scratch_shapes=[pltpu.VMEM((seq // slab, slab, Q_BLOCK), F32),
                        pltpu.VMEM((seq // slab, slab, Q_BLOCK), BF16),
                        pltpu.VMEM((heads // 2, lat, 2 * Q_BLOCK), F32),
                        pltpu.VMEM((heads // 2, 1, 2 * Q_BLOCK), F32),
                        pltpu.VMEM((heads // 2, 1, 2 * Q_BLOCK), F32),
                        pltpu.VMEM((2, heads // 2, 1, 2 * Q_BLOCK), F32),
                        pltpu.VMEM((2, heads // 2, kc, 2 * Q_BLOCK), BF16)],
        compiler_params=_cparams(("parallel", "arbitrary")),
        name="dsa_attn",
    )(qidx, wi_t, qlat, kidx.reshape(batch, seq, IDX_DIM), ckv.reshape(batch, seq, lat),
      ckv_t.reshape(batch, n_kc, lat, kc))

    wr_hi, wr_lo, br = router
    out_args = (o, xt, w_uv.astype(BF16), w_out.astype(BF16), ln_g[None], ln_b[None], wr_hi, wr_lo, br)
    return pl.pallas_call(
        functools.partial(_dsa_out_kernel, alpha=alpha),
        grid=(t // tm,),
        in_specs=[pl.BlockSpec((nq, rows, lat), lambda i: (i, 0, 0)),
                  pl.BlockSpec((tm, d), lambda i: (i, 0))] + [_full(a.shape) for a in out_args[2:]],
        out_specs=_mixer_out_specs(tm, d),
        out_shape=_mixer_out_shape(t, d),
        scratch_shapes=[pltpu.VMEM((tm, qw), BF16)],
        compiler_params=_cparams(("parallel",)),
        name="dsa_out",
    )(*out_args)


DMA_UNROLL = 8


def _start_token_moves(n, copy_of):
    def body(g, carry):
        for u in range(DMA_UNROLL):
            copy_of(g * DMA_UNROLL + u).start(priority=u % 2)
        return carry
    lax.fori_loop(0, n // DMA_UNROLL, body, 0)


def _tok(ref, i, rows):
    return ref.at[pl.ds(i * rows, rows)]


def _rows_of(ref, j, n, rows):
    return ref.at[pl.ds(j, n, stride=rows), :]


class _TokenGather:
    def __init__(self, i, n, idx_hbm, src_hbm, idx_smem, buf, isem, rsem):
        self.i, self.n = i, n
        self.idx_hbm, self.src_hbm, self.idx_smem, self.buf, self.isem, self.rsem = (
            idx_hbm, src_hbm, idx_smem, buf, isem, rsem)
        self.tm = idx_smem.shape[2]
        self.rows = buf.shape[1] // self.tm
        self.slot = i % 2

    def _idx_copy(self, blk, s):
        return pltpu.make_async_copy(self.idx_hbm.at[blk], self.idx_smem.at[s], self.isem.at[s])

    def _token_copy(self, s, r):
        return pltpu.make_async_copy(_tok(self.src_hbm, self.idx_smem[s, 0, r], self.rows),
                                     _tok(self.buf.at[s], r, self.rows), self.rsem.at[s])

    def _wait_tokens(self, s):
        pltpu.make_async_copy(self.src_hbm.at[pl.ds(0, self.tm * self.rows)], self.buf.at[s], self.rsem.at[s]).wait()

    def arrive(self):
        @pl.when(self.i == 0)
        def _():
            first = self._idx_copy(0, 0)
            first.start()
            first.wait()
            _start_token_moves(self.tm, lambda r: self._token_copy(0, r))
            self._idx_copy(min(1, self.n - 1), 1).start()

        self._idx_copy(jnp.minimum(self.i + 1, self.n - 1), 1 - self.slot).wait()
        self._wait_tokens(self.slot)

    def prefetch(self):
        nslot = 1 - self.slot
        for r in range(self.tm):
            self._token_copy(nslot, r).start(priority=r % 2)
        self._idx_copy(jnp.minimum(self.i + 2, self.n - 1), self.slot).start()

    def finish(self):
        @pl.when(self.i == self.n - 1)
        def _():
            self._wait_tokens(1 - self.slot)
            self._idx_copy(self.n - 1, self.slot).wait()


def _plan_kernel(route_ref, x_ref, pos_ref, tinfo_ref, xs_ref,
                 hist_ref, off_ref, carry_ref, stage_ref, ppos_vmem, ppos_smem, fill_vmem, fill_smem, zero_ref,
                 ssem, psem, zsem, *, tile):
    ps = pl.program_id(0)
    b = pl.program_id(1)
    nb = pl.num_programs(1)
    n_cls, pb = hist_ref.shape
    kt = tinfo_ref.shape[1]
    cls = route_ref[0:1, :].astype(I32)
    cid = lax.broadcasted_iota(I32, (n_cls, pb), 0)
    onehot = jnp.where(cid == cls, 1.0, 0.0)

    def zero_fill(start):
        def piece(tok0, n_tok):
            cp = pltpu.make_async_copy(zero_ref.at[pl.ds(0, n_tok * X_ROWS)],
                                       xs_ref.at[pl.ds(tok0 * X_ROWS, n_tok * X_ROWS)], zsem)
            cp.start() if start else cp.wait()

        def per_class(c, carry):
            first, n = fill_smem[0, c], fill_smem[1, c]
            bit = zero_ref.shape[0] // X_ROWS
            while bit:
                @pl.when((n & bit) != 0)
                def _(bit=bit):
                    piece(first + (n & -(2 * bit)), bit)
                bit //= 2
            return carry

        lax.fori_loop(0, n_cls, per_class, 0)

        def per_tile(k, carry):
            piece(k * tile, tile)
            return carry

        lax.fori_loop(fill_smem[2, 0], xs_ref.shape[0] // (tile * X_ROWS), per_tile, 0)

    @pl.when(ps == 0)
    def _():
        @pl.when(b == 0)
        def _():
            hist_ref[...] = jnp.zeros(hist_ref.shape, F32)
            zero_ref[...] = jnp.zeros(zero_ref.shape, F32)
        hist_ref[...] += onehot

    @pl.when((ps == 1) & (b == 0))
    def _():
        counts = jnp.sum(hist_ref[...], axis=1, keepdims=True)
        tiles = jnp.floor((counts + float(tile - 1)) * (1.0 / tile))
        cid_col = lax.broadcasted_iota(I32, (n_cls, 1), 0)
        per_group = n_cls // N_EGROUPS
        for g in range(N_EGROUPS):
            g_tiles = jnp.sum(tiles[g * per_group:(g + 1) * per_group], axis=0, keepdims=True)
            extra = jnp.ceil(g_tiles * (1.0 / MOE_STEP_TILES)) * float(MOE_STEP_TILES) - g_tiles
            tiles = jnp.where(cid_col == (g + 1) * per_group - 1, extra, tiles)
        lower = lax.broadcasted_iota(I32, (n_cls, n_cls), 1) < lax.broadcasted_iota(I32, (n_cls, n_cls), 0)
        lower = jnp.where(lower, 1.0, 0.0).astype(BF16)
        off = _dot(lower, jnp.broadcast_to(tiles, (n_cls, LANES)).astype(BF16))[:, :1]
        off_ref[...] = off * float(tile)
        carry_ref[...] = jnp.zeros(carry_ref.shape, F32)
        k = lax.broadcasted_iota(I32, (n_cls, kt), 1).astype(F32)
        cid_f = lax.broadcasted_iota(I32, (n_cls, kt), 0).astype(F32)
        mine = (k >= off) & (k < off + tiles)
        tcls = jnp.sum(jnp.where(mine, cid_f, 0.0), axis=0, keepdims=True)
        n_used = jnp.sum(tiles, axis=0, keepdims=True)
        last = jnp.max(jnp.where(tiles > 0.0, cid_f[:, :1], 0.0), axis=0, keepdims=True)
        tcls = jnp.where(k[:1] >= n_used, last, tcls)
        row = lax.broadcasted_iota(I32, (ROUTE_ROWS, kt), 0)
        tinfo_ref[...] = jnp.where(row == 0, tcls, jnp.where(row == 1, n_used, 0.0)).astype(I32)
        as_row = lambda col: jnp.broadcast_to(col, (n_cls, LANES)).T[0:1]
        frow = lax.broadcasted_iota(I32, fill_vmem.shape, 0)
        fill_vmem[...] = jnp.where(frow == 0, as_row(off * float(tile) + counts),
                                   jnp.where(frow == 1, as_row(tiles * float(tile) - counts), n_used)).astype(I32)
        to_fill = pltpu.make_async_copy(fill_vmem, fill_smem, psem)
        to_fill.start()
        to_fill.wait()
        zero_fill(start=True)

    @pl.when(ps == 1)
    def _():
        upper = lax.broadcasted_iota(I32, (pb, pb), 0) < lax.broadcasted_iota(I32, (pb, pb), 1)
        upper = jnp.where(upper, 1.0, 0.0).astype(BF16)
        before = _dot(onehot.astype(BF16), upper)
        ppos = jnp.sum(onehot * (before + carry_ref[...] + off_ref[...]), axis=0, keepdims=True).astype(I32)
        carry_ref[...] += jnp.sum(onehot, axis=1, keepdims=True)
        gb = pos_ref.shape[2]
        for h in range(pos_ref.shape[0]):
            pos_ref[h] = ppos[:, h * gb:(h + 1) * gb]
        ppos_vmem[...] = ppos
        to_smem = pltpu.make_async_copy(ppos_vmem, ppos_smem, psem)
        to_smem.start()

        slot = b % 2
        stage = stage_ref.at[slot]

        def drain(s):
            pltpu.make_async_copy(stage_ref.at[s], xs_ref.at[pl.ds(0, pb * X_ROWS)], ssem.at[s]).wait()

        @pl.when(b >= 2)
        def _():
            drain(slot)

        for j in range(Y_ROWS):
            _rows_of(stage, j, pb, X_ROWS)[...] = x_ref[:, j * LANES:(j + 1) * LANES]
        record = jnp.concatenate([route_ref[...], jnp.zeros((LANES - ROUTE_ROWS, pb), F32)], axis=0).T
        _rows_of(stage, Y_ROWS, pb, X_ROWS)[...] = record

        to_smem.wait()
        _start_token_moves(pb, lambda r: pltpu.make_async_copy(
            _tok(stage, r, X_ROWS), _tok(xs_ref, ppos_smem[0, r], X_ROWS), ssem.at[slot]))

        @pl.when(b == nb - 1)
        def _():
            drain(slot)

            @pl.when(nb > 1)
            def _():
                drain(1 - slot)

            zero_fill(start=False)


def _moe_kernel(tcls_ref, nused_ref, xs_ref, win_ref, wout_ref, ys_ref):
    i = pl.program_id(0)
    tm = xs_ref.shape[0] // (X_ROWS * MOE_STEP_TILES)
    ff = wout_ref.shape[1]

    @pl.when(i * MOE_STEP_TILES < nused_ref[0])
    def _():
        for jt in range(MOE_STEP_TILES):
            cls = tcls_ref[i * MOE_STEP_TILES + jt]
            x0, y0 = jt * tm * X_ROWS, jt * tm * Y_ROWS
            xb = jnp.concatenate([_rows_of(xs_ref, x0 + j, tm, X_ROWS)[...].astype(BF16) for j in range(Y_ROWS)],
                                 axis=1)
            record = _rows_of(xs_ref, x0 + Y_ROWS, tm, X_ROWS)[...]
            y = None
            for e, gate in (((cls >> 3) & 7, record[:, 1:2]), (cls & 7, record[:, 2:3])):
                h = _dot(xb, win_ref[e])
                a = h[:, :ff]
                act = (a * jax.nn.sigmoid(a) * h[:, ff:] * gate).astype(BF16)
                ye = _dot(act, wout_ref[e])
                y = ye if y is None else y + ye
            for j in range(Y_ROWS):
                _rows_of(ys_ref, y0 + j, tm, Y_ROWS)[...] = y[:, j * LANES:(j + 1) * LANES]

    @pl.when(i * MOE_STEP_TILES >= nused_ref[0])
    def _():
        ys_ref[...] = jnp.zeros(ys_ref.shape, F32)


def _post_kernel(pos_ref, ys_ref, x1_ref, p_ref, ln_g_ref, ln_b_ref, pw_ref, gw_ref, gb_ref, out_ref,
                 idx_smem, ybuf, isem, rsem, *, alpha, n_blocks):
    i = pl.program_id(0)
    tm = x1_ref.shape[0]
    gather = _TokenGather(i, n_blocks, pos_ref, ys_ref, idx_smem, ybuf, isem, rsem)
    gather.arrive()
    gather.prefetch()
    yb = ybuf.at[i % 2]
    ffn = jnp.concatenate([_rows_of(yb, j, tm, Y_ROWS)[...] for j in range(Y_ROWS)], axis=1)
    x2 = _layer_norm(alpha * x1_ref[...] + ffn, ln_g_ref[...], ln_b_ref[...])
    gate = jax.nn.sigmoid(_dot(x2.astype(BF16), gw_ref[...]) + gb_ref[...])
    out_ref[...] = x2 + _dot(p_ref[...].astype(BF16), pw_ref[...]) * gate
    gather.finish()


def _moe_and_post(x1, route, layer, p, e_w_in, e_w_out, ln_g, ln_b, ple_w, ple_gw, ple_gb, alpha):
    t, d = x1.shape
    assert d == LANES * Y_ROWS
    depth, n_exp, _, ff2 = e_w_in.shape
    epg = n_exp // N_EGROUPS
    pb = _tile(t, PLAN_BLOCK)
    nb = t // pb
    gb = _tile(pb, GATHER_BLOCK)
    nt = t // MOE_TILE + N_EGROUPS * (epg * (epg - 1) // 2) + N_EGROUPS * (MOE_STEP_TILES - 1)
    nt = -(-nt // MOE_STEP_TILES) * MOE_STEP_TILES
    kt = -(-nt // LANES) * LANES
    n_sorted = nt * MOE_TILE

    pos, tinfo, xs = pl.pallas_call(
        functools.partial(_plan_kernel, tile=MOE_TILE),
        grid=(2, nb),
        in_specs=[pl.BlockSpec((ROUTE_ROWS, pb), lambda ps, b: (0, b)),
                  pl.BlockSpec((pb, d), lambda ps, b: (ps * b, 0))],
        out_specs=[pl.BlockSpec((pb // gb, 1, gb), lambda ps, b: (ps * b, 0, 0)),
                   pl.BlockSpec((ROUTE_ROWS, kt), lambda ps, b: (0, 0)),
                   pl.BlockSpec(memory_space=pl.ANY)],
        out_shape=[jax.ShapeDtypeStruct((t // gb, 1, gb), I32),
                   jax.ShapeDtypeStruct((ROUTE_ROWS, kt), I32),
                   jax.ShapeDtypeStruct((n_sorted * X_ROWS, LANES), F32)],
        scratch_shapes=[pltpu.VMEM((N_CLASS, pb), F32),
                        pltpu.VMEM((N_CLASS, 1), F32),
                        pltpu.VMEM((N_CLASS, 1), F32),
                        pltpu.VMEM((2, pb * X_ROWS, LANES), F32),
                        pltpu.VMEM((1, pb), I32),
                        pltpu.SMEM((1, pb), I32),
                        pltpu.VMEM((ROUTE_ROWS, N_CLASS), I32),
                        pltpu.SMEM((ROUTE_ROWS, N_CLASS), I32),
                        pltpu.VMEM((MOE_TILE * ZERO_TILES * X_ROWS, LANES), F32),
                        pltpu.SemaphoreType.DMA((2,)),
                        pltpu.SemaphoreType.DMA(()),
                        pltpu.SemaphoreType.DMA(())],
        compiler_params=_cparams(("arbitrary", "arbitrary")),
        name="plan",
    )(route, x1)

    w_in = e_w_in.reshape(depth, N_EGROUPS, epg, d, ff2)
    w_out = e_w_out.reshape(depth, N_EGROUPS, epg, ff2 // 2, d)
    step_rows = MOE_TILE * MOE_STEP_TILES

    def group(i, tile_cls):
        return tile_cls[i * MOE_STEP_TILES] >> 6
    ys = pl.pallas_call(
        _moe_kernel,
        grid_spec=pltpu.PrefetchScalarGridSpec(
            num_scalar_prefetch=2,
            grid=(nt // MOE_STEP_TILES,),
            in_specs=[pl.BlockSpec((step_rows * X_ROWS, LANES), lambda i, tc, nu: (i, 0)),
                      pl.BlockSpec((None, None, epg, d, ff2), lambda i, tc, nu: (layer, group(i, tc), 0, 0, 0),
                                   pipeline_mode=pl.Buffered(1)),
                      pl.BlockSpec((None, None, epg, ff2 // 2, d), lambda i, tc, nu: (layer, group(i, tc), 0, 0, 0),
                                   pipeline_mode=pl.Buffered(1))],
            out_specs=pl.BlockSpec((step_rows * Y_ROWS, LANES), lambda i, tc, nu: (i, 0))),
        out_shape=jax.ShapeDtypeStruct((n_sorted * Y_ROWS, LANES), F32),
        compiler_params=_cparams(("arbitrary",)),
        name="moe",
    )(tinfo[0, :nt], tinfo[1, :1], xs, w_in, w_out)

    post_args = (pos, ys, x1, p, ln_g[None], ln_b[None], ple_w.astype(BF16), ple_gw.astype(BF16), ple_gb[None])
    return pl.pallas_call(
        functools.partial(_post_kernel, alpha=alpha, n_blocks=t // gb),
        grid=(t // gb,),
        in_specs=[pl.BlockSpec(memory_space=pl.ANY),
                  pl.BlockSpec(memory_space=pl.ANY),
                  pl.BlockSpec((gb, d), lambda i: (i, 0)),
                  pl.BlockSpec((None, gb, p.shape[2]), lambda i: (layer, i, 0))]
                 + [_full(a.shape) for a in post_args[4:]],
        out_specs=pl.BlockSpec((gb, d), lambda i: (i, 0)),
        out_shape=jax.ShapeDtypeStruct((t, d), F32),
        scratch_shapes=[pltpu.SMEM((2, 1, gb), I32), pltpu.VMEM((2, gb * Y_ROWS, LANES), F32),
                        pltpu.SemaphoreType.DMA((2,)), pltpu.SemaphoreType.DMA((2,))],
        compiler_params=_cparams(("arbitrary",)),
        name="post",
    )(*post_args)


def _router_params(wg, bg, we, be):
    d = wg.shape[0]
    w = jnp.zeros((d, ROUTE_COLS), F32).at[:, :N_EGROUPS].set(wg).at[:, 8:8 + we.shape[1]].set(we)
    b = jnp.zeros((1, ROUTE_COLS), F32).at[0, :N_EGROUPS].set(bg).at[0, 8:8 + be.shape[0]].set(be)
    hi = w.astype(BF16)
    return hi, (w - hi.astype(F32)).astype(BF16), b


def _tile(t, want):
    while t % want:
        want //= 2
    return want


def kernel(x, p, a_w_in, a_b_in, a_vn_g, a_vn_b, a_w_s, a_b_s, a_w_out, b_w_in, b_kv_g, b_w_uk, b_w_uv, b_w_out,
           ln1_g, ln1_b, ln2_g, ln2_b, r_wg, r_bg, r_we, r_be, e_w_in, e_w_out, ple_w, ple_gw, ple_gb):
    batch, seq, d = x.shape
    t = batch * seq
    depth = p.shape[0]
    alpha = (2 * depth) ** 0.25
    chunk = a_w_s.shape[-1]
    xt = x.reshape(t, d)
    pt = p.reshape(depth, t, p.shape[-1])
    for i in range(depth):
        j = i // 2
        router = _router_params(r_wg[i], r_bg[i], r_we[i], r_be[i])
        if i % 2 == 0:
            x1, route = _gmlp_layer(xt, a_w_in[j], a_b_in[j], a_vn_g[j], a_vn_b[j], a_w_s[j], a_b_s[j],
                                    a_w_out[j], ln1_g[i], ln1_b[i], router, alpha, max(chunk, _tile(t, MIXER_TILE)))
        else:
            x1, route = _dsa_layer(xt, batch, b_w_in[j], b_kv_g[j], b_w_uk[j], b_w_uv[j], b_w_out[j],
                                   ln1_g[i], ln1_b[i], router, alpha, max(KEY_CHUNK, _tile(t, MIXER_TILE)))
        xt = _moe_and_post(x1, route, i, pt, e_w_in, e_w_out, ln2_g[i], ln2_b[i],
                           ple_w[i], ple_gw[i], ple_gb[i], alpha)
    return xt.reshape(batch, seq, d)
```

```python
import functools

import jax
import jax.numpy as jnp
from jax import lax
from jax.experimental import pallas as pl
from jax.experimental.pallas import tpu as pltpu

F32, BF16, I32 = jnp.float32, jnp.bfloat16, jnp.int32

IDX_HEADS = 8
IDX_DIM = 64
TOPK_MAX = 256
N_EGROUPS = 4
LN_EPS = 1e-5
RMS_EPS = 1e-6

LANES = 128
MIXER_TILE = 1024
Q_BLOCK = 128
KEY_CHUNK = 512
COUNT_SLAB = 1024
COUNT_ACC = 64
LOG2E = 1.4426950408889634
VMEM_LIMIT = 48 * 1024 * 1024

ROUTE_COLS = LANES
ROUTE_ROWS = 8
N_CLASS = 64 * N_EGROUPS
Y_ROWS = 8
X_ROWS = Y_ROWS + 1
PLAN_BLOCK = 1024
GATHER_BLOCK = 512
MOE_TILE = 128
MOE_STEP_TILES = 4
ZERO_TILES = 1 << ((MOE_STEP_TILES - 1).bit_length() - 1) if MOE_STEP_TILES > 1 else 1
INT_MIN = -(2 ** 31)
NEG_BIG = -1e30


def _cparams(sem, vmem=VMEM_LIMIT):
    return pltpu.CompilerParams(dimension_semantics=sem, vmem_limit_bytes=vmem)


def _full(shape):
    n = len(shape)
    return pl.BlockSpec(shape, lambda *_: (0,) * n)


def _dot(a, b):
    return jnp.dot(a, b, preferred_element_type=F32)


def _dot_t(a, b):
    return lax.dot_general(a, b, (((1,), (1,)), ((), ())), preferred_element_type=F32)


def _layer_norm(x, g, b):
    mu = jnp.mean(x, axis=-1, keepdims=True)
    xc = x - mu
    var = jnp.mean(xc * xc, axis=-1, keepdims=True)
    return xc * lax.rsqrt(var + LN_EPS) * g + b


def _route_rows(lg_t):
    g = [lg_t[i:i + 1] for i in range(N_EGROUPS)]
    gmax = functools.reduce(jnp.maximum, g)
    gsel = jnp.where(g[0] >= gmax, 0.0, jnp.where(g[1] >= gmax, 1.0, jnp.where(g[2] >= gmax, 2.0, 3.0)))
    den = functools.reduce(lambda a, b: a + b, [jnp.exp(gi - gmax) for gi in g])
    p_g = 1.0 / den
    el = lg_t[8:16]
    for gi in range(1, N_EGROUPS):
        el = jnp.where(gsel == float(gi), lg_t[8 + 8 * gi:16 + 8 * gi], el)
    eidx = lax.broadcasted_iota(I32, el.shape, 0).astype(F32)
    m1 = jnp.max(el, axis=0, keepdims=True)
    i1 = jnp.min(jnp.where(el == m1, eidx, 8.0), axis=0, keepdims=True)
    el2 = jnp.where(eidx == i1, -jnp.inf, el)
    m2 = jnp.max(el2, axis=0, keepdims=True)
    i2 = jnp.min(jnp.where(el2 == m2, eidx, 8.0), axis=0, keepdims=True)
    r = jnp.exp(m2 - m1)
    inv = 1.0 / (1.0 + r)
    gate1 = p_g * inv
    gate2 = p_g * r * inv
    first_lo = i1 < i2
    lo = jnp.minimum(i1, i2)
    hi = jnp.maximum(i1, i2)
    cls = gsel * 64.0 + lo * 8.0 + hi
    return cls, jnp.where(first_lo, gate1, gate2), jnp.where(first_lo, gate2, gate1)


def _norm_and_route(y, ln_g, ln_b, wr_hi, wr_lo, br, x1_ref, route_ref):
    tm, _ = y.shape
    x1 = _layer_norm(y, ln_g, ln_b)
    x1_ref[...] = x1
    x_hi = x1.astype(BF16)
    x_lo = (x1 - x_hi.astype(F32)).astype(BF16)
    lg = _dot(x_hi, wr_hi) + _dot(x_lo, wr_hi) + _dot(x_hi, wr_lo) + br
    cls, g_lo, g_hi = _route_rows(lg.T)
    row = lax.broadcasted_iota(I32, (ROUTE_ROWS, tm), 0)
    route_ref[...] = jnp.where(row == 0, cls, jnp.where(row == 1, g_lo, jnp.where(row == 2, g_hi, 0.0)))


def _mixer_out_specs(tm, d):
    return [pl.BlockSpec((tm, d), lambda i: (i, 0)), pl.BlockSpec((ROUTE_ROWS, tm), lambda i: (0, i))]


def _mixer_out_shape(t, d):
    return [jax.ShapeDtypeStruct((t, d), F32), jax.ShapeDtypeStruct((ROUTE_ROWS, t), F32)]


def _gmlp_kernel(x_ref, w_in_ref, b_in_ref, vn_g_ref, vn_b_ref, w_s_ref, b_st_ref, w_out_ref,
                 ln_g_ref, ln_b_ref, wr_hi_ref, wr_lo_ref, br_ref, x1_ref, route_ref, gated_ref, *, alpha):
    x = x_ref[...]
    tm, _ = x.shape
    groups, chunk, _ = w_s_ref.shape
    z = _dot(x.astype(BF16), w_in_ref[...]) + b_in_ref[...]
    z = 0.5 * z * (1.0 + lax.erf(z * (2.0 ** -0.5)))
    half = z.shape[1] // 2
    gd = half // groups
    u = z[:, :half]
    v = _layer_norm(z[:, half:], vn_g_ref[...], vn_b_ref[...]).astype(BF16)
    r = lax.broadcasted_iota(I32, (chunk, chunk), 0)
    c = lax.broadcasted_iota(I32, (chunk, chunk), 1)
    causal = r >= c
    for g in range(groups):
        w_c = jnp.where(causal, w_s_ref[g], 0.0).astype(BF16)
        bias = b_st_ref[:, g:g + 1]
        for ci in range(tm // chunk):
            rows = slice(ci * chunk, (ci + 1) * chunk)
            cols = slice(g * gd, (g + 1) * gd)
            s = _dot(w_c, v[rows, cols]) + bias
            gated_ref[rows, cols] = (u[rows, cols] * s).astype(BF16)
    mix = _dot(gated_ref[...], w_out_ref[...])
    _norm_and_route(alpha * x + mix, ln_g_ref[...], ln_b_ref[...], wr_hi_ref[...], wr_lo_ref[...],
                    br_ref[...], x1_ref, route_ref)


def _gmlp_layer(xt, w_in, b_in, vn_g, vn_b, w_s, b_s, w_out, ln_g, ln_b, router, alpha, tm):
    t, d = xt.shape
    half = w_out.shape[0]
    wr_hi, wr_lo, br = router
    args = (xt, w_in.astype(BF16), b_in[None], vn_g[None], vn_b[None], w_s, b_s.T, w_out.astype(BF16),
            ln_g[None], ln_b[None], wr_hi, wr_lo, br)
    in_specs = [pl.BlockSpec((tm, d), lambda i: (i, 0))] + [_full(a.shape) for a in args[1:]]
    return pl.pallas_call(
        functools.partial(_gmlp_kernel, alpha=alpha),
        grid=(t // tm,),
        in_specs=in_specs,
        out_specs=_mixer_out_specs(tm, d),
        out_shape=_mixer_out_shape(t, d),
        scratch_shapes=[pltpu.VMEM((tm, half), BF16)],
        compiler_params=_cparams(("parallel",)),
        name="gmlp",
    )(*args)


def _dsa_proj_kernel(x_ref, wq_ref, wc_ref, wqi_ref, wkw_ref, kvg_ref, wuk_t_ref,
                     qlat_ref, ckv_ref, ckv_t_ref, qidx_ref, kidx_ref, wi_t_ref, *, scale_q, scale_w):
    xb = x_ref[...].astype(BF16)
    nq = qlat_ref.shape[0]
    kc = ckv_t_ref.shape[2]
    heads, hd, _ = wuk_t_ref.shape
    q = _dot(xb, wq_ref[...])
    for h in range(heads):
        ql = (_dot(q[:, h * hd:(h + 1) * hd].astype(BF16), wuk_t_ref[h]) * scale_q).astype(BF16)
        for b in range(nq):
            qlat_ref[b, h * Q_BLOCK:(h + 1) * Q_BLOCK, :] = ql[b * Q_BLOCK:(b + 1) * Q_BLOCK]
    c = _dot(xb, wc_ref[...])
    ms = jnp.mean(c * c, axis=-1, keepdims=True)
    c = c * lax.rsqrt(ms + RMS_EPS) * kvg_ref[...]
    ckv_ref[...] = c.astype(BF16)
    for b in range(ckv_t_ref.shape[0]):
        ckv_t_ref[b] = c[b * kc:(b + 1) * kc].T.astype(BF16)
    qi = _dot(xb, wqi_ref[...]).astype(BF16)
    for h in range(IDX_HEADS):
        for b in range(nq):
            qidx_ref[b, h * Q_BLOCK:(h + 1) * Q_BLOCK, :] = qi[b * Q_BLOCK:(b + 1) * Q_BLOCK,
                                                               h * IDX_DIM:(h + 1) * IDX_DIM]
    kw = _dot(xb, wkw_ref[...])
    kidx_ref[...] = kw[:, :IDX_DIM].astype(BF16)
    wi_t_ref[...] = kw.T[IDX_DIM:IDX_DIM + IDX_HEADS] * scale_w


def _dsa_attn_kernel(qidx_ref, wi_t_ref, qlat_ref, kidx_ref, ckv_ref, ckv_t_ref, o_ref,
                     key_ref, key16_ref, acc_ref, m_ref, l_ref, a_ref, p_ref, *, k_sel, heads):
    qb = pl.program_id(1)
    kc = ckv_t_ref.shape[2]
    slab = key_ref.shape[1]
    cps = slab // kc
    nq = Q_BLOCK
    pair = 2 * nq
    n_pairs = heads // 2
    n_chunks = (qb * nq + nq + kc - 1) // kc
    n_slabs = (n_chunks + cps - 1) // cps

    def key_chunk(j):
        return key_ref.at[j // cps, pl.ds(pl.multiple_of((j % cps) * kc, kc), kc), :]

    q_pos = qb * nq + lax.broadcasted_iota(I32, (kc, nq), 1)
    k_off = lax.broadcasted_iota(I32, (kc, nq), 0)

    def key16_chunk(j):
        return key16_ref.at[j // cps, pl.ds(pl.multiple_of((j % cps) * kc, kc), kc), :]

    def pad_chunk(j, carry):
        key_chunk(j)[...] = jnp.full((kc, nq), jnp.nan, F32)
        key16_chunk(j)[...] = jnp.full((kc, nq), jnp.nan, BF16)
        return carry

    lax.fori_loop(n_chunks, n_slabs * cps, pad_chunk, 0)

    def score_chunk(j, carry):
        kch = kidx_ref[pl.ds(pl.multiple_of(j * kc, kc), kc), :]
        tot = None
        for p in range(IDX_HEADS // 2):
            sc = _dot_t(kch, qidx_ref[p * pair:(p + 1) * pair, :])
            for hh in range(2):
                h = 2 * p + hh
                r = jnp.maximum(sc[:, hh * nq:(hh + 1) * nq], 0.0) * wi_t_ref[h:h + 1, :]
                tot = r if tot is None else tot + r
        tot = jnp.where(k_off + j * kc <= q_pos, tot, jnp.nan)
        key_chunk(j)[...] = tot
        key16_chunk(j)[...] = tot.astype(BF16)
        return carry

    lax.fori_loop(0, n_chunks, score_chunk, 0)

    def count(t, strict=False):
        tb = jnp.broadcast_to(t, (COUNT_ACC, nq))

        def body(s, acc):
            for g in range(slab // COUNT_ACC):
                blk = key_ref[s, g * COUNT_ACC:(g + 1) * COUNT_ACC, :]
                acc = jnp.where(blk > tb if strict else blk >= tb, acc + 1.0, acc)
            return acc

        acc = lax.fori_loop(0, n_slabs, body, jnp.zeros((COUNT_ACC, nq), F32))
        return jnp.sum(acc, axis=0, keepdims=True)

    def count16(t):
        rows16 = 2 * COUNT_ACC
        tb = jnp.broadcast_to(t.astype(BF16), (rows16, nq))

        def body(s, acc):
            for g in range(slab // rows16):
                blk = key16_ref[s, g * rows16:(g + 1) * rows16, :]
                acc = jnp.where(blk >= tb, acc + 1.0, acc)
            return acc

        acc = lax.fori_loop(0, n_slabs, body, jnp.zeros((rows16, nq), BF16))
        return jnp.sum(acc.astype(F32), axis=0, keepdims=True)

    def decode(code):
        return lax.bitcast_convert_type(code ^ ((code >> 31) & 0x7FFFFFFF), F32)

    def code16_to_32(c16):
        return jnp.left_shift(c16, 16) | jnp.where(c16 < 0, 0xFFFF, 0)

    def bit16_step(b, c16):
        t = c16 + jnp.left_shift(jnp.int32(1), 15 - b)
        return jnp.where(count16(decode(code16_to_32(t))) >= float(k_sel), t, c16)

    def bit_step(b, c):
        t = c + jnp.left_shift(jnp.int32(1), 16 - b)
        return jnp.where((count(decode(t)) >= float(k_sel)) & (t > c), t, c)

    no_code = -(2 ** 15)

    def search():
        c16 = lax.fori_loop(0, 16, bit16_step, jnp.full((1, nq), no_code, I32))
        base = code16_to_32(jnp.maximum(c16, no_code + 1)) - 2 ** 15
        c = lax.fori_loop(0, 17, bit_step, base)
        thr = jnp.where(c16 == no_code, -jnp.inf, decode(c))
        return thr, jnp.where(c16 == no_code, 0.0, count(thr))

    def select_all():
        return jnp.full((1, nq), -jnp.inf, F32), jnp.zeros((1, nq), F32)

    c_sel, n_ge = lax.cond(qb * nq + nq <= k_sel, select_all, search)

    has_tie = jnp.max(n_ge) > float(k_sel)

    @pl.when(has_tie)
    def _():
        need = float(k_sel) - count(c_sel, strict=True)
        tie_col = n_ge > float(k_sel)
        cb = jnp.broadcast_to(c_sel, (kc, nq))
        lower = (lax.broadcasted_iota(I32, (kc, kc), 1) < lax.broadcasted_iota(I32, (kc, kc), 0))
        lower = jnp.where(lower, 1.0, 0.0).astype(BF16)

        def fix(j, seen):
            blk = key_chunk(j)[...]
            eq = blk == cb
            eq_f = jnp.where(eq, 1.0, 0.0)
            rank = seen + _dot(lower, eq_f.astype(BF16))
            drop = eq & tie_col & (rank >= need)
            key_chunk(j)[...] = jnp.where(drop, jnp.nan, blk)
            return seen + jnp.sum(eq_f, axis=0, keepdims=True)

        lax.fori_loop(0, n_chunks, fix, jnp.zeros((1, nq), F32))

    thr = jnp.broadcast_to(c_sel, (kc, nq))
    m_ref[...] = jnp.full(m_ref.shape, NEG_BIG, F32)
    l_ref[...] = jnp.zeros(l_ref.shape, F32)
    acc_ref[...] = jnp.zeros(acc_ref.shape, F32)

    def softmax_chunk(j):
        ck = ckv_ref[pl.ds(pl.multiple_of(j * kc, kc), kc), :]
        bias = jnp.where(key_chunk(j)[...] >= thr, 0.0, NEG_BIG)
        bias = jnp.concatenate([bias, bias], axis=1)
        slot = j % 2
        for p in range(n_pairs):
            lg = _dot_t(ck, qlat_ref[p * pair:(p + 1) * pair, :]) + bias
            m_old = m_ref[p]
            m_new = jnp.maximum(m_old, jnp.max(lg, axis=0, keepdims=True))
            pr = jnp.exp2(lg - m_new)
            alpha = jnp.exp2(m_old - m_new)
            l_ref[p] = alpha * l_ref[p] + jnp.sum(pr, axis=0, keepdims=True)
            m_ref[p] = m_new
            a_ref[slot, p] = alpha
            p_ref[slot, p] = pr.astype(BF16)

    def accumulate_chunk(j):
        ck_t = ckv_t_ref[j]
        slot = j % 2
        for p in range(n_pairs):
            acc_ref[p] = a_ref[slot, p] * acc_ref[p] + _dot(ck_t, p_ref[slot, p])

    def att_step(j, carry):
        accumulate_chunk(j - 1)
        softmax_chunk(j)
        return carry

    softmax_chunk(0)
    lax.fori_loop(1, n_chunks, att_step, 0)
    accumulate_chunk(n_chunks - 1)
    for p in range(n_pairs):
        o_ref[p * pair:(p + 1) * pair, :] = (acc_ref[p] * (1.0 / l_ref[p])).T.astype(o_ref.dtype)


def _dsa_out_kernel(o_ref, x_ref, wuv_ref, wout_ref, ln_g_ref, ln_b_ref, wr_hi_ref, wr_lo_ref, br_ref,
                    x1_ref, route_ref, o2_ref, *, alpha):
    nq = o_ref.shape[0]
    heads, _, hd = wuv_ref.shape
    for b in range(nq):
        for h in range(heads):
            oh = o_ref[b, h * Q_BLOCK:(h + 1) * Q_BLOCK, :]
            o2_ref[b * Q_BLOCK:(b + 1) * Q_BLOCK, h * hd:(h + 1) * hd] = _dot(oh, wuv_ref[h]).astype(BF16)
    mix = _dot(o2_ref[...], wout_ref[...])
    _norm_and_route(alpha * x_ref[...] + mix, ln_g_ref[...], ln_b_ref[...], wr_hi_ref[...], wr_lo_ref[...],
                    br_ref[...], x1_ref, route_ref)


def _dsa_layer(xt, batch, w_in, kv_g, w_uk, w_uv, w_out, ln_g, ln_b, router, alpha, tm):
    t, d = xt.shape
    seq = t // batch
    heads, lat, hd = w_uk.shape
    qw = heads * hd
    nqb = t // Q_BLOCK
    nq = tm // Q_BLOCK
    k_sel = min(TOPK_MAX, seq // 4)
    kc = min(KEY_CHUNK, seq)
    slab = min(COUNT_SLAB, seq)
    assert seq // (2 * COUNT_ACC) <= 256, "bf16 partial counts of the threshold search must stay exact"
    w_in = w_in.astype(BF16)
    o1, o2, o3 = qw, qw + lat, qw + lat + IDX_HEADS * IDX_DIM
    wkw = jnp.pad(w_in[:, o3:], ((0, 0), (0, LANES - (w_in.shape[1] - o3))))
    proj_args = (xt, w_in[:, :o1], w_in[:, o1:o2], w_in[:, o2:o3], wkw, kv_g[None],
                 jnp.swapaxes(w_uk, 1, 2).astype(BF16))
    qlat, ckv, ckv_t, qidx, kidx, wi_t = pl.pallas_call(
        functools.partial(_dsa_proj_kernel, scale_q=hd ** -0.5 * LOG2E, scale_w=(IDX_HEADS * IDX_DIM) ** -0.5),
        grid=(t // tm,),
        in_specs=[pl.BlockSpec((tm, d), lambda i: (i, 0))] + [_full(a.shape) for a in proj_args[1:]],
        out_specs=[pl.BlockSpec((nq, heads * Q_BLOCK, lat), lambda i: (i, 0, 0)),
                   pl.BlockSpec((tm, lat), lambda i: (i, 0)),
                   pl.BlockSpec((tm // kc, lat, kc), lambda i: (i, 0, 0)),
                   pl.BlockSpec((nq, IDX_HEADS * Q_BLOCK, IDX_DIM), lambda i: (i, 0, 0)),
                   pl.BlockSpec((tm, IDX_DIM), lambda i: (i, 0)),
                   pl.BlockSpec((IDX_HEADS, tm), lambda i: (0, i))],
        out_shape=[jax.ShapeDtypeStruct((nqb, heads * Q_BLOCK, lat), BF16),
                   jax.ShapeDtypeStruct((t, lat), BF16),
                   jax.ShapeDtypeStruct((t // kc, lat, kc), BF16),
                   jax.ShapeDtypeStruct((nqb, IDX_HEADS * Q_BLOCK, IDX_DIM), BF16),
                   jax.ShapeDtypeStruct((t, IDX_DIM), BF16),
                   jax.ShapeDtypeStruct((IDX_HEADS, t), F32)],
        compiler_params=_cparams(("parallel",)),
        name="dsa_proj",
    )(*proj_args)

    nq_seq = seq // Q_BLOCK
    n_kc = seq // kc
    rows = heads * Q_BLOCK
    o = pl.pallas_call(
        functools.partial(_dsa_attn_kernel, k_sel=k_sel, heads=heads),
        grid=(batch, nq_seq),
        in_specs=[pl.BlockSpec((None, IDX_HEADS * Q_BLOCK, IDX_DIM), lambda b, q: (b * nq_seq + q, 0, 0)),
                  pl.BlockSpec((IDX_HEADS, Q_BLOCK), lambda b, q: (0, b * nq_seq + q)),
                  pl.BlockSpec((None, rows, lat), lambda b, q: (b * nq_seq + q, 0, 0)),
                  pl.BlockSpec((None, seq, IDX_DIM), lambda b, q: (b, 0, 0)),
                  pl.BlockSpec((None, seq, lat), lambda b, q: (b, 0, 0)),
                  pl.BlockSpec((None, n_kc, lat, kc), lambda b, q: (b, 0, 0, 0))],
        out_specs=pl.BlockSpec((None, rows, lat), lambda b, q: (b * nq_seq + q, 0, 0)),
        out_shape=jax.ShapeDtypeStruct((nqb, rows, lat), BF16),
        scratch_shapes=[pltpu.VMEM((seq // slab, slab, Q_BLOCK), F32),
                        pltpu.VMEM((seq // slab, slab, Q_BLOCK), BF16),
                        pltpu.VMEM((heads // 2, lat, 2 * Q_BLOCK), F32),
                        pltpu.VMEM((heads // 2, 1, 2 * Q_BLOCK), F32),
                        pltpu.VMEM((heads // 2, 1, 2 * Q_BLOCK), F32),
                        pltpu.VMEM((2, heads // 2, 1, 2 * Q_BLOCK), F32),
                        pltpu.VMEM((2, heads // 2, kc, 2 * Q_BLOCK), BF16)],
        compiler_params=_cparams(("parallel", "arbitrary")),
        name="dsa_attn",
    )(qidx, wi_t, qlat, kidx.reshape(batch, seq, IDX_DIM), ckv.reshape(batch, seq, lat),
      ckv_t.reshape(batch, n_kc, lat, kc))

    wr_hi, wr_lo, br = router
    out_args = (o, xt, w_uv.astype(BF16), w_out.astype(BF16), ln_g[None], ln_b[None], wr_hi, wr_lo, br)
    return pl.pallas_call(
        functools.partial(_dsa_out_kernel, alpha=alpha),
        grid=(t // tm,),
        in_specs=[pl.BlockSpec((nq, rows, lat), lambda i: (i, 0, 0)),
                  pl.BlockSpec((tm, d), lambda i: (i, 0))] + [_full(a.shape) for a in out_args[2:]],
        out_specs=_mixer_out_specs(tm, d),
        out_shape=_mixer_out_shape(t, d),
        scratch_shapes=[pltpu.VMEM((tm, qw), BF16)],
        compiler_params=_cparams(("parallel",)),
        name="dsa_out",
    )(*out_args)


DMA_UNROLL = 8


def _start_token_moves(n, copy_of):
    def body(g, carry):
        for u in range(DMA_UNROLL):
            copy_of(g * DMA_UNROLL + u).start(priority=u % 2)
        return carry
    lax.fori_loop(0, n // DMA_UNROLL, body, 0)


def _tok(ref, i, rows):
    return ref.at[pl.ds(i * rows, rows)]


def _rows_of(ref, j, n, rows):
    return ref.at[pl.ds(j, n, stride=rows), :]


class _TokenGather:
    def __init__(self, i, n, idx_hbm, src_hbm, idx_smem, buf, isem, rsem):
        self.i, self.n = i, n
        self.idx_hbm, self.src_hbm, self.idx_smem, self.buf, self.isem, self.rsem = (
            idx_hbm, src_hbm, idx_smem, buf, isem, rsem)
        self.tm = idx_smem.shape[2]
        self.rows = buf.shape[1] // self.tm
        self.slot = i % 2

    def _idx_copy(self, blk, s):
        return pltpu.make_async_copy(self.idx_hbm.at[blk], self.idx_smem.at[s], self.isem.at[s])

    def _token_copy(self, s, r):
        return pltpu.make_async_copy(_tok(self.src_hbm, self.idx_smem[s, 0, r], self.rows),
                                     _tok(self.buf.at[s], r, self.rows), self.rsem.at[s])

    def _wait_tokens(self, s):
        pltpu.make_async_copy(self.src_hbm.at[pl.ds(0, self.tm * self.rows)], self.buf.at[s], self.rsem.at[s]).wait()

    def arrive(self):
        @pl.when(self.i == 0)
        def _():
            first = self._idx_copy(0, 0)
            first.start()
            first.wait()
            _start_token_moves(self.tm, lambda r: self._token_copy(0, r))
            self._idx_copy(min(1, self.n - 1), 1).start()

        self._idx_copy(jnp.minimum(self.i + 1, self.n - 1), 1 - self.slot).wait()
        self._wait_tokens(self.slot)

    def prefetch(self):
        nslot = 1 - self.slot
        for r in range(self.tm):
            self._token_copy(nslot, r).start(priority=r % 2)
        self._idx_copy(jnp.minimum(self.i + 2, self.n - 1), self.slot).start()

    def finish(self):
        @pl.when(self.i == self.n - 1)
        def _():
            self._wait_tokens(1 - self.slot)
            self._idx_copy(self.n - 1, self.slot).wait()


def _plan_kernel(route_ref, x_ref, pos_ref, tinfo_ref, xs_ref,
                 hist_ref, off_ref, carry_ref, stage_ref, ppos_vmem, ppos_smem, fill_vmem, fill_smem, zero_ref,
                 ssem, psem, zsem, *, tile):
    ps = pl.program_id(0)
    b = pl.program_id(1)
    nb = pl.num_programs(1)
    n_cls, pb = hist_ref.shape
    kt = tinfo_ref.shape[1]
    cls = route_ref[0:1, :].astype(I32)
    cid = lax.broadcasted_iota(I32, (n_cls, pb), 0)
    onehot = jnp.where(cid == cls, 1.0, 0.0)

    def zero_fill(start):
        def piece(tok0, n_tok):
            cp = pltpu.make_async_copy(zero_ref.at[pl.ds(0, n_tok * X_ROWS)],
                                       xs_ref.at[pl.ds(tok0 * X_ROWS, n_tok * X_ROWS)], zsem)
            cp.start() if start else cp.wait()

        def per_class(c, carry):
            first, n = fill_smem[0, c], fill_smem[1, c]
            bit = zero_ref.shape[0] // X_ROWS
            while bit:
                @pl.when((n & bit) != 0)
                def _(bit=bit):
                    piece(first + (n & -(2 * bit)), bit)
                bit //= 2
            return carry

        lax.fori_loop(0, n_cls, per_class, 0)

        def per_tile(k, carry):
            piece(k * tile, tile)
            return carry

        lax.fori_loop(fill_smem[2, 0], xs_ref.shape[0] // (tile * X_ROWS), per_tile, 0)

    @pl.when(ps == 0)
    def _():
        @pl.when(b == 0)
        def _():
            hist_ref[...] = jnp.zeros(hist_ref.shape, F32)
            zero_ref[...] = jnp.zeros(zero_ref.shape, F32)
        hist_ref[...] += onehot

    @pl.when((ps == 1) & (b == 0))
    def _():
        counts = jnp.sum(hist_ref[...], axis=1, keepdims=True)
        tiles = jnp.floor((counts + float(tile - 1)) * (1.0 / tile))
        cid_col = lax.broadcasted_iota(I32, (n_cls, 1), 0)
        per_group = n_cls // N_EGROUPS
        for g in range(N_EGROUPS):
            g_tiles = jnp.sum(tiles[g * per_group:(g + 1) * per_group], axis=0, keepdims=True)
            extra = jnp.ceil(g_tiles * (1.0 / MOE_STEP_TILES)) * float(MOE_STEP_TILES) - g_tiles
            tiles = jnp.where(cid_col == (g + 1) * per_group - 1, extra, tiles)
        lower = lax.broadcasted_iota(I32, (n_cls, n_cls), 1) < lax.broadcasted_iota(I32, (n_cls, n_cls), 0)
        lower = jnp.where(lower, 1.0, 0.0).astype(BF16)
        off = _dot(lower, jnp.broadcast_to(tiles, (n_cls, LANES)).astype(BF16))[:, :1]
        off_ref[...] = off * float(tile)
        carry_ref[...] = jnp.zeros(carry_ref.shape, F32)
        k = lax.broadcasted_iota(I32, (n_cls, kt), 1).astype(F32)
        cid_f = lax.broadcasted_iota(I32, (n_cls, kt), 0).astype(F32)
        mine = (k >= off) & (k < off + tiles)
        tcls = jnp.sum(jnp.where(mine, cid_f, 0.0), axis=0, keepdims=True)
        n_used = jnp.sum(tiles, axis=0, keepdims=True)
        last = jnp.max(jnp.where(tiles > 0.0, cid_f[:, :1], 0.0), axis=0, keepdims=True)
        tcls = jnp.where(k[:1] >= n_used, last, tcls)
        row = lax.broadcasted_iota(I32, (ROUTE_ROWS, kt), 0)
        tinfo_ref[...] = jnp.where(row == 0, tcls, jnp.where(row == 1, n_used, 0.0)).astype(I32)
        as_row = lambda col: jnp.broadcast_to(col, (n_cls, LANES)).T[0:1]
        frow = lax.broadcasted_iota(I32, fill_vmem.shape, 0)
        fill_vmem[...] = jnp.where(frow == 0, as_row(off * float(tile) + counts),
                                   jnp.where(frow == 1, as_row(tiles * float(tile) - counts), n_used)).astype(I32)
        to_fill = pltpu.make_async_copy(fill_vmem, fill_smem, psem)
        to_fill.start()
        to_fill.wait()
        zero_fill(start=True)

    @pl.when(ps == 1)
    def _():
        upper = lax.broadcasted_iota(I32, (pb, pb), 0) < lax.broadcasted_iota(I32, (pb, pb), 1)
        upper = jnp.where(upper, 1.0, 0.0).astype(BF16)
        before = _dot(onehot.astype(BF16), upper)
        ppos = jnp.sum(onehot * (before + carry_ref[...] + off_ref[...]), axis=0, keepdims=True).astype(I32)
        carry_ref[...] += jnp.sum(onehot, axis=1, keepdims=True)
        gb = pos_ref.shape[2]
        for h in range(pos_ref.shape[0]):
            pos_ref[h] = ppos[:, h * gb:(h + 1) * gb]
        ppos_vmem[...] = ppos
        to_smem = pltpu.make_async_copy(ppos_vmem, ppos_smem, psem)
        to_smem.start()

        slot = b % 2
        stage = stage_ref.at[slot]

        def drain(s):
            pltpu.make_async_copy(stage_ref.at[s], xs_ref.at[pl.ds(0, pb * X_ROWS)], ssem.at[s]).wait()

        @pl.when(b >= 2)
        def _():
            drain(slot)

        for j in range(Y_ROWS):
            _rows_of(stage, j, pb, X_ROWS)[...] = x_ref[:, j * LANES:(j + 1) * LANES]
        record = jnp.concatenate([route_ref[...], jnp.zeros((LANES - ROUTE_ROWS, pb), F32)], axis=0).T
        _rows_of(stage, Y_ROWS, pb, X_ROWS)[...] = record

        to_smem.wait()
        _start_token_moves(pb, lambda r: pltpu.make_async_copy(
            _tok(stage, r, X_ROWS), _tok(xs_ref, ppos_smem[0, r], X_ROWS), ssem.at[slot]))

        @pl.when(b == nb - 1)
        def _():
            drain(slot)

            @pl.when(nb > 1)
            def _():
                drain(1 - slot)

            zero_fill(start=False)


def _moe_kernel(tcls_ref, nused_ref, xs_ref, win_ref, wout_ref, ys_ref):
    i = pl.program_id(0)
    tm = xs_ref.shape[0] // (X_ROWS * MOE_STEP_TILES)
    ff = wout_ref.shape[1]

    @pl.when(i * MOE_STEP_TILES < nused_ref[0])
    def _():
        for jt in range(MOE_STEP_TILES):
            cls = tcls_ref[i * MOE_STEP_TILES + jt]
            x0, y0 = jt * tm * X_ROWS, jt * tm * Y_ROWS
            xb = jnp.concatenate([_rows_of(xs_ref, x0 + j, tm, X_ROWS)[...].astype(BF16) for j in range(Y_ROWS)],
                                 axis=1)
            record = _rows_of(xs_ref, x0 + Y_ROWS, tm, X_ROWS)[...]
            y = None
            for e, gate in (((cls >> 3) & 7, record[:, 1:2]), (cls & 7, record[:, 2:3])):
                h = _dot(xb, win_ref[e])
                a = h[:, :ff]
                act = (a * jax.nn.sigmoid(a) * h[:, ff:] * gate).astype(BF16)
                ye = _dot(act, wout_ref[e])
                y = ye if y is None else y + ye
            for j in range(Y_ROWS):
                _rows_of(ys_ref, y0 + j, tm, Y_ROWS)[...] = y[:, j * LANES:(j + 1) * LANES]

    @pl.when(i * MOE_STEP_TILES >= nused_ref[0])
    def _():
        ys_ref[...] = jnp.zeros(ys_ref.shape, F32)


def _post_kernel(pos_ref, ys_ref, x1_ref, p_ref, ln_g_ref, ln_b_ref, pw_ref, gw_ref, gb_ref, out_ref,
                 idx_smem, ybuf, isem, rsem, *, alpha, n_blocks):
    i = pl.program_id(0)
    tm = x1_ref.shape[0]
    gather = _TokenGather(i, n_blocks, pos_ref, ys_ref, idx_smem, ybuf, isem, rsem)
    gather.arrive()
    gather.prefetch()
    yb = ybuf.at[i % 2]
    ffn = jnp.concatenate([_rows_of(yb, j, tm, Y_ROWS)[...] for j in range(Y_ROWS)], axis=1)
    x2 = _layer_norm(alpha * x1_ref[...] + ffn, ln_g_ref[...], ln_b_ref[...])
    gate = jax.nn.sigmoid(_dot(x2.astype(BF16), gw_ref[...]) + gb_ref[...])
    out_ref[...] = x2 + _dot(p_ref[...].astype(BF16), pw_ref[...]) * gate
    gather.finish()


def _moe_and_post(x1, route, layer, p, e_w_in, e_w_out, ln_g, ln_b, ple_w, ple_gw, ple_gb, alpha):
    t, d = x1.shape
    assert d == LANES * Y_ROWS
    depth, n_exp, _, ff2 = e_w_in.shape
    epg = n_exp // N_EGROUPS
    pb = _tile(t, PLAN_BLOCK)
    nb = t // pb
    gb = _tile(pb, GATHER_BLOCK)
    nt = t // MOE_TILE + N_EGROUPS * (epg * (epg - 1) // 2) + N_EGROUPS * (MOE_STEP_TILES - 1)
    nt = -(-nt // MOE_STEP_TILES) * MOE_STEP_TILES
    kt = -(-nt // LANES) * LANES
    n_sorted = nt * MOE_TILE

    pos, tinfo, xs = pl.pallas_call(
        functools.partial(_plan_kernel, tile=MOE_TILE),
        grid=(2, nb),
        in_specs=[pl.BlockSpec((ROUTE_ROWS, pb), lambda ps, b: (0, b)),
                  pl.BlockSpec((pb, d), lambda ps, b: (ps * b, 0))],
        out_specs=[pl.BlockSpec((pb // gb, 1, gb), lambda ps, b: (ps * b, 0, 0)),
                   pl.BlockSpec((ROUTE_ROWS, kt), lambda ps, b: (0, 0)),
                   pl.BlockSpec(memory_space=pl.ANY)],
        out_shape=[jax.ShapeDtypeStruct((t // gb, 1, gb), I32),
                   jax.ShapeDtypeStruct((ROUTE_ROWS, kt), I32),
                   jax.ShapeDtypeStruct((n_sorted * X_ROWS, LANES), F32)],
        scratch_shapes=[pltpu.VMEM((N_CLASS, pb), F32),
                        pltpu.VMEM((N_CLASS, 1), F32),
                        pltpu.VMEM((N_CLASS, 1), F32),
                        pltpu.VMEM((2, pb * X_ROWS, LANES), F32),
                        pltpu.VMEM((1, pb), I32),
                        pltpu.SMEM((1, pb), I32),
                        pltpu.VMEM((ROUTE_ROWS, N_CLASS), I32),
                        pltpu.SMEM((ROUTE_ROWS, N_CLASS), I32),
                        pltpu.VMEM((MOE_TILE * ZERO_TILES * X_ROWS, LANES), F32),
                        pltpu.SemaphoreType.DMA((2,)),
                        pltpu.SemaphoreType.DMA(()),
                        pltpu.SemaphoreType.DMA(())],
        compiler_params=_cparams(("arbitrary", "arbitrary")),
        name="plan",
    )(route, x1)

    w_in = e_w_in.reshape(depth, N_EGROUPS, epg, d, ff2)
    w_out = e_w_out.reshape(depth, N_EGROUPS, epg, ff2 // 2, d)
    step_rows = MOE_TILE * MOE_STEP_TILES

    def group(i, tile_cls):
        return tile_cls[i * MOE_STEP_TILES] >> 6
    ys = pl.pallas_call(
        _moe_kernel,
        grid_spec=pltpu.PrefetchScalarGridSpec(
            num_scalar_prefetch=2,
            grid=(nt // MOE_STEP_TILES,),
            in_specs=[pl.BlockSpec((step_rows * X_ROWS, LANES), lambda i, tc, nu: (i, 0)),
                      pl.BlockSpec((None, None, epg, d, ff2), lambda i, tc, nu: (layer, group(i, tc), 0, 0, 0),
                                   pipeline_mode=pl.Buffered(1)),
                      pl.BlockSpec((None, None, epg, ff2 // 2, d), lambda i, tc, nu: (layer, group(i, tc), 0, 0, 0),
                                   pipeline_mode=pl.Buffered(1))],
            out_specs=pl.BlockSpec((step_rows * Y_ROWS, LANES), lambda i, tc, nu: (i, 0))),
        out_shape=jax.ShapeDtypeStruct((n_sorted * Y_ROWS, LANES), F32),
        compiler_params=_cparams(("arbitrary",)),
        name="moe",
    )(tinfo[0, :nt], tinfo[1, :1], xs, w_in, w_out)

    post_args = (pos, ys, x1, p, ln_g[None], ln_b[None], ple_w.astype(BF16), ple_gw.astype(BF16), ple_gb[None])
    return pl.pallas_call(
        functools.partial(_post_kernel, alpha=alpha, n_blocks=t // gb),
        grid=(t // gb,),
        in_specs=[pl.BlockSpec(memory_space=pl.ANY),
                  pl.BlockSpec(memory_space=pl.ANY),
                  pl.BlockSpec((gb, d), lambda i: (i, 0)),
                  pl.BlockSpec((None, gb, p.shape[2]), lambda i: (layer, i, 0))]
                 + [_full(a.shape) for a in post_args[4:]],
        out_specs=pl.BlockSpec((gb, d), lambda i: (i, 0)),
        out_shape=jax.ShapeDtypeStruct((t, d), F32),
        scratch_shapes=[pltpu.SMEM((2, 1, gb), I32), pltpu.VMEM((2, gb * Y_ROWS, LANES), F32),
                        pltpu.SemaphoreType.DMA((2,)), pltpu.SemaphoreType.DMA((2,))],
        compiler_params=_cparams(("arbitrary",)),
        name="post",
    )(*post_args)


def _router_params(wg, bg, we, be):
    d = wg.shape[0]
    w = jnp.zeros((d, ROUTE_COLS), F32).at[:, :N_EGROUPS].set(wg).at[:, 8:8 + we.shape[1]].set(we)
    b = jnp.zeros((1, ROUTE_COLS), F32).at[0, :N_EGROUPS].set(bg).at[0, 8:8 + be.shape[0]].set(be)
    hi = w.astype(BF16)
    return hi, (w - hi.astype(F32)).astype(BF16), b


def _tile(t, want):
    while t % want:
        want //= 2
    return want


def kernel(x, p, a_w_in, a_b_in, a_vn_g, a_vn_b, a_w_s, a_b_s, a_w_out, b_w_in, b_kv_g, b_w_uk, b_w_uv, b_w_out,
           ln1_g, ln1_b, ln2_g, ln2_b, r_wg, r_bg, r_we, r_be, e_w_in, e_w_out, ple_w, ple_gw, ple_gb):
    batch, seq, d = x.shape
    t = batch * seq
    depth = p.shape[0]
    alpha = (2 * depth) ** 0.25
    chunk = a_w_s.shape[-1]
    xt = x.reshape(t, d)
    pt = p.reshape(depth, t, p.shape[-1])
    for i in range(depth):
        j = i // 2
        router = _router_params(r_wg[i], r_bg[i], r_we[i], r_be[i])
        if i % 2 == 0:
            x1, route = _gmlp_layer(xt, a_w_in[j], a_b_in[j], a_vn_g[j], a_vn_b[j], a_w_s[j], a_b_s[j],
                                    a_w_out[j], ln1_g[i], ln1_b[i], router, alpha, max(chunk, _tile(t, MIXER_TILE)))
        else:
            x1, route = _dsa_layer(xt, batch, b_w_in[j], b_kv_g[j], b_w_uk[j], b_w_uv[j], b_w_out[j],
                                   ln1_g[i], ln1_b[i], router, alpha, max(KEY_CHUNK, _tile(t, MIXER_TILE)))
        xt = _moe_and_post(x1, route, i, pt, e_w_in, e_w_out, ln2_g[i], ln2_b[i],
                           ple_w[i], ple_gw[i], ple_gb[i], alpha)
    return xt.reshape(batch, seq, d)
```
